```python
import math
import jax, jax.numpy as jnp
from jax import lax
import numpy as np

D_MODEL = 2048
BATCH = 16
SEQ = 256
DEPTH = 2
DEC_BATCH = 8
DEC_SEQ = 4096
PAST_LEN = 512

GRID_W = 64
NORM_EPS = 1e-6
F_FLOOR = 1e-30
N_BRANCHES = 4
FOURIER_WIDTH = 512
FOURIER_GROUPS = 4
FOURIER_GROUP_DIM = FOURIER_WIDTH // FOURIER_GROUPS
S5_WIDTH = 512
S5_GROUP_DIM = 16
S5_GROUPS = S5_WIDTH // S5_GROUP_DIM
S5_STATE = 64
N_HEADS = 8
N_KV_HEADS = 2
HEAD_DIM = 128
Q_PER_KV = N_HEADS // N_KV_HEADS
ATTN_WIDTH = N_HEADS * HEAD_DIM
KV_WIDTH = N_KV_HEADS * HEAD_DIM
ROPE_THETA = 10000.0
Q_BLOCK = 128
HGRN_HEADS = 4
HGRN_DK = 128
HGRN_DV = 128
HGRN_WIDTH = HGRN_HEADS * HGRN_DK
HGRN_VWIDTH = HGRN_HEADS * HGRN_DV
HGRN_CHUNK = 64

IN_SIZES = (FOURIER_WIDTH, FOURIER_WIDTH,
            S5_WIDTH, S5_WIDTH,
            ATTN_WIDTH, KV_WIDTH, KV_WIDTH, ATTN_WIDTH,
            HGRN_WIDTH, HGRN_VWIDTH, HGRN_WIDTH, HGRN_WIDTH, HGRN_VWIDTH,
            N_BRANCHES * D_MODEL)
IN_COLS = sum(IN_SIZES)

kernel_name = 'hybrid_diffusion_gated_branch_step'


def rmsnorm(x, g):
    x32 = x.astype(jnp.float32)
    y = x32 * lax.rsqrt(jnp.mean(x32 * x32, axis=-1, keepdims=True) + NORM_EPS)
    return (y * g.astype(jnp.float32)).astype(x.dtype)


def split_cols(proj):
    idx = [int(v) for v in np.cumsum(IN_SIZES)[:-1]]
    return jnp.split(proj, idx, axis=-1)


def fourier_mix(u, w):
    b, l, _ = u.shape
    ug = u.astype(jnp.float32).reshape(b, l, FOURIER_GROUPS, FOURIER_GROUP_DIM)
    f = jnp.fft.fftn(ug, axes=(1, 3), norm='ortho').real
    y = jnp.einsum('blgc,gcd->blgd', f, w.astype(jnp.float32))
    return y.reshape(b, l, FOURIER_WIDTH).astype(u.dtype)


def s5_discretize(lam_re, lam_im, log_step, b_re, b_im):
    lam_re = lam_re.astype(jnp.float32)
    lam_im = lam_im.astype(jnp.float32)
    step = jnp.exp(log_step.astype(jnp.float32))[:, None]
    mag = jnp.exp(lam_re * step)
    lb_re = mag * jnp.cos(lam_im * step)
    lb_im = mag * jnp.sin(lam_im * step)
    nr = lb_re - 1.0
    den = lam_re * lam_re + lam_im * lam_im
    fr = (nr * lam_re + lb_im * lam_im) / den
    fi = (lb_im * lam_re - nr * lam_im) / den
    b_re = b_re.astype(jnp.float32)
    b_im = b_im.astype(jnp.float32)
    bb_re = fr[..., None] * b_re - fi[..., None] * b_im
    bb_im = fr[..., None] * b_im + fi[..., None] * b_re
    return lb_re, lb_im, bb_re, bb_im


def s5_combine(e1, e2):
    a1r, a1i, b1r, b1i = e1
    a2r, a2i, b2r, b2i = e2
    return (a2r * a1r - a2i * a1i,
            a2r * a1i + a2i * a1r,
            a2r * b1r - a2i * b1i + b2r,
            a2r * b1i + a2i * b1r + b2i)


def s5_scan(u, lb_re, lb_im, bb_re, bb_im, c_re, c_im, h0_re, h0_im):
    bu_re = jnp.einsum('blgp,gnp->blgn', u, bb_re)
    bu_im = jnp.einsum('blgp,gnp->blgn', u, bb_im)
    bu_re = bu_re.at[:, 0].add(lb_re * h0_re - lb_im * h0_im)
    bu_im = bu_im.at[:, 0].add(lb_re * h0_im + lb_im * h0_re)
    a_re = jnp.broadcast_to(lb_re, bu_re.shape)
    a_im = jnp.broadcast_to(lb_im, bu_im.shape)
    _, _, h_re, h_im = lax.associative_scan(s5_combine, (a_re, a_im, bu_re, bu_im), axis=1)
    c_re = c_re.astype(jnp.float32)
    c_im = c_im.astype(jnp.float32)
    y = jnp.einsum('blgn,gpn->blgp', h_re, c_re) - jnp.einsum('blgn,gpn->blgp', h_im, c_im)
    return y, h_re[:, -1], h_im[:, -1]


def s5_branch(u, p, h0_re, h0_im):
    b, l, _ = u.shape
    u32 = u.astype(jnp.float32).reshape(b, l, S5_GROUPS, S5_GROUP_DIM)
    ys, fin_re, fin_im = [], [], []
    for d in range(2):
        lb_re, lb_im, bb_re, bb_im = s5_discretize(p['s5_lambda_re'][d], p['s5_lambda_im'][d],
                                                   p['s5_log_step'][d], p['s5_b_re'][d], p['s5_b_im'][d])
        ud = u32 if d == 0 else jnp.flip(u32, axis=1)
        yd, hr, hi = s5_scan(ud, lb_re, lb_im, bb_re, bb_im, p['s5_c_re'][d], p['s5_c_im'][d],
                             h0_re[:, d].astype(jnp.float32), h0_im[:, d].astype(jnp.float32))
        ys.append(yd if d == 0 else jnp.flip(yd, axis=1))
        fin_re.append(hr)
        fin_im.append(hi)
    dskip = p['s5_d'].astype(jnp.float32).reshape(S5_GROUPS, S5_GROUP_DIM)
    y = (ys[0] + ys[1] + dskip * u32).reshape(b, l, S5_WIDTH)
    y = jax.nn.gelu(y)
    y = y * jax.nn.sigmoid(y @ p['s5_glu_w'].astype(jnp.float32) + p['s5_glu_b'].astype(jnp.float32))
    return y.astype(u.dtype), jnp.stack(fin_re, axis=1), jnp.stack(fin_im, axis=1)


def attention_heads(q, k, v, q_norm, k_norm):
    b, l, _ = q.shape
    q = rmsnorm(q.reshape(b, l, N_KV_HEADS, Q_PER_KV, HEAD_DIM), q_norm)
    k = rmsnorm(k.reshape(b, l, N_KV_HEADS, HEAD_DIM), k_norm)
    v = v.reshape(b, l, N_KV_HEADS, HEAD_DIM)
    return q, k, v


def axial_angles(l):
    rows = l // GRID_W
    row = jnp.broadcast_to(jnp.arange(rows, dtype=jnp.float32)[:, None], (rows, GRID_W)).reshape(-1)
    col = jnp.broadcast_to(jnp.arange(GRID_W, dtype=jnp.float32)[None, :], (rows, GRID_W)).reshape(-1)
    half = HEAD_DIM // 2
    inv = ROPE_THETA ** (-jnp.arange(0, half, 2, dtype=jnp.float32) / half)
    return jnp.stack([row[:, None] * inv, col[:, None] * inv], axis=1)


def apply_axial_rope(x, ang):
    l = x.shape[1]
    xr = x.astype(jnp.float32).reshape(x.shape[:-1] + (2, 2, HEAD_DIM // 4))
    a = ang.reshape((l,) + (1,) * (x.ndim - 3) + (2, HEAD_DIM // 4))
    cos, sin = jnp.cos(a), jnp.sin(a)
    x1, x2 = xr[..., 0, :], xr[..., 1, :]
    out = jnp.stack([x1 * cos - x2 * sin, x2 * cos + x1 * sin], axis=-2)
    return out.reshape(x.shape).astype(x.dtype)


def block_attention(q, k, v):
    b, lq, hkv, g, hd = q.shape
    nb = lq // Q_BLOCK
    qb = q.reshape(b, nb, Q_BLOCK, hkv, g, hd).transpose(1, 0, 2, 3, 4, 5)
    scale = HEAD_DIM ** -0.5

    def one_block(qblk):
        s = jnp.einsum('bqhgd,bkhd->bhgqk', qblk, k).astype(jnp.float32) * scale
        pr = jax.nn.softmax(s, axis=-1).astype(v.dtype)
        return jnp.einsum('bhgqk,bkhd->bqhgd', pr, v)

    o = lax.map(one_block, qb)
    return o.transpose(1, 0, 2, 3, 4, 5).reshape(b, lq, hkv * g * hd)


def hgrn_chunk_scan(q, k, v, log_f, s0):
    b, l, h, dk = q.shape
    n = l // HGRN_CHUNK

    def to_chunks(t):
        return t.reshape(b, n, HGRN_CHUNK, h, t.shape[-1]).transpose(1, 0, 3, 2, 4)

    lower = jnp.tril(jnp.ones((HGRN_CHUNK, HGRN_CHUNK), dtype=bool))[:, :, None]

    def step(s, inp):
        qc, kc, vc, lfc = inp
        cum = jnp.cumsum(lfc, axis=2)
        inter = jnp.einsum('bhtd,bhde->bhte', qc * jnp.exp(cum), s)
        diff = cum[:, :, :, None, :] - cum[:, :, None, :, :]
        decay = jnp.where(lower, jnp.exp(jnp.where(lower, diff, 0.0)), 0.0)
        scores = jnp.einsum('bhtd,bhtsd,bhsd->bhts', qc, decay, kc)
        intra = jnp.einsum('bhts,bhse->bhte', scores, vc)
        last = cum[:, :, -1, :]
        s_new = jnp.exp(last)[..., None] * s + jnp.einsum(
            'bhsd,bhse->bhde', kc * jnp.exp(last[:, :, None, :] - cum), vc)
        return s_new, inter + intra

    s_fin, out = lax.scan(step, s0, (to_chunks(q), to_chunks(k), to_chunks(v), to_chunks(log_f)))
    out = out.transpose(1, 0, 3, 2, 4).reshape(b, l, h, v.shape[-1])
    return out, s_fin


def hgrn_branch(q, i, z_fwd, z_bwd, lower_bound, s0, norm_g):
    b, l, _ = q.shape
    q32 = q.astype(jnp.float32).reshape(b, l, HGRN_HEADS, HGRN_DK)
    v32 = i.astype(jnp.float32).reshape(b, l, HGRN_HEADS, HGRN_DV)
    outs, finals = [], []
    for d, z in enumerate((z_fwd, z_bwd)):
        z32 = z.astype(jnp.float32).reshape(b, l, HGRN_HEADS, HGRN_DK)
        lb = lower_bound[d].astype(jnp.float32).reshape(HGRN_HEADS, HGRN_DK)
        f = lb + (1.0 - lb) * jax.nn.sigmoid(z32)
        log_f = jnp.log(jnp.maximum(f, F_FLOOR))
        kk = 1.0 - f
        qd, kd, vd, fd = q32, kk, v32, log_f
        if d == 1:
            qd, kd, vd, fd = (jnp.flip(t, axis=1) for t in (qd, kd, vd, fd))
        o, s_fin = hgrn_chunk_scan(qd, kd, vd, fd, s0[:, d].astype(jnp.float32))
        outs.append(o if d == 0 else jnp.flip(o, axis=1))
        finals.append(s_fin)
    o = rmsnorm(outs[0] + outs[1], norm_g)
    return o.reshape(b, l, HGRN_VWIDTH).astype(q.dtype), jnp.stack(finals, axis=1)


def trunk_layer(x, shift, scale, gate, p, ctx):
    b, l, _ = x.shape
    h = rmsnorm(x, p['norm_pre']) * (1.0 + scale) + shift
    proj = h @ p['w_in']
    (u_a, g_a, u_b, g_b, q_c, k_c, v_c, g_c,
     q_d, i_d, zf_d, zb_d, g_d, m) = split_cols(proj)
    if ctx is None:
        s5_h0_re = jnp.zeros((b, 2, S5_GROUPS, S5_STATE), jnp.float32)
        s5_h0_im = jnp.zeros((b, 2, S5_GROUPS, S5_STATE), jnp.float32)
        hgrn_s0 = jnp.zeros((b, 2, HGRN_HEADS, HGRN_DK, HGRN_DV), jnp.float32)
    else:
        ctx_k, ctx_v, s5_h0_re, s5_h0_im, hgrn_s0 = ctx

    y_a = fourier_mix(u_a, p['fourier_w'])
    y_b, s5_re, s5_im = s5_branch(u_b, p, s5_h0_re, s5_h0_im)
    q, k, v = attention_heads(q_c, k_c, v_c, p['q_norm'], p['k_norm'])
    if ctx is None:
        y_c = block_attention(q, k, v)
    else:
        ang = axial_angles(l)
        q = apply_axial_rope(q, ang)
        k_lat = apply_axial_rope(k, ang)
        keys = jnp.concatenate([k_lat, ctx_k.astype(k.dtype)], axis=1)
        vals = jnp.concatenate([v, ctx_v.astype(v.dtype)], axis=1)
        y_c = block_attention(q, keys, vals)
    y_d, hgrn_s = hgrn_branch(q_d, i_d, zf_d, zb_d, p['lower_bound'], hgrn_s0, p['hgrn_norm'])

    merge = jax.nn.sigmoid(m.reshape(b, l, N_BRANCHES, D_MODEL))
    branches = ((y_a, g_a, p['w_proj_a']), (y_b, g_b, p['w_proj_b']),
                (y_c, g_c, p['w_proj_c']), (y_d, g_d, p['w_proj_d']))
    mixed = None
    for j, (y, g, w) in enumerate(branches):
        term = merge[:, :, j] * ((y * jax.nn.silu(g)) @ w)
        mixed = term if mixed is None else mixed + term
    out = mixed @ p['w_out']
    x_new = x + gate * rmsnorm(out, p['norm_post'])
    if ctx is None:
        return x_new, (k, v, s5_re, s5_im, hgrn_s)
    return x_new, None


def setup_inputs(seed: int = 0) -> dict:
    key = jax.random.key(seed)
    ks = iter(jax.random.split(key, 48))
    f32 = jnp.float32

    def nrm(shape, s=1.0):
        return jax.random.normal(next(ks), shape, f32) * s

    def gain(shape):
        return 1.0 + nrm(shape, 0.02)

    inp = {}
    inp['x_prompt'] = nrm((BATCH, SEQ, D_MODEL))
    inp['x_sample'] = nrm((DEC_BATCH, DEC_SEQ, D_MODEL))
    inp['c'] = nrm((DEC_BATCH, D_MODEL))
    inp['cache_k'] = nrm((DEC_BATCH, DEPTH, PAST_LEN, N_KV_HEADS, HEAD_DIM))
    inp['cache_v'] = nrm((DEC_BATCH, DEPTH, PAST_LEN, N_KV_HEADS, HEAD_DIM))
    inp['state_s5_re'] = nrm((DEC_BATCH, DEPTH, 2, S5_GROUPS, S5_STATE), 0.1)
    inp['state_s5_im'] = nrm((DEC_BATCH, DEPTH, 2, S5_GROUPS, S5_STATE), 0.1)
    inp['state_hgrn'] = nrm((DEC_BATCH, DEPTH, 2, HGRN_HEADS, HGRN_DK, HGRN_DV), 0.1)
    inp['c_ctx'] = nrm((D_MODEL,))
    inp['norm_pre'] = gain((DEPTH, D_MODEL))
    inp['norm_post'] = gain((DEPTH, D_MODEL))
    inp['w_mod'] = nrm((DEPTH, D_MODEL, 3 * D_MODEL), 0.3 * D_MODEL ** -0.5)
    inp['b_mod'] = nrm((DEPTH, 3 * D_MODEL), 0.02)
    inp['w_in'] = nrm((DEPTH, D_MODEL, IN_COLS), D_MODEL ** -0.5)
    inp['fourier_w'] = nrm((DEPTH, FOURIER_GROUPS, FOURIER_GROUP_DIM, FOURIER_GROUP_DIM), FOURIER_GROUP_DIM ** -0.5)
    inp['s5_lambda_re'] = -0.5 + nrm((DEPTH, 2, S5_GROUPS, S5_STATE), 0.01)
    inp['s5_lambda_im'] = math.pi * jnp.arange(S5_STATE, dtype=f32) + nrm((DEPTH, 2, S5_GROUPS, S5_STATE), 0.01)
    inp['s5_log_step'] = jax.random.uniform(next(ks), (DEPTH, 2, S5_GROUPS), f32, math.log(1e-3), math.log(1e-1))
    inp['s5_b_re'] = nrm((DEPTH, 2, S5_GROUPS, S5_STATE, S5_GROUP_DIM), (2 * S5_GROUP_DIM) ** -0.5)
    inp['s5_b_im'] = nrm((DEPTH, 2, S5_GROUPS, S5_STATE, S5_GROUP_DIM), (2 * S5_GROUP_DIM) ** -0.5)
    inp['s5_c_re'] = nrm((DEPTH, 2, S5_GROUPS, S5_GROUP_DIM, S5_STATE), (2 * S5_STATE) ** -0.5)
    inp['s5_c_im'] = nrm((DEPTH, 2, S5_GROUPS, S5_GROUP_DIM, S5_STATE), (2 * S5_STATE) ** -0.5)
    inp['s5_d'] = nrm((DEPTH, S5_WIDTH))
    inp['s5_glu_w'] = nrm((DEPTH, S5_WIDTH, S5_WIDTH), S5_WIDTH ** -0.5)
    inp['s5_glu_b'] = nrm((DEPTH, S5_WIDTH), 0.02)
    inp['q_norm'] = gain((DEPTH, HEAD_DIM))
    inp['k_norm'] = gain((DEPTH, HEAD_DIM))
    inp['hgrn_lb_logits'] = nrm((DEPTH, 2, HGRN_WIDTH))
    inp['hgrn_norm'] = gain((DEPTH, HGRN_DV))
    inp['w_proj_a'] = nrm((DEPTH, FOURIER_WIDTH, D_MODEL), FOURIER_WIDTH ** -0.5)
    inp['w_proj_b'] = nrm((DEPTH, S5_WIDTH, D_MODEL), S5_WIDTH ** -0.5)
    inp['w_proj_c'] = nrm((DEPTH, ATTN_WIDTH, D_MODEL), ATTN_WIDTH ** -0.5)
    inp['w_proj_d'] = nrm((DEPTH, HGRN_VWIDTH, D_MODEL), HGRN_VWIDTH ** -0.5)
    inp['w_out'] = nrm((DEPTH, D_MODEL, D_MODEL), D_MODEL ** -0.5)
    return inp


def reference(x_prompt, x_sample, c, cache_k, cache_v, state_s5_re, state_s5_im, state_hgrn, c_ctx,
              norm_pre, norm_post, w_mod, b_mod, w_in, fourier_w,
              s5_lambda_re, s5_lambda_im, s5_log_step, s5_b_re, s5_b_im, s5_c_re, s5_c_im,
              s5_d, s5_glu_w, s5_glu_b, q_norm, k_norm, hgrn_lb_logits, hgrn_norm,
              w_proj_a, w_proj_b, w_proj_c, w_proj_d, w_out):
    lb_w = jax.nn.softmax(hgrn_lb_logits.astype(jnp.float32), axis=0)
    lower_bounds = jnp.cumsum(lb_w, axis=0) - lb_w[0]

    y_p, y_s = x_prompt, x_sample
    ks, vs, s5r, s5i, hg = [], [], [], [], []
    for l in range(DEPTH):
        p = {'norm_pre': norm_pre[l], 'norm_post': norm_post[l], 'w_in': w_in[l],
             'fourier_w': fourier_w[l],
             's5_lambda_re': s5_lambda_re[l], 's5_lambda_im': s5_lambda_im[l],
             's5_log_step': s5_log_step[l], 's5_b_re': s5_b_re[l], 's5_b_im': s5_b_im[l],
             's5_c_re': s5_c_re[l], 's5_c_im': s5_c_im[l], 's5_d': s5_d[l],
             's5_glu_w': s5_glu_w[l], 's5_glu_b': s5_glu_b[l],
             'q_norm': q_norm[l], 'k_norm': k_norm[l],
             'lower_bound': lower_bounds[l], 'hgrn_norm': hgrn_norm[l],
             'w_proj_a': w_proj_a[l], 'w_proj_b': w_proj_b[l], 'w_proj_c': w_proj_c[l],
             'w_proj_d': w_proj_d[l], 'w_out': w_out[l]}
        mod_ctx = jax.nn.silu(c_ctx) @ w_mod[l] + b_mod[l]
        sh, sc, gt = jnp.split(mod_ctx, 3)
        y_p, (k_l, v_l, sr_l, si_l, hg_l) = trunk_layer(y_p, sh, sc, gt, p, None)
        ks.append(k_l)
        vs.append(v_l)
        s5r.append(sr_l)
        s5i.append(si_l)
        hg.append(hg_l)
        mod = jax.nn.silu(c) @ w_mod[l] + b_mod[l]
        sh, sc, gt = (t[:, None, :] for t in jnp.split(mod, 3, axis=-1))
        ctx = (cache_k[:, l], cache_v[:, l], state_s5_re[:, l], state_s5_im[:, l], state_hgrn[:, l])
        y_s, _ = trunk_layer(y_s, sh, sc, gt, p, ctx)

    new_cache_k = jnp.stack(ks, axis=1)
    new_cache_v = jnp.stack(vs, axis=1)
    new_state_s5_re = jnp.stack(s5r, axis=1)
    new_state_s5_im = jnp.stack(s5i, axis=1)
    new_state_hgrn = jnp.stack(hg, axis=1)
    return (y_p, y_s, new_cache_k, new_cache_v, new_state_s5_re, new_state_s5_im, new_state_hgrn)
```

```python
import functools
import math

import jax
import jax.numpy as jnp
import numpy as np
from jax import lax
from jax.experimental import pallas as pl
from jax.experimental.pallas import tpu as pltpu

F32 = jnp.float32
BF16 = jnp.bfloat16

NORM_EPS = 1e-6
F_FLOOR = 1e-30
LOG2E = 1.4426950408889634

D_MODEL = 2048
N_BRANCHES = 4
FOURIER_WIDTH = 512
FOURIER_GROUPS = 4
FOURIER_GROUP_DIM = 128
S5_WIDTH = 512
S5_GROUP_DIM = 16
S5_GROUPS = 32
S5_STATE = 64
S5_CHUNK = 16
N_HEADS = 8
N_KV_HEADS = 2
HEAD_DIM = 128
Q_PER_KV = 4
ATTN_WIDTH = 1024
KV_WIDTH = 256
ROPE_THETA = 10000.0
GRID_W = 64
HGRN_HEADS = 4
HGRN_DK = 128
HGRN_DV = 128
HGRN_WIDTH = 512
HGRN_CHUNK = 64

COL_UA, COL_GA, COL_UB, COL_GB = 0, 512, 1024, 1536
COL_QC, COL_KC, COL_VC, COL_GC = 2048, 3072, 3328, 3584
COL_QD, COL_ID, COL_ZF, COL_ZB, COL_GD = 4608, 5120, 5632, 6144, 6656
COL_M = 7168
IN_COLS = 15360

VMEM_LIMIT = 56 * 1024 * 1024


def _cparams(n_grid):
    return pltpu.CompilerParams(dimension_semantics=("arbitrary",) * n_grid,
                                vmem_limit_bytes=VMEM_LIMIT)


def _sigmoid(x):
    return 1.0 / (1.0 + jnp.exp(-x))


def _silu(x):
    return x * _sigmoid(x)


def _dot(a, b):
    return jnp.dot(a, b, preferred_element_type=F32)


def _dot_nt(a, b):
    return lax.dot_general(a, b, (((1,), (1,)), ((), ())), preferred_element_type=F32)


def _dot_tn(a, b):
    return lax.dot_general(a, b, (((0,), (0,)), ((), ())), preferred_element_type=F32)


def _split2(x):
    hi = x.astype(BF16)
    lo = (x - hi.astype(F32)).astype(BF16)
    return hi, lo


def _split3(x):
    hi = x.astype(BF16)
    r = x - hi.astype(F32)
    mid = r.astype(BF16)
    lo = (r - mid.astype(F32)).astype(BF16)
    return hi, mid, lo


def _dot3(a, b):
    ah, al = _split2(a)
    bh, bl = _split2(b)
    return _dot(ah, bh) + (_dot(ah, bl) + _dot(al, bh))


def _mod_kernel(c_ref, w_ref, b_ref, o_ref):
    a = _silu(c_ref[...])
    o_ref[0] = _dot3(a, w_ref[0]) + b_ref[0]


def _modulation(c_all, w_mod, b_mod):
    depth, d, n = w_mod.shape
    rows = c_all.shape[0]
    tn = 1024
    return pl.pallas_call(
        _mod_kernel,
        grid=(depth, n // tn),
        in_specs=[pl.BlockSpec((rows, d), lambda l, j: (0, 0)),
                  pl.BlockSpec((1, d, tn), lambda l, j: (l, 0, j)),
                  pl.BlockSpec((1, 1, tn), lambda l, j: (l, 0, j))],
        out_specs=pl.BlockSpec((1, rows, tn), lambda l, j: (l, 0, j)),
        out_shape=jax.ShapeDtypeStruct((depth, rows, n), F32),
        compiler_params=_cparams(2),
        name="modulation",
    )(c_all, w_mod, b_mod.reshape(depth, 1, n))


def _inproj_kernel(x_ref, sh_ref, sc_ref, g_ref, w_ref, o_ref, h_scr):
    @pl.when(pl.program_id(2) == 0)
    def _():
        x = x_ref[0]
        ms = jnp.mean(x * x, axis=-1, keepdims=True)
        y = x * lax.rsqrt(ms + NORM_EPS) * g_ref[...]
        h_scr[...] = (y * (1.0 + sc_ref[0]) + sh_ref[0]).astype(BF16)

    o_ref[0] = _dot(h_scr[...], w_ref[...])


def _inproj(x, shift, scale, g, w_bf, tm, tn):
    b, l, d = x.shape
    n = w_bf.shape[1]
    per_batch = shift.shape[0] > 1
    mod_map = (lambda bi, i, j: (bi, 0, 0)) if per_batch else (lambda bi, i, j: (0, 0, 0))
    return pl.pallas_call(
        _inproj_kernel,
        grid=(b, l // tm, n // tn),
        in_specs=[pl.BlockSpec((1, tm, d), lambda bi, i, j: (bi, i, 0)),
                  pl.BlockSpec((1, 1, d), mod_map),
                  pl.BlockSpec((1, 1, d), mod_map),
                  pl.BlockSpec((1, d), lambda bi, i, j: (0, 0)),
                  pl.BlockSpec((d, tn), lambda bi, i, j: (0, j))],
        out_specs=pl.BlockSpec((1, tm, tn), lambda bi, i, j: (bi, i, j)),
        out_shape=jax.ShapeDtypeStruct((b, l, n), F32),
        scratch_shapes=[pltpu.VMEM((tm, d), BF16)],
        compiler_params=_cparams(3),
        name="inproj",
    )(x, shift, scale, g, w_bf)


def _dft_tables(l):
    scale = 1.0 / math.sqrt(l * FOURIER_GROUP_DIM)
    k = np.arange(l, dtype=np.int64)
    t1 = np.arange(l // 64, dtype=np.int64)[:, None]
    t2 = np.arange(64, dtype=np.int64)[:, None]
    a = 2.0 * np.pi * ((t1 * 64 * k[None, :]) % l).astype(np.float64) / l
    bb = 2.0 * np.pi * ((t2 * k[None, :]) % l).astype(np.float64) / l
    ca, sa = jnp.asarray(np.cos(a), F32), jnp.asarray(np.sin(a), F32)
    cb, sb = jnp.asarray(np.cos(bb), F32), jnp.asarray(np.sin(bb), F32)
    cl = (ca[:, None, :] * cb[None, :, :] - sa[:, None, :] * sb[None, :, :]).reshape(l, l)
    sl = (sa[:, None, :] * cb[None, :, :] + ca[:, None, :] * sb[None, :, :]).reshape(l, l)
    dl = (jnp.concatenate([cl, -sl], axis=1) * scale).astype(BF16)
    c = np.arange(FOURIER_GROUP_DIM, dtype=np.int64)
    ang = 2.0 * np.pi * ((c[:, None] * c[None, :]) % FOURIER_GROUP_DIM) / FOURIER_GROUP_DIM
    cs = jnp.asarray(np.concatenate([np.cos(ang), np.sin(ang)], axis=1), F32).astype(BF16)
    return dl, cs


def _fourier_kernel(u_ref, g_ref, cs_ref, dl_ref, w_ref, o_ref, ucs_scr, *, l):
    gd = FOURIER_GROUP_DIM

    @pl.when(pl.program_id(1) == 0)
    def _():
        for g in range(FOURIER_GROUPS):
            ug = u_ref[0, :, g * gd:(g + 1) * gd].astype(BF16)
            t = _dot(ug, cs_ref[...])
            ucs_scr[0:l, g * gd:(g + 1) * gd] = t[:, :gd].astype(BF16)
            ucs_scr[l:2 * l, g * gd:(g + 1) * gd] = t[:, gd:].astype(BF16)

    f = _dot(dl_ref[...], ucs_scr[...])
    gate = _silu(g_ref[0])
    for g in range(FOURIER_GROUPS):
        y = _dot(f[:, g * gd:(g + 1) * gd].astype(BF16), w_ref[g].astype(BF16))
        o_ref[0, :, g * gd:(g + 1) * gd] = (y * gate[:, g * gd:(g + 1) * gd]).astype(BF16)


def _fourier_branch(proj, fourier_w, dl, cs):
    b, l, _ = proj.shape
    tr = min(l, 512)
    w = FOURIER_WIDTH
    return pl.pallas_call(
        functools.partial(_fourier_kernel, l=l),
        grid=(b, l // tr),
        in_specs=[pl.BlockSpec((1, l, w), lambda bi, r: (bi, 0, COL_UA // w)),
                  pl.BlockSpec((1, tr, w), lambda bi, r: (bi, r, COL_GA // w)),
                  pl.BlockSpec((FOURIER_GROUP_DIM, 2 * FOURIER_GROUP_DIM), lambda bi, r: (0, 0)),
                  pl.BlockSpec((tr, 2 * l), lambda bi, r: (r, 0)),
                  pl.BlockSpec((FOURIER_GROUPS, FOURIER_GROUP_DIM, FOURIER_GROUP_DIM),
                               lambda bi, r: (0, 0, 0))],
        out_specs=pl.BlockSpec((1, tr, w), lambda bi, r: (bi, r, 0)),
        out_shape=jax.ShapeDtypeStruct((b, l, w), BF16),
        scratch_shapes=[pltpu.VMEM((2 * l, w), BF16)],
        compiler_params=_cparams(2),
        name="fourier",
    )(proj, proj, cs, dl, fourier_w)


def _s5_matrices(lam_re, lam_im, log_step, b_re, b_im, c_re, c_im, dskip):
    t = S5_CHUNK
    g, n, p = S5_GROUPS, S5_STATE, S5_GROUP_DIM
    hp = lax.Precision.HIGHEST
    lam_re = lam_re.astype(F32)
    lam_im = lam_im.astype(F32)
    step = jnp.exp(log_step.astype(F32))[..., None]
    rho = lam_re * step
    th = lam_im * step
    mag = jnp.exp(rho)
    lb_re = mag * jnp.cos(th)
    lb_im = mag * jnp.sin(th)
    nr = lb_re - 1.0
    den = lam_re * lam_re + lam_im * lam_im
    fr = (nr * lam_re + lb_im * lam_im) / den
    fi = (lb_im * lam_re - nr * lam_im) / den
    b_re = b_re.astype(F32)
    b_im = b_im.astype(F32)
    bb_re = fr[..., None] * b_re - fi[..., None] * b_im
    bb_im = fr[..., None] * b_im + fi[..., None] * b_re
    tau = jnp.arange(t + 1, dtype=F32)
    pw_mag = jnp.exp(rho[..., None] * tau)
    pw_re = pw_mag * jnp.cos(th[..., None] * tau)
    pw_im = pw_mag * jnp.sin(th[..., None] * tau)
    c_re = c_re.astype(F32)
    c_im = c_im.astype(F32)
    ca_re = c_re[..., None] * pw_re[:, :, None] - c_im[..., None] * pw_im[:, :, None]
    ca_im = c_re[..., None] * pw_im[:, :, None] + c_im[..., None] * pw_re[:, :, None]
    kern = (jnp.einsum('dgpnt,dgnq->dgtpq', ca_re, bb_re, precision=hp)
            - jnp.einsum('dgpnt,dgnq->dgtpq', ca_im, bb_im, precision=hp))[:, :, :t]
    sig = np.arange(t)[:, None]
    ta = np.arange(t)[None, :]
    idx_f = np.clip(ta - sig, 0, t - 1)
    idx_b = np.clip(sig - ta, 0, t - 1)
    m_f = jnp.asarray((ta >= sig).astype(np.float32))[None, :, :, None, None]
    m_b = jnp.asarray((sig >= ta).astype(np.float32))[None, :, :, None, None]
    tf = kern[0][:, idx_f] * m_f
    tb = kern[1][:, idx_b] * m_b
    eye_t = jnp.asarray(np.eye(t, dtype=np.float32))[None, :, :, None, None]
    dmat = dskip.astype(F32)[:, None, None, :, None] * jnp.asarray(np.eye(p, dtype=np.float32))[None, None, None]
    tt = tf + tb + eye_t * dmat
    tmat = tt.transpose(0, 1, 4, 2, 3).reshape(g, t * p, t * p)

    def ba(d, powers):
        pr = pw_re[d][:, :, powers]
        pi = pw_im[d][:, :, powers]
        re = pr[..., None] * bb_re[d][:, :, None, :] - pi[..., None] * bb_im[d][:, :, None, :]
        im = pr[..., None] * bb_im[d][:, :, None, :] + pi[..., None] * bb_re[d][:, :, None, :]
        re = re.transpose(0, 2, 3, 1).reshape(g, t * p, n)
        im = im.transpose(0, 2, 3, 1).reshape(g, t * p, n)
        return re, im

    baf_re, baf_im = ba(0, np.arange(t - 1, -1, -1))
    bab_re, bab_im = ba(1, np.arange(t))
    bamat = jnp.concatenate([baf_re, bab_re, baf_im, bab_im], axis=-1)

    def ca(d, powers):
        re = ca_re[d][..., powers].transpose(0, 2, 3, 1).reshape(g, n, t * p)
        im = ca_im[d][..., powers].transpose(0, 2, 3, 1).reshape(g, n, t * p)
        return re, im

    caf_re, caf_im = ca(0, np.arange(1, t + 1))
    cab_re, cab_im = ca(1, np.arange(t, 0, -1))
    camat = jnp.concatenate([caf_re, cab_re, -caf_im, -cab_im], axis=1)
    a16 = jnp.stack([jnp.concatenate([pw_re[0][..., t], pw_re[1][..., t]], axis=-1),
                     jnp.concatenate([pw_im[0][..., t], pw_im[1][..., t]], axis=-1)], axis=1)
    return tmat, bamat, camat, a16


def _s5_kernel(u_ref, t_ref, ba_ref, ca_ref, a16_ref, h0_ref, y_ref, hf_ref, s_scr, hin_scr, *, nc, nb, gt):
    ns = S5_STATE
    for gi in range(gt):
        s_scr[gi] = _dot3(u_ref[gi], ba_ref[gi])
    is_fwd = lax.broadcasted_iota(jnp.int32, (nb, 2 * ns), 1) < ns
    a_re = [a16_ref[gi, 0:1, :] for gi in range(gt)]
    a_im = [a16_ref[gi, 1:2, :] for gi in range(gt)]

    def step(i, carry):
        rf = pl.multiple_of(i * nb, nb)
        rb = pl.multiple_of((nc - 1 - i) * nb, nb)
        out = []
        for gi in range(gt):
            re, im = carry[gi]
            hin_scr[gi, pl.ds(rf, nb), 0:ns] = re[:, 0:ns]
            hin_scr[gi, pl.ds(rb, nb), ns:2 * ns] = re[:, ns:2 * ns]
            hin_scr[gi, pl.ds(rf, nb), 2 * ns:3 * ns] = im[:, 0:ns]
            hin_scr[gi, pl.ds(rb, nb), 3 * ns:4 * ns] = im[:, ns:2 * ns]
            sf = s_scr[gi, pl.ds(rf, nb), :]
            sb = s_scr[gi, pl.ds(rb, nb), :]
            s_re = jnp.where(is_fwd, sf[:, 0:2 * ns], sb[:, 0:2 * ns])
            s_im = jnp.where(is_fwd, sf[:, 2 * ns:4 * ns], sb[:, 2 * ns:4 * ns])
            n_re = a_re[gi] * re - a_im[gi] * im + s_re
            n_im = a_re[gi] * im + a_im[gi] * re + s_im
            out.append((n_re, n_im))
        return tuple(out)

    init = tuple((h0_ref[gi, :, 0:2 * ns], h0_ref[gi, :, 2 * ns:4 * ns]) for gi in range(gt))
    fin = lax.fori_loop(0, nc, step, init)
    for gi in range(gt):
        hf_ref[gi, :, 0:2 * ns] = fin[gi][0]
        hf_ref[gi, :, 2 * ns:4 * ns] = fin[gi][1]
        y_ref[gi] = _dot3(u_ref[gi], t_ref[gi]) + _dot3(hin_scr[gi], ca_ref[gi])


def _s5_core(u_g, tmat, bamat, camat, a16, h0, nc, nb):
    g, r, w = u_g.shape
    gt = 2 if r >= 2048 else 8
    kern = functools.partial(_s5_kernel, nc=nc, nb=nb, gt=gt)
    mat_spec = pl.BlockSpec((gt, w, w), lambda i: (i, 0, 0))
    return pl.pallas_call(
        kern,
        grid=(g // gt,),
        in_specs=[pl.BlockSpec((gt, r, w), lambda i: (i, 0, 0)),
                  mat_spec, mat_spec, mat_spec,
                  pl.BlockSpec((gt, 2, 2 * S5_STATE), lambda i: (i, 0, 0)),
                  pl.BlockSpec((gt, nb, 4 * S5_STATE), lambda i: (i, 0, 0))],
        out_specs=[pl.BlockSpec((gt, r, w), lambda i: (i, 0, 0)),
                   pl.BlockSpec((gt, nb, 4 * S5_STATE), lambda i: (i, 0, 0))],
        out_shape=[jax.ShapeDtypeStruct((g, r, w), F32),
                   jax.ShapeDtypeStruct((g, nb, 4 * S5_STATE), F32)],
        scratch_shapes=[pltpu.VMEM((gt, r, w), F32), pltpu.VMEM((gt, r, w), F32)],
        compiler_params=_cparams(1),
        name="s5_core",
    )(u_g, tmat, bamat, camat, a16, h0)


def _s5_glu_kernel(y_ref, g_ref, w_ref, b_ref, o_ref):
    y = y_ref[0]
    y = 0.5 * y * (1.0 + jnp.tanh(math.sqrt(2.0 / math.pi) * (y + 0.044715 * (y * y * y))))
    z = _dot(y.astype(BF16), w_ref[...].astype(BF16)) + b_ref[...]
    o_ref[0] = (y * _sigmoid(z) * _silu(g_ref[0])).astype(BF16)


def _s5_glu(y, proj, glu_w, glu_b, tm):
    b, l, w = y.shape
    return pl.pallas_call(
        _s5_glu_kernel,
        grid=(b, l // tm),
        in_specs=[pl.BlockSpec((1, tm, w), lambda bi, i: (bi, i, 0)),
                  pl.BlockSpec((1, tm, w), lambda bi, i: (bi, i, COL_GB // w)),
                  pl.BlockSpec((w, w), lambda bi, i: (0, 0)),
                  pl.BlockSpec((1, w), lambda bi, i: (0, 0))],
        out_specs=pl.BlockSpec((1, tm, w), lambda bi, i: (bi, i, 0)),
        out_shape=jax.ShapeDtypeStruct((b, l, w), BF16),
        compiler_params=_cparams(2),
        name="s5_glu",
    )(y, proj, glu_w, glu_b.reshape(1, w))


def _s5_branch(proj, mats, glu_w, glu_b, h0_re, h0_im):
    b, l, _ = proj.shape
    g, p, t, n = S5_GROUPS, S5_GROUP_DIM, S5_CHUNK, S5_STATE
    nc = l // t
    u = proj[:, :, COL_UB:COL_UB + S5_WIDTH]
    u_g = u.reshape(b, nc, t, g, p).transpose(3, 1, 0, 2, 4).reshape(g, nc * b, t * p)
    if h0_re is None:
        h0 = jnp.zeros((g, b, 4 * n), F32)
    else:
        h0 = jnp.concatenate([h0_re[:, 0], h0_re[:, 1], h0_im[:, 0], h0_im[:, 1]], axis=-1)
        h0 = h0.astype(F32).transpose(1, 0, 2)
    y_g, hf = _s5_core(u_g, *mats, h0, nc, b)
    y = y_g.reshape(g, nc, b, t, p).transpose(2, 1, 3, 0, 4).reshape(b, l, S5_WIDTH)
    out = _s5_glu(y, proj, glu_w, glu_b, min(l, 1024))
    hf = hf.transpose(1, 0, 2)
    fin_re = jnp.stack([hf[..., 0:n], hf[..., n:2 * n]], axis=1)
    fin_im = jnp.stack([hf[..., 2 * n:3 * n], hf[..., 3 * n:4 * n]], axis=1)
    return out, fin_re, fin_im


def _rope_tables(l):
    rows = l // GRID_W
    row = jnp.broadcast_to(jnp.arange(rows, dtype=F32)[:, None], (rows, GRID_W)).reshape(-1)
    col = jnp.broadcast_to(jnp.arange(GRID_W, dtype=F32)[None, :], (rows, GRID_W)).reshape(-1)
    half = HEAD_DIM // 2
    inv = ROPE_THETA ** (-jnp.arange(0, half, 2, dtype=F32) / half)
    ar = row[:, None] * inv
    ac = col[:, None] * inv
    cos = jnp.concatenate([jnp.cos(ar), jnp.cos(ar), jnp.cos(ac), jnp.cos(ac)], axis=1)
    sin = jnp.concatenate([-jnp.sin(ar), jnp.sin(ar), -jnp.sin(ac), jnp.sin(ac)], axis=1)
    return cos, sin


def _qk_kernel(*refs, rope, want_kf):
    q_ref, k_ref, v_ref, qn_ref, kn_ref = refs[:5]
    pos = 5
    if rope:
        cos_ref, sin_ref = refs[pos:pos + 2]
        pos += 2
    qo_ref, ko_ref, vo_ref = refs[pos:pos + 3]
    pos += 3
    if want_kf:
        kf_ref = refs[pos]
    hd = HEAD_DIM
    tm = q_ref.shape[1]
    quarter = hd // 4
    if rope:
        cos = cos_ref[...]
        sin = sin_ref[...]
        first = (lax.broadcasted_iota(jnp.int32, (tm, hd), 1) % (2 * quarter)) < quarter

    def norm(x, gain):
        ms = jnp.mean(x * x, axis=-1, keepdims=True)
        return x * lax.rsqrt(ms + NORM_EPS) * gain

    def rot(x):
        if not rope:
            return x
        nxt = pltpu.roll(x, hd - quarter, 1)
        prv = pltpu.roll(x, quarter, 1)
        return x * cos + jnp.where(first, nxt, prv) * sin

    qscale = (hd ** -0.5) * LOG2E
    for h in range(N_HEADS):
        x = norm(q_ref[0, :, h * hd:(h + 1) * hd], qn_ref[...])
        qo_ref[0, :, h * hd:(h + 1) * hd] = (rot(x) * qscale).astype(BF16)
    for h in range(N_KV_HEADS):
        x = norm(k_ref[0, :, h * hd:(h + 1) * hd], kn_ref[...])
        if want_kf:
            kf_ref[0, :, h * hd:(h + 1) * hd] = x
        ko_ref[0, :, h * hd:(h + 1) * hd] = rot(x).astype(BF16)
    vo_ref[0] = v_ref[0].astype(BF16)


def _qk_prep(proj, q_norm, k_norm, rope_tabs, want_kf, tm):
    b, l, _ = proj.shape
    rope = rope_tabs is not None
    in_specs = [pl.BlockSpec((1, tm, ATTN_WIDTH), lambda bi, i: (bi, i, COL_QC // ATTN_WIDTH)),
                pl.BlockSpec((1, tm, KV_WIDTH), lambda bi, i: (bi, i, COL_KC // KV_WIDTH)),
                pl.BlockSpec((1, tm, KV_WIDTH), lambda bi, i: (bi, i, COL_VC // KV_WIDTH)),
                pl.BlockSpec((1, HEAD_DIM), lambda bi, i: (0, 0)),
                pl.BlockSpec((1, HEAD_DIM), lambda bi, i: (0, 0))]
    args = [proj, proj, proj, q_norm.reshape(1, HEAD_DIM), k_norm.reshape(1, HEAD_DIM)]
    if rope:
        in_specs += [pl.BlockSpec((tm, HEAD_DIM), lambda bi, i: (i, 0))] * 2
        args += list(rope_tabs)
    out_specs = [pl.BlockSpec((1, tm, ATTN_WIDTH), lambda bi, i: (bi, i, 0)),
                 pl.BlockSpec((1, tm, KV_WIDTH), lambda bi, i: (bi, i, 0)),
                 pl.BlockSpec((1, tm, KV_WIDTH), lambda bi, i: (bi, i, 0))]
    out_shape = [jax.ShapeDtypeStruct((b, l, ATTN_WIDTH), BF16),
                 jax.ShapeDtypeStruct((b, l, KV_WIDTH), BF16),
                 jax.ShapeDtypeStruct((b, l, KV_WIDTH), BF16)]
    if want_kf:
        out_specs.append(pl.BlockSpec((1, tm, KV_WIDTH), lambda bi, i: (bi, i, 0)))
        out_shape.append(jax.ShapeDtypeStruct((b, l, KV_WIDTH), F32))
    return pl.pallas_call(
        functools.partial(_qk_kernel, rope=rope, want_kf=want_kf),
        grid=(b, l // tm),
        in_specs=in_specs, out_specs=out_specs, out_shape=out_shape,
        compiler_params=_cparams(2),
        name="qk_prep",
    )(*args)


def _attn_kernel(*refs, has_ctx):
    if has_ctx:
        q_ref, k_ref, v_ref, ck_ref, cv_ref, g_ref, o_ref = refs
        ck = ck_ref[0, 0].astype(BF16)
        cv = cv_ref[0, 0].astype(BF16)
    else:
        q_ref, k_ref, v_ref, g_ref, o_ref = refs
    hd = HEAD_DIM
    kb = k_ref[0]
    vb = v_ref[0]
    gate = _silu(g_ref[0])
    for g in range(Q_PER_KV):
        q = q_ref[0, :, g * hd:(g + 1) * hd]
        s1 = _dot_nt(q, kb)
        m = jnp.max(s1, axis=-1, keepdims=True)
        if has_ctx:
            s2 = _dot_nt(q, ck)
            m = jnp.maximum(m, jnp.max(s2, axis=-1, keepdims=True))
        p1 = jnp.exp2(s1 - m)
        den = jnp.sum(p1, axis=-1, keepdims=True)
        o = _dot(p1.astype(BF16), vb)
        if has_ctx:
            p2 = jnp.exp2(s2 - m)
            den = den + jnp.sum(p2, axis=-1, keepdims=True)
            o = o + _dot(p2.astype(BF16), cv)
        o = o / den
        o_ref[0, :, g * hd:(g + 1) * hd] = (o * gate[:, g * hd:(g + 1) * hd]).astype(BF16)


def _attention(qs, kr, vb, proj, ctx_k, ctx_v, layer, tq):
    b, l, _ = qs.shape
    gw = Q_PER_KV * HEAD_DIM
    has_ctx = ctx_k is not None
    in_specs = [pl.BlockSpec((1, tq, gw), lambda bi, h, i: (bi, i, h)),
                pl.BlockSpec((1, l, HEAD_DIM), lambda bi, h, i: (bi, 0, h)),
                pl.BlockSpec((1, l, HEAD_DIM), lambda bi, h, i: (bi, 0, h))]
    args = [qs, kr, vb]
    if has_ctx:
        past = ctx_k.shape[2]
        ctx_spec = pl.BlockSpec((1, 1, past, HEAD_DIM), lambda bi, h, i: (bi, layer, 0, h))
        in_specs += [ctx_spec, ctx_spec]
        args += [ctx_k, ctx_v]
    in_specs.append(pl.BlockSpec((1, tq, gw), lambda bi, h, i: (bi, i, COL_GC // gw + h)))
    args.append(proj)
    return pl.pallas_call(
        functools.partial(_attn_kernel, has_ctx=has_ctx),
        grid=(b, N_KV_HEADS, l // tq),
        in_specs=in_specs,
        out_specs=pl.BlockSpec((1, tq, gw), lambda bi, h, i: (bi, i, h)),
        out_shape=jax.ShapeDtypeStruct((b, l, ATTN_WIDTH), BF16),
        compiler_params=_cparams(3),
        name="attention",
    )(*args)


HGRN_LEVELS = (32, 16, 8, 4, 2, 1)


def _hgrn_tables():
    c = HGRN_CHUNK
    t = np.arange(c)
    w01, qm, km = [], [], []
    for d in range(2):
        cum = (t[None, :] <= t[:, None]) if d == 0 else (t[None, :] >= t[:, None])
        cum = cum.astype(np.float32)
        blocks = [cum]
        qd, kd = [], []
        for hs in HGRN_LEVELS:
            base = t & ~(2 * hs - 1)
            ref_row = base + hs - 1 if d == 0 else base + hs
            blocks.append(cum[ref_row])
            is_q = ((t & hs) != 0) if d == 0 else ((t & hs) == 0)
            qd.append(np.broadcast_to(is_q[:, None].astype(np.float32), (c, HGRN_DK)))
            kd.append(np.broadcast_to((~is_q)[:, None].astype(np.float32), (c, HGRN_DK)))
        blocks.append(np.ones((c, c), np.float32))
        w01.append(np.concatenate(blocks, axis=0))
        qm.append(np.stack(qd))
        km.append(np.stack(kd))
    sm = [((t[:, None] // (2 * hs)) == (t[None, :] // (2 * hs))).astype(np.float32) for hs in HGRN_LEVELS]
    sm.append(np.eye(c, dtype=np.float32))
    return (jnp.asarray(np.stack(w01), BF16), jnp.asarray(np.stack(qm), F32),
            jnp.asarray(np.stack(km), F32), jnp.asarray(np.stack(sm), F32))


def _hgrn_chunk(q, v, z, lbv, st, w01, qm_ref, km_ref, sm_ref, d):
    c = HGRN_CHUNK
    nl = len(HGRN_LEVELS)
    f = lbv + (1.0 - lbv) * _sigmoid(z)
    lf = jnp.log(jnp.maximum(f, F_FLOOR))
    k = 1.0 - f
    p1, p2, p3 = _split3(lf)
    allc = _dot(w01, p1) + (_dot(w01, p2) + _dot(w01, p3))
    cum = allc[0:c]
    last = allc[(nl + 1) * c:(nl + 2) * c]
    inter = _dot_nt((q * jnp.exp(cum)).astype(BF16), st.astype(BF16))
    scores = sm_ref[nl] * _dot_nt(q.astype(BF16), k.astype(BF16))
    for li in range(nl):
        ref = allc[(li + 1) * c:(li + 2) * c]
        e = jnp.exp2(jnp.abs(cum - ref) * (-LOG2E))
        qt = (q * qm_ref[d, li]) * e
        kt = (k * km_ref[d, li]) * e
        raw = _dot_nt(qt.astype(BF16), kt.astype(BF16))
        scores = scores + (raw if li == 0 else sm_ref[li] * raw)
    intra = _dot(scores.astype(BF16), v.astype(BF16))
    ks = k * jnp.exp(last - cum)
    st_new = jnp.exp(last[0:1, :]) * st + _dot_tn(v.astype(BF16), ks.astype(BF16))
    return inter + intra, st_new


def _hgrn_kernel(*refs, nc, has_s0, want_state):
    q_ref, v_ref, zf_ref, zb_ref, g_ref, lb_ref, ng_ref, w01_ref, qm_ref, km_ref, sm_ref = refs[:11]
    pos = 11
    if has_s0:
        s0_ref = refs[pos]
        pos += 1
    o_ref = refs[pos]
    pos += 1
    if want_state:
        sf_ref = refs[pos]
        pos += 1
    of_scr, st_scr = refs[pos:pos + 2]
    c = HGRN_CHUNK

    def run(d, z_ref, finalize):
        if has_s0:
            st_scr[...] = s0_ref[0, d, 0]
        else:
            st_scr[...] = jnp.zeros_like(st_scr)
        lbv = lb_ref[d, 0]
        w01 = w01_ref[d]

        def body(i, carry):
            ci = i if d == 0 else nc - 1 - i
            r = pl.multiple_of(ci * c, c)
            out, st = _hgrn_chunk(q_ref[0, pl.ds(r, c), :], v_ref[0, pl.ds(r, c), :], z_ref[0, pl.ds(r, c), :],
                                  lbv, st_scr[...], w01, qm_ref, km_ref, sm_ref, d)
            st_scr[...] = st
            if not finalize:
                of_scr[pl.ds(r, c), :] = out
            else:
                tot = of_scr[pl.ds(r, c), :] + out
                ms = jnp.mean(tot * tot, axis=-1, keepdims=True)
                y = tot * lax.rsqrt(ms + NORM_EPS) * ng_ref[...]
                o_ref[0, pl.ds(r, c), :] = (y * _silu(g_ref[0, pl.ds(r, c), :])).astype(BF16)
            return carry

        lax.fori_loop(0, nc, body, 0)
        if want_state:
            sf_ref[0, d, 0] = st_scr[...]

    run(0, zf_ref, False)
    run(1, zb_ref, True)


def _hgrn_branch(proj, lower_bound, norm_g, tables, s0_t, want_state):
    b, l, _ = proj.shape
    nc = l // HGRN_CHUNK
    w = HGRN_DK
    has_s0 = s0_t is not None
    w01, qm, km, sm = tables

    def col(off):
        return lambda bi, h: (bi, 0, off // w + h)

    in_specs = [pl.BlockSpec((1, l, w), col(COL_QD)),
                pl.BlockSpec((1, l, w), col(COL_ID)),
                pl.BlockSpec((1, l, w), col(COL_ZF)),
                pl.BlockSpec((1, l, w), col(COL_ZB)),
                pl.BlockSpec((1, l, w), col(COL_GD)),
                pl.BlockSpec((2, 1, 1, w), lambda bi, h: (0, h, 0, 0)),
                pl.BlockSpec((1, w), lambda bi, h: (0, 0)),
                pl.BlockSpec(w01.shape, lambda bi, h: (0, 0, 0)),
                pl.BlockSpec(qm.shape, lambda bi, h: (0, 0, 0, 0)),
                pl.BlockSpec(km.shape, lambda bi, h: (0, 0, 0, 0)),
                pl.BlockSpec(sm.shape, lambda bi, h: (0, 0, 0))]
    args = [proj, proj, proj, proj, proj, lower_bound.reshape(2, HGRN_HEADS, 1, w), norm_g.reshape(1, w),
            w01, qm, km, sm]
    st_spec = pl.BlockSpec((1, 2, 1, HGRN_DV, HGRN_DK), lambda bi, h: (bi, 0, h, 0, 0))
    if has_s0:
        in_specs.append(st_spec)
        args.append(s0_t)
    out_specs = [pl.BlockSpec((1, l, w), lambda bi, h: (bi, 0, h))]
    out_shape = [jax.ShapeDtypeStruct((b, l, HGRN_HEADS * HGRN_DV), BF16)]
    if want_state:
        out_specs.append(st_spec)
        out_shape.append(jax.ShapeDtypeStruct((b, 2, HGRN_HEADS, HGRN_DV, HGRN_DK), F32))
    res = pl.pallas_call(
        functools.partial(_hgrn_kernel, nc=nc, has_s0=has_s0, want_state=want_state),
        grid=(b, HGRN_HEADS),
        in_specs=in_specs, out_specs=out_specs, out_shape=out_shape,
        scratch_shapes=[pltpu.VMEM((l, HGRN_DV), F32), pltpu.VMEM((HGRN_DV, HGRN_DK), F32)],
        compiler_params=_cparams(2),
        name="hgrn",
    )(*args)
    return res if want_state else (res[0], None)


def _final_kernel(x_ref, gt_ref, ma_ref, mb_ref, mc_ref, md_ref, ya_ref, yb_ref, yc_ref, yd_ref,
                  wa_ref, wb_ref, wc_ref, wd_ref, wo_ref, gp_ref, o_ref, acc, *, nct):
    ci = pl.program_id(2)

    @pl.when(ci == 0)
    def _():
        acc[...] = jnp.zeros_like(acc)

    mixed = _sigmoid(ma_ref[0]) * _dot(ya_ref[0], wa_ref[...])
    mixed = mixed + _sigmoid(mb_ref[0]) * _dot(yb_ref[0], wb_ref[...])
    mixed = mixed + _sigmoid(mc_ref[0]) * _dot(yc_ref[0], wc_ref[...])
    mixed = mixed + _sigmoid(md_ref[0]) * _dot(yd_ref[0], wd_ref[...])
    acc[...] += _dot(mixed.astype(BF16), wo_ref[...])

    @pl.when(ci == nct - 1)
    def _():
        out = acc[...]
        ms = jnp.mean(out * out, axis=-1, keepdims=True)
        y = out * lax.rsqrt(ms + NORM_EPS) * gp_ref[...]
        o_ref[0] = x_ref[0] + gt_ref[0] * y


def _final(x, gate, proj, ys, w_projs, w_out, norm_post, tm, tc):
    b, l, d = x.shape
    nct = d // tc
    per_batch = gate.shape[0] > 1
    gate_map = (lambda bi, i, c: (bi, 0, 0)) if per_batch else (lambda bi, i, c: (0, 0, 0))

    def mspec(j):
        return pl.BlockSpec((1, tm, tc), lambda bi, i, c: (bi, i, (COL_M + j * d) // tc + c))

    in_specs = [pl.BlockSpec((1, tm, d), lambda bi, i, c: (bi, i, 0)),
                pl.BlockSpec((1, 1, d), gate_map)]
    in_specs += [mspec(j) for j in range(N_BRANCHES)]
    in_specs += [pl.BlockSpec((1, tm, y.shape[-1]), lambda bi, i, c: (bi, i, 0)) for y in ys]
    in_specs += [pl.BlockSpec((w.shape[0], tc), lambda bi, i, c: (0, c)) for w in w_projs]
    in_specs += [pl.BlockSpec((tc, d), lambda bi, i, c: (c, 0)),
                 pl.BlockSpec((1, d), lambda bi, i, c: (0, 0))]
    return pl.pallas_call(
        functools.partial(_final_kernel, nct=nct),
        grid=(b, l // tm, nct),
        in_specs=in_specs,
        out_specs=pl.BlockSpec((1, tm, d), lambda bi, i, c: (bi, i, 0)),
        out_shape=jax.ShapeDtypeStruct((b, l, d), F32),
        scratch_shapes=[pltpu.VMEM((tm, d), F32)],
        compiler_params=_cparams(3),
        name="merge_out",
    )(x, gate, proj, proj, proj, proj, *ys, *w_projs, w_out, norm_post.reshape(1, d))


def _trunk_layer(x, shift, scale, gate, p, ctx, layer, consts):
    b, l, d = x.shape
    shared_mod = shift.shape[0] == 1
    tm = 1024
    if shared_mod:
        proj = _inproj(x.reshape(1, b * l, d), shift, scale, p['norm_pre'], p['w_in'], tm, 1024)
        proj = proj.reshape(b, l, IN_COLS)
    else:
        proj = _inproj(x, shift, scale, p['norm_pre'], p['w_in'], tm, 1024)

    dl, cs = consts['dft'][l]
    y_a = _fourier_branch(proj, p['fourier_w'], dl, cs)

    if ctx is None:
        y_b, s5_re, s5_im = _s5_branch(proj, p['s5_mats'], p['s5_glu_w'], p['s5_glu_b'], None, None)
        qs, kr, vb, kf = _qk_prep(proj, p['q_norm'], p['k_norm'], None, True, min(l, 512))
        y_c = _attention(qs, kr, vb, proj, None, None, layer, min(l, 256))
        y_d, hg = _hgrn_branch(proj, p['lower_bound'], p['hgrn_norm'], consts['hgrn'], None, True)
    else:
        cache_k, cache_v, st_re, st_im, s0_t = ctx
        y_b, _, _ = _s5_branch(proj, p['s5_mats'], p['s5_glu_w'], p['s5_glu_b'], st_re[:, layer], st_im[:, layer])
        qs, kr, vb = _qk_prep(proj, p['q_norm'], p['k_norm'], consts['rope'], False, min(l, 512))
        y_c = _attention(qs, kr, vb, proj, cache_k, cache_v, layer, min(l, 256))
        y_d, _ = _hgrn_branch(proj, p['lower_bound'], p['hgrn_norm'], consts['hgrn'], s0_t[:, layer], False)

    ys = (y_a, y_b, y_c, y_d)
    wps = (p['w_proj_a'], p['w_proj_b'], p['w_proj_c'], p['w_proj_d'])
    if shared_mod:
        n = b * l
        x_new = _final(x.reshape(1, n, d), gate, proj.reshape(1, n, IN_COLS),
                       tuple(y.reshape(1, n, y.shape[-1]) for y in ys), wps, p['w_out'], p['norm_post'], 512, 512)
        x_new = x_new.reshape(b, l, d)
    else:
        x_new = _final(x, gate, proj, ys, wps, p['w_out'], p['norm_post'], 512, 512)

    if ctx is None:
        v_f = proj[:, :, COL_VC:COL_VC + KV_WIDTH]
        return x_new, (kf, v_f, s5_re, s5_im, hg)
    return x_new, None


def kernel(x_prompt, x_sample, c, cache_k, cache_v, state_s5_re, state_s5_im, state_hgrn, c_ctx,
           norm_pre, norm_post, w_mod, b_mod, w_in, fourier_w,
           s5_lambda_re, s5_lambda_im, s5_log_step, s5_b_re, s5_b_im, s5_c_re, s5_c_im,
           s5_d, s5_glu_w, s5_glu_b, q_norm, k_norm, hgrn_lb_logits, hgrn_norm,
           w_proj_a, w_proj_b, w_proj_c, w_proj_d, w_out):
    depth = w_in.shape[0]
    bp, lp, d = x_prompt.shape
    bs, ls, _ = x_sample.shape

    lb_w = jax.nn.softmax(hgrn_lb_logits.astype(F32), axis=0)
    lower_bounds = jnp.cumsum(lb_w, axis=0) - lb_w[0]

    rows = ((bs + 1 + 7) // 8) * 8
    c_all = jnp.zeros((rows, d), F32).at[:bs].set(c).at[bs].set(c_ctx)
    mod = _modulation(c_all, w_mod, b_mod)

    w_in_bf = w_in.astype(BF16)
    wpa, wpb, wpc, wpd = (w.astype(BF16) for w in (w_proj_a, w_proj_b, w_proj_c, w_proj_d))
    w_out_bf = w_out.astype(BF16)

    consts = {'dft': {lp: _dft_tables(lp), ls: _dft_tables(ls)},
              'rope': _rope_tables(ls),
              'hgrn': _hgrn_tables()}
    past = cache_k.shape[2]
    ctx_k = cache_k.reshape(bs, depth, past, KV_WIDTH)
    ctx_v = cache_v.reshape(bs, depth, past, KV_WIDTH)
    s0_t = jnp.swapaxes(state_hgrn.astype(F32), -1, -2)

    y_p, y_s = x_prompt, x_sample
    ks, vs, s5r, s5i, hg = [], [], [], [], []
    for l in range(depth):
        p = {'norm_pre': norm_pre[l].reshape(1, d), 'norm_post': norm_post[l], 'w_in': w_in_bf[l],
             'fourier_w': fourier_w[l],
             's5_mats': _s5_matrices(s5_lambda_re[l], s5_lambda_im[l], s5_log_step[l], s5_b_re[l], s5_b_im[l],
                                     s5_c_re[l], s5_c_im[l], s5_d[l].reshape(S5_GROUPS, S5_GROUP_DIM)),
             's5_glu_w': s5_glu_w[l], 's5_glu_b': s5_glu_b[l],
             'q_norm': q_norm[l], 'k_norm': k_norm[l],
             'lower_bound': lower_bounds[l], 'hgrn_norm': hgrn_norm[l],
             'w_proj_a': wpa[l], 'w_proj_b': wpb[l], 'w_proj_c': wpc[l], 'w_proj_d': wpd[l],
             'w_out': w_out_bf[l]}
        m_ctx = mod[l, bs].reshape(1, 1, 3 * d)
        sh, sc, gt = (m_ctx[..., i * d:(i + 1) * d] for i in range(3))
        y_p, (k_l, v_l, sr_l, si_l, hg_l) = _trunk_layer(y_p, sh, sc, gt, p, None, l, consts)
        ks.append(k_l.reshape(bp, lp, N_KV_HEADS, HEAD_DIM))
        vs.append(v_l.reshape(bp, lp, N_KV_HEADS, HEAD_DIM))
        s5r.append(sr_l)
        s5i.append(si_l)
        hg.append(jnp.swapaxes(hg_l, -1, -2))
        m_s = mod[l, :bs].reshape(bs, 1, 3 * d)
        sh, sc, gt = (m_s[..., i * d:(i + 1) * d] for i in range(3))
        ctx = (ctx_k, ctx_v, state_s5_re, state_s5_im, s0_t)
        y_s, _ = _trunk_layer(y_s, sh, sc, gt, p, ctx, l, consts)

    return (y_p, y_s, jnp.stack(ks, axis=1), jnp.stack(vs, axis=1),
            jnp.stack(s5r, axis=1), jnp.stack(s5i, axis=1), jnp.stack(hg, axis=1))
```

```python
import functools
import math

import jax
import jax.numpy as jnp
import numpy as np
from jax import lax
from jax.experimental import pallas as pl
from jax.experimental.pallas import tpu as pltpu

F32 = jnp.float32
BF16 = jnp.bfloat16

NORM_EPS = 1e-6
F_FLOOR = 1e-30
LOG2E = 1.4426950408889634

D_MODEL = 2048
N_BRANCHES = 4
FOURIER_WIDTH = 512
FOURIER_GROUPS = 4
FOURIER_GROUP_DIM = 128
S5_WIDTH = 512
S5_GROUP_DIM = 16
S5_GROUPS = 32
S5_STATE = 64
S5_CHUNK = 8
S5_GB = 8
S5_NBLK = S5_GROUPS // S5_GB
S5_K = S5_CHUNK * S5_GB * S5_GROUP_DIM
S5_PART = S5_GB * S5_STATE
N_HEADS = 8
N_KV_HEADS = 2
HEAD_DIM = 128
Q_PER_KV = 4
ATTN_WIDTH = 1024
KV_WIDTH = 256
ROPE_THETA = 10000.0
GRID_W = 64
HGRN_HEADS = 4
HGRN_DK = 128
HGRN_DV = 128
HGRN_WIDTH = 512
HGRN_CHUNK = 64

COL_UA, COL_GA, COL_UB, COL_GB = 0, 512, 1024, 1536
COL_QC, COL_KC, COL_VC, COL_GC = 2048, 3072, 3328, 3584
COL_QD, COL_ID, COL_ZF, COL_ZB, COL_GD = 4608, 5120, 5632, 6144, 6656
COL_M = 7168
IN_COLS = 15360

FINAL_ROWS = 256
VMEM_LIMIT = 56 * 1024 * 1024


def _cparams(n_grid):
    return pltpu.CompilerParams(dimension_semantics=("arbitrary",) * n_grid,
                                vmem_limit_bytes=VMEM_LIMIT)


def _sigmoid(x):
    return 1.0 / (1.0 + jnp.exp(-x))


def _gate_sigmoid(x):
    return 0.5 * jnp.tanh(0.5 * x) + 0.5


def _silu(x):
    return x * _gate_sigmoid(x)


def _dot(a, b):
    return jnp.dot(a, b, preferred_element_type=F32)


def _dot_nt(a, b):
    return lax.dot_general(a, b, (((1,), (1,)), ((), ())), preferred_element_type=F32)


def _dot_tn(a, b):
    return lax.dot_general(a, b, (((0,), (0,)), ((), ())), preferred_element_type=F32)


def _split2(x):
    hi = x.astype(BF16)
    lo = (x - hi.astype(F32)).astype(BF16)
    return hi, lo


def _split3(x):
    hi = x.astype(BF16)
    r = x - hi.astype(F32)
    mid = r.astype(BF16)
    lo = (r - mid.astype(F32)).astype(BF16)
    return hi, mid, lo


def _dot3(a, b):
    ah, al = _split2(a)
    bh, bl = _split2(b)
    return _dot(ah, bh) + (_dot(ah, bl) + _dot(al, bh))


def _mod_kernel(c_ref, w_ref, b_ref, o_ref):
    a = _silu(c_ref[...])
    o_ref[0] = _dot3(a, w_ref[0]) + b_ref[0]


def _modulation(c_all, w_mod, b_mod):
    depth, d, n = w_mod.shape
    rows = c_all.shape[0]
    tn = 1024
    return pl.pallas_call(
        _mod_kernel,
        grid=(depth, n // tn),
        in_specs=[pl.BlockSpec((rows, d), lambda l, j: (0, 0)),
                  pl.BlockSpec((1, d, tn), lambda l, j: (l, 0, j)),
                  pl.BlockSpec((1, 1, tn), lambda l, j: (l, 0, j))],
        out_specs=pl.BlockSpec((1, rows, tn), lambda l, j: (l, 0, j)),
        out_shape=jax.ShapeDtypeStruct((depth, rows, n), F32),
        compiler_params=_cparams(2),
        name="modulation",
    )(c_all, w_mod, b_mod.reshape(depth, 1, n))


def _inproj_kernel(x_ref, sh_ref, sc_ref, g_ref, w_ref, o_ref, h_scr):
    @pl.when(pl.program_id(2) == 0)
    def _():
        x = x_ref[0]
        ms = jnp.mean(x * x, axis=-1, keepdims=True)
        y = x * lax.rsqrt(ms + NORM_EPS) * g_ref[...]
        h_scr[...] = (y * (1.0 + sc_ref[0]) + sh_ref[0]).astype(BF16)

    o_ref[0] = _dot(h_scr[...], w_ref[...])


def _inproj(x, shift, scale, g, w_bf, tm, tn):
    b, l, d = x.shape
    n = w_bf.shape[1]
    per_batch = shift.shape[0] > 1
    mod_map = (lambda bi, i, j: (bi, 0, 0)) if per_batch else (lambda bi, i, j: (0, 0, 0))
    return pl.pallas_call(
        _inproj_kernel,
        grid=(b, l // tm, n // tn),
        in_specs=[pl.BlockSpec((1, tm, d), lambda bi, i, j: (bi, i, 0)),
                  pl.BlockSpec((1, 1, d), mod_map),
                  pl.BlockSpec((1, 1, d), mod_map),
                  pl.BlockSpec((1, d), lambda bi, i, j: (0, 0)),
                  pl.BlockSpec((d, tn), lambda bi, i, j: (0, j))],
        out_specs=pl.BlockSpec((1, tm, tn), lambda bi, i, j: (bi, i, j)),
        out_shape=jax.ShapeDtypeStruct((b, l, n), F32),
        scratch_shapes=[pltpu.VMEM((tm, d), BF16)],
        compiler_params=_cparams(3),
        name="inproj",
    )(x, shift, scale, g, w_bf)


def _dft_tables(l):
    scale = 1.0 / math.sqrt(l * FOURIER_GROUP_DIM)
    k = np.arange(l, dtype=np.int64)
    t1 = np.arange(l // 64, dtype=np.int64)[:, None]
    t2 = np.arange(64, dtype=np.int64)[:, None]
    a = 2.0 * np.pi * ((t1 * 64 * k[None, :]) % l).astype(np.float64) / l
    bb = 2.0 * np.pi * ((t2 * k[None, :]) % l).astype(np.float64) / l
    ca, sa = jnp.asarray(np.cos(a), F32), jnp.asarray(np.sin(a), F32)
    cb, sb = jnp.asarray(np.cos(bb), F32), jnp.asarray(np.sin(bb), F32)
    cl = (ca[:, None, :] * cb[None, :, :] - sa[:, None, :] * sb[None, :, :]).reshape(l, l)
    sl = (sa[:, None, :] * cb[None, :, :] + ca[:, None, :] * sb[None, :, :]).reshape(l, l)
    dl = (jnp.concatenate([cl, -sl], axis=1) * scale).astype(BF16)
    c = np.arange(FOURIER_GROUP_DIM, dtype=np.int64)
    ang = 2.0 * np.pi * ((c[:, None] * c[None, :]) % FOURIER_GROUP_DIM) / FOURIER_GROUP_DIM
    cs = jnp.asarray(np.concatenate([np.cos(ang), np.sin(ang)], axis=1), F32).astype(BF16)
    return dl, cs


def _fourier_kernel(u_ref, g_ref, cs_ref, dl_ref, w_ref, o_ref, ucs_scr, *, l):
    gd = FOURIER_GROUP_DIM

    @pl.when(pl.program_id(1) == 0)
    def _():
        for g in range(FOURIER_GROUPS):
            ug = u_ref[0, :, g * gd:(g + 1) * gd].astype(BF16)
            t = _dot(ug, cs_ref[...])
            ucs_scr[0:l, g * gd:(g + 1) * gd] = t[:, :gd].astype(BF16)
            ucs_scr[l:2 * l, g * gd:(g + 1) * gd] = t[:, gd:].astype(BF16)

    f = _dot(dl_ref[...], ucs_scr[...])
    gate = _silu(g_ref[0])
    for g in range(FOURIER_GROUPS):
        y = _dot(f[:, g * gd:(g + 1) * gd].astype(BF16), w_ref[g].astype(BF16))
        o_ref[0, :, g * gd:(g + 1) * gd] = (y * gate[:, g * gd:(g + 1) * gd]).astype(BF16)


def _fourier_branch(proj, fourier_w, dl, cs):
    b, l, _ = proj.shape
    tr = min(l, 512)
    w = FOURIER_WIDTH
    return pl.pallas_call(
        functools.partial(_fourier_kernel, l=l),
        grid=(b, l // tr),
        in_specs=[pl.BlockSpec((1, l, w), lambda bi, r: (bi, 0, COL_UA // w)),
                  pl.BlockSpec((1, tr, w), lambda bi, r: (bi, r, COL_GA // w)),
                  pl.BlockSpec((FOURIER_GROUP_DIM, 2 * FOURIER_GROUP_DIM), lambda bi, r: (0, 0)),
                  pl.BlockSpec((tr, 2 * l), lambda bi, r: (r, 0)),
                  pl.BlockSpec((FOURIER_GROUPS, FOURIER_GROUP_DIM, FOURIER_GROUP_DIM),
                               lambda bi, r: (0, 0, 0))],
        out_specs=pl.BlockSpec((1, tr, w), lambda bi, r: (bi, r, 0)),
        out_shape=jax.ShapeDtypeStruct((b, l, w), BF16),
        scratch_shapes=[pltpu.VMEM((2 * l, w), BF16)],
        compiler_params=_cparams(2),
        name="fourier",
    )(proj, proj, cs, dl, fourier_w)


def _s5_matrices(lam_re, lam_im, log_step, b_re, b_im, c_re, c_im, dskip):
    t = S5_CHUNK
    g, n, p = S5_GROUPS, S5_STATE, S5_GROUP_DIM
    hp = lax.Precision.HIGHEST
    lam_re = lam_re.astype(F32)
    lam_im = lam_im.astype(F32)
    step = jnp.exp(log_step.astype(F32))[..., None]
    rho = lam_re * step
    th = lam_im * step
    mag = jnp.exp(rho)
    lb_re = mag * jnp.cos(th)
    lb_im = mag * jnp.sin(th)
    nr = lb_re - 1.0
    den = lam_re * lam_re + lam_im * lam_im
    fr = (nr * lam_re + lb_im * lam_im) / den
    fi = (lb_im * lam_re - nr * lam_im) / den
    b_re = b_re.astype(F32)
    b_im = b_im.astype(F32)
    bb_re = fr[..., None] * b_re - fi[..., None] * b_im
    bb_im = fr[..., None] * b_im + fi[..., None] * b_re
    tau = jnp.arange(t + 1, dtype=F32)
    pw_mag = jnp.exp(rho[..., None] * tau)
    pw_re = pw_mag * jnp.cos(th[..., None] * tau)
    pw_im = pw_mag * jnp.sin(th[..., None] * tau)
    c_re = c_re.astype(F32)
    c_im = c_im.astype(F32)
    ca_re = c_re[..., None] * pw_re[:, :, None] - c_im[..., None] * pw_im[:, :, None]
    ca_im = c_re[..., None] * pw_im[:, :, None] + c_im[..., None] * pw_re[:, :, None]
    kern = (jnp.einsum('dgpnt,dgnq->dgtpq', ca_re, bb_re, precision=hp)
            - jnp.einsum('dgpnt,dgnq->dgtpq', ca_im, bb_im, precision=hp))[:, :, :t]
    sig = np.arange(t)[:, None]
    ta = np.arange(t)[None, :]
    idx_f = np.clip(ta - sig, 0, t - 1)
    idx_b = np.clip(sig - ta, 0, t - 1)
    m_f = jnp.asarray((ta >= sig).astype(np.float32))[None, :, :, None, None]
    m_b = jnp.asarray((sig >= ta).astype(np.float32))[None, :, :, None, None]
    tf = kern[0][:, idx_f] * m_f
    tb = kern[1][:, idx_b] * m_b
    eye_t = jnp.asarray(np.eye(t, dtype=np.float32))[None, :, :, None, None]
    dmat = dskip.astype(F32)[:, None, None, :, None] * jnp.asarray(np.eye(p, dtype=np.float32))[None, None, None]
    tt = tf + tb + eye_t * dmat
    nb, gb = S5_NBLK, S5_GB
    eye_g = jnp.asarray(np.eye(gb, dtype=np.float32))
    tmat = jnp.einsum('jastpq,ab->jsaqtbp', tt.reshape(nb, gb, t, t, p, p), eye_g).reshape(nb, S5_K, S5_K)

    def ba(d, powers):
        pr = pw_re[d][:, :, powers]
        pi = pw_im[d][:, :, powers]
        re = pr[..., None] * bb_re[d][:, :, None, :] - pi[..., None] * bb_im[d][:, :, None, :]
        im = pr[..., None] * bb_im[d][:, :, None, :] + pi[..., None] * bb_re[d][:, :, None, :]
        blk = lambda m: jnp.einsum('jantq,ab->jtaqbn', m.reshape(nb, gb, n, t, p), eye_g).reshape(nb, S5_K, S5_PART)
        return blk(re), blk(im)

    baf_re, baf_im = ba(0, np.arange(t - 1, -1, -1))
    bab_re, bab_im = ba(1, np.arange(t))
    bamat = jnp.concatenate([baf_re, bab_re, baf_im, bab_im], axis=-1)

    def ca(d, powers):
        blk = lambda m: jnp.einsum('japnt,ab->jbntap', m[..., powers].reshape(nb, gb, p, n, t),
                                   eye_g).reshape(nb, S5_PART, S5_K)
        return blk(ca_re[d]), blk(ca_im[d])

    caf_re, caf_im = ca(0, np.arange(1, t + 1))
    cab_re, cab_im = ca(1, np.arange(t, 0, -1))
    camat = jnp.concatenate([caf_re, cab_re, -caf_im, -cab_im], axis=1)
    part = lambda m: m[..., t].reshape(nb, S5_PART)
    a_pow = jnp.stack([jnp.concatenate([part(pw_re[0]), part(pw_re[1])], axis=-1),
                       jnp.concatenate([part(pw_im[0]), part(pw_im[1])], axis=-1)], axis=1)
    return tmat.astype(BF16), bamat.astype(BF16), camat.astype(BF16), a_pow


def _s5_kernel(u_ref, t_ref, ba_ref, ca_ref, a_ref, h0_ref, y_ref, hf_ref, ub_scr, s_scr, hin_scr, *, nc):
    t, w, hp = S5_CHUNK, S5_GB * S5_GROUP_DIM, S5_PART
    for tau in range(t):
        ub_scr[:, tau * w:(tau + 1) * w] = u_ref[0, pl.ds(tau, nc, stride=t), :].astype(BF16)
    s_scr[...] = _dot(ub_scr[...], ba_ref[0])
    ar_f, ar_b = a_ref[0, 0:1, 0:hp], a_ref[0, 0:1, hp:2 * hp]
    ai_f, ai_b = a_ref[0, 1:2, 0:hp], a_ref[0, 1:2, hp:2 * hp]

    def step(i, carry):
        re_f, im_f, re_b, im_b = carry
        cb = nc - 1 - i
        hin_scr[pl.ds(i, 1), 0:hp] = re_f
        hin_scr[pl.ds(i, 1), 2 * hp:3 * hp] = im_f
        hin_scr[pl.ds(cb, 1), hp:2 * hp] = re_b
        hin_scr[pl.ds(cb, 1), 3 * hp:4 * hp] = im_b
        n_re_f = ar_f * re_f - ai_f * im_f + s_scr[pl.ds(i, 1), 0:hp]
        n_im_f = ar_f * im_f + ai_f * re_f + s_scr[pl.ds(i, 1), 2 * hp:3 * hp]
        n_re_b = ar_b * re_b - ai_b * im_b + s_scr[pl.ds(cb, 1), hp:2 * hp]
        n_im_b = ar_b * im_b + ai_b * re_b + s_scr[pl.ds(cb, 1), 3 * hp:4 * hp]
        return n_re_f, n_im_f, n_re_b, n_im_b

    init = (h0_ref[0, 0, :, 0:hp], h0_ref[0, 0, :, 2 * hp:3 * hp],
            h0_ref[0, 0, :, hp:2 * hp], h0_ref[0, 0, :, 3 * hp:4 * hp])
    fin = lax.fori_loop(0, nc, step, init, unroll=8)
    hf_ref[0, 0, :, 0:hp] = fin[0]
    hf_ref[0, 0, :, 2 * hp:3 * hp] = fin[1]
    hf_ref[0, 0, :, hp:2 * hp] = fin[2]
    hf_ref[0, 0, :, 3 * hp:4 * hp] = fin[3]
    y = _dot(ub_scr[...], t_ref[0]) + _dot(hin_scr[...].astype(BF16), ca_ref[0])
    for tau in range(t):
        y_ref[0, pl.ds(tau, nc, stride=t), :] = y[:, tau * w:(tau + 1) * w]


def _s5_core(proj, tmat, bamat, camat, a_pow, h0):
    b, l, _ = proj.shape
    nc = l // S5_CHUNK
    w = S5_GB * S5_GROUP_DIM
    st_spec = pl.BlockSpec((1, 1, 1, 4 * S5_PART), lambda j, bi: (bi, j, 0, 0))
    return pl.pallas_call(
        functools.partial(_s5_kernel, nc=nc),
        grid=(S5_NBLK, b),
        in_specs=[pl.BlockSpec((1, l, w), lambda j, bi: (bi, 0, COL_UB // w + j)),
                  pl.BlockSpec((1, S5_K, S5_K), lambda j, bi: (j, 0, 0)),
                  pl.BlockSpec((1, S5_K, 4 * S5_PART), lambda j, bi: (j, 0, 0)),
                  pl.BlockSpec((1, 4 * S5_PART, S5_K), lambda j, bi: (j, 0, 0)),
                  pl.BlockSpec((1, 2, 2 * S5_PART), lambda j, bi: (j, 0, 0)),
                  st_spec],
        out_specs=[pl.BlockSpec((1, l, w), lambda j, bi: (bi, 0, j)), st_spec],
        out_shape=[jax.ShapeDtypeStruct((b, l, S5_WIDTH), F32),
                   jax.ShapeDtypeStruct((b, S5_NBLK, 1, 4 * S5_PART), F32)],
        scratch_shapes=[pltpu.VMEM((nc, S5_K), BF16), pltpu.VMEM((nc, 4 * S5_PART), F32),
                        pltpu.VMEM((nc, 4 * S5_PART), F32)],
        compiler_params=_cparams(2),
        name="s5_core",
    )(proj, tmat, bamat, camat, a_pow, h0)


def _s5_glu_kernel(y_ref, g_ref, w_ref, b_ref, o_ref):
    y = y_ref[0]
    y = 0.5 * y * (1.0 + jnp.tanh(math.sqrt(2.0 / math.pi) * (y + 0.044715 * (y * y * y))))
    z = _dot(y.astype(BF16), w_ref[...].astype(BF16)) + b_ref[...]
    o_ref[0] = (y * _sigmoid(z) * _silu(g_ref[0])).astype(BF16)


def _s5_glu(y, proj, glu_w, glu_b, tm):
    b, l, w = y.shape
    return pl.pallas_call(
        _s5_glu_kernel,
        grid=(b, l // tm),
        in_specs=[pl.BlockSpec((1, tm, w), lambda bi, i: (bi, i, 0)),
                  pl.BlockSpec((1, tm, w), lambda bi, i: (bi, i, COL_GB // w)),
                  pl.BlockSpec((w, w), lambda bi, i: (0, 0)),
                  pl.BlockSpec((1, w), lambda bi, i: (0, 0))],
        out_specs=pl.BlockSpec((1, tm, w), lambda bi, i: (bi, i, 0)),
        out_shape=jax.ShapeDtypeStruct((b, l, w), BF16),
        compiler_params=_cparams(2),
        name="s5_glu",
    )(y, proj, glu_w, glu_b.reshape(1, w))


def _s5_branch(proj, mats, glu_w, glu_b, h0_re, h0_im):
    b, l, _ = proj.shape
    g, n = S5_GROUPS, S5_STATE
    if h0_re is None:
        h0 = jnp.zeros((b, S5_NBLK, 1, 4 * S5_PART), F32)
    else:
        blk = lambda h: h.astype(F32).reshape(b, S5_NBLK, 1, S5_PART)
        h0 = jnp.concatenate([blk(h0_re[:, 0]), blk(h0_re[:, 1]), blk(h0_im[:, 0]), blk(h0_im[:, 1])], axis=-1)
    y, hf = _s5_core(proj, *mats, h0)
    out = _s5_glu(y, proj, glu_w, glu_b, min(l, 1024))
    part = lambda i: hf[..., i * S5_PART:(i + 1) * S5_PART].reshape(b, g, n)
    fin_re = jnp.stack([part(0), part(1)], axis=1)
    fin_im = jnp.stack([part(2), part(3)], axis=1)
    return out, fin_re, fin_im


def _rope_tables(l):
    rows = l // GRID_W
    row = jnp.broadcast_to(jnp.arange(rows, dtype=F32)[:, None], (rows, GRID_W)).reshape(-1)
    col = jnp.broadcast_to(jnp.arange(GRID_W, dtype=F32)[None, :], (rows, GRID_W)).reshape(-1)
    half = HEAD_DIM // 2
    inv = ROPE_THETA ** (-jnp.arange(0, half, 2, dtype=F32) / half)
    ar = row[:, None] * inv
    ac = col[:, None] * inv
    cos = jnp.concatenate([jnp.cos(ar), jnp.cos(ar), jnp.cos(ac), jnp.cos(ac)], axis=1)
    sin = jnp.concatenate([-jnp.sin(ar), jnp.sin(ar), -jnp.sin(ac), jnp.sin(ac)], axis=1)
    return cos, sin


def _qk_kernel(*refs, rope, want_kf):
    q_ref, k_ref, v_ref, qn_ref, kn_ref = refs[:5]
    pos = 5
    if rope:
        cos_ref, sin_ref = refs[pos:pos + 2]
        pos += 2
    qo_ref, ko_ref, vo_ref = refs[pos:pos + 3]
    pos += 3
    if want_kf:
        kf_ref = refs[pos]
    hd = HEAD_DIM
    tm = q_ref.shape[1]
    quarter = hd // 4
    if rope:
        cos = cos_ref[...]
        sin = sin_ref[...]
        first = (lax.broadcasted_iota(jnp.int32, (tm, hd), 1) % (2 * quarter)) < quarter

    def norm(x, gain):
        ms = jnp.mean(x * x, axis=-1, keepdims=True)
        return x * lax.rsqrt(ms + NORM_EPS) * gain

    def rot(x):
        if not rope:
            return x
        nxt = pltpu.roll(x, hd - quarter, 1)
        prv = pltpu.roll(x, quarter, 1)
        return x * cos + jnp.where(first, nxt, prv) * sin

    qscale = (hd ** -0.5) * LOG2E
    for h in range(N_HEADS):
        x = norm(q_ref[0, :, h * hd:(h + 1) * hd], qn_ref[...])
        qo_ref[0, :, h * hd:(h + 1) * hd] = (rot(x) * qscale).astype(BF16)
    for h in range(N_KV_HEADS):
        x = norm(k_ref[0, :, h * hd:(h + 1) * hd], kn_ref[...])
        if want_kf:
            kf_ref[0, :, h * hd:(h + 1) * hd] = x
        ko_ref[0, :, h * hd:(h + 1) * hd] = rot(x).astype(BF16)
    vo_ref[0] = v_ref[0].astype(BF16)


def _qk_prep(proj, q_norm, k_norm, rope_tabs, want_kf, tm):
    b, l, _ = proj.shape
    rope = rope_tabs is not None
    in_specs = [pl.BlockSpec((1, tm, ATTN_WIDTH), lambda bi, i: (bi, i, COL_QC // ATTN_WIDTH)),
                pl.BlockSpec((1, tm, KV_WIDTH), lambda bi, i: (bi, i, COL_KC // KV_WIDTH)),
                pl.BlockSpec((1, tm, KV_WIDTH), lambda bi, i: (bi, i, COL_VC // KV_WIDTH)),
                pl.BlockSpec((1, HEAD_DIM), lambda bi, i: (0, 0)),
                pl.BlockSpec((1, HEAD_DIM), lambda bi, i: (0, 0))]
    args = [proj, proj, proj, q_norm.reshape(1, HEAD_DIM), k_norm.reshape(1, HEAD_DIM)]
    if rope:
        in_specs += [pl.BlockSpec((tm, HEAD_DIM), lambda bi, i: (i, 0))] * 2
        args += list(rope_tabs)
    out_specs = [pl.BlockSpec((1, tm, ATTN_WIDTH), lambda bi, i: (bi, i, 0)),
                 pl.BlockSpec((1, tm, KV_WIDTH), lambda bi, i: (bi, i, 0)),
                 pl.BlockSpec((1, tm, KV_WIDTH), lambda bi, i: (bi, i, 0))]
    out_shape = [jax.ShapeDtypeStruct((b, l, ATTN_WIDTH), BF16),
                 jax.ShapeDtypeStruct((b, l, KV_WIDTH), BF16),
                 jax.ShapeDtypeStruct((b, l, KV_WIDTH), BF16)]
    if want_kf:
        out_specs.append(pl.BlockSpec((1, tm, KV_WIDTH), lambda bi, i: (bi, i, 0)))
        out_shape.append(jax.ShapeDtypeStruct((b, l, KV_WIDTH), F32))
    return pl.pallas_call(
        functools.partial(_qk_kernel, rope=rope, want_kf=want_kf),
        grid=(b, l // tm),
        in_specs=in_specs, out_specs=out_specs, out_shape=out_shape,
        compiler_params=_cparams(2),
        name="qk_prep",
    )(*args)


def _attn_kernel(*refs, has_ctx):
    if has_ctx:
        q_ref, k_ref, v_ref, ck_ref, cv_ref, g_ref, o_ref = refs
    else:
        q_ref, k_ref, v_ref, g_ref, o_ref = refs
    hd = HEAD_DIM
    kb = k_ref[0]
    vb = v_ref[0]
    if has_ctx:
        ck = ck_ref[0, 0].astype(BF16)
        cv = cv_ref[0, 0].astype(BF16)
    gate = _silu(g_ref[0])
    for g in range(Q_PER_KV):
        q = q_ref[0, :, g * hd:(g + 1) * hd]
        s1 = _dot_nt(q, kb)
        m = jnp.max(s1, axis=-1, keepdims=True)
        if has_ctx:
            s2 = _dot_nt(q, ck)
            m = jnp.maximum(m, jnp.max(s2, axis=-1, keepdims=True))
        p1 = jnp.exp2(s1 - m)
        den = jnp.sum(p1, axis=-1, keepdims=True)
        o = _dot(p1.astype(BF16), vb)
        if has_ctx:
            p2 = jnp.exp2(s2 - m)
            den = den + jnp.sum(p2, axis=-1, keepdims=True)
            o = o + _dot(p2.astype(BF16), cv)
        o = o / den
        o_ref[0, :, g * hd:(g + 1) * hd] = (o * gate[:, g * hd:(g + 1) * hd]).astype(BF16)


def _attention(qs, kr, vb, proj, ctx_k, ctx_v, layer, tq):
    b, l, _ = qs.shape
    gw = Q_PER_KV * HEAD_DIM
    has_ctx = ctx_k is not None
    in_specs = [pl.BlockSpec((1, tq, gw), lambda bi, h, i: (bi, i, h)),
                pl.BlockSpec((1, l, HEAD_DIM), lambda bi, h, i: (bi, 0, h)),
                pl.BlockSpec((1, l, HEAD_DIM), lambda bi, h, i: (bi, 0, h))]
    args = [qs, kr, vb]
    if has_ctx:
        past = ctx_k.shape[2]
        ctx_spec = pl.BlockSpec((1, 1, past, HEAD_DIM), lambda bi, h, i: (bi, layer, 0, h))
        in_specs += [ctx_spec, ctx_spec]
        args += [ctx_k, ctx_v]
    in_specs.append(pl.BlockSpec((1, tq, gw), lambda bi, h, i: (bi, i, COL_GC // gw + h)))
    args.append(proj)
    return pl.pallas_call(
        functools.partial(_attn_kernel, has_ctx=has_ctx),
        grid=(b, N_KV_HEADS, l // tq),
        in_specs=in_specs,
        out_specs=pl.BlockSpec((1, tq, gw), lambda bi, h, i: (bi, i, h)),
        out_shape=jax.ShapeDtypeStruct((b, l, ATTN_WIDTH), BF16),
        compiler_params=_cparams(3),
        name="attention",
    )(*args)


HGRN_COARSE = (32, 16, 8, 4)
HGRN_FINE = (2, 1)
HGRN_LEVELS = HGRN_COARSE + HGRN_FINE


def _hgrn_tables():
    c = HGRN_CHUNK
    t = np.arange(c)
    w01, sm = [], []
    for d in range(2):
        cum = (t[None, :] <= t[:, None]) if d == 0 else (t[None, :] >= t[:, None])
        cum = cum.astype(np.float32)
        w01.append(np.concatenate([cum] + [cum[_hgrn_ref_row(t, hs, d)] for hs in HGRN_FINE], axis=0))
        sd = []
        for hs in HGRN_LEVELS:
            is_q = _hgrn_is_query(t, hs, d)
            same = (t[:, None] // (2 * hs)) == (t[None, :] // (2 * hs))
            sd.append((same & is_q[:, None] & ~is_q[None, :]).astype(np.float32))
        sd.append(np.eye(c, dtype=np.float32))
        sm.append(np.stack(sd))
    return jnp.asarray(np.stack(w01), BF16), jnp.asarray(np.stack(sm), F32)


def _hgrn_ref_row(t, hs, d):
    base = t & ~(2 * hs - 1)
    return base + hs - 1 if d == 0 else base + hs


def _hgrn_is_query(t, hs, d):
    return ((t & hs) != 0) if d == 0 else ((t & hs) == 0)


def _hgrn_cums(z, lbv, w01):
    f = lbv + (1.0 - lbv) * _sigmoid(z)
    p1, p2, p3 = _split3(jnp.log2(jnp.maximum(f, F_FLOOR)))
    return 1.0 - f, _dot(w01, p1) + (_dot(w01, p2) + _dot(w01, p3))


def _hgrn_products(q, k, allc, rows, d):
    c = HGRN_CHUNK
    sub = 8
    cum = allc[0:c]
    refs = {}
    for hs in HGRN_COARSE:
        pieces = []
        for j in range(c // sub):
            rr = int(_hgrn_ref_row(np.int64(j * sub), hs, d))
            pieces.append(jnp.broadcast_to(cum[rr:rr + 1, :], (sub, HGRN_DK)))
        refs[hs] = jnp.concatenate(pieces, axis=0)
    for i, hs in enumerate(HGRN_FINE):
        refs[hs] = allc[(i + 1) * c:(i + 2) * c]
    lrow = c - 1 if d == 0 else 0
    last = cum[lrow:lrow + 1, :]
    raws = [_dot_nt(q.astype(BF16), k.astype(BF16))]
    for hs in HGRN_LEVELS:
        is_q = ((rows & hs) != 0) if d == 0 else ((rows & hs) == 0)
        x = (jnp.where(is_q, q, k) * jnp.exp2((cum - refs[hs]) * jnp.where(is_q, 1.0, -1.0))).astype(BF16)
        raws.append(_dot_nt(x, x))
    qi = (q * jnp.exp2(cum)).astype(BF16)
    ks = (k * jnp.exp2(last - cum)).astype(BF16)
    return raws, qi, ks, jnp.exp2(last)


def _hgrn_kernel(*refs, nc, has_s0, want_state):
    q_ref, v_ref, zf_ref, zb_ref, g_ref, lb_ref, ng_ref, w01_ref, sm_ref = refs[:9]
    pos = 9
    if has_s0:
        s0_ref = refs[pos]
        pos += 1
    o_ref = refs[pos]
    pos += 1
    if want_state:
        sf_ref = refs[pos]
        pos += 1
    o_scr, st_scr = refs[pos:pos + 2]
    c = HGRN_CHUNK
    l = nc * c
    for d in range(2):
        if has_s0:
            st_scr[d] = s0_ref[0, d, 0]
        else:
            st_scr[d] = jnp.zeros((HGRN_DV, HGRN_DK), F32)
    rows = lax.broadcasted_iota(jnp.int32, (c, HGRN_DK), 0)

    def body(i, carry):
        insts = []
        for u in range(2):
            for d in range(2):
                ci = 2 * i + u if d == 0 else nc - 1 - (2 * i + u)
                insts.append((d, pl.multiple_of(ci * c, c)))
        gates = [_hgrn_cums((zf_ref, zb_ref)[d][0, pl.ds(r, c), :], lb_ref[d, 0], w01_ref[d]) for d, r in insts]
        prods = [_hgrn_products(q_ref[0, pl.ds(r, c), :], k, allc, rows, d)
                 for (d, r), (k, allc) in zip(insts, gates)]
        nl = len(HGRN_LEVELS)
        for (d, r), (raws, qi, ks, dec) in zip(insts, prods):
            scores = sm_ref[d, nl] * raws[0]
            for li in range(nl):
                scores = scores + sm_ref[d, li] * raws[li + 1]
            vb = v_ref[0, pl.ds(r, c), :].astype(BF16)
            st = st_scr[d]
            o_scr[d, pl.ds(r, c), :] = _dot(scores.astype(BF16), vb) + _dot_nt(qi, st.astype(BF16))
            st_scr[d] = dec * st + _dot_tn(vb, ks)
        return carry

    lax.fori_loop(0, nc // 2, body, 0)
    if want_state:
        for d in range(2):
            sf_ref[0, d, 0] = st_scr[d]

    fb = min(l, 256)

    def finish(i, carry):
        r = pl.multiple_of(i * fb, fb)
        tot = o_scr[0, pl.ds(r, fb), :] + o_scr[1, pl.ds(r, fb), :]
        ms = jnp.mean(tot * tot, axis=-1, keepdims=True)
        y = tot * lax.rsqrt(ms + NORM_EPS) * ng_ref[...]
        o_ref[0, pl.ds(r, fb), :] = (y * _silu(g_ref[0, pl.ds(r, fb), :])).astype(BF16)
        return carry

    lax.fori_loop(0, l // fb, finish, 0)


def _hgrn_branch(proj, lower_bound, norm_g, tables, s0_t, want_state):
    b, l, _ = proj.shape
    nc = l // HGRN_CHUNK
    w = HGRN_DK
    has_s0 = s0_t is not None
    w01, sm = tables

    def col(off):
        return lambda bi, h: (bi, 0, off // w + h)

    in_specs = [pl.BlockSpec((1, l, w), col(COL_QD)),
                pl.BlockSpec((1, l, w), col(COL_ID)),
                pl.BlockSpec((1, l, w), col(COL_ZF)),
                pl.BlockSpec((1, l, w), col(COL_ZB)),
                pl.BlockSpec((1, l, w), col(COL_GD)),
                pl.BlockSpec((2, 1, 1, w), lambda bi, h: (0, h, 0, 0)),
                pl.BlockSpec((1, w), lambda bi, h: (0, 0)),
                pl.BlockSpec(w01.shape, lambda bi, h: (0, 0, 0)),
                pl.BlockSpec(sm.shape, lambda bi, h: (0, 0, 0, 0))]
    args = [proj, proj, proj, proj, proj, lower_bound.reshape(2, HGRN_HEADS, 1, w), norm_g.reshape(1, w),
            w01, sm]
    st_spec = pl.BlockSpec((1, 2, 1, HGRN_DV, HGRN_DK), lambda bi, h: (bi, 0, h, 0, 0))
    if has_s0:
        in_specs.append(st_spec)
        args.append(s0_t)
    out_specs = [pl.BlockSpec((1, l, w), lambda bi, h: (bi, 0, h))]
    out_shape = [jax.ShapeDtypeStruct((b, l, HGRN_HEADS * HGRN_DV), BF16)]
    if want_state:
        out_specs.append(st_spec)
        out_shape.append(jax.ShapeDtypeStruct((b, 2, HGRN_HEADS, HGRN_DV, HGRN_DK), F32))
    res = pl.pallas_call(
        functools.partial(_hgrn_kernel, nc=nc, has_s0=has_s0, want_state=want_state),
        grid=(b, HGRN_HEADS),
        in_specs=in_specs, out_specs=out_specs, out_shape=out_shape,
        scratch_shapes=[pltpu.VMEM((2, l, HGRN_DV), F32), pltpu.VMEM((2, HGRN_DV, HGRN_DK), F32)],
        compiler_params=_cparams(2),
        name="hgrn",
    )(*args)
    return res if want_state else (res[0], None)


def _final_kernel(x_ref, gt_ref, ma_ref, mb_ref, mc_ref, md_ref, ya_ref, yb_ref, yc_ref, yd_ref,
                  wa_ref, wb_ref, wc_ref, wd_ref, wo_ref, gp_ref, o_ref, acc, *, nct):
    ci = pl.program_id(2)

    @pl.when(ci == 0)
    def _():
        acc[...] = jnp.zeros_like(acc)

    tm = acc.shape[0]
    for r0 in range(0, tm, FINAL_ROWS):
        rs = slice(r0, r0 + FINAL_ROWS)
        mixed = _gate_sigmoid(ma_ref[0, rs, :]) * _dot(ya_ref[0, rs, :], wa_ref[...])
        mixed = mixed + _gate_sigmoid(mb_ref[0, rs, :]) * _dot(yb_ref[0, rs, :], wb_ref[...])
        mixed = mixed + _gate_sigmoid(mc_ref[0, rs, :]) * _dot(yc_ref[0, rs, :], wc_ref[...])
        mixed = mixed + _gate_sigmoid(md_ref[0, rs, :]) * _dot(yd_ref[0, rs, :], wd_ref[...])
        acc[rs, :] += _dot(mixed.astype(BF16), wo_ref[...])

    @pl.when(ci == nct - 1)
    def _():
        out = acc[...]
        ms = jnp.mean(out * out, axis=-1, keepdims=True)
        y = out * lax.rsqrt(ms + NORM_EPS) * gp_ref[...]
        o_ref[0] = x_ref[0] + gt_ref[0] * y


def _final(x, gate, proj, ys, w_projs, w_out, norm_post, tm, tc):
    b, l, d = x.shape
    nct = d // tc
    per_batch = gate.shape[0] > 1
    gate_map = (lambda bi, i, c: (bi, 0, 0)) if per_batch else (lambda bi, i, c: (0, 0, 0))

    def mspec(j):
        return pl.BlockSpec((1, tm, tc), lambda bi, i, c: (bi, i, (COL_M + j * d) // tc + c))

    in_specs = [pl.BlockSpec((1, tm, d), lambda bi, i, c: (bi, i, 0)),
                pl.BlockSpec((1, 1, d), gate_map)]
    in_specs += [mspec(j) for j in range(N_BRANCHES)]
    in_specs += [pl.BlockSpec((1, tm, y.shape[-1]), lambda bi, i, c: (bi, i, 0)) for y in ys]
    in_specs += [pl.BlockSpec((w.shape[0], tc), lambda bi, i, c: (0, c)) for w in w_projs]
    in_specs += [pl.BlockSpec((tc, d), lambda bi, i, c: (c, 0)),
                 pl.BlockSpec((1, d), lambda bi, i, c: (0, 0))]
    return pl.pallas_call(
        functools.partial(_final_kernel, nct=nct),
        grid=(b, l // tm, nct),
        in_specs=in_specs,
        out_specs=pl.BlockSpec((1, tm, d), lambda bi, i, c: (bi, i, 0)),
        out_shape=jax.ShapeDtypeStruct((b, l, d), F32),
        scratch_shapes=[pltpu.VMEM((tm, d), F32)],
        compiler_params=_cparams(3),
        name="merge_out",
    )(x, gate, proj, proj, proj, proj, *ys, *w_projs, w_out, norm_post.reshape(1, d))


def _trunk_layer(x, shift, scale, gate, p, ctx, layer, consts):
    b, l, d = x.shape
    shared_mod = shift.shape[0] == 1
    tm = 1024
    if shared_mod:
        proj = _inproj(x.reshape(1, b * l, d), shift, scale, p['norm_pre'], p['w_in'], tm, 1024)
        proj = proj.reshape(b, l, IN_COLS)
    else:
        proj = _inproj(x, shift, scale, p['norm_pre'], p['w_in'], tm, 1024)

    dl, cs = consts['dft'][l]
    y_a = _fourier_branch(proj, p['fourier_w'], dl, cs)

    if ctx is None:
        y_b, s5_re, s5_im = _s5_branch(proj, p['s5_mats'], p['s5_glu_w'], p['s5_glu_b'], None, None)
        qs, kr, vb, kf = _qk_prep(proj, p['q_norm'], p['k_norm'], None, True, min(l, 512))
        y_c = _attention(qs, kr, vb, proj, None, None, layer, min(l, 256))
        y_d, hg = _hgrn_branch(proj, p['lower_bound'], p['hgrn_norm'], consts['hgrn'], None, True)
    else:
        cache_k, cache_v, st_re, st_im, s0_t = ctx
        y_b, _, _ = _s5_branch(proj, p['s5_mats'], p['s5_glu_w'], p['s5_glu_b'], st_re[:, layer], st_im[:, layer])
        qs, kr, vb = _qk_prep(proj, p['q_norm'], p['k_norm'], consts['rope'], False, min(l, 512))
        y_c = _attention(qs, kr, vb, proj, cache_k, cache_v, layer, min(l, 256))
        y_d, _ = _hgrn_branch(proj, p['lower_bound'], p['hgrn_norm'], consts['hgrn'], s0_t[:, layer], False)

    ys = (y_a, y_b, y_c, y_d)
    wps = (p['w_proj_a'], p['w_proj_b'], p['w_proj_c'], p['w_proj_d'])
    if shared_mod:
        n = b * l
        x_new = _final(x.reshape(1, n, d), gate, proj.reshape(1, n, IN_COLS),
                       tuple(y.reshape(1, n, y.shape[-1]) for y in ys), wps, p['w_out'], p['norm_post'], 512, 512)
        x_new = x_new.reshape(b, l, d)
    else:
        x_new = _final(x, gate, proj, ys, wps, p['w_out'], p['norm_post'], 512, 512)

    if ctx is None:
        v_f = proj[:, :, COL_VC:COL_VC + KV_WIDTH]
        return x_new, (kf, v_f, s5_re, s5_im, hg)
    return x_new, None


def kernel(x_prompt, x_sample, c, cache_k, cache_v, state_s5_re, state_s5_im, state_hgrn, c_ctx,
           norm_pre, norm_post, w_mod, b_mod, w_in, fourier_w,
           s5_lambda_re, s5_lambda_im, s5_log_step, s5_b_re, s5_b_im, s5_c_re, s5_c_im,
           s5_d, s5_glu_w, s5_glu_b, q_norm, k_norm, hgrn_lb_logits, hgrn_norm,
           w_proj_a, w_proj_b, w_proj_c, w_proj_d, w_out):
    depth = w_in.shape[0]
    bp, lp, d = x_prompt.shape
    bs, ls, _ = x_sample.shape

    lb_w = jax.nn.softmax(hgrn_lb_logits.astype(F32), axis=0)
    lower_bounds = jnp.cumsum(lb_w, axis=0) - lb_w[0]

    rows = ((bs + 1 + 7) // 8) * 8
    c_all = jnp.zeros((rows, d), F32).at[:bs].set(c).at[bs].set(c_ctx)
    mod = _modulation(c_all, w_mod, b_mod)

    w_in_bf = w_in.astype(BF16)
    wpa, wpb, wpc, wpd = (w.astype(BF16) for w in (w_proj_a, w_proj_b, w_proj_c, w_proj_d))
    w_out_bf = w_out.astype(BF16)

    consts = {'dft': {lp: _dft_tables(lp), ls: _dft_tables(ls)},
              'rope': _rope_tables(ls),
              'hgrn': _hgrn_tables()}
    past = cache_k.shape[2]
    ctx_k = cache_k.reshape(bs, depth, past, KV_WIDTH)
    ctx_v = cache_v.reshape(bs, depth, past, KV_WIDTH)
    s0_t = jnp.swapaxes(state_hgrn.astype(F32), -1, -2)

    y_p, y_s = x_prompt, x_sample
    ks, vs, s5r, s5i, hg = [], [], [], [], []
    for l in range(depth):
        p = {'norm_pre': norm_pre[l].reshape(1, d), 'norm_post': norm_post[l], 'w_in': w_in_bf[l],
             'fourier_w': fourier_w[l],
             's5_mats': _s5_matrices(s5_lambda_re[l], s5_lambda_im[l], s5_log_step[l], s5_b_re[l], s5_b_im[l],
                                     s5_c_re[l], s5_c_im[l], s5_d[l].reshape(S5_GROUPS, S5_GROUP_DIM)),
             's5_glu_w': s5_glu_w[l], 's5_glu_b': s5_glu_b[l],
             'q_norm': q_norm[l], 'k_norm': k_norm[l],
             'lower_bound': lower_bounds[l], 'hgrn_norm': hgrn_norm[l],
             'w_proj_a': wpa[l], 'w_proj_b': wpb[l], 'w_proj_c': wpc[l], 'w_proj_d': wpd[l],
             'w_out': w_out_bf[l]}
        m_ctx = mod[l, bs].reshape(1, 1, 3 * d)
        sh, sc, gt = (m_ctx[..., i * d:(i + 1) * d] for i in range(3))
        y_p, (k_l, v_l, sr_l, si_l, hg_l) = _trunk_layer(y_p, sh, sc, gt, p, None, l, consts)
        ks.append(k_l.reshape(bp, lp, N_KV_HEADS, HEAD_DIM))
        vs.append(v_l.reshape(bp, lp, N_KV_HEADS, HEAD_DIM))
        s5r.append(sr_l)
        s5i.append(si_l)
        hg.append(jnp.swapaxes(hg_l, -1, -2))
        m_s = mod[l, :bs].reshape(bs, 1, 3 * d)
        sh, sc, gt = (m_s[..., i * d:(i + 1) * d] for i in range(3))
        ctx = (ctx_k, ctx_v, state_s5_re, state_s5_im, s0_t)
        y_s, _ = _trunk_layer(y_s, sh, sc, gt, p, ctx, l, consts)

    return (y_p, y_s, jnp.stack(ks, axis=1), jnp.stack(vs, axis=1),
            jnp.stack(s5r, axis=1), jnp.stack(s5i, axis=1), jnp.stack(hg, axis=1))
```

```python
import functools
import math

import jax
import jax.numpy as jnp
import numpy as np
from jax import lax
from jax.experimental import pallas as pl
from jax.experimental.pallas import tpu as pltpu

F32 = jnp.float32
BF16 = jnp.bfloat16

NORM_EPS = 1e-6
F_FLOOR = 1e-30
LOG2E = 1.4426950408889634

D_MODEL = 2048
N_BRANCHES = 4
FOURIER_WIDTH = 512
FOURIER_GROUPS = 4
FOURIER_GROUP_DIM = 128
S5_WIDTH = 512
S5_GROUP_DIM = 16
S5_GROUPS = 32
S5_STATE = 64
S5_CHUNK = 8
S5_GB = 8
S5_NBLK = S5_GROUPS // S5_GB
S5_K = S5_CHUNK * S5_GB * S5_GROUP_DIM
S5_PART = S5_GB * S5_STATE
S5_MAX_ROWS = 512
N_HEADS = 8
N_KV_HEADS = 2
HEAD_DIM = 128
Q_PER_KV = 4
ATTN_WIDTH = 1024
KV_WIDTH = 256
ROPE_THETA = 10000.0
GRID_W = 64
HGRN_HEADS = 4
HGRN_DK = 128
HGRN_DV = 128
HGRN_WIDTH = 512
HGRN_CHUNK = 64

COL_UA, COL_GA, COL_UB, COL_GB = 0, 512, 1024, 1536
COL_QC, COL_KC, COL_VC, COL_GC = 2048, 3072, 3328, 3584
COL_QD, COL_ID, COL_ZF, COL_ZB, COL_GD = 4608, 5120, 5632, 6144, 6656
COL_M = 7168
IN_COLS = 15360

FINAL_ROWS = 256
ATTN_KEY_CHUNK = 1024
VMEM_LIMIT = 56 * 1024 * 1024


def _cparams(n_grid):
    return pltpu.CompilerParams(dimension_semantics=("arbitrary",) * n_grid,
                                vmem_limit_bytes=VMEM_LIMIT)


def _sigmoid(x):
    return 1.0 / (1.0 + jnp.exp(-x))


def _gate_sigmoid(x):
    return 0.5 * jnp.tanh(0.5 * x) + 0.5


def _silu(x):
    return x * _gate_sigmoid(x)


def _dot(a, b):
    return jnp.dot(a, b, preferred_element_type=F32)


def _dot_nt(a, b):
    return lax.dot_general(a, b, (((1,), (1,)), ((), ())), preferred_element_type=F32)


def _dot_tn(a, b):
    return lax.dot_general(a, b, (((0,), (0,)), ((), ())), preferred_element_type=F32)


def _split2(x):
    hi = x.astype(BF16)
    lo = (x - hi.astype(F32)).astype(BF16)
    return hi, lo


def _split3(x):
    hi = x.astype(BF16)
    r = x - hi.astype(F32)
    mid = r.astype(BF16)
    lo = (r - mid.astype(F32)).astype(BF16)
    return hi, mid, lo


def _dot3(a, b):
    ah, al = _split2(a)
    bh, bl = _split2(b)
    return _dot(ah, bh) + (_dot(ah, bl) + _dot(al, bh))


def _mod_kernel(c_ref, w_ref, b_ref, o_ref):
    a = _silu(c_ref[...])
    o_ref[0] = _dot3(a, w_ref[0]) + b_ref[0]


def _modulation(c_all, w_mod, b_mod):
    depth, d, n = w_mod.shape
    rows = c_all.shape[0]
    tn = 1024
    return pl.pallas_call(
        _mod_kernel,
        grid=(depth, n // tn),
        in_specs=[pl.BlockSpec((rows, d), lambda l, j: (0, 0)),
                  pl.BlockSpec((1, d, tn), lambda l, j: (l, 0, j)),
                  pl.BlockSpec((1, 1, tn), lambda l, j: (l, 0, j))],
        out_specs=pl.BlockSpec((1, rows, tn), lambda l, j: (l, 0, j)),
        out_shape=jax.ShapeDtypeStruct((depth, rows, n), F32),
        compiler_params=_cparams(2),
        name="modulation",
    )(c_all, w_mod, b_mod.reshape(depth, 1, n))


def _inproj_kernel(x_ref, sh_ref, sc_ref, g_ref, w_ref, o_ref, h_scr):
    @pl.when(pl.program_id(2) == 0)
    def _():
        x = x_ref[0]
        ms = jnp.mean(x * x, axis=-1, keepdims=True)
        y = x * lax.rsqrt(ms + NORM_EPS) * g_ref[...]
        h_scr[...] = (y * (1.0 + sc_ref[0]) + sh_ref[0]).astype(BF16)

    o_ref[0] = _dot(h_scr[...], w_ref[...])


def _inproj(x, shift, scale, g, w_bf, tm, tn):
    b, l, d = x.shape
    n = w_bf.shape[1]
    per_batch = shift.shape[0] > 1
    mod_map = (lambda bi, i, j: (bi, 0, 0)) if per_batch else (lambda bi, i, j: (0, 0, 0))
    return pl.pallas_call(
        _inproj_kernel,
        grid=(b, l // tm, n // tn),
        in_specs=[pl.BlockSpec((1, tm, d), lambda bi, i, j: (bi, i, 0)),
                  pl.BlockSpec((1, 1, d), mod_map),
                  pl.BlockSpec((1, 1, d), mod_map),
                  pl.BlockSpec((1, d), lambda bi, i, j: (0, 0)),
                  pl.BlockSpec((d, tn), lambda bi, i, j: (0, j))],
        out_specs=pl.BlockSpec((1, tm, tn), lambda bi, i, j: (bi, i, j)),
        out_shape=jax.ShapeDtypeStruct((b, l, n), F32),
        scratch_shapes=[pltpu.VMEM((tm, d), BF16)],
        compiler_params=_cparams(3),
        name="inproj",
    )(x, shift, scale, g, w_bf)


def _dft_tables(l):
    scale = 1.0 / math.sqrt(l * FOURIER_GROUP_DIM)
    k = np.arange(l, dtype=np.int64)
    t1 = np.arange(l // 64, dtype=np.int64)[:, None]
    t2 = np.arange(64, dtype=np.int64)[:, None]
    a = 2.0 * np.pi * ((t1 * 64 * k[None, :]) % l).astype(np.float64) / l
    bb = 2.0 * np.pi * ((t2 * k[None, :]) % l).astype(np.float64) / l
    ca, sa = jnp.asarray(np.cos(a) * scale, F32), jnp.asarray(np.sin(a) * scale, F32)
    cb, sb = jnp.asarray(np.cos(bb), F32), jnp.asarray(np.sin(bb), F32)
    dlc = (ca[:, None, :] * cb[None, :, :] - sa[:, None, :] * sb[None, :, :]).astype(BF16).reshape(l, l)
    dls = (-sa[:, None, :] * cb[None, :, :] - ca[:, None, :] * sb[None, :, :]).astype(BF16).reshape(l, l)
    c = np.arange(FOURIER_GROUP_DIM, dtype=np.int64)
    ang = 2.0 * np.pi * ((c[:, None] * c[None, :]) % FOURIER_GROUP_DIM) / FOURIER_GROUP_DIM
    cs = jnp.asarray(np.concatenate([np.cos(ang), np.sin(ang)], axis=1), F32).astype(BF16)
    return (dlc, dls), cs


def _fourier_kernel(u_ref, g_ref, cs_ref, dlc_ref, dls_ref, w_ref, o_ref, ucs_scr, *, l):
    gd = FOURIER_GROUP_DIM

    @pl.when(pl.program_id(1) == 0)
    def _():
        for g in range(FOURIER_GROUPS):
            ug = u_ref[0, :, g * gd:(g + 1) * gd].astype(BF16)
            t = _dot(ug, cs_ref[...])
            ucs_scr[0:l, g * gd:(g + 1) * gd] = t[:, :gd].astype(BF16)
            ucs_scr[l:2 * l, g * gd:(g + 1) * gd] = t[:, gd:].astype(BF16)

    f = _dot(dlc_ref[...], ucs_scr[0:l, :]) + _dot(dls_ref[...], ucs_scr[l:2 * l, :])
    gate = _silu(g_ref[0])
    for g in range(FOURIER_GROUPS):
        y = _dot(f[:, g * gd:(g + 1) * gd].astype(BF16), w_ref[g].astype(BF16))
        o_ref[0, :, g * gd:(g + 1) * gd] = (y * gate[:, g * gd:(g + 1) * gd]).astype(BF16)


def _fourier_branch(proj, fourier_w, dl, cs):
    b, l, _ = proj.shape
    tr = min(l, 512)
    w = FOURIER_WIDTH
    return pl.pallas_call(
        functools.partial(_fourier_kernel, l=l),
        grid=(b, l // tr),
        in_specs=[pl.BlockSpec((1, l, w), lambda bi, r: (bi, 0, COL_UA // w)),
                  pl.BlockSpec((1, tr, w), lambda bi, r: (bi, r, COL_GA // w)),
                  pl.BlockSpec((FOURIER_GROUP_DIM, 2 * FOURIER_GROUP_DIM), lambda bi, r: (0, 0)),
                  pl.BlockSpec((tr, l), lambda bi, r: (r, 0)),
                  pl.BlockSpec((tr, l), lambda bi, r: (r, 0)),
                  pl.BlockSpec((FOURIER_GROUPS, FOURIER_GROUP_DIM, FOURIER_GROUP_DIM),
                               lambda bi, r: (0, 0, 0))],
        out_specs=pl.BlockSpec((1, tr, w), lambda bi, r: (bi, r, 0)),
        out_shape=jax.ShapeDtypeStruct((b, l, w), BF16),
        scratch_shapes=[pltpu.VMEM((2 * l, w), BF16)],
        compiler_params=_cparams(2),
        name="fourier",
    )(proj, proj, cs, dl[0], dl[1], fourier_w)


def _s5_matrices(lam_re, lam_im, log_step, b_re, b_im, c_re, c_im, dskip):
    t = S5_CHUNK
    g, n, p = S5_GROUPS, S5_STATE, S5_GROUP_DIM
    hp = lax.Precision.HIGHEST
    lam_re = lam_re.astype(F32)
    lam_im = lam_im.astype(F32)
    step = jnp.exp(log_step.astype(F32))[..., None]
    rho = lam_re * step
    th = lam_im * step
    mag = jnp.exp(rho)
    lb_re = mag * jnp.cos(th)
    lb_im = mag * jnp.sin(th)
    nr = lb_re - 1.0
    den = lam_re * lam_re + lam_im * lam_im
    fr = (nr * lam_re + lb_im * lam_im) / den
    fi = (lb_im * lam_re - nr * lam_im) / den
    b_re = b_re.astype(F32)
    b_im = b_im.astype(F32)
    bb_re = fr[..., None] * b_re - fi[..., None] * b_im
    bb_im = fr[..., None] * b_im + fi[..., None] * b_re
    tau = jnp.arange(t + 1, dtype=F32)
    pw_mag = jnp.exp(rho[..., None] * tau)
    pw_re = pw_mag * jnp.cos(th[..., None] * tau)
    pw_im = pw_mag * jnp.sin(th[..., None] * tau)
    c_re = c_re.astype(F32)
    c_im = c_im.astype(F32)
    ca_re = c_re[..., None] * pw_re[:, :, None] - c_im[..., None] * pw_im[:, :, None]
    ca_im = c_re[..., None] * pw_im[:, :, None] + c_im[..., None] * pw_re[:, :, None]
    kern = (jnp.einsum('dgpnt,dgnq->dgtpq', ca_re, bb_re, precision=hp)
            - jnp.einsum('dgpnt,dgnq->dgtpq', ca_im, bb_im, precision=hp))[:, :, :t]
    sig = np.arange(t)[:, None]
    ta = np.arange(t)[None, :]
    idx_f = np.clip(ta - sig, 0, t - 1)
    idx_b = np.clip(sig - ta, 0, t - 1)
    m_f = jnp.asarray((ta >= sig).astype(np.float32))[None, :, :, None, None]
    m_b = jnp.asarray((sig >= ta).astype(np.float32))[None, :, :, None, None]
    tf = kern[0][:, idx_f] * m_f
    tb = kern[1][:, idx_b] * m_b
    eye_t = jnp.asarray(np.eye(t, dtype=np.float32))[None, :, :, None, None]
    dmat = dskip.astype(F32)[:, None, None, :, None] * jnp.asarray(np.eye(p, dtype=np.float32))[None, None, None]
    tt = tf + tb + eye_t * dmat
    nb, gb = S5_NBLK, S5_GB
    eye_g = jnp.asarray(np.eye(gb, dtype=np.float32))
    tmat = (tt.reshape(nb, gb, t, t, p, p).transpose(0, 2, 1, 5, 3, 4)[:, :, :, :, :, None, :]
            * eye_g[None, None, :, None, None, :, None]).reshape(nb, S5_K, S5_K)

    def ba(d, powers):
        pr = pw_re[d][:, :, powers]
        pi = pw_im[d][:, :, powers]
        re = pr[..., None] * bb_re[d][:, :, None, :] - pi[..., None] * bb_im[d][:, :, None, :]
        im = pr[..., None] * bb_im[d][:, :, None, :] + pi[..., None] * bb_re[d][:, :, None, :]
        blk = lambda m: (m.reshape(nb, gb, n, t, p).transpose(0, 3, 1, 4, 2)[:, :, :, :, None, :]
                         * eye_g[None, None, :, None, :, None]).reshape(nb, S5_K, S5_PART)
        return blk(re), blk(im)

    baf_re, baf_im = ba(0, np.arange(t - 1, -1, -1))
    bab_re, bab_im = ba(1, np.arange(t))
    bamat = jnp.concatenate([baf_re, bab_re, baf_im, bab_im], axis=-1)

    def ca(d, powers):
        blk = lambda m: (m[..., powers].reshape(nb, gb, p, n, t).transpose(0, 3, 4, 1, 2)[:, None]
                         * eye_g[None, :, None, None, :, None]).reshape(nb, S5_PART, S5_K)
        return blk(ca_re[d]), blk(ca_im[d])

    caf_re, caf_im = ca(0, np.arange(1, t + 1))
    cab_re, cab_im = ca(1, np.arange(t, 0, -1))
    camat = jnp.concatenate([caf_re, cab_re, -caf_im, -cab_im], axis=1)
    part = lambda m: m[..., t].reshape(nb, S5_PART)
    a_pow = jnp.stack([jnp.concatenate([part(pw_re[0]), part(pw_re[1])], axis=-1),
                       jnp.concatenate([part(pw_im[0]), part(pw_im[1])], axis=-1)], axis=1)
    return tmat.astype(BF16), bamat.astype(BF16), camat.astype(BF16), a_pow


def _s5_kernel(u_ref, t_ref, ba_ref, ca_ref, a_ref, h0_ref, y_ref, hf_ref, ub_scr, s_scr, hin_scr, y_scr,
               *, nc, bt):
    t, w, hp = S5_CHUNK, S5_GB * S5_GROUP_DIM, S5_PART
    for bi in range(bt):
        for tau in range(t):
            ub_scr[tau, pl.ds(bi, nc, stride=bt), :] = u_ref[bi, pl.ds(tau, nc, stride=t), :]
    ub = jnp.concatenate([ub_scr[tau] for tau in range(t)], axis=1).astype(BF16)
    s_scr[...] = _dot(ub, ba_ref[0])
    ar_f, ar_b = a_ref[0, 0:1, 0:hp], a_ref[0, 0:1, hp:2 * hp]
    ai_f, ai_b = a_ref[0, 1:2, 0:hp], a_ref[0, 1:2, hp:2 * hp]

    def step(i, carry):
        re_f, im_f, re_b, im_b = carry
        rf = pl.ds(pl.multiple_of(i * bt, bt), bt)
        rb = pl.ds(pl.multiple_of((nc - 1 - i) * bt, bt), bt)
        hin_scr[rf, 0:hp] = re_f
        hin_scr[rf, 2 * hp:3 * hp] = im_f
        hin_scr[rb, hp:2 * hp] = re_b
        hin_scr[rb, 3 * hp:4 * hp] = im_b
        n_re_f = ar_f * re_f - ai_f * im_f + s_scr[rf, 0:hp]
        n_im_f = ar_f * im_f + ai_f * re_f + s_scr[rf, 2 * hp:3 * hp]
        n_re_b = ar_b * re_b - ai_b * im_b + s_scr[rb, hp:2 * hp]
        n_im_b = ar_b * im_b + ai_b * re_b + s_scr[rb, 3 * hp:4 * hp]
        return n_re_f, n_im_f, n_re_b, n_im_b

    h0 = h0_ref[...].reshape(bt, 4 * hp)
    init = (h0[:, 0:hp], h0[:, 2 * hp:3 * hp], h0[:, hp:2 * hp], h0[:, 3 * hp:4 * hp])
    fin = lax.fori_loop(0, nc, step, init, unroll=8)
    hf_ref[...] = jnp.concatenate([fin[0], fin[2], fin[1], fin[3]], axis=1).reshape(hf_ref.shape)
    y = _dot(ub, t_ref[0]) + _dot(hin_scr[...].astype(BF16), ca_ref[0])
    for tau in range(t):
        y_scr[tau] = y[:, tau * w:(tau + 1) * w]
    for bi in range(bt):
        for tau in range(t):
            y_ref[bi, pl.ds(tau, nc, stride=t), :] = y_scr[tau, pl.ds(bi, nc, stride=bt), :]


def _s5_core(proj, tmat, bamat, camat, a_pow, h0):
    b, l, _ = proj.shape
    nc = l // S5_CHUNK
    w = S5_GB * S5_GROUP_DIM
    bt = b if nc * b <= S5_MAX_ROWS else 1
    if bt == 1:
        h0 = h0.reshape(S5_NBLK, b, 1, 4 * S5_PART)
        st_spec = pl.BlockSpec((1, 1, 1, 4 * S5_PART), lambda j, bi: (j, bi, 0, 0))
    else:
        st_spec = pl.BlockSpec((1, bt, 4 * S5_PART), lambda j, bi: (j, bi, 0))
    rows = nc * bt
    y, hf = pl.pallas_call(
        functools.partial(_s5_kernel, nc=nc, bt=bt),
        grid=(S5_NBLK, b // bt),
        in_specs=[pl.BlockSpec((bt, l, w), lambda j, bi: (bi, 0, COL_UB // w + j)),
                  pl.BlockSpec((1, S5_K, S5_K), lambda j, bi: (j, 0, 0)),
                  pl.BlockSpec((1, S5_K, 4 * S5_PART), lambda j, bi: (j, 0, 0)),
                  pl.BlockSpec((1, 4 * S5_PART, S5_K), lambda j, bi: (j, 0, 0)),
                  pl.BlockSpec((1, 2, 2 * S5_PART), lambda j, bi: (j, 0, 0)),
                  st_spec],
        out_specs=[pl.BlockSpec((bt, l, w), lambda j, bi: (bi, 0, j)), st_spec],
        out_shape=[jax.ShapeDtypeStruct((b, l, S5_WIDTH), F32),
                   jax.ShapeDtypeStruct(h0.shape, F32)],
        scratch_shapes=[pltpu.VMEM((S5_CHUNK, rows, w), F32), pltpu.VMEM((rows, 4 * S5_PART), F32),
                        pltpu.VMEM((rows, 4 * S5_PART), F32), pltpu.VMEM((S5_CHUNK, rows, w), F32)],
        compiler_params=_cparams(2),
        name="s5_core",
    )(proj, tmat, bamat, camat, a_pow, h0)
    return y, hf.reshape(S5_NBLK, b, 4 * S5_PART)


def _s5_glu_kernel(y_ref, g_ref, w_ref, b_ref, o_ref):
    y = y_ref[0]
    y = 0.5 * y * (1.0 + jnp.tanh(math.sqrt(2.0 / math.pi) * (y + 0.044715 * (y * y * y))))
    z = _dot(y.astype(BF16), w_ref[...].astype(BF16)) + b_ref[...]
    o_ref[0] = (y * _sigmoid(z) * _silu(g_ref[0])).astype(BF16)


def _s5_glu(y, proj, glu_w, glu_b, tm):
    b, l, w = y.shape
    return pl.pallas_call(
        _s5_glu_kernel,
        grid=(b, l // tm),
        in_specs=[pl.BlockSpec((1, tm, w), lambda bi, i: (bi, i, 0)),
                  pl.BlockSpec((1, tm, w), lambda bi, i: (bi, i, COL_GB // w)),
                  pl.BlockSpec((w, w), lambda bi, i: (0, 0)),
                  pl.BlockSpec((1, w), lambda bi, i: (0, 0))],
        out_specs=pl.BlockSpec((1, tm, w), lambda bi, i: (bi, i, 0)),
        out_shape=jax.ShapeDtypeStruct((b, l, w), BF16),
        compiler_params=_cparams(2),
        name="s5_glu",
    )(y, proj, glu_w, glu_b.reshape(1, w))


def _s5_branch(proj, mats, glu_w, glu_b, h0_re, h0_im):
    b, l, _ = proj.shape
    g, n = S5_GROUPS, S5_STATE
    if h0_re is None:
        h0 = jnp.zeros((S5_NBLK, b, 4 * S5_PART), F32)
    else:
        blk = lambda h: h.astype(F32).reshape(b, S5_NBLK, S5_PART).transpose(1, 0, 2)
        h0 = jnp.concatenate([blk(h0_re[:, 0]), blk(h0_re[:, 1]), blk(h0_im[:, 0]), blk(h0_im[:, 1])], axis=-1)
    y, hf = _s5_core(proj, *mats, h0)
    out = _s5_glu(y, proj, glu_w, glu_b, min(l, 1024))
    part = lambda i: hf[..., i * S5_PART:(i + 1) * S5_PART].transpose(1, 0, 2).reshape(b, g, n)
    fin_re = jnp.stack([part(0), part(1)], axis=1)
    fin_im = jnp.stack([part(2), part(3)], axis=1)
    return out, fin_re, fin_im


def _rope_tables(l):
    rows = l // GRID_W
    row = jnp.broadcast_to(jnp.arange(rows, dtype=F32)[:, None], (rows, GRID_W)).reshape(-1)
    col = jnp.broadcast_to(jnp.arange(GRID_W, dtype=F32)[None, :], (rows, GRID_W)).reshape(-1)
    half = HEAD_DIM // 2
    inv = ROPE_THETA ** (-jnp.arange(0, half, 2, dtype=F32) / half)
    ar = row[:, None] * inv
    ac = col[:, None] * inv
    cos = jnp.concatenate([jnp.cos(ar), jnp.cos(ar), jnp.cos(ac), jnp.cos(ac)], axis=1)
    sin = jnp.concatenate([-jnp.sin(ar), jnp.sin(ar), -jnp.sin(ac), jnp.sin(ac)], axis=1)
    return cos, sin


def _qk_kernel(*refs, rope, want_kf):
    q_ref, k_ref, v_ref, qn_ref, kn_ref = refs[:5]
    pos = 5
    if rope:
        cos_ref, sin_ref = refs[pos:pos + 2]
        pos += 2
    qo_ref, ko_ref, vo_ref = refs[pos:pos + 3]
    pos += 3
    if want_kf:
        kf_ref = refs[pos]
    hd = HEAD_DIM
    tm = q_ref.shape[1]
    quarter = hd // 4
    if rope:
        cos = cos_ref[...]
        sin = sin_ref[...]
        first = (lax.broadcasted_iota(jnp.int32, (tm, hd), 1) % (2 * quarter)) < quarter

    def norm(x, gain):
        ms = jnp.mean(x * x, axis=-1, keepdims=True)
        return x * lax.rsqrt(ms + NORM_EPS) * gain

    def rot(x):
        if not rope:
            return x
        nxt = pltpu.roll(x, hd - quarter, 1)
        prv = pltpu.roll(x, quarter, 1)
        return x * cos + jnp.where(first, nxt, prv) * sin

    qscale = (hd ** -0.5) * LOG2E
    for h in range(N_HEADS):
        x = norm(q_ref[0, :, h * hd:(h + 1) * hd], qn_ref[...])
        qo_ref[0, :, h * hd:(h + 1) * hd] = (rot(x) * qscale).astype(BF16)
    for h in range(N_KV_HEADS):
        x = norm(k_ref[0, :, h * hd:(h + 1) * hd], kn_ref[...])
        if want_kf:
            kf_ref[0, :, h * hd:(h + 1) * hd] = x
        ko_ref[0, :, h * hd:(h + 1) * hd] = rot(x).astype(BF16)
    vo_ref[0] = v_ref[0].astype(BF16)


def _qk_prep(proj, q_norm, k_norm, rope_tabs, want_kf, tm):
    b, l, _ = proj.shape
    rope = rope_tabs is not None
    in_specs = [pl.BlockSpec((1, tm, ATTN_WIDTH), lambda bi, i: (bi, i, COL_QC // ATTN_WIDTH)),
                pl.BlockSpec((1, tm, KV_WIDTH), lambda bi, i: (bi, i, COL_KC // KV_WIDTH)),
                pl.BlockSpec((1, tm, KV_WIDTH), lambda bi, i: (bi, i, COL_VC // KV_WIDTH)),
                pl.BlockSpec((1, HEAD_DIM), lambda bi, i: (0, 0)),
                pl.BlockSpec((1, HEAD_DIM), lambda bi, i: (0, 0))]
    args = [proj, proj, proj, q_norm.reshape(1, HEAD_DIM), k_norm.reshape(1, HEAD_DIM)]
    if rope:
        in_specs += [pl.BlockSpec((tm, HEAD_DIM), lambda bi, i: (i, 0))] * 2
        args += list(rope_tabs)
    out_specs = [pl.BlockSpec((1, tm, ATTN_WIDTH), lambda bi, i: (bi, i, 0)),
                 pl.BlockSpec((1, tm, KV_WIDTH), lambda bi, i: (bi, i, 0)),
                 pl.BlockSpec((1, tm, KV_WIDTH), lambda bi, i: (bi, i, 0))]
    out_shape = [jax.ShapeDtypeStruct((b, l, ATTN_WIDTH), BF16),
                 jax.ShapeDtypeStruct((b, l, KV_WIDTH), BF16),
                 jax.ShapeDtypeStruct((b, l, KV_WIDTH), BF16)]
    if want_kf:
        out_specs.append(pl.BlockSpec((1, tm, KV_WIDTH), lambda bi, i: (bi, i, 0)))
        out_shape.append(jax.ShapeDtypeStruct((b, l, KV_WIDTH), F32))
    return pl.pallas_call(
        functools.partial(_qk_kernel, rope=rope, want_kf=want_kf),
        grid=(b, l // tm),
        in_specs=in_specs, out_specs=out_specs, out_shape=out_shape,
        compiler_params=_cparams(2),
        name="qk_prep",
    )(*args)


def _attn_kernel(*refs, has_ctx):
    s_scr = refs[-1]
    q_ref, k_ref, v_ref = refs[:3]
    if has_ctx:
        ck_ref, cv_ref, g_ref, o_ref = refs[3:7]
        ck = ck_ref[0, 0].astype(BF16)
        cvt = cv_ref[0, 0, 0].astype(BF16)
        past = ck.shape[0]
    else:
        g_ref, o_ref = refs[3:5]
    hd = HEAD_DIM
    tq = q_ref.shape[1]
    lk = k_ref.shape[1]
    cr = min(lk, ATTN_KEY_CHUNK)
    chunks = [(c * cr, cr, (lambda c=c: k_ref[0, c * cr:(c + 1) * cr, :]),
               (lambda c=c: v_ref[0, 0, :, c * cr:(c + 1) * cr])) for c in range(lk // cr)]
    if has_ctx:
        chunks.append((lk, past, lambda: ck, lambda: cvt))
    qs = [q_ref[0, :, g * hd:(g + 1) * hd] for g in range(Q_PER_KV)]
    gate = _silu(g_ref[0])

    def score(g, chunk, m):
        off, rows, keys, _ = chunk
        s = _dot_nt(keys(), qs[g])
        s_scr[g % 2, off:off + rows, :] = s
        return jnp.maximum(m, jnp.max(s, axis=0, keepdims=True))

    def accumulate(g, chunk, m, den, ot):
        off, rows, _, values_t = chunk
        p = jnp.exp2(s_scr[g % 2, off:off + rows, :] - m)
        return den + jnp.sum(p, axis=0, keepdims=True), ot + _dot(values_t(), p.astype(BF16))

    neg = jnp.full((1, tq), -jnp.inf, F32)
    m_cur = neg
    for chunk in chunks:
        m_cur = score(0, chunk, m_cur)
    for g in range(Q_PER_KV):
        m_next = neg
        den = jnp.zeros((1, tq), F32)
        ot = jnp.zeros((hd, tq), F32)
        for chunk in chunks:
            if g + 1 < Q_PER_KV:
                m_next = score(g + 1, chunk, m_next)
            den, ot = accumulate(g, chunk, m_cur, den, ot)
        m_cur = m_next
        o = (ot / den).T
        o_ref[0, :, g * hd:(g + 1) * hd] = (o * gate[:, g * hd:(g + 1) * hd]).astype(BF16)


def _attention(qs, kr, vb, proj, ctx_k, ctx_v, layer, tq):
    b, l, _ = qs.shape
    gw = Q_PER_KV * HEAD_DIM
    has_ctx = ctx_k is not None
    vt = vb.reshape(b, l, N_KV_HEADS, HEAD_DIM).transpose(0, 2, 3, 1)
    in_specs = [pl.BlockSpec((1, tq, gw), lambda bi, h, i: (bi, i, h)),
                pl.BlockSpec((1, l, HEAD_DIM), lambda bi, h, i: (bi, 0, h)),
                pl.BlockSpec((1, 1, HEAD_DIM, l), lambda bi, h, i: (bi, h, 0, 0))]
    args = [qs, kr, vt]
    if has_ctx:
        past = ctx_k.shape[2]
        in_specs += [pl.BlockSpec((1, 1, past, HEAD_DIM), lambda bi, h, i: (bi, layer, 0, h)),
                     pl.BlockSpec((1, 1, 1, HEAD_DIM, past), lambda bi, h, i: (bi, layer, h, 0, 0))]
        args += [ctx_k, ctx_v]
    in_specs.append(pl.BlockSpec((1, tq, gw), lambda bi, h, i: (bi, i, COL_GC // gw + h)))
    args.append(proj)
    return pl.pallas_call(
        functools.partial(_attn_kernel, has_ctx=has_ctx),
        grid=(b, N_KV_HEADS, l // tq),
        in_specs=in_specs,
        out_specs=pl.BlockSpec((1, tq, gw), lambda bi, h, i: (bi, i, h)),
        out_shape=jax.ShapeDtypeStruct((b, l, ATTN_WIDTH), BF16),
        scratch_shapes=[pltpu.VMEM((2, l + (ctx_k.shape[2] if has_ctx else 0), tq), F32)],
        compiler_params=_cparams(3),
        name="attention",
    )(*args)


HGRN_COARSE = (32, 16, 8, 4)
HGRN_FINE = (2, 1)
HGRN_LEVELS = HGRN_COARSE + HGRN_FINE


def _hgrn_tables():
    c = HGRN_CHUNK
    t = np.arange(c)
    w01, sm = [], []
    for d in range(2):
        cum = (t[None, :] <= t[:, None]) if d == 0 else (t[None, :] >= t[:, None])
        cum = cum.astype(np.float32)
        w01.append(np.concatenate([cum] + [cum[_hgrn_ref_row(t, hs, d)] for hs in HGRN_FINE], axis=0))
        sd = []
        for hs in HGRN_LEVELS:
            is_q = _hgrn_is_query(t, hs, d)
            same = (t[:, None] // (2 * hs)) == (t[None, :] // (2 * hs))
            sd.append((same & is_q[:, None] & ~is_q[None, :]).astype(np.float32))
        sd.append(np.eye(c, dtype=np.float32))
        sm.append(np.stack(sd))
    return jnp.asarray(np.stack(w01), BF16), jnp.asarray(np.stack(sm), F32)


def _hgrn_ref_row(t, hs, d):
    base = t & ~(2 * hs - 1)
    return base + hs - 1 if d == 0 else base + hs


def _hgrn_is_query(t, hs, d):
    return ((t & hs) != 0) if d == 0 else ((t & hs) == 0)


def _hgrn_cums(z, lbv, w01):
    f = lbv + (1.0 - lbv) * _sigmoid(z)
    p1, p2, p3 = _split3(jnp.log2(jnp.maximum(f, F_FLOOR)))
    return 1.0 - f, _dot(w01, p1) + (_dot(w01, p2) + _dot(w01, p3))


def _hgrn_products(q, k, allc, rows, d):
    c = HGRN_CHUNK
    sub = 8
    cum = allc[0:c]
    refs = {}
    for hs in HGRN_COARSE:
        pieces = []
        for j in range(c // sub):
            rr = int(_hgrn_ref_row(np.int64(j * sub), hs, d))
            pieces.append(jnp.broadcast_to(cum[rr:rr + 1, :], (sub, HGRN_DK)))
        refs[hs] = jnp.concatenate(pieces, axis=0)
    for i, hs in enumerate(HGRN_FINE):
        refs[hs] = allc[(i + 1) * c:(i + 2) * c]
    lrow = c - 1 if d == 0 else 0
    last = cum[lrow:lrow + 1, :]
    raws = [_dot_nt(q.astype(BF16), k.astype(BF16))]
    for hs in HGRN_LEVELS:
        is_q = ((rows & hs) != 0) if d == 0 else ((rows & hs) == 0)
        x = (jnp.where(is_q, q, k) * jnp.exp2((cum - refs[hs]) * jnp.where(is_q, 1.0, -1.0))).astype(BF16)
        raws.append(_dot_nt(x, x))
    qi = (q * jnp.exp2(cum)).astype(BF16)
    ks = (k * jnp.exp2(last - cum)).astype(BF16)
    return raws, qi, ks, jnp.exp2(last)


def _hgrn_kernel(*refs, nc, has_s0, want_state):
    q_ref, v_ref, zf_ref, zb_ref, g_ref, lb_ref, ng_ref, w01_ref, sm_ref = refs[:9]
    pos = 9
    if has_s0:
        s0_ref = refs[pos]
        pos += 1
    o_ref = refs[pos]
    pos += 1
    if want_state:
        sf_ref = refs[pos]
        pos += 1
    o_scr, st_scr = refs[pos:pos + 2]
    c = HGRN_CHUNK
    l = nc * c
    for d in range(2):
        if has_s0:
            st_scr[d] = s0_ref[0, d, 0]
        else:
            st_scr[d] = jnp.zeros((HGRN_DV, HGRN_DK), F32)
    rows = lax.broadcasted_iota(jnp.int32, (c, HGRN_DK), 0)

    def body(i, carry):
        insts = []
        for u in range(2):
            for d in range(2):
                ci = 2 * i + u if d == 0 else nc - 1 - (2 * i + u)
                insts.append((d, pl.multiple_of(ci * c, c)))
        gates = [_hgrn_cums((zf_ref, zb_ref)[d][0, pl.ds(r, c), :], lb_ref[d, 0], w01_ref[d]) for d, r in insts]
        prods = [_hgrn_products(q_ref[0, pl.ds(r, c), :], k, allc, rows, d)
                 for (d, r), (k, allc) in zip(insts, gates)]
        nl = len(HGRN_LEVELS)
        for (d, r), (raws, qi, ks, dec) in zip(insts, prods):
            scores = sm_ref[d, nl] * raws[0]
            for li in range(nl):
                scores = scores + sm_ref[d, li] * raws[li + 1]
            vb = v_ref[0, pl.ds(r, c), :].astype(BF16)
            st = st_scr[d]
            o_scr[d, pl.ds(r, c), :] = _dot(scores.astype(BF16), vb) + _dot_nt(qi, st.astype(BF16))
            st_scr[d] = dec * st + _dot_tn(vb, ks)
        return carry

    lax.fori_loop(0, nc // 2, body, 0)
    if want_state:
        for d in range(2):
            sf_ref[0, d, 0] = st_scr[d]

    fb = min(l, 256)

    def finish(i, carry):
        r = pl.multiple_of(i * fb, fb)
        tot = o_scr[0, pl.ds(r, fb), :] + o_scr[1, pl.ds(r, fb), :]
        ms = jnp.mean(tot * tot, axis=-1, keepdims=True)
        y = tot * lax.rsqrt(ms + NORM_EPS) * ng_ref[...]
        o_ref[0, pl.ds(r, fb), :] = (y * _silu(g_ref[0, pl.ds(r, fb), :])).astype(BF16)
        return carry

    lax.fori_loop(0, l // fb, finish, 0)


def _hgrn_branch(proj, lower_bound, norm_g, tables, s0_t, want_state):
    b, l, _ = proj.shape
    nc = l // HGRN_CHUNK
    w = HGRN_DK
    has_s0 = s0_t is not None
    w01, sm = tables

    def col(off):
        return lambda bi, h: (bi, 0, off // w + h)

    in_specs = [pl.BlockSpec((1, l, w), col(COL_QD)),
                pl.BlockSpec((1, l, w), col(COL_ID)),
                pl.BlockSpec((1, l, w), col(COL_ZF)),
                pl.BlockSpec((1, l, w), col(COL_ZB)),
                pl.BlockSpec((1, l, w), col(COL_GD)),
                pl.BlockSpec((2, 1, 1, w), lambda bi, h: (0, h, 0, 0)),
                pl.BlockSpec((1, w), lambda bi, h: (0, 0)),
                pl.BlockSpec(w01.shape, lambda bi, h: (0, 0, 0)),
                pl.BlockSpec(sm.shape, lambda bi, h: (0, 0, 0, 0))]
    args = [proj, proj, proj, proj, proj, lower_bound.reshape(2, HGRN_HEADS, 1, w), norm_g.reshape(1, w),
            w01, sm]
    st_spec = pl.BlockSpec((1, 2, 1, HGRN_DV, HGRN_DK), lambda bi, h: (bi, 0, h, 0, 0))
    if has_s0:
        in_specs.append(st_spec)
        args.append(s0_t)
    out_specs = [pl.BlockSpec((1, l, w), lambda bi, h: (bi, 0, h))]
    out_shape = [jax.ShapeDtypeStruct((b, l, HGRN_HEADS * HGRN_DV), BF16)]
    if want_state:
        out_specs.append(st_spec)
        out_shape.append(jax.ShapeDtypeStruct((b, 2, HGRN_HEADS, HGRN_DV, HGRN_DK), F32))
    res = pl.pallas_call(
        functools.partial(_hgrn_kernel, nc=nc, has_s0=has_s0, want_state=want_state),
        grid=(b, HGRN_HEADS),
        in_specs=in_specs, out_specs=out_specs, out_shape=out_shape,
        scratch_shapes=[pltpu.VMEM((2, l, HGRN_DV), F32), pltpu.VMEM((2, HGRN_DV, HGRN_DK), F32)],
        compiler_params=_cparams(2),
        name="hgrn",
    )(*args)
    return res if want_state else (res[0], None)


def _final_kernel(x_ref, gt_ref, ma_ref, mb_ref, mc_ref, md_ref, ya_ref, yb_ref, yc_ref, yd_ref,
                  wa_ref, wb_ref, wc_ref, wd_ref, wo_ref, gp_ref, o_ref, acc, *, nct):
    ci = pl.program_id(2)

    @pl.when(ci == 0)
    def _():
        acc[...] = jnp.zeros_like(acc)

    tm = acc.shape[0]
    for r0 in range(0, tm, FINAL_ROWS):
        rs = slice(r0, r0 + FINAL_ROWS)
        mixed = _gate_sigmoid(ma_ref[0, rs, :]) * _dot(ya_ref[0, rs, :], wa_ref[...])
        mixed = mixed + _gate_sigmoid(mb_ref[0, rs, :]) * _dot(yb_ref[0, rs, :], wb_ref[...])
        mixed = mixed + _gate_sigmoid(mc_ref[0, rs, :]) * _dot(yc_ref[0, rs, :], wc_ref[...])
        mixed = mixed + _gate_sigmoid(md_ref[0, rs, :]) * _dot(yd_ref[0, rs, :], wd_ref[...])
        acc[rs, :] += _dot(mixed.astype(BF16), wo_ref[...])

    @pl.when(ci == nct - 1)
    def _():
        out = acc[...]
        ms = jnp.mean(out * out, axis=-1, keepdims=True)
        y = out * lax.rsqrt(ms + NORM_EPS) * gp_ref[...]
        o_ref[0] = x_ref[0] + gt_ref[0] * y


def _final(x, gate, proj, ys, w_projs, w_out, norm_post, tm, tc):
    b, l, d = x.shape
    nct = d // tc
    per_batch = gate.shape[0] > 1
    gate_map = (lambda bi, i, c: (bi, 0, 0)) if per_batch else (lambda bi, i, c: (0, 0, 0))

    def mspec(j):
        return pl.BlockSpec((1, tm, tc), lambda bi, i, c: (bi, i, (COL_M + j * d) // tc + c))

    in_specs = [pl.BlockSpec((1, tm, d), lambda bi, i, c: (bi, i, 0)),
                pl.BlockSpec((1, 1, d), gate_map)]
    in_specs += [mspec(j) for j in range(N_BRANCHES)]
    in_specs += [pl.BlockSpec((1, tm, y.shape[-1]), lambda bi, i, c: (bi, i, 0)) for y in ys]
    in_specs += [pl.BlockSpec((w.shape[0], tc), lambda bi, i, c: (0, c)) for w in w_projs]
    in_specs += [pl.BlockSpec((tc, d), lambda bi, i, c: (c, 0)),
                 pl.BlockSpec((1, d), lambda bi, i, c: (0, 0))]
    return pl.pallas_call(
        functools.partial(_final_kernel, nct=nct),
        grid=(b, l // tm, nct),
        in_specs=in_specs,
        out_specs=pl.BlockSpec((1, tm, d), lambda bi, i, c: (bi, i, 0)),
        out_shape=jax.ShapeDtypeStruct((b, l, d), F32),
        scratch_shapes=[pltpu.VMEM((tm, d), F32)],
        compiler_params=_cparams(3),
        name="merge_out",
    )(x, gate, proj, proj, proj, proj, *ys, *w_projs, w_out, norm_post.reshape(1, d))


def _trunk_layer(x, shift, scale, gate, p, ctx, layer, consts):
    b, l, d = x.shape
    shared_mod = shift.shape[0] == 1
    tm = 1024
    if shared_mod:
        proj = _inproj(x.reshape(1, b * l, d), shift, scale, p['norm_pre'], p['w_in'], tm, 1024)
        proj = proj.reshape(b, l, IN_COLS)
    else:
        proj = _inproj(x, shift, scale, p['norm_pre'], p['w_in'], tm, 1024)

    dl, cs = consts['dft'][l]
    y_a = _fourier_branch(proj, p['fourier_w'], dl, cs)

    if ctx is None:
        y_b, s5_re, s5_im = _s5_branch(proj, p['s5_mats'], p['s5_glu_w'], p['s5_glu_b'], None, None)
        qs, kr, vb, kf = _qk_prep(proj, p['q_norm'], p['k_norm'], None, True, min(l, 512))
        y_c = _attention(qs, kr, vb, proj, None, None, layer, min(l, 256))
        y_d, hg = _hgrn_branch(proj, p['lower_bound'], p['hgrn_norm'], consts['hgrn'], None, True)
    else:
        cache_k, cache_v, st_re, st_im, s0_t = ctx
        y_b, _, _ = _s5_branch(proj, p['s5_mats'], p['s5_glu_w'], p['s5_glu_b'], st_re[:, layer], st_im[:, layer])
        qs, kr, vb = _qk_prep(proj, p['q_norm'], p['k_norm'], consts['rope'], False, min(l, 512))
        y_c = _attention(qs, kr, vb, proj, cache_k, cache_v, layer, min(l, 256))
        y_d, _ = _hgrn_branch(proj, p['lower_bound'], p['hgrn_norm'], consts['hgrn'], s0_t[:, layer], False)

    ys = (y_a, y_b, y_c, y_d)
    wps = (p['w_proj_a'], p['w_proj_b'], p['w_proj_c'], p['w_proj_d'])
    if shared_mod:
        n = b * l
        x_new = _final(x.reshape(1, n, d), gate, proj.reshape(1, n, IN_COLS),
                       tuple(y.reshape(1, n, y.shape[-1]) for y in ys), wps, p['w_out'], p['norm_post'], 512, 512)
        x_new = x_new.reshape(b, l, d)
    else:
        x_new = _final(x, gate, proj, ys, wps, p['w_out'], p['norm_post'], 512, 512)

    if ctx is None:
        v_f = proj[:, :, COL_VC:COL_VC + KV_WIDTH]
        return x_new, (kf, v_f, s5_re, s5_im, hg)
    return x_new, None


def kernel(x_prompt, x_sample, c, cache_k, cache_v, state_s5_re, state_s5_im, state_hgrn, c_ctx,
           norm_pre, norm_post, w_mod, b_mod, w_in, fourier_w,
           s5_lambda_re, s5_lambda_im, s5_log_step, s5_b_re, s5_b_im, s5_c_re, s5_c_im,
           s5_d, s5_glu_w, s5_glu_b, q_norm, k_norm, hgrn_lb_logits, hgrn_norm,
           w_proj_a, w_proj_b, w_proj_c, w_proj_d, w_out):
    depth = w_in.shape[0]
    bp, lp, d = x_prompt.shape
    bs, ls, _ = x_sample.shape

    lb_w = jax.nn.softmax(hgrn_lb_logits.astype(F32), axis=0)
    lower_bounds = jnp.cumsum(lb_w, axis=0) - lb_w[0]

    rows = ((bs + 1 + 7) // 8) * 8
    c_all = jnp.zeros((rows, d), F32).at[:bs].set(c).at[bs].set(c_ctx)
    mod = _modulation(c_all, w_mod, b_mod)

    w_in_bf = w_in.astype(BF16)
    wpa, wpb, wpc, wpd = (w.astype(BF16) for w in (w_proj_a, w_proj_b, w_proj_c, w_proj_d))
    w_out_bf = w_out.astype(BF16)

    consts = {'dft': {lp: _dft_tables(lp), ls: _dft_tables(ls)},
              'rope': _rope_tables(ls),
              'hgrn': _hgrn_tables()}
    past = cache_k.shape[2]
    ctx_k = cache_k.reshape(bs, depth, past, KV_WIDTH)
    ctx_v = cache_v.transpose(0, 1, 3, 4, 2)
    s0_t = jnp.swapaxes(state_hgrn.astype(F32), -1, -2)

    y_p, y_s = x_prompt, x_sample
    ks, vs, s5r, s5i, hg = [], [], [], [], []
    for l in range(depth):
        p = {'norm_pre': norm_pre[l].reshape(1, d), 'norm_post': norm_post[l], 'w_in': w_in_bf[l],
             'fourier_w': fourier_w[l],
             's5_mats': _s5_matrices(s5_lambda_re[l], s5_lambda_im[l], s5_log_step[l], s5_b_re[l], s5_b_im[l],
                                     s5_c_re[l], s5_c_im[l], s5_d[l].reshape(S5_GROUPS, S5_GROUP_DIM)),
             's5_glu_w': s5_glu_w[l], 's5_glu_b': s5_glu_b[l],
             'q_norm': q_norm[l], 'k_norm': k_norm[l],
             'lower_bound': lower_bounds[l], 'hgrn_norm': hgrn_norm[l],
             'w_proj_a': wpa[l], 'w_proj_b': wpb[l], 'w_proj_c': wpc[l], 'w_proj_d': wpd[l],
             'w_out': w_out_bf[l]}
        m_ctx = mod[l, bs].reshape(1, 1, 3 * d)
        sh, sc, gt = (m_ctx[..., i * d:(i + 1) * d] for i in range(3))
        y_p, (k_l, v_l, sr_l, si_l, hg_l) = _trunk_layer(y_p, sh, sc, gt, p, None, l, consts)
        ks.append(k_l.reshape(bp, lp, N_KV_HEADS, HEAD_DIM))
        vs.append(v_l.reshape(bp, lp, N_KV_HEADS, HEAD_DIM))
        s5r.append(sr_l)
        s5i.append(si_l)
        hg.append(jnp.swapaxes(hg_l, -1, -2))
        m_s = mod[l, :bs].reshape(bs, 1, 3 * d)
        sh, sc, gt = (m_s[..., i * d:(i + 1) * d] for i in range(3))
        ctx = (ctx_k, ctx_v, state_s5_re, state_s5_im, s0_t)
        y_s, _ = _trunk_layer(y_s, sh, sc, gt, p, ctx, l, consts)

    return (y_p, y_s, jnp.stack(ks, axis=1), jnp.stack(vs, axis=1),
            jnp.stack(s5r, axis=1), jnp.stack(s5i, axis=1), jnp.stack(hg, axis=1))
```

```python
import functools
import math

import jax
import jax.numpy as jnp
import numpy as np
from jax import lax
from jax.experimental import pallas as pl
from jax.experimental.pallas import tpu as pltpu

F32 = jnp.float32
BF16 = jnp.bfloat16

NORM_EPS = 1e-6
F_FLOOR = 1e-30
LOG2E = 1.4426950408889634

D_MODEL = 2048
N_BRANCHES = 4
FOURIER_WIDTH = 512
FOURIER_GROUPS = 4
FOURIER_GROUP_DIM = 128
S5_WIDTH = 512
S5_GROUP_DIM = 16
S5_GROUPS = 32
S5_STATE = 64
S5_CHUNK = 8
S5_GB = 8
S5_NBLK = S5_GROUPS // S5_GB
S5_K = S5_CHUNK * S5_GB * S5_GROUP_DIM
S5_PART = S5_GB * S5_STATE
S5_MAX_ROWS = 512
N_HEADS = 8
N_KV_HEADS = 2
HEAD_DIM = 128
Q_PER_KV = 4
ATTN_WIDTH = 1024
KV_WIDTH = 256
ROPE_THETA = 10000.0
GRID_W = 64
HGRN_HEADS = 4
HGRN_DK = 128
HGRN_DV = 128
HGRN_WIDTH = 512
HGRN_CHUNK = 64

COL_UA, COL_GA, COL_UB, COL_GB = 0, 512, 1024, 1536
COL_QC, COL_KC, COL_VC, COL_GC = 2048, 3072, 3328, 3584
COL_QD, COL_ID, COL_ZF, COL_ZB, COL_GD = 4608, 5120, 5632, 6144, 6656
COL_M = 7168
IN_COLS = 15360

FINAL_ROWS = 256
ATTN_KEY_CHUNK = 1024
INPROJ_TM = 1024
INPROJ_TN = 1536
VMEM_LIMIT = 56 * 1024 * 1024


def _cparams(n_grid):
    return pltpu.CompilerParams(dimension_semantics=("arbitrary",) * n_grid,
                                vmem_limit_bytes=VMEM_LIMIT)


def _sigmoid(x):
    return 1.0 / (1.0 + jnp.exp(-x))


def _gate_sigmoid(x):
    return 0.5 * jnp.tanh(0.5 * x) + 0.5


def _silu(x):
    return x * _gate_sigmoid(x)


def _dot(a, b):
    return jnp.dot(a, b, preferred_element_type=F32)


def _dot_nt(a, b):
    return lax.dot_general(a, b, (((1,), (1,)), ((), ())), preferred_element_type=F32)


def _dot_tn(a, b):
    return lax.dot_general(a, b, (((0,), (0,)), ((), ())), preferred_element_type=F32)


def _split2(x):
    hi = x.astype(BF16)
    lo = (x - hi.astype(F32)).astype(BF16)
    return hi, lo


def _split3(x):
    hi = x.astype(BF16)
    r = x - hi.astype(F32)
    mid = r.astype(BF16)
    lo = (r - mid.astype(F32)).astype(BF16)
    return hi, mid, lo


def _dot3(a, b):
    ah, al = _split2(a)
    bh, bl = _split2(b)
    return _dot(ah, bh) + (_dot(ah, bl) + _dot(al, bh))


def _mod_kernel(c_ref, w_ref, b_ref, o_ref):
    a = _silu(c_ref[...])
    o_ref[0] = _dot3(a, w_ref[0]) + b_ref[0]


def _modulation(c_all, w_mod, b_mod):
    depth, d, n = w_mod.shape
    rows = c_all.shape[0]
    tn = 1024
    return pl.pallas_call(
        _mod_kernel,
        grid=(depth, n // tn),
        in_specs=[pl.BlockSpec((rows, d), lambda l, j: (0, 0)),
                  pl.BlockSpec((1, d, tn), lambda l, j: (l, 0, j)),
                  pl.BlockSpec((1, 1, tn), lambda l, j: (l, 0, j))],
        out_specs=pl.BlockSpec((1, rows, tn), lambda l, j: (l, 0, j)),
        out_shape=jax.ShapeDtypeStruct((depth, rows, n), F32),
        compiler_params=_cparams(2),
        name="modulation",
    )(c_all, w_mod, b_mod.reshape(depth, 1, n))


def _inproj_kernel(x_ref, sh_ref, sc_ref, g_ref, w_ref, o_ref, ub_ref, h_scr):
    first = pl.program_id(2) == 0

    @pl.when(first)
    def _():
        x = x_ref[0]
        ms = jnp.mean(x * x, axis=-1, keepdims=True)
        y = x * lax.rsqrt(ms + NORM_EPS) * g_ref[...]
        h_scr[...] = (y * (1.0 + sc_ref[0]) + sh_ref[0]).astype(BF16)

    res = _dot(h_scr[...], w_ref[...])
    o_ref[0] = res.astype(BF16)

    @pl.when(first)
    def _():
        ub_ref[0] = res[:, COL_UB:COL_UB + S5_WIDTH]


def _inproj(x, shift, scale, g, w_bf, tm, tn):
    b, l, d = x.shape
    n = w_bf.shape[1]
    assert COL_UB + S5_WIDTH <= tn
    per_batch = shift.shape[0] > 1
    mod_map = (lambda bi, i, j: (bi, 0, 0)) if per_batch else (lambda bi, i, j: (0, 0, 0))
    return pl.pallas_call(
        _inproj_kernel,
        grid=(b, l // tm, n // tn),
        in_specs=[pl.BlockSpec((1, tm, d), lambda bi, i, j: (bi, i, 0)),
                  pl.BlockSpec((1, 1, d), mod_map),
                  pl.BlockSpec((1, 1, d), mod_map),
                  pl.BlockSpec((1, d), lambda bi, i, j: (0, 0)),
                  pl.BlockSpec((d, tn), lambda bi, i, j: (0, j))],
        out_specs=[pl.BlockSpec((1, tm, tn), lambda bi, i, j: (bi, i, j)),
                   pl.BlockSpec((1, tm, S5_WIDTH), lambda bi, i, j: (bi, i, 0))],
        out_shape=[jax.ShapeDtypeStruct((b, l, n), BF16),
                   jax.ShapeDtypeStruct((b, l, S5_WIDTH), F32)],
        scratch_shapes=[pltpu.VMEM((tm, d), BF16)],
        compiler_params=_cparams(3),
        name="inproj",
    )(x, shift, scale, g, w_bf)


def _dft_tables(l):
    scale = 1.0 / math.sqrt(l * FOURIER_GROUP_DIM)
    k = np.arange(l, dtype=np.int64)
    t1 = np.arange(l // 64, dtype=np.int64)[:, None]
    t2 = np.arange(64, dtype=np.int64)[:, None]
    a = 2.0 * np.pi * ((t1 * 64 * k[None, :]) % l).astype(np.float64) / l
    bb = 2.0 * np.pi * ((t2 * k[None, :]) % l).astype(np.float64) / l
    ca, sa = jnp.asarray(np.cos(a) * scale, F32), jnp.asarray(np.sin(a) * scale, F32)
    cb, sb = jnp.asarray(np.cos(bb), F32), jnp.asarray(np.sin(bb), F32)
    dlc = (ca[:, None, :] * cb[None, :, :] - sa[:, None, :] * sb[None, :, :]).astype(BF16).reshape(l, l)
    dls = (-sa[:, None, :] * cb[None, :, :] - ca[:, None, :] * sb[None, :, :]).astype(BF16).reshape(l, l)
    c = np.arange(FOURIER_GROUP_DIM, dtype=np.int64)
    ang = 2.0 * np.pi * ((c[:, None] * c[None, :]) % FOURIER_GROUP_DIM) / FOURIER_GROUP_DIM
    cs = jnp.asarray(np.concatenate([np.cos(ang), np.sin(ang)], axis=1), F32).astype(BF16)
    return (dlc, dls), cs


def _fourier_kernel(u_ref, g_ref, cs_ref, dlc_ref, dls_ref, w_ref, o_ref, ucs_scr, *, l):
    gd = FOURIER_GROUP_DIM

    @pl.when(pl.program_id(1) == 0)
    def _():
        for g in range(FOURIER_GROUPS):
            ug = u_ref[0, :, g * gd:(g + 1) * gd].astype(BF16)
            t = _dot(ug, cs_ref[...])
            ucs_scr[0:l, g * gd:(g + 1) * gd] = t[:, :gd].astype(BF16)
            ucs_scr[l:2 * l, g * gd:(g + 1) * gd] = t[:, gd:].astype(BF16)

    f = _dot(dlc_ref[...], ucs_scr[0:l, :]) + _dot(dls_ref[...], ucs_scr[l:2 * l, :])
    gate = _silu(g_ref[0].astype(F32))
    for g in range(FOURIER_GROUPS):
        y = _dot(f[:, g * gd:(g + 1) * gd].astype(BF16), w_ref[g].astype(BF16))
        o_ref[0, :, g * gd:(g + 1) * gd] = (y * gate[:, g * gd:(g + 1) * gd]).astype(BF16)


def _fourier_branch(proj, fourier_w, dl, cs):
    b, l, _ = proj.shape
    tr = min(l, 512)
    w = FOURIER_WIDTH
    return pl.pallas_call(
        functools.partial(_fourier_kernel, l=l),
        grid=(b, l // tr),
        in_specs=[pl.BlockSpec((1, l, w), lambda bi, r: (bi, 0, COL_UA // w)),
                  pl.BlockSpec((1, tr, w), lambda bi, r: (bi, r, COL_GA // w)),
                  pl.BlockSpec((FOURIER_GROUP_DIM, 2 * FOURIER_GROUP_DIM), lambda bi, r: (0, 0)),
                  pl.BlockSpec((tr, l), lambda bi, r: (r, 0)),
                  pl.BlockSpec((tr, l), lambda bi, r: (r, 0)),
                  pl.BlockSpec((FOURIER_GROUPS, FOURIER_GROUP_DIM, FOURIER_GROUP_DIM),
                               lambda bi, r: (0, 0, 0))],
        out_specs=pl.BlockSpec((1, tr, w), lambda bi, r: (bi, r, 0)),
        out_shape=jax.ShapeDtypeStruct((b, l, w), BF16),
        scratch_shapes=[pltpu.VMEM((2 * l, w), BF16)],
        compiler_params=_cparams(2),
        name="fourier",
    )(proj, proj, cs, dl[0], dl[1], fourier_w)


def _s5_matrices(lam_re, lam_im, log_step, b_re, b_im, c_re, c_im, dskip):
    t = S5_CHUNK
    g, n, p = S5_GROUPS, S5_STATE, S5_GROUP_DIM
    hp = lax.Precision.HIGHEST
    lam_re = lam_re.astype(F32)
    lam_im = lam_im.astype(F32)
    step = jnp.exp(log_step.astype(F32))[..., None]
    rho = lam_re * step
    th = lam_im * step
    mag = jnp.exp(rho)
    lb_re = mag * jnp.cos(th)
    lb_im = mag * jnp.sin(th)
    nr = lb_re - 1.0
    den = lam_re * lam_re + lam_im * lam_im
    fr = (nr * lam_re + lb_im * lam_im) / den
    fi = (lb_im * lam_re - nr * lam_im) / den
    b_re = b_re.astype(F32)
    b_im = b_im.astype(F32)
    bb_re = fr[..., None] * b_re - fi[..., None] * b_im
    bb_im = fr[..., None] * b_im + fi[..., None] * b_re
    tau = jnp.arange(t + 1, dtype=F32)
    pw_mag = jnp.exp(rho[..., None] * tau)
    pw_re = pw_mag * jnp.cos(th[..., None] * tau)
    pw_im = pw_mag * jnp.sin(th[..., None] * tau)
    c_re = c_re.astype(F32)
    c_im = c_im.astype(F32)
    ca_re = c_re[..., None] * pw_re[:, :, None] - c_im[..., None] * pw_im[:, :, None]
    ca_im = c_re[..., None] * pw_im[:, :, None] + c_im[..., None] * pw_re[:, :, None]
    kern = (jnp.einsum('dgpnt,dgnq->dgtpq', ca_re, bb_re, precision=hp)
            - jnp.einsum('dgpnt,dgnq->dgtpq', ca_im, bb_im, precision=hp))[:, :, :t]
    sig = np.arange(t)[:, None]
    ta = np.arange(t)[None, :]
    idx_f = np.clip(ta - sig, 0, t - 1)
    idx_b = np.clip(sig - ta, 0, t - 1)
    m_f = jnp.asarray((ta >= sig).astype(np.float32))[None, :, :, None, None]
    m_b = jnp.asarray((sig >= ta).astype(np.float32))[None, :, :, None, None]
    tf = kern[0][:, idx_f] * m_f
    tb = kern[1][:, idx_b] * m_b
    eye_t = jnp.asarray(np.eye(t, dtype=np.float32))[None, :, :, None, None]
    dmat = dskip.astype(F32)[:, None, None, :, None] * jnp.asarray(np.eye(p, dtype=np.float32))[None, None, None]
    tt = tf + tb + eye_t * dmat
    nb, gb = S5_NBLK, S5_GB
    eye_g = jnp.asarray(np.eye(gb, dtype=np.float32))
    tmat = (tt.reshape(nb, gb, t, t, p, p).transpose(0, 2, 1, 5, 3, 4)[:, :, :, :, :, None, :]
            * eye_g[None, None, :, None, None, :, None]).reshape(nb, S5_K, S5_K)

    def ba(d, powers):
        pr = pw_re[d][:, :, powers]
        pi = pw_im[d][:, :, powers]
        re = pr[..., None] * bb_re[d][:, :, None, :] - pi[..., None] * bb_im[d][:, :, None, :]
        im = pr[..., None] * bb_im[d][:, :, None, :] + pi[..., None] * bb_re[d][:, :, None, :]
        blk = lambda m: (m.reshape(nb, gb, n, t, p).transpose(0, 3, 1, 4, 2)[:, :, :, :, None, :]
                         * eye_g[None, None, :, None, :, None]).reshape(nb, S5_K, S5_PART)
        return blk(re), blk(im)

    baf_re, baf_im = ba(0, np.arange(t - 1, -1, -1))
    bab_re, bab_im = ba(1, np.arange(t))
    bamat = jnp.concatenate([baf_re, bab_re, baf_im, bab_im], axis=-1)

    def ca(d, powers):
        blk = lambda m: (m[..., powers].reshape(nb, gb, p, n, t).transpose(0, 3, 4, 1, 2)[:, None]
                         * eye_g[None, :, None, None, :, None]).reshape(nb, S5_PART, S5_K)
        return blk(ca_re[d]), blk(ca_im[d])

    caf_re, caf_im = ca(0, np.arange(1, t + 1))
    cab_re, cab_im = ca(1, np.arange(t, 0, -1))
    camat = jnp.concatenate([caf_re, cab_re, -caf_im, -cab_im], axis=1)
    part = lambda m: m[..., t].reshape(nb, S5_PART)
    a_pow = jnp.stack([jnp.concatenate([part(pw_re[0]), part(pw_re[1])], axis=-1),
                       jnp.concatenate([part(pw_im[0]), part(pw_im[1])], axis=-1)], axis=1)
    return tmat.astype(BF16), bamat.astype(BF16), camat.astype(BF16), a_pow


def _s5_kernel(u_ref, t_ref, ba_ref, ca_ref, a_ref, h0_ref, y_ref, hf_ref, ub_scr, s_scr, hin_scr, y_scr,
               *, nc, bt):
    t, w, hp = S5_CHUNK, S5_GB * S5_GROUP_DIM, S5_PART
    for bi in range(bt):
        for tau in range(t):
            ub_scr[tau, pl.ds(bi, nc, stride=bt), :] = u_ref[bi, pl.ds(tau, nc, stride=t), :]
    ub = jnp.concatenate([ub_scr[tau] for tau in range(t)], axis=1).astype(BF16)
    s_scr[...] = _dot(ub, ba_ref[0])
    ar_f, ar_b = a_ref[0, 0:1, 0:hp], a_ref[0, 0:1, hp:2 * hp]
    ai_f, ai_b = a_ref[0, 1:2, 0:hp], a_ref[0, 1:2, hp:2 * hp]

    def step(i, carry):
        re_f, im_f, re_b, im_b = carry
        rf = pl.ds(pl.multiple_of(i * bt, bt), bt)
        rb = pl.ds(pl.multiple_of((nc - 1 - i) * bt, bt), bt)
        hin_scr[rf, 0:hp] = re_f
        hin_scr[rf, 2 * hp:3 * hp] = im_f
        hin_scr[rb, hp:2 * hp] = re_b
        hin_scr[rb, 3 * hp:4 * hp] = im_b
        n_re_f = ar_f * re_f - ai_f * im_f + s_scr[rf, 0:hp]
        n_im_f = ar_f * im_f + ai_f * re_f + s_scr[rf, 2 * hp:3 * hp]
        n_re_b = ar_b * re_b - ai_b * im_b + s_scr[rb, hp:2 * hp]
        n_im_b = ar_b * im_b + ai_b * re_b + s_scr[rb, 3 * hp:4 * hp]
        return n_re_f, n_im_f, n_re_b, n_im_b

    h0 = h0_ref[...].reshape(bt, 4 * hp)
    init = (h0[:, 0:hp], h0[:, 2 * hp:3 * hp], h0[:, hp:2 * hp], h0[:, 3 * hp:4 * hp])
    fin = lax.fori_loop(0, nc, step, init, unroll=8)
    hf_ref[...] = jnp.concatenate([fin[0], fin[2], fin[1], fin[3]], axis=1).reshape(hf_ref.shape)
    y = _dot(ub, t_ref[0]) + _dot(hin_scr[...].astype(BF16), ca_ref[0])
    for tau in range(t):
        y_scr[tau] = y[:, tau * w:(tau + 1) * w]
    for bi in range(bt):
        for tau in range(t):
            y_ref[bi, pl.ds(tau, nc, stride=t), :] = y_scr[tau, pl.ds(bi, nc, stride=bt), :]


def _s5_core(ub, tmat, bamat, camat, a_pow, h0):
    b, l, _ = ub.shape
    nc = l // S5_CHUNK
    w = S5_GB * S5_GROUP_DIM
    bt = b if nc * b <= S5_MAX_ROWS else 1
    if bt == 1:
        h0 = h0.reshape(S5_NBLK, b, 1, 4 * S5_PART)
        st_spec = pl.BlockSpec((1, 1, 1, 4 * S5_PART), lambda j, bi: (j, bi, 0, 0))
    else:
        st_spec = pl.BlockSpec((1, bt, 4 * S5_PART), lambda j, bi: (j, bi, 0))
    rows = nc * bt
    y, hf = pl.pallas_call(
        functools.partial(_s5_kernel, nc=nc, bt=bt),
        grid=(S5_NBLK, b // bt),
        in_specs=[pl.BlockSpec((bt, l, w), lambda j, bi: (bi, 0, j)),
                  pl.BlockSpec((1, S5_K, S5_K), lambda j, bi: (j, 0, 0)),
                  pl.BlockSpec((1, S5_K, 4 * S5_PART), lambda j, bi: (j, 0, 0)),
                  pl.BlockSpec((1, 4 * S5_PART, S5_K), lambda j, bi: (j, 0, 0)),
                  pl.BlockSpec((1, 2, 2 * S5_PART), lambda j, bi: (j, 0, 0)),
                  st_spec],
        out_specs=[pl.BlockSpec((bt, l, w), lambda j, bi: (bi, 0, j)), st_spec],
        out_shape=[jax.ShapeDtypeStruct((b, l, S5_WIDTH), F32),
                   jax.ShapeDtypeStruct(h0.shape, F32)],
        scratch_shapes=[pltpu.VMEM((S5_CHUNK, rows, w), F32), pltpu.VMEM((rows, 4 * S5_PART), F32),
                        pltpu.VMEM((rows, 4 * S5_PART), F32), pltpu.VMEM((S5_CHUNK, rows, w), F32)],
        compiler_params=_cparams(2),
        name="s5_core",
    )(ub, tmat, bamat, camat, a_pow, h0)
    return y, hf.reshape(S5_NBLK, b, 4 * S5_PART)


def _s5_glu_kernel(y_ref, g_ref, w_ref, b_ref, o_ref):
    y = y_ref[0]
    y = 0.5 * y * (1.0 + jnp.tanh(math.sqrt(2.0 / math.pi) * (y + 0.044715 * (y * y * y))))
    z = _dot(y.astype(BF16), w_ref[...].astype(BF16)) + b_ref[...]
    o_ref[0] = (y * _sigmoid(z) * _silu(g_ref[0].astype(F32))).astype(BF16)


def _s5_glu(y, proj, glu_w, glu_b, tm):
    b, l, w = y.shape
    return pl.pallas_call(
        _s5_glu_kernel,
        grid=(b, l // tm),
        in_specs=[pl.BlockSpec((1, tm, w), lambda bi, i: (bi, i, 0)),
                  pl.BlockSpec((1, tm, w), lambda bi, i: (bi, i, COL_GB // w)),
                  pl.BlockSpec((w, w), lambda bi, i: (0, 0)),
                  pl.BlockSpec((1, w), lambda bi, i: (0, 0))],
        out_specs=pl.BlockSpec((1, tm, w), lambda bi, i: (bi, i, 0)),
        out_shape=jax.ShapeDtypeStruct((b, l, w), BF16),
        compiler_params=_cparams(2),
        name="s5_glu",
    )(y, proj, glu_w, glu_b.reshape(1, w))


def _s5_branch(proj, ub, mats, glu_w, glu_b, h0_re, h0_im):
    b, l, _ = proj.shape
    g, n = S5_GROUPS, S5_STATE
    if h0_re is None:
        h0 = jnp.zeros((S5_NBLK, b, 4 * S5_PART), F32)
    else:
        blk = lambda h: h.astype(F32).reshape(b, S5_NBLK, S5_PART).transpose(1, 0, 2)
        h0 = jnp.concatenate([blk(h0_re[:, 0]), blk(h0_re[:, 1]), blk(h0_im[:, 0]), blk(h0_im[:, 1])], axis=-1)
    y, hf = _s5_core(ub, *mats, h0)
    out = _s5_glu(y, proj, glu_w, glu_b, min(l, 1024))
    part = lambda i: hf[..., i * S5_PART:(i + 1) * S5_PART].transpose(1, 0, 2).reshape(b, g, n)
    fin_re = jnp.stack([part(0), part(1)], axis=1)
    fin_im = jnp.stack([part(2), part(3)], axis=1)
    return out, fin_re, fin_im


def _rope_tables(l):
    rows = l // GRID_W
    row = jnp.broadcast_to(jnp.arange(rows, dtype=F32)[:, None], (rows, GRID_W)).reshape(-1)
    col = jnp.broadcast_to(jnp.arange(GRID_W, dtype=F32)[None, :], (rows, GRID_W)).reshape(-1)
    half = HEAD_DIM // 2
    inv = ROPE_THETA ** (-jnp.arange(0, half, 2, dtype=F32) / half)
    ar = row[:, None] * inv
    ac = col[:, None] * inv
    cos = jnp.concatenate([jnp.cos(ar), jnp.cos(ar), jnp.cos(ac), jnp.cos(ac)], axis=1)
    sin = jnp.concatenate([-jnp.sin(ar), jnp.sin(ar), -jnp.sin(ac), jnp.sin(ac)], axis=1)
    return cos, sin


def _qk_kernel(*refs, rope, want_kf):
    q_ref, k_ref, v_ref, qn_ref, kn_ref = refs[:5]
    pos = 5
    if rope:
        cos_ref, sin_ref = refs[pos:pos + 2]
        pos += 2
    qo_ref, ko_ref, vo_ref = refs[pos:pos + 3]
    pos += 3
    if want_kf:
        kf_ref = refs[pos]
    hd = HEAD_DIM
    tm = q_ref.shape[1]
    quarter = hd // 4
    if rope:
        cos = cos_ref[...]
        sin = sin_ref[...]
        first = (lax.broadcasted_iota(jnp.int32, (tm, hd), 1) % (2 * quarter)) < quarter

    def norm(x, gain):
        ms = jnp.mean(x * x, axis=-1, keepdims=True)
        return x * lax.rsqrt(ms + NORM_EPS) * gain

    def rot(x):
        if not rope:
            return x
        nxt = pltpu.roll(x, hd - quarter, 1)
        prv = pltpu.roll(x, quarter, 1)
        return x * cos + jnp.where(first, nxt, prv) * sin

    qscale = (hd ** -0.5) * LOG2E
    for h in range(N_HEADS):
        x = norm(q_ref[0, :, h * hd:(h + 1) * hd].astype(F32), qn_ref[...])
        qo_ref[0, :, h * hd:(h + 1) * hd] = (rot(x) * qscale).astype(BF16)
    for h in range(N_KV_HEADS):
        x = norm(k_ref[0, :, h * hd:(h + 1) * hd].astype(F32), kn_ref[...])
        if want_kf:
            kf_ref[0, :, h * hd:(h + 1) * hd] = x
        ko_ref[0, :, h * hd:(h + 1) * hd] = rot(x).astype(BF16)
        vo_ref[0, h] = v_ref[0, :, h * hd:(h + 1) * hd].astype(F32).T.astype(BF16)


def _qk_prep(proj, q_norm, k_norm, rope_tabs, want_kf, tm):
    b, l, _ = proj.shape
    rope = rope_tabs is not None
    in_specs = [pl.BlockSpec((1, tm, ATTN_WIDTH), lambda bi, i: (bi, i, COL_QC // ATTN_WIDTH)),
                pl.BlockSpec((1, tm, KV_WIDTH), lambda bi, i: (bi, i, COL_KC // KV_WIDTH)),
                pl.BlockSpec((1, tm, KV_WIDTH), lambda bi, i: (bi, i, COL_VC // KV_WIDTH)),
                pl.BlockSpec((1, HEAD_DIM), lambda bi, i: (0, 0)),
                pl.BlockSpec((1, HEAD_DIM), lambda bi, i: (0, 0))]
    args = [proj, proj, proj, q_norm.reshape(1, HEAD_DIM), k_norm.reshape(1, HEAD_DIM)]
    if rope:
        in_specs += [pl.BlockSpec((tm, HEAD_DIM), lambda bi, i: (i, 0))] * 2
        args += list(rope_tabs)
    out_specs = [pl.BlockSpec((1, tm, ATTN_WIDTH), lambda bi, i: (bi, i, 0)),
                 pl.BlockSpec((1, tm, KV_WIDTH), lambda bi, i: (bi, i, 0)),
                 pl.BlockSpec((1, N_KV_HEADS, HEAD_DIM, tm), lambda bi, i: (bi, 0, 0, i))]
    out_shape = [jax.ShapeDtypeStruct((b, l, ATTN_WIDTH), BF16),
                 jax.ShapeDtypeStruct((b, l, KV_WIDTH), BF16),
                 jax.ShapeDtypeStruct((b, N_KV_HEADS, HEAD_DIM, l), BF16)]
    if want_kf:
        out_specs.append(pl.BlockSpec((1, tm, KV_WIDTH), lambda bi, i: (bi, i, 0)))
        out_shape.append(jax.ShapeDtypeStruct((b, l, KV_WIDTH), F32))
    return pl.pallas_call(
        functools.partial(_qk_kernel, rope=rope, want_kf=want_kf),
        grid=(b, l // tm),
        in_specs=in_specs, out_specs=out_specs, out_shape=out_shape,
        compiler_params=_cparams(2),
        name="qk_prep",
    )(*args)


def _attn_kernel(*refs, has_ctx):
    s_scr = refs[-1]
    q_ref, k_ref, v_ref = refs[:3]
    if has_ctx:
        ck_ref, cv_ref, g_ref, o_ref = refs[3:7]
        ck = ck_ref[0, 0].astype(BF16)
        cvt = cv_ref[0, 0, 0].astype(BF16)
        past = ck.shape[0]
    else:
        g_ref, o_ref = refs[3:5]
    hd = HEAD_DIM
    tq = q_ref.shape[1]
    lk = k_ref.shape[1]
    cr = min(lk, ATTN_KEY_CHUNK)
    chunks = [(c * cr, cr, (lambda c=c: k_ref[0, c * cr:(c + 1) * cr, :]),
               (lambda c=c: v_ref[0, 0, :, c * cr:(c + 1) * cr])) for c in range(lk // cr)]
    if has_ctx:
        chunks.append((lk, past, lambda: ck, lambda: cvt))
    qs = [q_ref[0, :, g * hd:(g + 1) * hd] for g in range(Q_PER_KV)]
    gate = _silu(g_ref[0].astype(F32))

    def score(g, chunk, m):
        off, rows, keys, _ = chunk
        s = _dot_nt(keys(), qs[g])
        s_scr[g % 2, off:off + rows, :] = s
        return jnp.maximum(m, jnp.max(s, axis=0, keepdims=True))

    def accumulate(g, chunk, m, den, ot):
        off, rows, _, values_t = chunk
        p = jnp.exp2(s_scr[g % 2, off:off + rows, :] - m)
        return den + jnp.sum(p, axis=0, keepdims=True), ot + _dot(values_t(), p.astype(BF16))

    neg = jnp.full((1, tq), -jnp.inf, F32)
    m_cur = neg
    for chunk in chunks:
        m_cur = score(0, chunk, m_cur)
    for g in range(Q_PER_KV):
        m_next = neg
        den = jnp.zeros((1, tq), F32)
        ot = jnp.zeros((hd, tq), F32)
        for chunk in chunks:
            if g + 1 < Q_PER_KV:
                m_next = score(g + 1, chunk, m_next)
            den, ot = accumulate(g, chunk, m_cur, den, ot)
        m_cur = m_next
        o = (ot / den).T
        o_ref[0, :, g * hd:(g + 1) * hd] = (o * gate[:, g * hd:(g + 1) * hd]).astype(BF16)


def _attention(qs, kr, vt, proj, ctx_k, ctx_v, layer, tq):
    b, l, _ = qs.shape
    gw = Q_PER_KV * HEAD_DIM
    has_ctx = ctx_k is not None
    in_specs = [pl.BlockSpec((1, tq, gw), lambda bi, h, i: (bi, i, h)),
                pl.BlockSpec((1, l, HEAD_DIM), lambda bi, h, i: (bi, 0, h)),
                pl.BlockSpec((1, 1, HEAD_DIM, l), lambda bi, h, i: (bi, h, 0, 0))]
    args = [qs, kr, vt]
    if has_ctx:
        past = ctx_k.shape[2]
        in_specs += [pl.BlockSpec((1, 1, past, HEAD_DIM), lambda bi, h, i: (bi, layer, 0, h)),
                     pl.BlockSpec((1, 1, 1, HEAD_DIM, past), lambda bi, h, i: (bi, layer, h, 0, 0))]
        args += [ctx_k, ctx_v]
    in_specs.append(pl.BlockSpec((1, tq, gw), lambda bi, h, i: (bi, i, COL_GC // gw + h)))
    args.append(proj)
    return pl.pallas_call(
        functools.partial(_attn_kernel, has_ctx=has_ctx),
        grid=(b, N_KV_HEADS, l // tq),
        in_specs=in_specs,
        out_specs=pl.BlockSpec((1, tq, gw), lambda bi, h, i: (bi, i, h)),
        out_shape=jax.ShapeDtypeStruct((b, l, ATTN_WIDTH), BF16),
        scratch_shapes=[pltpu.VMEM((2, l + (ctx_k.shape[2] if has_ctx else 0), tq), F32)],
        compiler_params=_cparams(3),
        name="attention",
    )(*args)


HGRN_COARSE = (32, 16, 8, 4)
HGRN_FINE = (2, 1)
HGRN_LEVELS = HGRN_COARSE + HGRN_FINE


def _hgrn_tables():
    c = HGRN_CHUNK
    t = np.arange(c)
    w01, sm = [], []
    for d in range(2):
        cum = (t[None, :] <= t[:, None]) if d == 0 else (t[None, :] >= t[:, None])
        cum = cum.astype(np.float32)
        w01.append(np.concatenate([cum] + [cum[_hgrn_ref_row(t, hs, d)] for hs in HGRN_FINE], axis=0))
        sd = []
        for hs in HGRN_LEVELS:
            is_q = _hgrn_is_query(t, hs, d)
            same = (t[:, None] // (2 * hs)) == (t[None, :] // (2 * hs))
            sd.append((same & is_q[:, None] & ~is_q[None, :]).astype(np.float32))
        sd.append(np.eye(c, dtype=np.float32))
        sm.append(np.stack(sd))
    return jnp.asarray(np.stack(w01), BF16), jnp.asarray(np.stack(sm), F32)


def _hgrn_ref_row(t, hs, d):
    base = t & ~(2 * hs - 1)
    return base + hs - 1 if d == 0 else base + hs


def _hgrn_is_query(t, hs, d):
    return ((t & hs) != 0) if d == 0 else ((t & hs) == 0)


def _hgrn_cums(z, lbv, w01):
    f = lbv + (1.0 - lbv) * _sigmoid(z)
    p1, p2, p3 = _split3(jnp.log2(jnp.maximum(f, F_FLOOR)))
    return 1.0 - f, _dot(w01, p1) + (_dot(w01, p2) + _dot(w01, p3))


def _hgrn_products(q, k, allc, rows, d):
    c = HGRN_CHUNK
    sub = 8
    cum = allc[0:c]
    refs = {}
    for hs in HGRN_COARSE:
        pieces = []
        for j in range(c // sub):
            rr = int(_hgrn_ref_row(np.int64(j * sub), hs, d))
            pieces.append(jnp.broadcast_to(cum[rr:rr + 1, :], (sub, HGRN_DK)))
        refs[hs] = jnp.concatenate(pieces, axis=0)
    for i, hs in enumerate(HGRN_FINE):
        refs[hs] = allc[(i + 1) * c:(i + 2) * c]
    lrow = c - 1 if d == 0 else 0
    last = cum[lrow:lrow + 1, :]
    raws = [_dot_nt(q.astype(BF16), k.astype(BF16))]
    for hs in HGRN_LEVELS:
        is_q = ((rows & hs) != 0) if d == 0 else ((rows & hs) == 0)
        x = (jnp.where(is_q, q, k) * jnp.exp2((cum - refs[hs]) * jnp.where(is_q, 1.0, -1.0))).astype(BF16)
        raws.append(_dot_nt(x, x))
    qi = (q * jnp.exp2(cum)).astype(BF16)
    ks = (k * jnp.exp2(last - cum)).astype(BF16)
    return raws, qi, ks, jnp.exp2(last)


def _hgrn_kernel(*refs, nc, has_s0, want_state):
    q_ref, v_ref, zf_ref, zb_ref, g_ref, lb_ref, ng_ref, w01_ref, sm_ref = refs[:9]
    pos = 9
    if has_s0:
        s0_ref = refs[pos]
        pos += 1
    o_ref = refs[pos]
    pos += 1
    if want_state:
        sf_ref = refs[pos]
        pos += 1
    o_scr, st_scr = refs[pos:pos + 2]
    c = HGRN_CHUNK
    l = nc * c
    for d in range(2):
        if has_s0:
            st_scr[d] = s0_ref[0, d, 0]
        else:
            st_scr[d] = jnp.zeros((HGRN_DV, HGRN_DK), F32)
    rows = lax.broadcasted_iota(jnp.int32, (c, HGRN_DK), 0)

    def body(i, carry):
        insts = []
        for u in range(2):
            for d in range(2):
                ci = 2 * i + u if d == 0 else nc - 1 - (2 * i + u)
                insts.append((d, pl.multiple_of(ci * c, c)))
        gates = [_hgrn_cums((zf_ref, zb_ref)[d][0, pl.ds(r, c), :].astype(F32), lb_ref[d, 0], w01_ref[d])
                 for d, r in insts]
        prods = [_hgrn_products(q_ref[0, pl.ds(r, c), :].astype(F32), k, allc, rows, d)
                 for (d, r), (k, allc) in zip(insts, gates)]
        nl = len(HGRN_LEVELS)
        for (d, r), (raws, qi, ks, dec) in zip(insts, prods):
            scores = sm_ref[d, nl] * raws[0]
            for li in range(nl):
                scores = scores + sm_ref[d, li] * raws[li + 1]
            vb = v_ref[0, pl.ds(r, c), :].astype(BF16)
            st = st_scr[d]
            o_scr[d, pl.ds(r, c), :] = _dot(scores.astype(BF16), vb) + _dot_nt(qi, st.astype(BF16))
            st_scr[d] = dec * st + _dot_tn(vb, ks)
        return carry

    lax.fori_loop(0, nc // 2, body, 0)
    if want_state:
        for d in range(2):
            sf_ref[0, d, 0] = st_scr[d]

    fb = min(l, 256)

    def finish(i, carry):
        r = pl.multiple_of(i * fb, fb)
        tot = o_scr[0, pl.ds(r, fb), :] + o_scr[1, pl.ds(r, fb), :]
        ms = jnp.mean(tot * tot, axis=-1, keepdims=True)
        y = tot * lax.rsqrt(ms + NORM_EPS) * ng_ref[...]
        o_ref[0, pl.ds(r, fb), :] = (y * _silu(g_ref[0, pl.ds(r, fb), :].astype(F32))).astype(BF16)
        return carry

    lax.fori_loop(0, l // fb, finish, 0)


def _hgrn_branch(proj, lower_bound, norm_g, tables, s0_t, want_state):
    b, l, _ = proj.shape
    nc = l // HGRN_CHUNK
    w = HGRN_DK
    has_s0 = s0_t is not None
    w01, sm = tables

    def col(off):
        return lambda bi, h: (bi, 0, off // w + h)

    in_specs = [pl.BlockSpec((1, l, w), col(COL_QD)),
                pl.BlockSpec((1, l, w), col(COL_ID)),
                pl.BlockSpec((1, l, w), col(COL_ZF)),
                pl.BlockSpec((1, l, w), col(COL_ZB)),
                pl.BlockSpec((1, l, w), col(COL_GD)),
                pl.BlockSpec((2, 1, 1, w), lambda bi, h: (0, h, 0, 0)),
                pl.BlockSpec((1, w), lambda bi, h: (0, 0)),
                pl.BlockSpec(w01.shape, lambda bi, h: (0, 0, 0)),
                pl.BlockSpec(sm.shape, lambda bi, h: (0, 0, 0, 0))]
    args = [proj, proj, proj, proj, proj, lower_bound.reshape(2, HGRN_HEADS, 1, w), norm_g.reshape(1, w),
            w01, sm]
    st_spec = pl.BlockSpec((1, 2, 1, HGRN_DV, HGRN_DK), lambda bi, h: (bi, 0, h, 0, 0))
    if has_s0:
        in_specs.append(st_spec)
        args.append(s0_t)
    out_specs = [pl.BlockSpec((1, l, w), lambda bi, h: (bi, 0, h))]
    out_shape = [jax.ShapeDtypeStruct((b, l, HGRN_HEADS * HGRN_DV), BF16)]
    if want_state:
        out_specs.append(st_spec)
        out_shape.append(jax.ShapeDtypeStruct((b, 2, HGRN_HEADS, HGRN_DV, HGRN_DK), F32))
    res = pl.pallas_call(
        functools.partial(_hgrn_kernel, nc=nc, has_s0=has_s0, want_state=want_state),
        grid=(b, HGRN_HEADS),
        in_specs=in_specs, out_specs=out_specs, out_shape=out_shape,
        scratch_shapes=[pltpu.VMEM((2, l, HGRN_DV), F32), pltpu.VMEM((2, HGRN_DV, HGRN_DK), F32)],
        compiler_params=_cparams(2),
        name="hgrn",
    )(*args)
    return res if want_state else (res[0], None)


def _final_kernel(x_ref, gt_ref, ma_ref, mb_ref, mc_ref, md_ref, ya_ref, yb_ref, yc_ref, yd_ref,
                  wa_ref, wb_ref, wc_ref, wd_ref, wo_ref, gp_ref, o_ref, acc, *, nct):
    ci = pl.program_id(2)

    @pl.when(ci == 0)
    def _():
        acc[...] = jnp.zeros_like(acc)

    tm = acc.shape[0]
    for r0 in range(0, tm, FINAL_ROWS):
        rs = slice(r0, r0 + FINAL_ROWS)
        mixed = _gate_sigmoid(ma_ref[0, rs, :].astype(F32)) * _dot(ya_ref[0, rs, :], wa_ref[...])
        mixed = mixed + _gate_sigmoid(mb_ref[0, rs, :].astype(F32)) * _dot(yb_ref[0, rs, :], wb_ref[...])
        mixed = mixed + _gate_sigmoid(mc_ref[0, rs, :].astype(F32)) * _dot(yc_ref[0, rs, :], wc_ref[...])
        mixed = mixed + _gate_sigmoid(md_ref[0, rs, :].astype(F32)) * _dot(yd_ref[0, rs, :], wd_ref[...])
        acc[rs, :] += _dot(mixed.astype(BF16), wo_ref[...])

    @pl.when(ci == nct - 1)
    def _():
        out = acc[...]
        ms = jnp.mean(out * out, axis=-1, keepdims=True)
        y = out * lax.rsqrt(ms + NORM_EPS) * gp_ref[...]
        o_ref[0] = x_ref[0] + gt_ref[0] * y


def _final(x, gate, proj, ys, w_projs, w_out, norm_post, tm, tc):
    b, l, d = x.shape
    nct = d // tc
    per_batch = gate.shape[0] > 1
    gate_map = (lambda bi, i, c: (bi, 0, 0)) if per_batch else (lambda bi, i, c: (0, 0, 0))

    def mspec(j):
        return pl.BlockSpec((1, tm, tc), lambda bi, i, c: (bi, i, (COL_M + j * d) // tc + c))

    in_specs = [pl.BlockSpec((1, tm, d), lambda bi, i, c: (bi, i, 0)),
                pl.BlockSpec((1, 1, d), gate_map)]
    in_specs += [mspec(j) for j in range(N_BRANCHES)]
    in_specs += [pl.BlockSpec((1, tm, y.shape[-1]), lambda bi, i, c: (bi, i, 0)) for y in ys]
    in_specs += [pl.BlockSpec((w.shape[0], tc), lambda bi, i, c: (0, c)) for w in w_projs]
    in_specs += [pl.BlockSpec((tc, d), lambda bi, i, c: (c, 0)),
                 pl.BlockSpec((1, d), lambda bi, i, c: (0, 0))]
    return pl.pallas_call(
        functools.partial(_final_kernel, nct=nct),
        grid=(b, l // tm, nct),
        in_specs=in_specs,
        out_specs=pl.BlockSpec((1, tm, d), lambda bi, i, c: (bi, i, 0)),
        out_shape=jax.ShapeDtypeStruct((b, l, d), F32),
        scratch_shapes=[pltpu.VMEM((tm, d), F32)],
        compiler_params=_cparams(3),
        name="merge_out",
    )(x, gate, proj, proj, proj, proj, *ys, *w_projs, w_out, norm_post.reshape(1, d))


def _trunk_layer(x, shift, scale, gate, p, ctx, layer, consts):
    b, l, d = x.shape
    shared_mod = shift.shape[0] == 1
    if shared_mod:
        proj, ub = _inproj(x.reshape(1, b * l, d), shift, scale, p['norm_pre'], p['w_in'], INPROJ_TM, INPROJ_TN)
        proj = proj.reshape(b, l, IN_COLS)
        ub = ub.reshape(b, l, S5_WIDTH)
    else:
        proj, ub = _inproj(x, shift, scale, p['norm_pre'], p['w_in'], INPROJ_TM, INPROJ_TN)

    dl, cs = consts['dft'][l]
    y_a = _fourier_branch(proj, p['fourier_w'], dl, cs)

    if ctx is None:
        y_b, s5_re, s5_im = _s5_branch(proj, ub, p['s5_mats'], p['s5_glu_w'], p['s5_glu_b'], None, None)
        qs, kr, vt, kf = _qk_prep(proj, p['q_norm'], p['k_norm'], None, True, min(l, 512))
        y_c = _attention(qs, kr, vt, proj, None, None, layer, min(l, 256))
        y_d, hg = _hgrn_branch(proj, p['lower_bound'], p['hgrn_norm'], consts['hgrn'], None, True)
    else:
        cache_k, cache_v, st_re, st_im, s0_t = ctx
        y_b, _, _ = _s5_branch(proj, ub, p['s5_mats'], p['s5_glu_w'], p['s5_glu_b'],
                               st_re[:, layer], st_im[:, layer])
        qs, kr, vt = _qk_prep(proj, p['q_norm'], p['k_norm'], consts['rope'], False, min(l, 512))
        y_c = _attention(qs, kr, vt, proj, cache_k, cache_v, layer, min(l, 256))
        y_d, _ = _hgrn_branch(proj, p['lower_bound'], p['hgrn_norm'], consts['hgrn'], s0_t[:, layer], False)

    ys = (y_a, y_b, y_c, y_d)
    wps = (p['w_proj_a'], p['w_proj_b'], p['w_proj_c'], p['w_proj_d'])
    if shared_mod:
        n = b * l
        x_new = _final(x.reshape(1, n, d), gate, proj.reshape(1, n, IN_COLS),
                       tuple(y.reshape(1, n, y.shape[-1]) for y in ys), wps, p['w_out'], p['norm_post'], 512, 512)
        x_new = x_new.reshape(b, l, d)
    else:
        x_new = _final(x, gate, proj, ys, wps, p['w_out'], p['norm_post'], 512, 512)

    if ctx is None:
        v_f = proj[:, :, COL_VC:COL_VC + KV_WIDTH].astype(F32)
        return x_new, (kf, v_f, s5_re, s5_im, hg)
    return x_new, None


def kernel(x_prompt, x_sample, c, cache_k, cache_v, state_s5_re, state_s5_im, state_hgrn, c_ctx,
           norm_pre, norm_post, w_mod, b_mod, w_in, fourier_w,
           s5_lambda_re, s5_lambda_im, s5_log_step, s5_b_re, s5_b_im, s5_c_re, s5_c_im,
           s5_d, s5_glu_w, s5_glu_b, q_norm, k_norm, hgrn_lb_logits, hgrn_norm,
           w_proj_a, w_proj_b, w_proj_c, w_proj_d, w_out):
    depth = w_in.shape[0]
    bp, lp, d = x_prompt.shape
    bs, ls, _ = x_sample.shape

    lb_w = jax.nn.softmax(hgrn_lb_logits.astype(F32), axis=0)
    lower_bounds = jnp.cumsum(lb_w, axis=0) - lb_w[0]

    rows = ((bs + 1 + 7) // 8) * 8
    c_all = jnp.zeros((rows, d), F32).at[:bs].set(c).at[bs].set(c_ctx)
    mod = _modulation(c_all, w_mod, b_mod)

    w_in_bf = w_in.astype(BF16)
    wpa, wpb, wpc, wpd = (w.astype(BF16) for w in (w_proj_a, w_proj_b, w_proj_c, w_proj_d))
    w_out_bf = w_out.astype(BF16)

    consts = {'dft': {lp: _dft_tables(lp), ls: _dft_tables(ls)},
              'rope': _rope_tables(ls),
              'hgrn': _hgrn_tables()}
    past = cache_k.shape[2]
    ctx_k = cache_k.reshape(bs, depth, past, KV_WIDTH)
    ctx_v = cache_v.transpose(0, 1, 3, 4, 2)
    s0_t = jnp.swapaxes(state_hgrn.astype(F32), -1, -2)

    s5_mats = jax.vmap(_s5_matrices)(s5_lambda_re, s5_lambda_im, s5_log_step, s5_b_re, s5_b_im, s5_c_re, s5_c_im,
                                     s5_d.reshape(depth, S5_GROUPS, S5_GROUP_DIM))

    y_p, y_s = x_prompt, x_sample
    ks, vs, s5r, s5i, hg = [], [], [], [], []
    for l in range(depth):
        p = {'norm_pre': norm_pre[l].reshape(1, d), 'norm_post': norm_post[l], 'w_in': w_in_bf[l],
             'fourier_w': fourier_w[l],
             's5_mats': tuple(m[l] for m in s5_mats),
             's5_glu_w': s5_glu_w[l], 's5_glu_b': s5_glu_b[l],
             'q_norm': q_norm[l], 'k_norm': k_norm[l],
             'lower_bound': lower_bounds[l], 'hgrn_norm': hgrn_norm[l],
             'w_proj_a': wpa[l], 'w_proj_b': wpb[l], 'w_proj_c': wpc[l], 'w_proj_d': wpd[l],
             'w_out': w_out_bf[l]}
        m_ctx = mod[l, bs].reshape(1, 1, 3 * d)
        sh, sc, gt = (m_ctx[..., i * d:(i + 1) * d] for i in range(3))
        y_p, (k_l, v_l, sr_l, si_l, hg_l) = _trunk_layer(y_p, sh, sc, gt, p, None, l, consts)
        ks.append(k_l.reshape(bp, lp, N_KV_HEADS, HEAD_DIM))
        vs.append(v_l.reshape(bp, lp, N_KV_HEADS, HEAD_DIM))
        s5r.append(sr_l)
        s5i.append(si_l)
        hg.append(jnp.swapaxes(hg_l, -1, -2))
        m_s = mod[l, :bs].reshape(bs, 1, 3 * d)
        sh, sc, gt = (m_s[..., i * d:(i + 1) * d] for i in range(3))
        ctx = (ctx_k, ctx_v, state_s5_re, state_s5_im, s0_t)
        y_s, _ = _trunk_layer(y_s, sh, sc, gt, p, ctx, l, consts)

    return (y_p, y_s, jnp.stack(ks, axis=1), jnp.stack(vs, axis=1),
            jnp.stack(s5r, axis=1), jnp.stack(s5i, axis=1), jnp.stack(hg, axis=1))
```

```python
import functools
import math

import jax
import jax.numpy as jnp
import numpy as np
from jax import lax
from jax.experimental import pallas as pl
from jax.experimental.pallas import tpu as pltpu

F32 = jnp.float32
BF16 = jnp.bfloat16

NORM_EPS = 1e-6
F_FLOOR = 1e-30
LOG2E = 1.4426950408889634

D_MODEL = 2048
N_BRANCHES = 4
FOURIER_WIDTH = 512
FOURIER_GROUPS = 4
FOURIER_GROUP_DIM = 128
S5_WIDTH = 512
S5_GROUP_DIM = 16
S5_GROUPS = 32
S5_STATE = 64
S5_CHUNK = 8
S5_GB = 8
S5_NBLK = S5_GROUPS // S5_GB
S5_K = S5_CHUNK * S5_GB * S5_GROUP_DIM
S5_PART = S5_GB * S5_STATE
S5_MAX_ROWS = 512
N_HEADS = 8
N_KV_HEADS = 2
HEAD_DIM = 128
Q_PER_KV = 4
ATTN_WIDTH = 1024
KV_WIDTH = 256
ROPE_THETA = 10000.0
GRID_W = 64
HGRN_HEADS = 4
HGRN_DK = 128
HGRN_DV = 128
HGRN_WIDTH = 512
HGRN_CHUNK = 64

COL_UA, COL_GA, COL_UB, COL_GB = 0, 512, 1024, 1536
COL_QC, COL_KC, COL_VC, COL_GC = 2048, 3072, 3328, 3584
COL_QD, COL_ID, COL_ZF, COL_ZB, COL_GD = 4608, 5120, 5632, 6144, 6656
COL_M = 7168
IN_COLS = 15360

FINAL_ROWS = 256
ATTN_KEY_CHUNK = 1024
INPROJ_TM = 1024
INPROJ_TN = 1536
VMEM_LIMIT = 56 * 1024 * 1024


def _cparams(n_grid):
    return pltpu.CompilerParams(dimension_semantics=("arbitrary",) * n_grid,
                                vmem_limit_bytes=VMEM_LIMIT)


def _sigmoid(x):
    return 1.0 / (1.0 + jnp.exp(-x))


def _gate_sigmoid(x):
    return 0.5 * jnp.tanh(0.5 * x) + 0.5


def _silu(x):
    return x * _gate_sigmoid(x)


def _dot(a, b):
    return jnp.dot(a, b, preferred_element_type=F32)


def _dot_nt(a, b):
    return lax.dot_general(a, b, (((1,), (1,)), ((), ())), preferred_element_type=F32)


def _dot_tn(a, b):
    return lax.dot_general(a, b, (((0,), (0,)), ((), ())), preferred_element_type=F32)


def _split2(x):
    hi = x.astype(BF16)
    lo = (x - hi.astype(F32)).astype(BF16)
    return hi, lo


def _split3(x):
    hi = x.astype(BF16)
    r = x - hi.astype(F32)
    mid = r.astype(BF16)
    lo = (r - mid.astype(F32)).astype(BF16)
    return hi, mid, lo


def _dot3(a, b):
    ah, al = _split2(a)
    bh, bl = _split2(b)
    return _dot(ah, bh) + (_dot(ah, bl) + _dot(al, bh))


def _mod_kernel(c_ref, w_ref, b_ref, o_ref):
    a = _silu(c_ref[...])
    o_ref[0] = _dot3(a, w_ref[0]) + b_ref[0]


def _modulation(c_all, w_mod, b_mod):
    depth, d, n = w_mod.shape
    rows = c_all.shape[0]
    tn = 1024
    return pl.pallas_call(
        _mod_kernel,
        grid=(depth, n // tn),
        in_specs=[pl.BlockSpec((rows, d), lambda l, j: (0, 0)),
                  pl.BlockSpec((1, d, tn), lambda l, j: (l, 0, j)),
                  pl.BlockSpec((1, 1, tn), lambda l, j: (l, 0, j))],
        out_specs=pl.BlockSpec((1, rows, tn), lambda l, j: (l, 0, j)),
        out_shape=jax.ShapeDtypeStruct((depth, rows, n), F32),
        compiler_params=_cparams(2),
        name="modulation",
    )(c_all, w_mod, b_mod.reshape(depth, 1, n))


def _inproj_kernel(x_ref, sh_ref, sc_ref, g_ref, w_ref, o_ref, ub_ref, h_scr):
    first = pl.program_id(2) == 0

    @pl.when(first)
    def _():
        x = x_ref[0]
        ms = jnp.mean(x * x, axis=-1, keepdims=True)
        y = x * lax.rsqrt(ms + NORM_EPS) * g_ref[...]
        h_scr[...] = (y * (1.0 + sc_ref[0]) + sh_ref[0]).astype(BF16)

    res = _dot(h_scr[...], w_ref[0])
    o_ref[0] = res.astype(BF16)

    @pl.when(first)
    def _():
        ub_ref[0] = res[:, COL_UB:COL_UB + S5_WIDTH]


def _inproj(x, shift, scale, g, w_bf, layer, tm, tn):
    b, l, d = x.shape
    n = w_bf.shape[2]
    assert COL_UB + S5_WIDTH <= tn
    per_batch = shift.shape[0] > 1
    mod_map = (lambda bi, i, j: (bi, 0, 0)) if per_batch else (lambda bi, i, j: (0, 0, 0))
    return pl.pallas_call(
        _inproj_kernel,
        grid=(b, l // tm, n // tn),
        in_specs=[pl.BlockSpec((1, tm, d), lambda bi, i, j: (bi, i, 0)),
                  pl.BlockSpec((1, 1, d), mod_map),
                  pl.BlockSpec((1, 1, d), mod_map),
                  pl.BlockSpec((1, d), lambda bi, i, j: (0, 0)),
                  pl.BlockSpec((1, d, tn), lambda bi, i, j: (layer, 0, j))],
        out_specs=[pl.BlockSpec((1, tm, tn), lambda bi, i, j: (bi, i, j)),
                   pl.BlockSpec((1, tm, S5_WIDTH), lambda bi, i, j: (bi, i, 0))],
        out_shape=[jax.ShapeDtypeStruct((b, l, n), BF16),
                   jax.ShapeDtypeStruct((b, l, S5_WIDTH), F32)],
        scratch_shapes=[pltpu.VMEM((tm, d), BF16)],
        compiler_params=_cparams(3),
        name="inproj",
    )(x, shift, scale, g, w_bf)


def _dft_tables(l):
    scale = 1.0 / math.sqrt(l * FOURIER_GROUP_DIM)
    k = np.arange(l, dtype=np.int64)
    t1 = np.arange(l // 64, dtype=np.int64)[:, None]
    t2 = np.arange(64, dtype=np.int64)[:, None]
    a = 2.0 * np.pi * ((t1 * 64 * k[None, :]) % l).astype(np.float64) / l
    bb = 2.0 * np.pi * ((t2 * k[None, :]) % l).astype(np.float64) / l
    ca, sa = jnp.asarray(np.cos(a) * scale, F32), jnp.asarray(np.sin(a) * scale, F32)
    cb, sb = jnp.asarray(np.cos(bb), F32), jnp.asarray(np.sin(bb), F32)
    dlc = (ca[:, None, :] * cb[None, :, :] - sa[:, None, :] * sb[None, :, :]).astype(BF16).reshape(l, l)
    dls = (-sa[:, None, :] * cb[None, :, :] - ca[:, None, :] * sb[None, :, :]).astype(BF16).reshape(l, l)
    c = np.arange(FOURIER_GROUP_DIM, dtype=np.int64)
    ang = 2.0 * np.pi * ((c[:, None] * c[None, :]) % FOURIER_GROUP_DIM) / FOURIER_GROUP_DIM
    cs = jnp.asarray(np.concatenate([np.cos(ang), np.sin(ang)], axis=1), F32).astype(BF16)
    return (dlc, dls), cs


def _fourier_kernel(u_ref, g_ref, cs_ref, dlc_ref, dls_ref, w_ref, o_ref, ucs_scr, *, l):
    gd = FOURIER_GROUP_DIM

    @pl.when(pl.program_id(1) == 0)
    def _():
        for g in range(FOURIER_GROUPS):
            ug = u_ref[0, :, g * gd:(g + 1) * gd].astype(BF16)
            t = _dot(ug, cs_ref[...])
            ucs_scr[0:l, g * gd:(g + 1) * gd] = t[:, :gd].astype(BF16)
            ucs_scr[l:2 * l, g * gd:(g + 1) * gd] = t[:, gd:].astype(BF16)

    f = _dot(dlc_ref[...], ucs_scr[0:l, :]) + _dot(dls_ref[...], ucs_scr[l:2 * l, :])
    gate = _silu(g_ref[0].astype(F32))
    for g in range(FOURIER_GROUPS):
        y = _dot(f[:, g * gd:(g + 1) * gd].astype(BF16), w_ref[g].astype(BF16))
        o_ref[0, :, g * gd:(g + 1) * gd] = (y * gate[:, g * gd:(g + 1) * gd]).astype(BF16)


def _fourier_branch(proj, fourier_w, dl, cs):
    b, l, _ = proj.shape
    tr = min(l, 512)
    w = FOURIER_WIDTH
    return pl.pallas_call(
        functools.partial(_fourier_kernel, l=l),
        grid=(b, l // tr),
        in_specs=[pl.BlockSpec((1, l, w), lambda bi, r: (bi, 0, COL_UA // w)),
                  pl.BlockSpec((1, tr, w), lambda bi, r: (bi, r, COL_GA // w)),
                  pl.BlockSpec((FOURIER_GROUP_DIM, 2 * FOURIER_GROUP_DIM), lambda bi, r: (0, 0)),
                  pl.BlockSpec((tr, l), lambda bi, r: (r, 0)),
                  pl.BlockSpec((tr, l), lambda bi, r: (r, 0)),
                  pl.BlockSpec((FOURIER_GROUPS, FOURIER_GROUP_DIM, FOURIER_GROUP_DIM),
                               lambda bi, r: (0, 0, 0))],
        out_specs=pl.BlockSpec((1, tr, w), lambda bi, r: (bi, r, 0)),
        out_shape=jax.ShapeDtypeStruct((b, l, w), BF16),
        scratch_shapes=[pltpu.VMEM((2 * l, w), BF16)],
        compiler_params=_cparams(2),
        name="fourier",
    )(proj, proj, cs, dl[0], dl[1], fourier_w)


def _s5_matrices(lam_re, lam_im, log_step, b_re, b_im, c_re, c_im, dskip):
    t = S5_CHUNK
    g, n, p = S5_GROUPS, S5_STATE, S5_GROUP_DIM
    hp = lax.Precision.HIGHEST
    lam_re = lam_re.astype(F32)
    lam_im = lam_im.astype(F32)
    step = jnp.exp(log_step.astype(F32))[..., None]
    rho = lam_re * step
    th = lam_im * step
    mag = jnp.exp(rho)
    lb_re = mag * jnp.cos(th)
    lb_im = mag * jnp.sin(th)
    nr = lb_re - 1.0
    den = lam_re * lam_re + lam_im * lam_im
    fr = (nr * lam_re + lb_im * lam_im) / den
    fi = (lb_im * lam_re - nr * lam_im) / den
    b_re = b_re.astype(F32)
    b_im = b_im.astype(F32)
    bb_re = fr[..., None] * b_re - fi[..., None] * b_im
    bb_im = fr[..., None] * b_im + fi[..., None] * b_re
    tau = jnp.arange(t + 1, dtype=F32)
    pw_mag = jnp.exp(rho[..., None] * tau)
    pw_re = pw_mag * jnp.cos(th[..., None] * tau)
    pw_im = pw_mag * jnp.sin(th[..., None] * tau)
    c_re = c_re.astype(F32)
    c_im = c_im.astype(F32)
    ca_re = c_re[..., None] * pw_re[:, :, None] - c_im[..., None] * pw_im[:, :, None]
    ca_im = c_re[..., None] * pw_im[:, :, None] + c_im[..., None] * pw_re[:, :, None]
    kern = (jnp.einsum('dgpnt,dgnq->dgtpq', ca_re, bb_re, precision=hp)
            - jnp.einsum('dgpnt,dgnq->dgtpq', ca_im, bb_im, precision=hp))[:, :, :t]
    sig = np.arange(t)[:, None]
    ta = np.arange(t)[None, :]
    idx_f = np.clip(ta - sig, 0, t - 1)
    idx_b = np.clip(sig - ta, 0, t - 1)
    m_f = jnp.asarray((ta >= sig).astype(np.float32))[None, :, :, None, None]
    m_b = jnp.asarray((sig >= ta).astype(np.float32))[None, :, :, None, None]
    tf = kern[0][:, idx_f] * m_f
    tb = kern[1][:, idx_b] * m_b
    eye_t = jnp.asarray(np.eye(t, dtype=np.float32))[None, :, :, None, None]
    dmat = dskip.astype(F32)[:, None, None, :, None] * jnp.asarray(np.eye(p, dtype=np.float32))[None, None, None]
    tt = tf + tb + eye_t * dmat
    nb, gb = S5_NBLK, S5_GB
    one = np.ones
    eye = lambda k: np.eye(k, dtype=np.float32)
    exp_t = jnp.asarray(np.einsum('tu,pq,b->tpubq', eye(t), eye(p), one(gb)).reshape(t * p, S5_K))
    exp_n = jnp.asarray(np.einsum('nm,b->nbm', eye(n), one(gb)).reshape(n, S5_PART))
    mask_t = jnp.asarray(np.einsum('ab,s,q,u,p->saqubp', eye(gb), one(t), one(p), one(t), one(p))
                         .reshape(S5_K, S5_K))
    mask_ba = jnp.asarray(np.einsum('ab,s,q,n->saqbn', eye(gb), one(t), one(p), one(n)).reshape(S5_K, S5_PART))
    t_small = tt.reshape(nb, gb, t, t, p, p).transpose(0, 2, 1, 5, 3, 4).reshape(nb, S5_K, t * p)
    tmat = jnp.matmul(t_small, exp_t, precision=hp) * mask_t

    def ba(d, powers):
        pr = pw_re[d][:, :, powers]
        pi = pw_im[d][:, :, powers]
        re = pr[..., None] * bb_re[d][:, :, None, :] - pi[..., None] * bb_im[d][:, :, None, :]
        im = pr[..., None] * bb_im[d][:, :, None, :] + pi[..., None] * bb_re[d][:, :, None, :]
        blk = lambda m: jnp.matmul(m.reshape(nb, gb, n, t, p).transpose(0, 3, 1, 4, 2).reshape(nb, S5_K, n),
                                   exp_n, precision=hp) * mask_ba
        return blk(re), blk(im)

    baf_re, baf_im = ba(0, np.arange(t - 1, -1, -1))
    bab_re, bab_im = ba(1, np.arange(t))
    bamat = jnp.concatenate([baf_re, bab_re, baf_im, bab_im], axis=-1)

    def ca(d, powers):
        blk = lambda m: jnp.matmul(exp_n.T, m[..., powers].reshape(nb, gb, p, n, t).transpose(0, 3, 4, 1, 2)
                                   .reshape(nb, n, S5_K), precision=hp) * mask_ba.T
        return blk(ca_re[d]), blk(ca_im[d])

    caf_re, caf_im = ca(0, np.arange(1, t + 1))
    cab_re, cab_im = ca(1, np.arange(t, 0, -1))
    camat = jnp.concatenate([caf_re, cab_re, -caf_im, -cab_im], axis=1)
    part = lambda m: m[..., t].reshape(nb, S5_PART)
    a_pow = jnp.stack([jnp.concatenate([part(pw_re[0]), part(pw_re[1])], axis=-1),
                       jnp.concatenate([part(pw_im[0]), part(pw_im[1])], axis=-1)], axis=1)
    return tmat.astype(BF16), bamat.astype(BF16), camat.astype(BF16), a_pow


def _s5_kernel(u_ref, t_ref, ba_ref, ca_ref, a_ref, h0_ref, y_ref, hf_ref, ub_scr, s_scr, hin_scr, y_scr,
               *, nc, bt):
    t, w, hp = S5_CHUNK, S5_GB * S5_GROUP_DIM, S5_PART
    for bi in range(bt):
        for tau in range(t):
            ub_scr[tau, pl.ds(bi, nc, stride=bt), :] = u_ref[bi, pl.ds(tau, nc, stride=t), :]
    ub = jnp.concatenate([ub_scr[tau] for tau in range(t)], axis=1).astype(BF16)
    s_scr[...] = _dot(ub, ba_ref[0, 0])
    ar_f, ar_b = a_ref[0, 0, 0:1, 0:hp], a_ref[0, 0, 0:1, hp:2 * hp]
    ai_f, ai_b = a_ref[0, 0, 1:2, 0:hp], a_ref[0, 0, 1:2, hp:2 * hp]

    def step(i, carry):
        re_f, im_f, re_b, im_b = carry
        rf = pl.ds(pl.multiple_of(i * bt, bt), bt)
        rb = pl.ds(pl.multiple_of((nc - 1 - i) * bt, bt), bt)
        hin_scr[rf, 0:hp] = re_f
        hin_scr[rf, 2 * hp:3 * hp] = im_f
        hin_scr[rb, hp:2 * hp] = re_b
        hin_scr[rb, 3 * hp:4 * hp] = im_b
        n_re_f = ar_f * re_f - ai_f * im_f + s_scr[rf, 0:hp]
        n_im_f = ar_f * im_f + ai_f * re_f + s_scr[rf, 2 * hp:3 * hp]
        n_re_b = ar_b * re_b - ai_b * im_b + s_scr[rb, hp:2 * hp]
        n_im_b = ar_b * im_b + ai_b * re_b + s_scr[rb, 3 * hp:4 * hp]
        return n_re_f, n_im_f, n_re_b, n_im_b

    h0 = h0_ref[...].reshape(bt, 4 * hp)
    init = (h0[:, 0:hp], h0[:, 2 * hp:3 * hp], h0[:, hp:2 * hp], h0[:, 3 * hp:4 * hp])
    fin = lax.fori_loop(0, nc, step, init, unroll=8)
    hf_ref[...] = jnp.concatenate([fin[0], fin[2], fin[1], fin[3]], axis=1).reshape(hf_ref.shape)
    y = _dot(ub, t_ref[0, 0]) + _dot(hin_scr[...].astype(BF16), ca_ref[0, 0])
    for tau in range(t):
        y_scr[tau] = y[:, tau * w:(tau + 1) * w]
    for bi in range(bt):
        for tau in range(t):
            y_ref[bi, pl.ds(tau, nc, stride=t), :] = y_scr[tau, pl.ds(bi, nc, stride=bt), :]


def _s5_core(ub, tmat, bamat, camat, a_pow, h0, layer):
    b, l, _ = ub.shape
    nc = l // S5_CHUNK
    w = S5_GB * S5_GROUP_DIM
    bt = b if nc * b <= S5_MAX_ROWS else 1
    if bt == 1:
        h0 = h0.reshape(S5_NBLK, b, 1, 4 * S5_PART)
        st_spec = pl.BlockSpec((1, 1, 1, 4 * S5_PART), lambda j, bi: (j, bi, 0, 0))
    else:
        st_spec = pl.BlockSpec((1, bt, 4 * S5_PART), lambda j, bi: (j, bi, 0))
    rows = nc * bt
    y, hf = pl.pallas_call(
        functools.partial(_s5_kernel, nc=nc, bt=bt),
        grid=(S5_NBLK, b // bt),
        in_specs=[pl.BlockSpec((bt, l, w), lambda j, bi: (bi, 0, j)),
                  pl.BlockSpec((1, 1, S5_K, S5_K), lambda j, bi: (layer, j, 0, 0)),
                  pl.BlockSpec((1, 1, S5_K, 4 * S5_PART), lambda j, bi: (layer, j, 0, 0)),
                  pl.BlockSpec((1, 1, 4 * S5_PART, S5_K), lambda j, bi: (layer, j, 0, 0)),
                  pl.BlockSpec((1, 1, 2, 2 * S5_PART), lambda j, bi: (layer, j, 0, 0)),
                  st_spec],
        out_specs=[pl.BlockSpec((bt, l, w), lambda j, bi: (bi, 0, j)), st_spec],
        out_shape=[jax.ShapeDtypeStruct((b, l, S5_WIDTH), F32),
                   jax.ShapeDtypeStruct(h0.shape, F32)],
        scratch_shapes=[pltpu.VMEM((S5_CHUNK, rows, w), F32), pltpu.VMEM((rows, 4 * S5_PART), F32),
                        pltpu.VMEM((rows, 4 * S5_PART), F32), pltpu.VMEM((S5_CHUNK, rows, w), F32)],
        compiler_params=_cparams(2),
        name="s5_core",
    )(ub, tmat, bamat, camat, a_pow, h0)
    return y, hf.reshape(S5_NBLK, b, 4 * S5_PART)


def _s5_glu_kernel(y_ref, g_ref, w_ref, b_ref, o_ref):
    y = y_ref[0]
    y = 0.5 * y * (1.0 + jnp.tanh(math.sqrt(2.0 / math.pi) * (y + 0.044715 * (y * y * y))))
    z = _dot(y.astype(BF16), w_ref[...].astype(BF16)) + b_ref[...]
    o_ref[0] = (y * _sigmoid(z) * _silu(g_ref[0].astype(F32))).astype(BF16)


def _s5_glu(y, proj, glu_w, glu_b, tm):
    b, l, w = y.shape
    return pl.pallas_call(
        _s5_glu_kernel,
        grid=(b, l // tm),
        in_specs=[pl.BlockSpec((1, tm, w), lambda bi, i: (bi, i, 0)),
                  pl.BlockSpec((1, tm, w), lambda bi, i: (bi, i, COL_GB // w)),
                  pl.BlockSpec((w, w), lambda bi, i: (0, 0)),
                  pl.BlockSpec((1, w), lambda bi, i: (0, 0))],
        out_specs=pl.BlockSpec((1, tm, w), lambda bi, i: (bi, i, 0)),
        out_shape=jax.ShapeDtypeStruct((b, l, w), BF16),
        compiler_params=_cparams(2),
        name="s5_glu",
    )(y, proj, glu_w, glu_b.reshape(1, w))


def _s5_branch(proj, ub, mats, layer, glu_w, glu_b, h0_re, h0_im):
    b, l, _ = proj.shape
    g, n = S5_GROUPS, S5_STATE
    if h0_re is None:
        h0 = jnp.zeros((S5_NBLK, b, 4 * S5_PART), F32)
    else:
        blk = lambda h: h.astype(F32).reshape(b, S5_NBLK, S5_PART).transpose(1, 0, 2)
        h0 = jnp.concatenate([blk(h0_re[:, 0]), blk(h0_re[:, 1]), blk(h0_im[:, 0]), blk(h0_im[:, 1])], axis=-1)
    y, hf = _s5_core(ub, *mats, h0, layer)
    out = _s5_glu(y, proj, glu_w, glu_b, min(l, 1024))
    part = lambda i: hf[..., i * S5_PART:(i + 1) * S5_PART].transpose(1, 0, 2).reshape(b, g, n)
    fin_re = jnp.stack([part(0), part(1)], axis=1)
    fin_im = jnp.stack([part(2), part(3)], axis=1)
    return out, fin_re, fin_im


def _rope_tables(l):
    rows = l // GRID_W
    row = jnp.broadcast_to(jnp.arange(rows, dtype=F32)[:, None], (rows, GRID_W)).reshape(-1)
    col = jnp.broadcast_to(jnp.arange(GRID_W, dtype=F32)[None, :], (rows, GRID_W)).reshape(-1)
    half = HEAD_DIM // 2
    inv = ROPE_THETA ** (-jnp.arange(0, half, 2, dtype=F32) / half)
    ar = row[:, None] * inv
    ac = col[:, None] * inv
    cos = jnp.concatenate([jnp.cos(ar), jnp.cos(ar), jnp.cos(ac), jnp.cos(ac)], axis=1)
    sin = jnp.concatenate([-jnp.sin(ar), jnp.sin(ar), -jnp.sin(ac), jnp.sin(ac)], axis=1)
    return cos, sin


def _qk_kernel(*refs, rope, want_kf):
    q_ref, k_ref, v_ref, qn_ref, kn_ref = refs[:5]
    pos = 5
    if rope:
        cos_ref, sin_ref = refs[pos:pos + 2]
        pos += 2
    qo_ref, ko_ref, vo_ref = refs[pos:pos + 3]
    pos += 3
    if want_kf:
        kf_ref = refs[pos]
    hd = HEAD_DIM
    tm = q_ref.shape[1]
    quarter = hd // 4
    if rope:
        cos = cos_ref[...]
        sin = sin_ref[...]
        first = (lax.broadcasted_iota(jnp.int32, (tm, hd), 1) % (2 * quarter)) < quarter

    def norm(x, gain):
        ms = jnp.mean(x * x, axis=-1, keepdims=True)
        return x * lax.rsqrt(ms + NORM_EPS) * gain

    def rot(x):
        if not rope:
            return x
        nxt = pltpu.roll(x, hd - quarter, 1)
        prv = pltpu.roll(x, quarter, 1)
        return x * cos + jnp.where(first, nxt, prv) * sin

    qscale = (hd ** -0.5) * LOG2E
    for h in range(N_HEADS):
        x = norm(q_ref[0, :, h * hd:(h + 1) * hd].astype(F32), qn_ref[...])
        qo_ref[0, :, h * hd:(h + 1) * hd] = (rot(x) * qscale).astype(BF16)
    for h in range(N_KV_HEADS):
        x = norm(k_ref[0, :, h * hd:(h + 1) * hd].astype(F32), kn_ref[...])
        if want_kf:
            kf_ref[0, :, h * hd:(h + 1) * hd] = x
        ko_ref[0, :, h * hd:(h + 1) * hd] = rot(x).astype(BF16)
        vo_ref[0, h] = v_ref[0, :, h * hd:(h + 1) * hd].astype(F32).T.astype(BF16)


def _qk_prep(proj, q_norm, k_norm, rope_tabs, want_kf, tm):
    b, l, _ = proj.shape
    rope = rope_tabs is not None
    in_specs = [pl.BlockSpec((1, tm, ATTN_WIDTH), lambda bi, i: (bi, i, COL_QC // ATTN_WIDTH)),
                pl.BlockSpec((1, tm, KV_WIDTH), lambda bi, i: (bi, i, COL_KC // KV_WIDTH)),
                pl.BlockSpec((1, tm, KV_WIDTH), lambda bi, i: (bi, i, COL_VC // KV_WIDTH)),
                pl.BlockSpec((1, HEAD_DIM), lambda bi, i: (0, 0)),
                pl.BlockSpec((1, HEAD_DIM), lambda bi, i: (0, 0))]
    args = [proj, proj, proj, q_norm.reshape(1, HEAD_DIM), k_norm.reshape(1, HEAD_DIM)]
    if rope:
        in_specs += [pl.BlockSpec((tm, HEAD_DIM), lambda bi, i: (i, 0))] * 2
        args += list(rope_tabs)
    out_specs = [pl.BlockSpec((1, tm, ATTN_WIDTH), lambda bi, i: (bi, i, 0)),
                 pl.BlockSpec((1, tm, KV_WIDTH), lambda bi, i: (bi, i, 0)),
                 pl.BlockSpec((1, N_KV_HEADS, HEAD_DIM, tm), lambda bi, i: (bi, 0, 0, i))]
    out_shape = [jax.ShapeDtypeStruct((b, l, ATTN_WIDTH), BF16),
                 jax.ShapeDtypeStruct((b, l, KV_WIDTH), BF16),
                 jax.ShapeDtypeStruct((b, N_KV_HEADS, HEAD_DIM, l), BF16)]
    if want_kf:
        out_specs.append(pl.BlockSpec((1, tm, KV_WIDTH), lambda bi, i: (bi, i, 0)))
        out_shape.append(jax.ShapeDtypeStruct((b, l, KV_WIDTH), F32))
    return pl.pallas_call(
        functools.partial(_qk_kernel, rope=rope, want_kf=want_kf),
        grid=(b, l // tm),
        in_specs=in_specs, out_specs=out_specs, out_shape=out_shape,
        compiler_params=_cparams(2),
        name="qk_prep",
    )(*args)


def _attn_kernel(*refs, has_ctx):
    s_scr = refs[-1]
    q_ref, k_ref, v_ref = refs[:3]
    if has_ctx:
        ck_ref, cv_ref, g_ref, o_ref = refs[3:7]
        ck = ck_ref[0, 0].astype(BF16)
        cvt = cv_ref[0, 0, 0].astype(BF16)
        past = ck.shape[0]
    else:
        g_ref, o_ref = refs[3:5]
    hd = HEAD_DIM
    tq = q_ref.shape[1]
    lk = k_ref.shape[1]
    cr = min(lk, ATTN_KEY_CHUNK)
    chunks = [(c * cr, cr, (lambda c=c: k_ref[0, c * cr:(c + 1) * cr, :]),
               (lambda c=c: v_ref[0, 0, :, c * cr:(c + 1) * cr])) for c in range(lk // cr)]
    if has_ctx:
        chunks.append((lk, past, lambda: ck, lambda: cvt))
    qs = [q_ref[0, :, g * hd:(g + 1) * hd] for g in range(Q_PER_KV)]
    gate = _silu(g_ref[0].astype(F32))

    def score(g, chunk, m):
        off, rows, keys, _ = chunk
        s = _dot_nt(keys(), qs[g])
        s_scr[g % 2, off:off + rows, :] = s
        return jnp.maximum(m, jnp.max(s, axis=0, keepdims=True))

    def accumulate(g, chunk, m, den, ot):
        off, rows, _, values_t = chunk
        p = jnp.exp2(s_scr[g % 2, off:off + rows, :] - m)
        return den + jnp.sum(p, axis=0, keepdims=True), ot + _dot(values_t(), p.astype(BF16))

    neg = jnp.full((1, tq), -jnp.inf, F32)
    m_cur = neg
    for chunk in chunks:
        m_cur = score(0, chunk, m_cur)
    for g in range(Q_PER_KV):
        m_next = neg
        den = jnp.zeros((1, tq), F32)
        ot = jnp.zeros((hd, tq), F32)
        for chunk in chunks:
            if g + 1 < Q_PER_KV:
                m_next = score(g + 1, chunk, m_next)
            den, ot = accumulate(g, chunk, m_cur, den, ot)
        m_cur = m_next
        o = (ot / den).T
        o_ref[0, :, g * hd:(g + 1) * hd] = (o * gate[:, g * hd:(g + 1) * hd]).astype(BF16)


def _attention(qs, kr, vt, proj, ctx_k, ctx_v, layer, tq):
    b, l, _ = qs.shape
    gw = Q_PER_KV * HEAD_DIM
    has_ctx = ctx_k is not None
    in_specs = [pl.BlockSpec((1, tq, gw), lambda bi, h, i: (bi, i, h)),
                pl.BlockSpec((1, l, HEAD_DIM), lambda bi, h, i: (bi, 0, h)),
                pl.BlockSpec((1, 1, HEAD_DIM, l), lambda bi, h, i: (bi, h, 0, 0))]
    args = [qs, kr, vt]
    if has_ctx:
        past = ctx_k.shape[2]
        in_specs += [pl.BlockSpec((1, 1, past, HEAD_DIM), lambda bi, h, i: (bi, layer, 0, h)),
                     pl.BlockSpec((1, 1, 1, HEAD_DIM, past), lambda bi, h, i: (bi, layer, h, 0, 0))]
        args += [ctx_k, ctx_v]
    in_specs.append(pl.BlockSpec((1, tq, gw), lambda bi, h, i: (bi, i, COL_GC // gw + h)))
    args.append(proj)
    return pl.pallas_call(
        functools.partial(_attn_kernel, has_ctx=has_ctx),
        grid=(b, N_KV_HEADS, l // tq),
        in_specs=in_specs,
        out_specs=pl.BlockSpec((1, tq, gw), lambda bi, h, i: (bi, i, h)),
        out_shape=jax.ShapeDtypeStruct((b, l, ATTN_WIDTH), BF16),
        scratch_shapes=[pltpu.VMEM((2, l + (ctx_k.shape[2] if has_ctx else 0), tq), F32)],
        compiler_params=_cparams(3),
        name="attention",
    )(*args)


HGRN_COARSE = (32, 16, 8, 4)
HGRN_FINE = (2, 1)
HGRN_LEVELS = HGRN_COARSE + HGRN_FINE
HGRN_CHUNKS_PER_TRIP = 4


def _hgrn_tables():
    c = HGRN_CHUNK
    t = np.arange(c)
    w01, sm = [], []
    for d in range(2):
        cum = (t[None, :] <= t[:, None]) if d == 0 else (t[None, :] >= t[:, None])
        cum = cum.astype(np.float32)
        w01.append(np.concatenate([cum] + [cum[_hgrn_ref_row(t, hs, d)] for hs in HGRN_FINE], axis=0))
        sd = []
        for hs in HGRN_LEVELS:
            is_q = _hgrn_is_query(t, hs, d)
            same = (t[:, None] // (2 * hs)) == (t[None, :] // (2 * hs))
            sd.append((same & is_q[:, None] & ~is_q[None, :]).astype(np.float32))
        sd.append(np.eye(c, dtype=np.float32))
        sm.append(np.stack(sd))
    return jnp.asarray(np.stack(w01), BF16), jnp.asarray(np.stack(sm), F32)


def _hgrn_ref_row(t, hs, d):
    base = t & ~(2 * hs - 1)
    return base + hs - 1 if d == 0 else base + hs


def _hgrn_is_query(t, hs, d):
    return ((t & hs) != 0) if d == 0 else ((t & hs) == 0)


def _hgrn_cums(z, lbv, w01):
    f = lbv + (1.0 - lbv) * _sigmoid(z)
    p1, p2, p3 = _split3(jnp.log2(jnp.maximum(f, F_FLOOR)))
    return 1.0 - f, _dot(w01, p1) + (_dot(w01, p2) + _dot(w01, p3))


def _hgrn_products(q, k, allc, rows, d):
    c = HGRN_CHUNK
    sub = 8
    cum = allc[0:c]
    refs = {}
    for hs in HGRN_COARSE:
        pieces = []
        for j in range(c // sub):
            rr = int(_hgrn_ref_row(np.int64(j * sub), hs, d))
            pieces.append(jnp.broadcast_to(cum[rr:rr + 1, :], (sub, HGRN_DK)))
        refs[hs] = jnp.concatenate(pieces, axis=0)
    for i, hs in enumerate(HGRN_FINE):
        refs[hs] = allc[(i + 1) * c:(i + 2) * c]
    lrow = c - 1 if d == 0 else 0
    last = cum[lrow:lrow + 1, :]
    raws = [_dot_nt(q.astype(BF16), k.astype(BF16))]
    for hs in HGRN_LEVELS:
        is_q = ((rows & hs) != 0) if d == 0 else ((rows & hs) == 0)
        x = (jnp.where(is_q, q, k) * jnp.exp2((cum - refs[hs]) * jnp.where(is_q, 1.0, -1.0))).astype(BF16)
        raws.append(_dot_nt(x, x))
    qi = (q * jnp.exp2(cum)).astype(BF16)
    ks = (k * jnp.exp2(last - cum)).astype(BF16)
    return raws, qi, ks, jnp.exp2(last)


def _hgrn_kernel(*refs, nc, has_s0, want_state):
    q_ref, v_ref, zf_ref, zb_ref, g_ref, lb_ref, ng_ref, w01_ref, sm_ref = refs[:9]
    pos = 9
    if has_s0:
        s0_ref = refs[pos]
        pos += 1
    o_ref = refs[pos]
    pos += 1
    if want_state:
        sf_ref = refs[pos]
        pos += 1
    o_scr, st_scr = refs[pos:pos + 2]
    c = HGRN_CHUNK
    l = nc * c
    for d in range(2):
        if has_s0:
            st_scr[d] = s0_ref[0, d, 0]
        else:
            st_scr[d] = jnp.zeros((HGRN_DV, HGRN_DK), F32)
    rows = lax.broadcasted_iota(jnp.int32, (c, HGRN_DK), 0)

    per_trip = math.gcd(nc, HGRN_CHUNKS_PER_TRIP)

    def body(i, carry):
        insts = []
        for u in range(per_trip):
            for d in range(2):
                ci = per_trip * i + u if d == 0 else nc - 1 - (per_trip * i + u)
                insts.append((d, pl.multiple_of(ci * c, c)))
        gates = [_hgrn_cums((zf_ref, zb_ref)[d][0, pl.ds(r, c), :].astype(F32), lb_ref[d, 0], w01_ref[d])
                 for d, r in insts]
        prods = [_hgrn_products(q_ref[0, pl.ds(r, c), :].astype(F32), k, allc, rows, d)
                 for (d, r), (k, allc) in zip(insts, gates)]
        nl = len(HGRN_LEVELS)
        for (d, r), (raws, qi, ks, dec) in zip(insts, prods):
            scores = sm_ref[d, nl] * raws[0]
            for li in range(nl):
                scores = scores + sm_ref[d, li] * raws[li + 1]
            vb = v_ref[0, pl.ds(r, c), :].astype(BF16)
            st = st_scr[d]
            o_scr[d, pl.ds(r, c), :] = _dot(scores.astype(BF16), vb) + _dot_nt(qi, st.astype(BF16))
            st_scr[d] = dec * st + _dot_tn(vb, ks)
        return carry

    lax.fori_loop(0, nc // per_trip, body, 0)
    if want_state:
        for d in range(2):
            sf_ref[0, d, 0] = st_scr[d]

    fb = min(l, 256)

    def finish(i, carry):
        r = pl.multiple_of(i * fb, fb)
        tot = o_scr[0, pl.ds(r, fb), :] + o_scr[1, pl.ds(r, fb), :]
        ms = jnp.mean(tot * tot, axis=-1, keepdims=True)
        y = tot * lax.rsqrt(ms + NORM_EPS) * ng_ref[...]
        o_ref[0, pl.ds(r, fb), :] = (y * _silu(g_ref[0, pl.ds(r, fb), :].astype(F32))).astype(BF16)
        return carry

    lax.fori_loop(0, l // fb, finish, 0)


def _hgrn_branch(proj, lower_bound, norm_g, tables, s0_t, want_state):
    b, l, _ = proj.shape
    nc = l // HGRN_CHUNK
    w = HGRN_DK
    has_s0 = s0_t is not None
    w01, sm = tables

    def col(off):
        return lambda bi, h: (bi, 0, off // w + h)

    in_specs = [pl.BlockSpec((1, l, w), col(COL_QD)),
                pl.BlockSpec((1, l, w), col(COL_ID)),
                pl.BlockSpec((1, l, w), col(COL_ZF)),
                pl.BlockSpec((1, l, w), col(COL_ZB)),
                pl.BlockSpec((1, l, w), col(COL_GD)),
                pl.BlockSpec((2, 1, 1, w), lambda bi, h: (0, h, 0, 0)),
                pl.BlockSpec((1, w), lambda bi, h: (0, 0)),
                pl.BlockSpec(w01.shape, lambda bi, h: (0, 0, 0)),
                pl.BlockSpec(sm.shape, lambda bi, h: (0, 0, 0, 0))]
    args = [proj, proj, proj, proj, proj, lower_bound.reshape(2, HGRN_HEADS, 1, w), norm_g.reshape(1, w),
            w01, sm]
    st_spec = pl.BlockSpec((1, 2, 1, HGRN_DV, HGRN_DK), lambda bi, h: (bi, 0, h, 0, 0))
    if has_s0:
        in_specs.append(st_spec)
        args.append(s0_t)
    out_specs = [pl.BlockSpec((1, l, w), lambda bi, h: (bi, 0, h))]
    out_shape = [jax.ShapeDtypeStruct((b, l, HGRN_HEADS * HGRN_DV), BF16)]
    if want_state:
        out_specs.append(st_spec)
        out_shape.append(jax.ShapeDtypeStruct((b, 2, HGRN_HEADS, HGRN_DV, HGRN_DK), F32))
    res = pl.pallas_call(
        functools.partial(_hgrn_kernel, nc=nc, has_s0=has_s0, want_state=want_state),
        grid=(b, HGRN_HEADS),
        in_specs=in_specs, out_specs=out_specs, out_shape=out_shape,
        scratch_shapes=[pltpu.VMEM((2, l, HGRN_DV), F32), pltpu.VMEM((2, HGRN_DV, HGRN_DK), F32)],
        compiler_params=_cparams(2),
        name="hgrn",
    )(*args)
    return res if want_state else (res[0], None)


def _final_kernel(x_ref, gt_ref, ma_ref, mb_ref, mc_ref, md_ref, ya_ref, yb_ref, yc_ref, yd_ref,
                  wa_ref, wb_ref, wc_ref, wd_ref, wo_ref, gp_ref, o_ref, acc, *, nct):
    ci = pl.program_id(2)

    @pl.when(ci == 0)
    def _():
        acc[...] = jnp.zeros_like(acc)

    tm = acc.shape[0]
    subs = [slice(r0, r0 + FINAL_ROWS) for r0 in range(0, tm, FINAL_ROWS)]
    m_refs = (ma_ref, mb_ref, mc_ref, md_ref)
    y_refs = (ya_ref, yb_ref, yc_ref, yd_ref)
    w_refs = (wa_ref, wb_ref, wc_ref, wd_ref)
    dots = [[_dot(y_ref[0, rs, :], w_ref[0]) for y_ref, w_ref in zip(y_refs, w_refs)] for rs in subs]
    for rs, branch in zip(subs, dots):
        mixed = None
        for m_ref, dot in zip(m_refs, branch):
            term = _gate_sigmoid(m_ref[0, rs, :].astype(F32)) * dot
            mixed = term if mixed is None else mixed + term
        acc[rs, :] += _dot(mixed.astype(BF16), wo_ref[0])

    @pl.when(ci == nct - 1)
    def _():
        out = acc[...]
        ms = jnp.mean(out * out, axis=-1, keepdims=True)
        y = out * lax.rsqrt(ms + NORM_EPS) * gp_ref[...]
        o_ref[0] = x_ref[0] + gt_ref[0] * y


def _final(x, gate, proj, ys, w_projs, w_out, norm_post, layer, tm, tc):
    b, l, d = x.shape
    nct = d // tc
    per_batch = gate.shape[0] > 1
    gate_map = (lambda bi, i, c: (bi, 0, 0)) if per_batch else (lambda bi, i, c: (0, 0, 0))

    def mspec(j):
        return pl.BlockSpec((1, tm, tc), lambda bi, i, c: (bi, i, (COL_M + j * d) // tc + c))

    in_specs = [pl.BlockSpec((1, tm, d), lambda bi, i, c: (bi, i, 0)),
                pl.BlockSpec((1, 1, d), gate_map)]
    in_specs += [mspec(j) for j in range(N_BRANCHES)]
    in_specs += [pl.BlockSpec((1, tm, y.shape[-1]), lambda bi, i, c: (bi, i, 0)) for y in ys]
    in_specs += [pl.BlockSpec((1, w.shape[1], tc), lambda bi, i, c: (layer, 0, c)) for w in w_projs]
    in_specs += [pl.BlockSpec((1, tc, d), lambda bi, i, c: (layer, c, 0)),
                 pl.BlockSpec((1, d), lambda bi, i, c: (0, 0))]
    return pl.pallas_call(
        functools.partial(_final_kernel, nct=nct),
        grid=(b, l // tm, nct),
        in_specs=in_specs,
        out_specs=pl.BlockSpec((1, tm, d), lambda bi, i, c: (bi, i, 0)),
        out_shape=jax.ShapeDtypeStruct((b, l, d), F32),
        scratch_shapes=[pltpu.VMEM((tm, d), F32)],
        compiler_params=_cparams(3),
        name="merge_out",
    )(x, gate, proj, proj, proj, proj, *ys, *w_projs, w_out, norm_post.reshape(1, d))


def _trunk_layer(x, shift, scale, gate, p, ctx, layer, consts):
    b, l, d = x.shape
    shared_mod = shift.shape[0] == 1
    if shared_mod:
        proj, ub = _inproj(x.reshape(1, b * l, d), shift, scale, p['norm_pre'], p['w_in'], layer,
                           INPROJ_TM, INPROJ_TN)
        proj = proj.reshape(b, l, IN_COLS)
        ub = ub.reshape(b, l, S5_WIDTH)
    else:
        proj, ub = _inproj(x, shift, scale, p['norm_pre'], p['w_in'], layer, INPROJ_TM, INPROJ_TN)

    dl, cs = consts['dft'][l]
    y_a = _fourier_branch(proj, p['fourier_w'], dl, cs)

    if ctx is None:
        y_b, s5_re, s5_im = _s5_branch(proj, ub, p['s5_mats'], layer, p['s5_glu_w'], p['s5_glu_b'], None, None)
        qs, kr, vt, kf = _qk_prep(proj, p['q_norm'], p['k_norm'], None, True, min(l, 512))
        y_c = _attention(qs, kr, vt, proj, None, None, layer, min(l, 256))
        y_d, hg = _hgrn_branch(proj, p['lower_bound'], p['hgrn_norm'], consts['hgrn'], None, True)
    else:
        cache_k, cache_v, st_re, st_im, s0_t = ctx
        y_b, _, _ = _s5_branch(proj, ub, p['s5_mats'], layer, p['s5_glu_w'], p['s5_glu_b'],
                               st_re[:, layer], st_im[:, layer])
        qs, kr, vt = _qk_prep(proj, p['q_norm'], p['k_norm'], consts['rope'], False, min(l, 512))
        y_c = _attention(qs, kr, vt, proj, cache_k, cache_v, layer, min(l, 256))
        y_d, _ = _hgrn_branch(proj, p['lower_bound'], p['hgrn_norm'], consts['hgrn'], s0_t[:, layer], False)

    ys = (y_a, y_b, y_c, y_d)
    wps = (p['w_proj_a'], p['w_proj_b'], p['w_proj_c'], p['w_proj_d'])
    if shared_mod:
        n = b * l
        x_new = _final(x.reshape(1, n, d), gate, proj.reshape(1, n, IN_COLS),
                       tuple(y.reshape(1, n, y.shape[-1]) for y in ys), wps, p['w_out'], p['norm_post'], layer,
                       512, 512)
        x_new = x_new.reshape(b, l, d)
    else:
        x_new = _final(x, gate, proj, ys, wps, p['w_out'], p['norm_post'], layer, 512, 512)

    if ctx is None:
        v_f = proj[:, :, COL_VC:COL_VC + KV_WIDTH].astype(F32)
        return x_new, (kf, v_f, s5_re, s5_im, hg)
    return x_new, None


def kernel(x_prompt, x_sample, c, cache_k, cache_v, state_s5_re, state_s5_im, state_hgrn, c_ctx,
           norm_pre, norm_post, w_mod, b_mod, w_in, fourier_w,
           s5_lambda_re, s5_lambda_im, s5_log_step, s5_b_re, s5_b_im, s5_c_re, s5_c_im,
           s5_d, s5_glu_w, s5_glu_b, q_norm, k_norm, hgrn_lb_logits, hgrn_norm,
           w_proj_a, w_proj_b, w_proj_c, w_proj_d, w_out):
    depth = w_in.shape[0]
    bp, lp, d = x_prompt.shape
    bs, ls, _ = x_sample.shape

    lb_w = jax.nn.softmax(hgrn_lb_logits.astype(F32), axis=0)
    lower_bounds = jnp.cumsum(lb_w, axis=0) - lb_w[0]

    rows = ((bs + 1 + 7) // 8) * 8
    c_all = jnp.zeros((rows, d), F32).at[:bs].set(c).at[bs].set(c_ctx)
    mod = _modulation(c_all, w_mod, b_mod)

    w_in_bf = w_in.astype(BF16)
    wpa, wpb, wpc, wpd = (w.astype(BF16) for w in (w_proj_a, w_proj_b, w_proj_c, w_proj_d))
    w_out_bf = w_out.astype(BF16)

    consts = {'dft': {lp: _dft_tables(lp), ls: _dft_tables(ls)},
              'rope': _rope_tables(ls),
              'hgrn': _hgrn_tables()}
    past = cache_k.shape[2]
    ctx_k = cache_k.reshape(bs, depth, past, KV_WIDTH)
    ctx_v = cache_v.transpose(0, 1, 3, 4, 2)
    s0_t = jnp.swapaxes(state_hgrn.astype(F32), -1, -2)

    s5_mats = jax.vmap(_s5_matrices)(s5_lambda_re, s5_lambda_im, s5_log_step, s5_b_re, s5_b_im, s5_c_re, s5_c_im,
                                     s5_d.reshape(depth, S5_GROUPS, S5_GROUP_DIM))

    y_p, y_s = x_prompt, x_sample
    ks, vs, s5r, s5i, hg = [], [], [], [], []
    for l in range(depth):
        p = {'norm_pre': norm_pre[l].reshape(1, d), 'norm_post': norm_post[l], 'w_in': w_in_bf,
             'fourier_w': fourier_w[l],
             's5_mats': s5_mats,
             's5_glu_w': s5_glu_w[l], 's5_glu_b': s5_glu_b[l],
             'q_norm': q_norm[l], 'k_norm': k_norm[l],
             'lower_bound': lower_bounds[l], 'hgrn_norm': hgrn_norm[l],
             'w_proj_a': wpa, 'w_proj_b': wpb, 'w_proj_c': wpc, 'w_proj_d': wpd,
             'w_out': w_out_bf}
        m_ctx = mod[l, bs].reshape(1, 1, 3 * d)
        sh, sc, gt = (m_ctx[..., i * d:(i + 1) * d] for i in range(3))
        y_p, (k_l, v_l, sr_l, si_l, hg_l) = _trunk_layer(y_p, sh, sc, gt, p, None, l, consts)
        ks.append(k_l.reshape(bp, lp, N_KV_HEADS, HEAD_DIM))
        vs.append(v_l.reshape(bp, lp, N_KV_HEADS, HEAD_DIM))
        s5r.append(sr_l)
        s5i.append(si_l)
        hg.append(jnp.swapaxes(hg_l, -1, -2))
        m_s = mod[l, :bs].reshape(bs, 1, 3 * d)
        sh, sc, gt = (m_s[..., i * d:(i + 1) * d] for i in range(3))
        ctx = (ctx_k, ctx_v, state_s5_re, state_s5_im, s0_t)
        y_s, _ = _trunk_layer(y_s, sh, sc, gt, p, ctx, l, consts)

    return (y_p, y_s, jnp.stack(ks, axis=1), jnp.stack(vs, axis=1),
            jnp.stack(s5r, axis=1), jnp.stack(s5i, axis=1), jnp.stack(hg, axis=1))
```

```python
import functools
import math

import jax
import jax.numpy as jnp
import numpy as np
from jax import lax
from jax.experimental import pallas as pl
from jax.experimental.pallas import tpu as pltpu

F32 = jnp.float32
BF16 = jnp.bfloat16

NORM_EPS = 1e-6
F_FLOOR = 1e-30
LOG2E = 1.4426950408889634

D_MODEL = 2048
N_BRANCHES = 4
FOURIER_WIDTH = 512
FOURIER_GROUPS = 4
FOURIER_GROUP_DIM = 128
S5_WIDTH = 512
S5_GROUP_DIM = 16
S5_GROUPS = 32
S5_STATE = 64
S5_CHUNK = 8
S5_GB = 8
S5_NBLK = S5_GROUPS // S5_GB
S5_K = S5_CHUNK * S5_GB * S5_GROUP_DIM
S5_PART = S5_GB * S5_STATE
S5_MAX_ROWS = 512
N_HEADS = 8
N_KV_HEADS = 2
HEAD_DIM = 128
Q_PER_KV = 4
ATTN_WIDTH = 1024
KV_WIDTH = 256
ROPE_THETA = 10000.0
GRID_W = 64
HGRN_HEADS = 4
HGRN_DK = 128
HGRN_DV = 128
HGRN_WIDTH = 512
HGRN_CHUNK = 64

COL_UA, COL_GA, COL_UB, COL_GB = 0, 512, 1024, 1536
COL_QC, COL_KC, COL_VC, COL_GC = 2048, 3072, 3328, 3584
COL_QD, COL_ID, COL_ZF, COL_ZB, COL_GD = 4608, 5120, 5632, 6144, 6656
COL_M = 7168
IN_COLS = 15360

FINAL_ROWS = 256
ATTN_KEY_CHUNK = 1024
INPROJ_TM = 1024
INPROJ_TN = 1536
INPROJ_ROWS = 256
VMEM_LIMIT = 56 * 1024 * 1024


def _cparams(n_grid):
    return pltpu.CompilerParams(dimension_semantics=("arbitrary",) * n_grid,
                                vmem_limit_bytes=VMEM_LIMIT)


def _sigmoid(x):
    return 1.0 / (1.0 + jnp.exp(-x))


def _gate_sigmoid(x):
    return 0.5 * jnp.tanh(0.5 * x) + 0.5


def _silu(x):
    return x * _gate_sigmoid(x)


def _dot(a, b):
    return jnp.dot(a, b, preferred_element_type=F32)


def _dot_nt(a, b):
    return lax.dot_general(a, b, (((1,), (1,)), ((), ())), preferred_element_type=F32)


def _dot_tn(a, b):
    return lax.dot_general(a, b, (((0,), (0,)), ((), ())), preferred_element_type=F32)


def _split2(x):
    hi = x.astype(BF16)
    lo = (x - hi.astype(F32)).astype(BF16)
    return hi, lo


def _split3(x):
    hi = x.astype(BF16)
    r = x - hi.astype(F32)
    mid = r.astype(BF16)
    lo = (r - mid.astype(F32)).astype(BF16)
    return hi, mid, lo


def _dot3(a, b):
    ah, al = _split2(a)
    bh, bl = _split2(b)
    return _dot(ah, bh) + (_dot(ah, bl) + _dot(al, bh))


def _mod_kernel(c_ref, w_ref, b_ref, o_ref):
    a = _silu(c_ref[...])
    o_ref[0] = _dot3(a, w_ref[0]) + b_ref[0]


def _modulation(c_all, w_mod, b_mod):
    depth, d, n = w_mod.shape
    rows = c_all.shape[0]
    tn = 1024
    return pl.pallas_call(
        _mod_kernel,
        grid=(depth, n // tn),
        in_specs=[pl.BlockSpec((rows, d), lambda l, j: (0, 0)),
                  pl.BlockSpec((1, d, tn), lambda l, j: (l, 0, j)),
                  pl.BlockSpec((1, 1, tn), lambda l, j: (l, 0, j))],
        out_specs=pl.BlockSpec((1, rows, tn), lambda l, j: (l, 0, j)),
        out_shape=jax.ShapeDtypeStruct((depth, rows, n), F32),
        compiler_params=_cparams(2),
        name="modulation",
    )(c_all, w_mod, b_mod.reshape(depth, 1, n))


def _inproj_kernel(x_ref, sh_ref, sc_ref, g_ref, w_ref, o_ref, ub_ref, h_scr):
    first = pl.program_id(2) == 0
    tm = h_scr.shape[0]

    @pl.when(first)
    def _():
        for r0 in range(0, tm, INPROJ_ROWS):
            rs = slice(r0, r0 + INPROJ_ROWS)
            x = x_ref[0, rs, :]
            ms = jnp.mean(x * x, axis=-1, keepdims=True)
            y = x * lax.rsqrt(ms + NORM_EPS) * g_ref[...]
            h = (y * (1.0 + sc_ref[0]) + sh_ref[0]).astype(BF16)
            h_scr[rs, :] = h
            res = _dot(h, w_ref[0])
            o_ref[0, rs, :] = res.astype(BF16)
            ub_ref[0, rs, :] = res[:, COL_UB:COL_UB + S5_WIDTH]

    @pl.when(jnp.logical_not(first))
    def _():
        o_ref[0] = _dot(h_scr[...], w_ref[0]).astype(BF16)


def _inproj(x, shift, scale, g, w_bf, layer, tm, tn):
    b, l, d = x.shape
    n = w_bf.shape[2]
    assert COL_UB + S5_WIDTH <= tn
    per_batch = shift.shape[0] > 1
    mod_map = (lambda bi, i, j: (bi, 0, 0)) if per_batch else (lambda bi, i, j: (0, 0, 0))
    return pl.pallas_call(
        _inproj_kernel,
        grid=(b, l // tm, n // tn),
        in_specs=[pl.BlockSpec((1, tm, d), lambda bi, i, j: (bi, i, 0)),
                  pl.BlockSpec((1, 1, d), mod_map),
                  pl.BlockSpec((1, 1, d), mod_map),
                  pl.BlockSpec((1, d), lambda bi, i, j: (0, 0)),
                  pl.BlockSpec((1, d, tn), lambda bi, i, j: (layer, 0, j))],
        out_specs=[pl.BlockSpec((1, tm, tn), lambda bi, i, j: (bi, i, j)),
                   pl.BlockSpec((1, tm, S5_WIDTH), lambda bi, i, j: (bi, i, 0))],
        out_shape=[jax.ShapeDtypeStruct((b, l, n), BF16),
                   jax.ShapeDtypeStruct((b, l, S5_WIDTH), F32)],
        scratch_shapes=[pltpu.VMEM((tm, d), BF16)],
        compiler_params=_cparams(3),
        name="inproj",
    )(x, shift, scale, g, w_bf)


def _dft_tables(l):
    scale = 1.0 / math.sqrt(l * FOURIER_GROUP_DIM)
    k = np.arange(l, dtype=np.int64)
    t1 = np.arange(l // 64, dtype=np.int64)[:, None]
    t2 = np.arange(64, dtype=np.int64)[:, None]
    a = 2.0 * np.pi * ((t1 * 64 * k[None, :]) % l).astype(np.float64) / l
    bb = 2.0 * np.pi * ((t2 * k[None, :]) % l).astype(np.float64) / l
    ca, sa = jnp.asarray(np.cos(a) * scale, F32), jnp.asarray(np.sin(a) * scale, F32)
    cb, sb = jnp.asarray(np.cos(bb), F32), jnp.asarray(np.sin(bb), F32)
    dlc = (ca[:, None, :] * cb[None, :, :] - sa[:, None, :] * sb[None, :, :]).astype(BF16).reshape(l, l)
    dls = (-sa[:, None, :] * cb[None, :, :] - ca[:, None, :] * sb[None, :, :]).astype(BF16).reshape(l, l)
    c = np.arange(FOURIER_GROUP_DIM, dtype=np.int64)
    ang = 2.0 * np.pi * ((c[:, None] * c[None, :]) % FOURIER_GROUP_DIM) / FOURIER_GROUP_DIM
    cs = jnp.asarray(np.concatenate([np.cos(ang), np.sin(ang)], axis=1), F32).astype(BF16)
    return (dlc, dls), cs


def _fourier_kernel(u_ref, g_ref, cs_ref, dlc_ref, dls_ref, w_ref, o_ref, ucs_scr, *, l):
    gd = FOURIER_GROUP_DIM

    @pl.when(pl.program_id(1) == 0)
    def _():
        for g in range(FOURIER_GROUPS):
            ug = u_ref[0, :, g * gd:(g + 1) * gd].astype(BF16)
            t = _dot(ug, cs_ref[...])
            ucs_scr[0:l, g * gd:(g + 1) * gd] = t[:, :gd].astype(BF16)
            ucs_scr[l:2 * l, g * gd:(g + 1) * gd] = t[:, gd:].astype(BF16)

    f = _dot(dlc_ref[...], ucs_scr[0:l, :]) + _dot(dls_ref[...], ucs_scr[l:2 * l, :])
    gate = _silu(g_ref[0].astype(F32))
    for g in range(FOURIER_GROUPS):
        y = _dot(f[:, g * gd:(g + 1) * gd].astype(BF16), w_ref[g].astype(BF16))
        o_ref[0, :, g * gd:(g + 1) * gd] = (y * gate[:, g * gd:(g + 1) * gd]).astype(BF16)


def _fourier_branch(proj, fourier_w, dl, cs):
    b, l, _ = proj.shape
    tr = min(l, 512)
    w = FOURIER_WIDTH
    return pl.pallas_call(
        functools.partial(_fourier_kernel, l=l),
        grid=(b, l // tr),
        in_specs=[pl.BlockSpec((1, l, w), lambda bi, r: (bi, 0, COL_UA // w)),
                  pl.BlockSpec((1, tr, w), lambda bi, r: (bi, r, COL_GA // w)),
                  pl.BlockSpec((FOURIER_GROUP_DIM, 2 * FOURIER_GROUP_DIM), lambda bi, r: (0, 0)),
                  pl.BlockSpec((tr, l), lambda bi, r: (r, 0)),
                  pl.BlockSpec((tr, l), lambda bi, r: (r, 0)),
                  pl.BlockSpec((FOURIER_GROUPS, FOURIER_GROUP_DIM, FOURIER_GROUP_DIM),
                               lambda bi, r: (0, 0, 0))],
        out_specs=pl.BlockSpec((1, tr, w), lambda bi, r: (bi, r, 0)),
        out_shape=jax.ShapeDtypeStruct((b, l, w), BF16),
        scratch_shapes=[pltpu.VMEM((2 * l, w), BF16)],
        compiler_params=_cparams(2),
        name="fourier",
    )(proj, proj, cs, dl[0], dl[1], fourier_w)


def _s5_matrices(lam_re, lam_im, log_step, b_re, b_im, c_re, c_im, dskip):
    t = S5_CHUNK
    g, n, p = S5_GROUPS, S5_STATE, S5_GROUP_DIM
    hp = lax.Precision.HIGHEST
    lam_re = lam_re.astype(F32)
    lam_im = lam_im.astype(F32)
    step = jnp.exp(log_step.astype(F32))[..., None]
    rho = lam_re * step
    th = lam_im * step
    mag = jnp.exp(rho)
    lb_re = mag * jnp.cos(th)
    lb_im = mag * jnp.sin(th)
    nr = lb_re - 1.0
    den = lam_re * lam_re + lam_im * lam_im
    fr = (nr * lam_re + lb_im * lam_im) / den
    fi = (lb_im * lam_re - nr * lam_im) / den
    b_re = b_re.astype(F32)
    b_im = b_im.astype(F32)
    bb_re = fr[..., None] * b_re - fi[..., None] * b_im
    bb_im = fr[..., None] * b_im + fi[..., None] * b_re
    tau = jnp.arange(t + 1, dtype=F32)
    pw_mag = jnp.exp(rho[..., None] * tau)
    pw_re = pw_mag * jnp.cos(th[..., None] * tau)
    pw_im = pw_mag * jnp.sin(th[..., None] * tau)
    c_re = c_re.astype(F32)
    c_im = c_im.astype(F32)
    ca_re = c_re[..., None] * pw_re[:, :, None] - c_im[..., None] * pw_im[:, :, None]
    ca_im = c_re[..., None] * pw_im[:, :, None] + c_im[..., None] * pw_re[:, :, None]
    kern = (jnp.einsum('dgpnt,dgnq->dgtpq', ca_re, bb_re, precision=hp)
            - jnp.einsum('dgpnt,dgnq->dgtpq', ca_im, bb_im, precision=hp))[:, :, :t]
    sig = np.arange(t)[:, None]
    ta = np.arange(t)[None, :]
    idx_f = np.clip(ta - sig, 0, t - 1)
    idx_b = np.clip(sig - ta, 0, t - 1)
    m_f = jnp.asarray((ta >= sig).astype(np.float32))[None, :, :, None, None]
    m_b = jnp.asarray((sig >= ta).astype(np.float32))[None, :, :, None, None]
    tf = kern[0][:, idx_f] * m_f
    tb = kern[1][:, idx_b] * m_b
    eye_t = jnp.asarray(np.eye(t, dtype=np.float32))[None, :, :, None, None]
    dmat = dskip.astype(F32)[:, None, None, :, None] * jnp.asarray(np.eye(p, dtype=np.float32))[None, None, None]
    tt = tf + tb + eye_t * dmat
    nb, gb = S5_NBLK, S5_GB
    one = np.ones
    eye = lambda k: np.eye(k, dtype=np.float32)
    exp_t = jnp.asarray(np.einsum('tu,pq,b->tpubq', eye(t), eye(p), one(gb)).reshape(t * p, S5_K))
    exp_n = jnp.asarray(np.einsum('nm,b->nbm', eye(n), one(gb)).reshape(n, S5_PART))
    mask_t = jnp.asarray(np.einsum('ab,s,q,u,p->saqubp', eye(gb), one(t), one(p), one(t), one(p))
                         .reshape(S5_K, S5_K))
    mask_ba = jnp.asarray(np.einsum('ab,s,q,n->saqbn', eye(gb), one(t), one(p), one(n)).reshape(S5_K, S5_PART))
    t_small = tt.reshape(nb, gb, t, t, p, p).transpose(0, 2, 1, 5, 3, 4).reshape(nb, S5_K, t * p)
    tmat = jnp.matmul(t_small, exp_t, precision=hp) * mask_t

    def ba(d, powers):
        pr = pw_re[d][:, :, powers]
        pi = pw_im[d][:, :, powers]
        re = pr[..., None] * bb_re[d][:, :, None, :] - pi[..., None] * bb_im[d][:, :, None, :]
        im = pr[..., None] * bb_im[d][:, :, None, :] + pi[..., None] * bb_re[d][:, :, None, :]
        blk = lambda m: jnp.matmul(m.reshape(nb, gb, n, t, p).transpose(0, 3, 1, 4, 2).reshape(nb, S5_K, n),
                                   exp_n, precision=hp) * mask_ba
        return blk(re), blk(im)

    baf_re, baf_im = ba(0, np.arange(t - 1, -1, -1))
    bab_re, bab_im = ba(1, np.arange(t))
    bamat = jnp.concatenate([baf_re, bab_re, baf_im, bab_im], axis=-1)

    def ca(d, powers):
        blk = lambda m: jnp.matmul(exp_n.T, m[..., powers].reshape(nb, gb, p, n, t).transpose(0, 3, 4, 1, 2)
                                   .reshape(nb, n, S5_K), precision=hp) * mask_ba.T
        return blk(ca_re[d]), blk(ca_im[d])

    caf_re, caf_im = ca(0, np.arange(1, t + 1))
    cab_re, cab_im = ca(1, np.arange(t, 0, -1))
    camat = jnp.concatenate([caf_re, cab_re, -caf_im, -cab_im], axis=1)
    part = lambda m: m[..., t].reshape(nb, S5_PART)
    a_pow = jnp.stack([jnp.concatenate([part(pw_re[0]), part(pw_re[1])], axis=-1),
                       jnp.concatenate([part(pw_im[0]), part(pw_im[1])], axis=-1)], axis=1)
    return tmat.astype(BF16), bamat.astype(BF16), camat.astype(BF16), a_pow


def _s5_kernel(u_ref, t_ref, ba_ref, ca_ref, a_ref, h0_ref, y_ref, hf_ref, ub_scr, s_scr, hin_scr, y_scr,
               *, nc, bt):
    t, w, hp = S5_CHUNK, S5_GB * S5_GROUP_DIM, S5_PART
    for bi in range(bt):
        for tau in range(t):
            ub_scr[tau, pl.ds(bi, nc, stride=bt), :] = u_ref[bi, pl.ds(tau, nc, stride=t), :]
    ub = jnp.concatenate([ub_scr[tau] for tau in range(t)], axis=1).astype(BF16)
    s_scr[...] = _dot(ub, ba_ref[0, 0])
    ar_f, ar_b = a_ref[0, 0, 0:1, 0:hp], a_ref[0, 0, 0:1, hp:2 * hp]
    ai_f, ai_b = a_ref[0, 0, 1:2, 0:hp], a_ref[0, 0, 1:2, hp:2 * hp]

    def step(i, carry):
        re_f, im_f, re_b, im_b = carry
        rf = pl.ds(pl.multiple_of(i * bt, bt), bt)
        rb = pl.ds(pl.multiple_of((nc - 1 - i) * bt, bt), bt)
        hin_scr[rf, 0:hp] = re_f
        hin_scr[rf, 2 * hp:3 * hp] = im_f
        hin_scr[rb, hp:2 * hp] = re_b
        hin_scr[rb, 3 * hp:4 * hp] = im_b
        n_re_f = ar_f * re_f - ai_f * im_f + s_scr[rf, 0:hp]
        n_im_f = ar_f * im_f + ai_f * re_f + s_scr[rf, 2 * hp:3 * hp]
        n_re_b = ar_b * re_b - ai_b * im_b + s_scr[rb, hp:2 * hp]
        n_im_b = ar_b * im_b + ai_b * re_b + s_scr[rb, 3 * hp:4 * hp]
        return n_re_f, n_im_f, n_re_b, n_im_b

    h0 = h0_ref[...].reshape(bt, 4 * hp)
    init = (h0[:, 0:hp], h0[:, 2 * hp:3 * hp], h0[:, hp:2 * hp], h0[:, 3 * hp:4 * hp])
    fin = lax.fori_loop(0, nc, step, init, unroll=8)
    hf_ref[...] = jnp.concatenate([fin[0], fin[2], fin[1], fin[3]], axis=1).reshape(hf_ref.shape)
    y = _dot(ub, t_ref[0, 0]) + _dot(hin_scr[...].astype(BF16), ca_ref[0, 0])
    for tau in range(t):
        y_scr[tau] = y[:, tau * w:(tau + 1) * w]
    for bi in range(bt):
        for tau in range(t):
            y_ref[bi, pl.ds(tau, nc, stride=t), :] = y_scr[tau, pl.ds(bi, nc, stride=bt), :]


def _s5_core(ub, tmat, bamat, camat, a_pow, h0, layer):
    b, l, _ = ub.shape
    nc = l // S5_CHUNK
    w = S5_GB * S5_GROUP_DIM
    bt = b if nc * b <= S5_MAX_ROWS else 1
    if bt == 1:
        h0 = h0.reshape(S5_NBLK, b, 1, 4 * S5_PART)
        st_spec = pl.BlockSpec((1, 1, 1, 4 * S5_PART), lambda j, bi: (j, bi, 0, 0))
    else:
        st_spec = pl.BlockSpec((1, bt, 4 * S5_PART), lambda j, bi: (j, bi, 0))
    rows = nc * bt
    y, hf = pl.pallas_call(
        functools.partial(_s5_kernel, nc=nc, bt=bt),
        grid=(S5_NBLK, b // bt),
        in_specs=[pl.BlockSpec((bt, l, w), lambda j, bi: (bi, 0, j)),
                  pl.BlockSpec((1, 1, S5_K, S5_K), lambda j, bi: (layer, j, 0, 0)),
                  pl.BlockSpec((1, 1, S5_K, 4 * S5_PART), lambda j, bi: (layer, j, 0, 0)),
                  pl.BlockSpec((1, 1, 4 * S5_PART, S5_K), lambda j, bi: (layer, j, 0, 0)),
                  pl.BlockSpec((1, 1, 2, 2 * S5_PART), lambda j, bi: (layer, j, 0, 0)),
                  st_spec],
        out_specs=[pl.BlockSpec((bt, l, w), lambda j, bi: (bi, 0, j)), st_spec],
        out_shape=[jax.ShapeDtypeStruct((b, l, S5_WIDTH), F32),
                   jax.ShapeDtypeStruct(h0.shape, F32)],
        scratch_shapes=[pltpu.VMEM((S5_CHUNK, rows, w), F32), pltpu.VMEM((rows, 4 * S5_PART), F32),
                        pltpu.VMEM((rows, 4 * S5_PART), F32), pltpu.VMEM((S5_CHUNK, rows, w), F32)],
        compiler_params=_cparams(2),
        name="s5_core",
    )(ub, tmat, bamat, camat, a_pow, h0)
    return y, hf.reshape(S5_NBLK, b, 4 * S5_PART)


def _s5_glu_kernel(y_ref, g_ref, w_ref, b_ref, o_ref):
    y = y_ref[0]
    y = 0.5 * y * (1.0 + jnp.tanh(math.sqrt(2.0 / math.pi) * (y + 0.044715 * (y * y * y))))
    z = _dot(y.astype(BF16), w_ref[...].astype(BF16)) + b_ref[...]
    o_ref[0] = (y * _sigmoid(z) * _silu(g_ref[0].astype(F32))).astype(BF16)


def _s5_glu(y, proj, glu_w, glu_b, tm):
    b, l, w = y.shape
    return pl.pallas_call(
        _s5_glu_kernel,
        grid=(b, l // tm),
        in_specs=[pl.BlockSpec((1, tm, w), lambda bi, i: (bi, i, 0)),
                  pl.BlockSpec((1, tm, w), lambda bi, i: (bi, i, COL_GB // w)),
                  pl.BlockSpec((w, w), lambda bi, i: (0, 0)),
                  pl.BlockSpec((1, w), lambda bi, i: (0, 0))],
        out_specs=pl.BlockSpec((1, tm, w), lambda bi, i: (bi, i, 0)),
        out_shape=jax.ShapeDtypeStruct((b, l, w), BF16),
        compiler_params=_cparams(2),
        name="s5_glu",
    )(y, proj, glu_w, glu_b.reshape(1, w))


def _s5_branch(proj, ub, mats, layer, glu_w, glu_b, h0_re, h0_im):
    b, l, _ = proj.shape
    g, n = S5_GROUPS, S5_STATE
    if h0_re is None:
        h0 = jnp.zeros((S5_NBLK, b, 4 * S5_PART), F32)
    else:
        blk = lambda h: h.astype(F32).reshape(b, S5_NBLK, S5_PART).transpose(1, 0, 2)
        h0 = jnp.concatenate([blk(h0_re[:, 0]), blk(h0_re[:, 1]), blk(h0_im[:, 0]), blk(h0_im[:, 1])], axis=-1)
    y, hf = _s5_core(ub, *mats, h0, layer)
    out = _s5_glu(y, proj, glu_w, glu_b, min(l, 1024))
    part = lambda i: hf[..., i * S5_PART:(i + 1) * S5_PART].transpose(1, 0, 2).reshape(b, g, n)
    fin_re = jnp.stack([part(0), part(1)], axis=1)
    fin_im = jnp.stack([part(2), part(3)], axis=1)
    return out, fin_re, fin_im


def _rope_tables(l):
    rows = l // GRID_W
    row = jnp.broadcast_to(jnp.arange(rows, dtype=F32)[:, None], (rows, GRID_W)).reshape(-1)
    col = jnp.broadcast_to(jnp.arange(GRID_W, dtype=F32)[None, :], (rows, GRID_W)).reshape(-1)
    half = HEAD_DIM // 2
    inv = ROPE_THETA ** (-jnp.arange(0, half, 2, dtype=F32) / half)
    ar = row[:, None] * inv
    ac = col[:, None] * inv
    cos = jnp.concatenate([jnp.cos(ar), jnp.cos(ar), jnp.cos(ac), jnp.cos(ac)], axis=1)
    sin = jnp.concatenate([-jnp.sin(ar), jnp.sin(ar), -jnp.sin(ac), jnp.sin(ac)], axis=1)
    return cos, sin


def _qk_kernel(*refs, rope, want_kf):
    q_ref, k_ref, v_ref, qn_ref, kn_ref = refs[:5]
    pos = 5
    if rope:
        cos_ref, sin_ref = refs[pos:pos + 2]
        pos += 2
    qo_ref, ko_ref, vo_ref = refs[pos:pos + 3]
    pos += 3
    if want_kf:
        kf_ref = refs[pos]
    hd = HEAD_DIM
    tm = q_ref.shape[1]
    quarter = hd // 4
    if rope:
        cos = cos_ref[...]
        sin = sin_ref[...]
        first = (lax.broadcasted_iota(jnp.int32, (tm, hd), 1) % (2 * quarter)) < quarter

    def norm(x, gain):
        ms = jnp.mean(x * x, axis=-1, keepdims=True)
        return x * lax.rsqrt(ms + NORM_EPS) * gain

    def rot(x):
        if not rope:
            return x
        nxt = pltpu.roll(x, hd - quarter, 1)
        prv = pltpu.roll(x, quarter, 1)
        return x * cos + jnp.where(first, nxt, prv) * sin

    qscale = (hd ** -0.5) * LOG2E
    for h in range(N_HEADS):
        x = norm(q_ref[0, :, h * hd:(h + 1) * hd].astype(F32), qn_ref[...])
        qo_ref[0, :, h * hd:(h + 1) * hd] = (rot(x) * qscale).astype(BF16)
    for h in range(N_KV_HEADS):
        x = norm(k_ref[0, :, h * hd:(h + 1) * hd].astype(F32), kn_ref[...])
        if want_kf:
            kf_ref[0, :, h * hd:(h + 1) * hd] = x
        ko_ref[0, :, h * hd:(h + 1) * hd] = rot(x).astype(BF16)
        vo_ref[0, h] = v_ref[0, :, h * hd:(h + 1) * hd].astype(F32).T.astype(BF16)


def _qk_prep(proj, q_norm, k_norm, rope_tabs, want_kf, tm):
    b, l, _ = proj.shape
    rope = rope_tabs is not None
    in_specs = [pl.BlockSpec((1, tm, ATTN_WIDTH), lambda bi, i: (bi, i, COL_QC // ATTN_WIDTH)),
                pl.BlockSpec((1, tm, KV_WIDTH), lambda bi, i: (bi, i, COL_KC // KV_WIDTH)),
                pl.BlockSpec((1, tm, KV_WIDTH), lambda bi, i: (bi, i, COL_VC // KV_WIDTH)),
                pl.BlockSpec((1, HEAD_DIM), lambda bi, i: (0, 0)),
                pl.BlockSpec((1, HEAD_DIM), lambda bi, i: (0, 0))]
    args = [proj, proj, proj, q_norm.reshape(1, HEAD_DIM), k_norm.reshape(1, HEAD_DIM)]
    if rope:
        in_specs += [pl.BlockSpec((tm, HEAD_DIM), lambda bi, i: (i, 0))] * 2
        args += list(rope_tabs)
    out_specs = [pl.BlockSpec((1, tm, ATTN_WIDTH), lambda bi, i: (bi, i, 0)),
                 pl.BlockSpec((1, tm, KV_WIDTH), lambda bi, i: (bi, i, 0)),
                 pl.BlockSpec((1, N_KV_HEADS, HEAD_DIM, tm), lambda bi, i: (bi, 0, 0, i))]
    out_shape = [jax.ShapeDtypeStruct((b, l, ATTN_WIDTH), BF16),
                 jax.ShapeDtypeStruct((b, l, KV_WIDTH), BF16),
                 jax.ShapeDtypeStruct((b, N_KV_HEADS, HEAD_DIM, l), BF16)]
    if want_kf:
        out_specs.append(pl.BlockSpec((1, tm, KV_WIDTH), lambda bi, i: (bi, i, 0)))
        out_shape.append(jax.ShapeDtypeStruct((b, l, KV_WIDTH), F32))
    return pl.pallas_call(
        functools.partial(_qk_kernel, rope=rope, want_kf=want_kf),
        grid=(b, l // tm),
        in_specs=in_specs, out_specs=out_specs, out_shape=out_shape,
        compiler_params=_cparams(2),
        name="qk_prep",
    )(*args)


def _attn_kernel(*refs, has_ctx):
    s_scr = refs[-1]
    q_ref, k_ref, v_ref = refs[:3]
    if has_ctx:
        ck_ref, cv_ref, g_ref, o_ref = refs[3:7]
        ck = ck_ref[0, 0].astype(BF16)
        cvt = cv_ref[0, 0, 0].astype(BF16)
        past = ck.shape[0]
    else:
        g_ref, o_ref = refs[3:5]
    hd = HEAD_DIM
    tq = q_ref.shape[1]
    lk = k_ref.shape[1]
    cr = min(lk, ATTN_KEY_CHUNK)
    chunks = [(c * cr, cr, (lambda c=c: k_ref[0, c * cr:(c + 1) * cr, :]),
               (lambda c=c: v_ref[0, 0, :, c * cr:(c + 1) * cr])) for c in range(lk // cr)]
    if has_ctx:
        chunks.append((lk, past, lambda: ck, lambda: cvt))
    qs = [q_ref[0, :, g * hd:(g + 1) * hd] for g in range(Q_PER_KV)]
    gate = _silu(g_ref[0].astype(F32))

    def score(g, chunk, m):
        off, rows, keys, _ = chunk
        s = _dot_nt(keys(), qs[g])
        s_scr[g % 2, off:off + rows, :] = s
        return jnp.maximum(m, jnp.max(s, axis=0, keepdims=True))

    def accumulate(g, chunk, m, den, ot):
        off, rows, _, values_t = chunk
        p = jnp.exp2(s_scr[g % 2, off:off + rows, :] - m)
        return den + jnp.sum(p, axis=0, keepdims=True), ot + _dot(values_t(), p.astype(BF16))

    neg = jnp.full((1, tq), -jnp.inf, F32)
    m_cur = neg
    for chunk in chunks:
        m_cur = score(0, chunk, m_cur)
    for g in range(Q_PER_KV):
        m_next = neg
        den = jnp.zeros((1, tq), F32)
        ot = jnp.zeros((hd, tq), F32)
        for chunk in chunks:
            if g + 1 < Q_PER_KV:
                m_next = score(g + 1, chunk, m_next)
            den, ot = accumulate(g, chunk, m_cur, den, ot)
        m_cur = m_next
        o = (ot / den).T
        o_ref[0, :, g * hd:(g + 1) * hd] = (o * gate[:, g * hd:(g + 1) * hd]).astype(BF16)


def _attention(qs, kr, vt, proj, ctx_k, ctx_v, layer, tq):
    b, l, _ = qs.shape
    gw = Q_PER_KV * HEAD_DIM
    has_ctx = ctx_k is not None
    in_specs = [pl.BlockSpec((1, tq, gw), lambda bi, h, i: (bi, i, h)),
                pl.BlockSpec((1, l, HEAD_DIM), lambda bi, h, i: (bi, 0, h)),
                pl.BlockSpec((1, 1, HEAD_DIM, l), lambda bi, h, i: (bi, h, 0, 0))]
    args = [qs, kr, vt]
    if has_ctx:
        past = ctx_k.shape[2]
        in_specs += [pl.BlockSpec((1, 1, past, HEAD_DIM), lambda bi, h, i: (bi, layer, 0, h)),
                     pl.BlockSpec((1, 1, 1, HEAD_DIM, past), lambda bi, h, i: (bi, layer, h, 0, 0))]
        args += [ctx_k, ctx_v]
    in_specs.append(pl.BlockSpec((1, tq, gw), lambda bi, h, i: (bi, i, COL_GC // gw + h)))
    args.append(proj)
    return pl.pallas_call(
        functools.partial(_attn_kernel, has_ctx=has_ctx),
        grid=(b, N_KV_HEADS, l // tq),
        in_specs=in_specs,
        out_specs=pl.BlockSpec((1, tq, gw), lambda bi, h, i: (bi, i, h)),
        out_shape=jax.ShapeDtypeStruct((b, l, ATTN_WIDTH), BF16),
        scratch_shapes=[pltpu.VMEM((2, l + (ctx_k.shape[2] if has_ctx else 0), tq), F32)],
        compiler_params=_cparams(3),
        name="attention",
    )(*args)


HGRN_COARSE = (32, 16, 8, 4)
HGRN_FINE = (2, 1)
HGRN_LEVELS = HGRN_COARSE + HGRN_FINE
HGRN_CHUNKS_PER_TRIP = 4


def _hgrn_tables():
    c = HGRN_CHUNK
    t = np.arange(c)
    w01, sm = [], []
    for d in range(2):
        cum = (t[None, :] <= t[:, None]) if d == 0 else (t[None, :] >= t[:, None])
        cum = cum.astype(np.float32)
        w01.append(np.concatenate([cum] + [cum[_hgrn_ref_row(t, hs, d)] for hs in HGRN_FINE], axis=0))
        sd = []
        for hs in HGRN_LEVELS:
            is_q = _hgrn_is_query(t, hs, d)
            same = (t[:, None] // (2 * hs)) == (t[None, :] // (2 * hs))
            sd.append((same & is_q[:, None] & ~is_q[None, :]).astype(np.float32))
        sd.append(np.eye(c, dtype=np.float32))
        sm.append(np.stack(sd))
    return jnp.asarray(np.stack(w01), BF16), jnp.asarray(np.stack(sm), F32)


def _hgrn_ref_row(t, hs, d):
    base = t & ~(2 * hs - 1)
    return base + hs - 1 if d == 0 else base + hs


def _hgrn_is_query(t, hs, d):
    return ((t & hs) != 0) if d == 0 else ((t & hs) == 0)


def _hgrn_cums(z, lbv, w01):
    f = lbv + (1.0 - lbv) * _sigmoid(z)
    p1, p2, p3 = _split3(jnp.log2(jnp.maximum(f, F_FLOOR)))
    return 1.0 - f, _dot(w01, p1) + (_dot(w01, p2) + _dot(w01, p3))


def _hgrn_products(q, k, allc, rows, d):
    c = HGRN_CHUNK
    sub = 8
    cum = allc[0:c]
    refs = {}
    for hs in HGRN_COARSE:
        pieces = []
        for j in range(c // sub):
            rr = int(_hgrn_ref_row(np.int64(j * sub), hs, d))
            pieces.append(jnp.broadcast_to(cum[rr:rr + 1, :], (sub, HGRN_DK)))
        refs[hs] = jnp.concatenate(pieces, axis=0)
    for i, hs in enumerate(HGRN_FINE):
        refs[hs] = allc[(i + 1) * c:(i + 2) * c]
    lrow = c - 1 if d == 0 else 0
    last = cum[lrow:lrow + 1, :]
    raws = [_dot_nt(q.astype(BF16), k.astype(BF16))]
    for hs in HGRN_LEVELS:
        is_q = ((rows & hs) != 0) if d == 0 else ((rows & hs) == 0)
        x = (jnp.where(is_q, q, k) * jnp.exp2((cum - refs[hs]) * jnp.where(is_q, 1.0, -1.0))).astype(BF16)
        raws.append(_dot_nt(x, x))
    qi = (q * jnp.exp2(cum)).astype(BF16)
    ks = (k * jnp.exp2(last - cum)).astype(BF16)
    return raws, qi, ks, jnp.exp2(last)


def _hgrn_kernel(*refs, nc, has_s0, want_state):
    q_ref, v_ref, zf_ref, zb_ref, g_ref, lb_ref, ng_ref, w01_ref, sm_ref = refs[:9]
    pos = 9
    if has_s0:
        s0_ref = refs[pos]
        pos += 1
    o_ref = refs[pos]
    pos += 1
    if want_state:
        sf_ref = refs[pos]
        pos += 1
    o_scr, st_scr = refs[pos:pos + 2]
    c = HGRN_CHUNK
    l = nc * c
    for d in range(2):
        if has_s0:
            st_scr[d] = s0_ref[0, d, 0]
        else:
            st_scr[d] = jnp.zeros((HGRN_DV, HGRN_DK), F32)
    rows = lax.broadcasted_iota(jnp.int32, (c, HGRN_DK), 0)

    per_trip = math.gcd(nc, HGRN_CHUNKS_PER_TRIP)

    def body(i, carry):
        insts = []
        for u in range(per_trip):
            for d in range(2):
                ci = per_trip * i + u if d == 0 else nc - 1 - (per_trip * i + u)
                insts.append((d, pl.multiple_of(ci * c, c)))
        gates = [_hgrn_cums((zf_ref, zb_ref)[d][0, pl.ds(r, c), :].astype(F32), lb_ref[d, 0], w01_ref[d])
                 for d, r in insts]
        prods = [_hgrn_products(q_ref[0, pl.ds(r, c), :].astype(F32), k, allc, rows, d)
                 for (d, r), (k, allc) in zip(insts, gates)]
        nl = len(HGRN_LEVELS)
        for (d, r), (raws, qi, ks, dec) in zip(insts, prods):
            scores = sm_ref[d, nl] * raws[0]
            for li in range(nl):
                scores = scores + sm_ref[d, li] * raws[li + 1]
            vb = v_ref[0, pl.ds(r, c), :].astype(BF16)
            st = st_scr[d]
            o_scr[d, pl.ds(r, c), :] = _dot(scores.astype(BF16), vb) + _dot_nt(qi, st.astype(BF16))
            st_scr[d] = dec * st + _dot_tn(vb, ks)
        return carry

    lax.fori_loop(0, nc // per_trip, body, 0)
    if want_state:
        for d in range(2):
            sf_ref[0, d, 0] = st_scr[d]

    fb = min(l, 256)

    def finish(i, carry):
        r = pl.multiple_of(i * fb, fb)
        tot = o_scr[0, pl.ds(r, fb), :] + o_scr[1, pl.ds(r, fb), :]
        ms = jnp.mean(tot * tot, axis=-1, keepdims=True)
        y = tot * lax.rsqrt(ms + NORM_EPS) * ng_ref[...]
        o_ref[0, pl.ds(r, fb), :] = (y * _silu(g_ref[0, pl.ds(r, fb), :].astype(F32))).astype(BF16)
        return carry

    lax.fori_loop(0, l // fb, finish, 0)


def _hgrn_branch(proj, lower_bound, norm_g, tables, s0_t, want_state):
    b, l, _ = proj.shape
    nc = l // HGRN_CHUNK
    w = HGRN_DK
    has_s0 = s0_t is not None
    w01, sm = tables

    def col(off):
        return lambda bi, h: (bi, 0, off // w + h)

    in_specs = [pl.BlockSpec((1, l, w), col(COL_QD)),
                pl.BlockSpec((1, l, w), col(COL_ID)),
                pl.BlockSpec((1, l, w), col(COL_ZF)),
                pl.BlockSpec((1, l, w), col(COL_ZB)),
                pl.BlockSpec((1, l, w), col(COL_GD)),
                pl.BlockSpec((2, 1, 1, w), lambda bi, h: (0, h, 0, 0)),
                pl.BlockSpec((1, w), lambda bi, h: (0, 0)),
                pl.BlockSpec(w01.shape, lambda bi, h: (0, 0, 0)),
                pl.BlockSpec(sm.shape, lambda bi, h: (0, 0, 0, 0))]
    args = [proj, proj, proj, proj, proj, lower_bound.reshape(2, HGRN_HEADS, 1, w), norm_g.reshape(1, w),
            w01, sm]
    st_spec = pl.BlockSpec((1, 2, 1, HGRN_DV, HGRN_DK), lambda bi, h: (bi, 0, h, 0, 0))
    if has_s0:
        in_specs.append(st_spec)
        args.append(s0_t)
    out_specs = [pl.BlockSpec((1, l, w), lambda bi, h: (bi, 0, h))]
    out_shape = [jax.ShapeDtypeStruct((b, l, HGRN_HEADS * HGRN_DV), BF16)]
    if want_state:
        out_specs.append(st_spec)
        out_shape.append(jax.ShapeDtypeStruct((b, 2, HGRN_HEADS, HGRN_DV, HGRN_DK), F32))
    res = pl.pallas_call(
        functools.partial(_hgrn_kernel, nc=nc, has_s0=has_s0, want_state=want_state),
        grid=(b, HGRN_HEADS),
        in_specs=in_specs, out_specs=out_specs, out_shape=out_shape,
        scratch_shapes=[pltpu.VMEM((2, l, HGRN_DV), F32), pltpu.VMEM((2, HGRN_DV, HGRN_DK), F32)],
        compiler_params=_cparams(2),
        name="hgrn",
    )(*args)
    return res if want_state else (res[0], None)


def _final_kernel(x_ref, gt_ref, ma_ref, mb_ref, mc_ref, md_ref, ya_ref, yb_ref, yc_ref, yd_ref,
                  wa_ref, wb_ref, wc_ref, wd_ref, wo_ref, gp_ref, o_ref, acc, *, nct):
    ci = pl.program_id(2)
    tm = acc.shape[0]
    subs = [slice(r0, r0 + FINAL_ROWS) for r0 in range(0, tm, FINAL_ROWS)]
    m_refs = (ma_ref, mb_ref, mc_ref, md_ref)
    y_refs = (ya_ref, yb_ref, yc_ref, yd_ref)
    w_refs = (wa_ref, wb_ref, wc_ref, wd_ref)

    def step(first, last):
        dots = [[_dot(y_ref[0, rs, :], w_ref[0]) for y_ref, w_ref in zip(y_refs, w_refs)] for rs in subs]
        for rs, branch in zip(subs, dots):
            mixed = None
            for m_ref, dot in zip(m_refs, branch):
                term = _gate_sigmoid(m_ref[0, rs, :].astype(F32)) * dot
                mixed = term if mixed is None else mixed + term
            contrib = _dot(mixed.astype(BF16), wo_ref[0])
            out = contrib if first else acc[rs, :] + contrib
            if last:
                ms = jnp.mean(out * out, axis=-1, keepdims=True)
                y = out * lax.rsqrt(ms + NORM_EPS) * gp_ref[...]
                o_ref[0, rs, :] = x_ref[0, rs, :] + gt_ref[0] * y
            else:
                acc[rs, :] = out

    if nct == 1:
        step(True, True)
    else:
        pl.when(ci == 0)(lambda: step(True, False))
        pl.when(jnp.logical_and(ci > 0, ci < nct - 1))(lambda: step(False, False))
        pl.when(ci == nct - 1)(lambda: step(False, True))


def _final(x, gate, proj, ys, w_projs, w_out, norm_post, layer, tm, tc):
    b, l, d = x.shape
    nct = d // tc
    per_batch = gate.shape[0] > 1
    gate_map = (lambda bi, i, c: (bi, 0, 0)) if per_batch else (lambda bi, i, c: (0, 0, 0))

    def mspec(j):
        return pl.BlockSpec((1, tm, tc), lambda bi, i, c: (bi, i, (COL_M + j * d) // tc + c))

    in_specs = [pl.BlockSpec((1, tm, d), lambda bi, i, c: (bi, i, 0)),
                pl.BlockSpec((1, 1, d), gate_map)]
    in_specs += [mspec(j) for j in range(N_BRANCHES)]
    in_specs += [pl.BlockSpec((1, tm, y.shape[-1]), lambda bi, i, c: (bi, i, 0)) for y in ys]
    in_specs += [pl.BlockSpec((1, w.shape[1], tc), lambda bi, i, c: (layer, 0, c)) for w in w_projs]
    in_specs += [pl.BlockSpec((1, tc, d), lambda bi, i, c: (layer, c, 0)),
                 pl.BlockSpec((1, d), lambda bi, i, c: (0, 0))]
    return pl.pallas_call(
        functools.partial(_final_kernel, nct=nct),
        grid=(b, l // tm, nct),
        in_specs=in_specs,
        out_specs=pl.BlockSpec((1, tm, d), lambda bi, i, c: (bi, i, 0)),
        out_shape=jax.ShapeDtypeStruct((b, l, d), F32),
        scratch_shapes=[pltpu.VMEM((tm, d), F32)],
        compiler_params=_cparams(3),
        name="merge_out",
    )(x, gate, proj, proj, proj, proj, *ys, *w_projs, w_out, norm_post.reshape(1, d))


def _trunk_layer(x, shift, scale, gate, p, ctx, layer, consts):
    b, l, d = x.shape
    shared_mod = shift.shape[0] == 1
    if shared_mod:
        proj, ub = _inproj(x.reshape(1, b * l, d), shift, scale, p['norm_pre'], p['w_in'], layer,
                           INPROJ_TM, INPROJ_TN)
        proj = proj.reshape(b, l, IN_COLS)
        ub = ub.reshape(b, l, S5_WIDTH)
    else:
        proj, ub = _inproj(x, shift, scale, p['norm_pre'], p['w_in'], layer, INPROJ_TM, INPROJ_TN)

    dl, cs = consts['dft'][l]
    y_a = _fourier_branch(proj, p['fourier_w'], dl, cs)

    if ctx is None:
        y_b, s5_re, s5_im = _s5_branch(proj, ub, p['s5_mats'], layer, p['s5_glu_w'], p['s5_glu_b'], None, None)
        qs, kr, vt, kf = _qk_prep(proj, p['q_norm'], p['k_norm'], None, True, min(l, 512))
        y_c = _attention(qs, kr, vt, proj, None, None, layer, min(l, 256))
        y_d, hg = _hgrn_branch(proj, p['lower_bound'], p['hgrn_norm'], consts['hgrn'], None, True)
    else:
        cache_k, cache_v, st_re, st_im, s0_t = ctx
        y_b, _, _ = _s5_branch(proj, ub, p['s5_mats'], layer, p['s5_glu_w'], p['s5_glu_b'],
                               st_re[:, layer], st_im[:, layer])
        qs, kr, vt = _qk_prep(proj, p['q_norm'], p['k_norm'], consts['rope'], False, min(l, 512))
        y_c = _attention(qs, kr, vt, proj, cache_k, cache_v, layer, min(l, 256))
        y_d, _ = _hgrn_branch(proj, p['lower_bound'], p['hgrn_norm'], consts['hgrn'], s0_t[:, layer], False)

    ys = (y_a, y_b, y_c, y_d)
    wps = (p['w_proj_a'], p['w_proj_b'], p['w_proj_c'], p['w_proj_d'])
    if shared_mod:
        n = b * l
        x_new = _final(x.reshape(1, n, d), gate, proj.reshape(1, n, IN_COLS),
                       tuple(y.reshape(1, n, y.shape[-1]) for y in ys), wps, p['w_out'], p['norm_post'], layer,
                       512, 512)
        x_new = x_new.reshape(b, l, d)
    else:
        x_new = _final(x, gate, proj, ys, wps, p['w_out'], p['norm_post'], layer, 512, 512)

    if ctx is None:
        v_f = proj[:, :, COL_VC:COL_VC + KV_WIDTH].astype(F32)
        return x_new, (kf, v_f, s5_re, s5_im, hg)
    return x_new, None


def kernel(x_prompt, x_sample, c, cache_k, cache_v, state_s5_re, state_s5_im, state_hgrn, c_ctx,
           norm_pre, norm_post, w_mod, b_mod, w_in, fourier_w,
           s5_lambda_re, s5_lambda_im, s5_log_step, s5_b_re, s5_b_im, s5_c_re, s5_c_im,
           s5_d, s5_glu_w, s5_glu_b, q_norm, k_norm, hgrn_lb_logits, hgrn_norm,
           w_proj_a, w_proj_b, w_proj_c, w_proj_d, w_out):
    depth = w_in.shape[0]
    bp, lp, d = x_prompt.shape
    bs, ls, _ = x_sample.shape

    lb_w = jax.nn.softmax(hgrn_lb_logits.astype(F32), axis=0)
    lower_bounds = jnp.cumsum(lb_w, axis=0) - lb_w[0]

    rows = ((bs + 1 + 7) // 8) * 8
    c_all = jnp.zeros((rows, d), F32).at[:bs].set(c).at[bs].set(c_ctx)
    mod = _modulation(c_all, w_mod, b_mod)

    w_in_bf = w_in.astype(BF16)
    wpa, wpb, wpc, wpd = (w.astype(BF16) for w in (w_proj_a, w_proj_b, w_proj_c, w_proj_d))
    w_out_bf = w_out.astype(BF16)

    consts = {'dft': {lp: _dft_tables(lp), ls: _dft_tables(ls)},
              'rope': _rope_tables(ls),
              'hgrn': _hgrn_tables()}
    past = cache_k.shape[2]
    ctx_k = cache_k.reshape(bs, depth, past, KV_WIDTH)
    ctx_v = cache_v.transpose(0, 1, 3, 4, 2)
    s0_t = jnp.swapaxes(state_hgrn.astype(F32), -1, -2)

    s5_mats = jax.vmap(_s5_matrices)(s5_lambda_re, s5_lambda_im, s5_log_step, s5_b_re, s5_b_im, s5_c_re, s5_c_im,
                                     s5_d.reshape(depth, S5_GROUPS, S5_GROUP_DIM))

    y_p, y_s = x_prompt, x_sample
    ks, vs, s5r, s5i, hg = [], [], [], [], []
    for l in range(depth):
        p = {'norm_pre': norm_pre[l].reshape(1, d), 'norm_post': norm_post[l], 'w_in': w_in_bf,
             'fourier_w': fourier_w[l],
             's5_mats': s5_mats,
             's5_glu_w': s5_glu_w[l], 's5_glu_b': s5_glu_b[l],
             'q_norm': q_norm[l], 'k_norm': k_norm[l],
             'lower_bound': lower_bounds[l], 'hgrn_norm': hgrn_norm[l],
             'w_proj_a': wpa, 'w_proj_b': wpb, 'w_proj_c': wpc, 'w_proj_d': wpd,
             'w_out': w_out_bf}
        m_ctx = mod[l, bs].reshape(1, 1, 3 * d)
        sh, sc, gt = (m_ctx[..., i * d:(i + 1) * d] for i in range(3))
        y_p, (k_l, v_l, sr_l, si_l, hg_l) = _trunk_layer(y_p, sh, sc, gt, p, None, l, consts)
        ks.append(k_l.reshape(bp, lp, N_KV_HEADS, HEAD_DIM))
        vs.append(v_l.reshape(bp, lp, N_KV_HEADS, HEAD_DIM))
        s5r.append(sr_l)
        s5i.append(si_l)
        hg.append(jnp.swapaxes(hg_l, -1, -2))
        m_s = mod[l, :bs].reshape(bs, 1, 3 * d)
        sh, sc, gt = (m_s[..., i * d:(i + 1) * d] for i in range(3))
        ctx = (ctx_k, ctx_v, state_s5_re, state_s5_im, s0_t)
        y_s, _ = _trunk_layer(y_s, sh, sc, gt, p, ctx, l, consts)

    return (y_p, y_s, jnp.stack(ks, axis=1), jnp.stack(vs, axis=1),
            jnp.stack(s5r, axis=1), jnp.stack(s5i, axis=1), jnp.stack(hg, axis=1))
```

```python
import functools
import math

import jax
import jax.numpy as jnp
import numpy as np
from jax import lax
from jax.experimental import pallas as pl
from jax.experimental.pallas import tpu as pltpu

F32 = jnp.float32
BF16 = jnp.bfloat16

NORM_EPS = 1e-6
F_FLOOR = 1e-30
LOG2E = 1.4426950408889634

D_MODEL = 2048
N_BRANCHES = 4
FOURIER_WIDTH = 512
FOURIER_GROUPS = 4
FOURIER_GROUP_DIM = 128
S5_WIDTH = 512
S5_GROUP_DIM = 16
S5_GROUPS = 32
S5_STATE = 64
S5_CHUNK = 8
S5_GB = 8
S5_NBLK = S5_GROUPS // S5_GB
S5_K = S5_CHUNK * S5_GB * S5_GROUP_DIM
S5_PART = S5_GB * S5_STATE
S5_MAX_ROWS = 512
N_HEADS = 8
N_KV_HEADS = 2
HEAD_DIM = 128
Q_PER_KV = 4
ATTN_WIDTH = 1024
KV_WIDTH = 256
ROPE_THETA = 10000.0
GRID_W = 64
HGRN_HEADS = 4
HGRN_DK = 128
HGRN_DV = 128
HGRN_WIDTH = 512
HGRN_CHUNK = 64

COL_UA, COL_GA, COL_UB, COL_GB = 0, 512, 1024, 1536
COL_QC, COL_KC, COL_VC, COL_GC = 2048, 3072, 3328, 3584
COL_QD, COL_ID, COL_ZF, COL_ZB, COL_GD = 4608, 5120, 5632, 6144, 6656
COL_M = 7168
IN_COLS = 15360

FFT_RADIX = 8
FINAL_ROWS = 256
ATTN_KEY_CHUNK = 1024
ATTN_TQ = 512
INPROJ_TM = 1024
INPROJ_TN = 1536
INPROJ_ROWS = 256
VMEM_LIMIT = 56 * 1024 * 1024


def _cparams(n_grid):
    return pltpu.CompilerParams(dimension_semantics=("arbitrary",) * n_grid,
                                vmem_limit_bytes=VMEM_LIMIT)


def _sigmoid(x):
    return 1.0 / (1.0 + jnp.exp(-x))


def _gate_sigmoid(x):
    return 0.5 * jnp.tanh(0.5 * x) + 0.5


def _silu(x):
    return x * _gate_sigmoid(x)


def _dot(a, b):
    return jnp.dot(a, b, preferred_element_type=F32)


def _dot_nt(a, b):
    return lax.dot_general(a, b, (((1,), (1,)), ((), ())), preferred_element_type=F32)


def _dot_tn(a, b):
    return lax.dot_general(a, b, (((0,), (0,)), ((), ())), preferred_element_type=F32)


def _split2(x):
    hi = x.astype(BF16)
    lo = (x - hi.astype(F32)).astype(BF16)
    return hi, lo


def _split3(x):
    hi = x.astype(BF16)
    r = x - hi.astype(F32)
    mid = r.astype(BF16)
    lo = (r - mid.astype(F32)).astype(BF16)
    return hi, mid, lo


def _dot3(a, b):
    ah, al = _split2(a)
    bh, bl = _split2(b)
    return _dot(ah, bh) + (_dot(ah, bl) + _dot(al, bh))


def _mod_kernel(c_ref, w_ref, b_ref, o_ref):
    a = _silu(c_ref[...])
    o_ref[0] = _dot3(a, w_ref[0]) + b_ref[0]


def _modulation(c_all, w_mod, b_mod):
    depth, d, n = w_mod.shape
    rows = c_all.shape[0]
    tn = 1024
    return pl.pallas_call(
        _mod_kernel,
        grid=(depth, n // tn),
        in_specs=[pl.BlockSpec((rows, d), lambda l, j: (0, 0)),
                  pl.BlockSpec((1, d, tn), lambda l, j: (l, 0, j)),
                  pl.BlockSpec((1, 1, tn), lambda l, j: (l, 0, j))],
        out_specs=pl.BlockSpec((1, rows, tn), lambda l, j: (l, 0, j)),
        out_shape=jax.ShapeDtypeStruct((depth, rows, n), F32),
        compiler_params=_cparams(2),
        name="modulation",
    )(c_all, w_mod, b_mod.reshape(depth, 1, n))


def _inproj_kernel(x_ref, sh_ref, sc_ref, g_ref, w_ref, o_ref, ub_ref, h_scr):
    first = pl.program_id(2) == 0
    tm = h_scr.shape[0]

    @pl.when(first)
    def _():
        for r0 in range(0, tm, INPROJ_ROWS):
            rs = slice(r0, r0 + INPROJ_ROWS)
            x = x_ref[0, rs, :]
            ms = jnp.mean(x * x, axis=-1, keepdims=True)
            y = x * lax.rsqrt(ms + NORM_EPS) * g_ref[...]
            h = (y * (1.0 + sc_ref[0]) + sh_ref[0]).astype(BF16)
            h_scr[rs, :] = h
            res = _dot(h, w_ref[0])
            o_ref[0, rs, :] = res.astype(BF16)
            ub_ref[0, rs, :] = res[:, COL_UB:COL_UB + S5_WIDTH]

    @pl.when(jnp.logical_not(first))
    def _():
        o_ref[0] = _dot(h_scr[...], w_ref[0]).astype(BF16)


def _inproj(x, shift, scale, g, w_bf, layer, tm, tn):
    b, l, d = x.shape
    n = w_bf.shape[2]
    assert COL_UB + S5_WIDTH <= tn
    per_batch = shift.shape[0] > 1
    mod_map = (lambda bi, i, j: (bi, 0, 0)) if per_batch else (lambda bi, i, j: (0, 0, 0))
    return pl.pallas_call(
        _inproj_kernel,
        grid=(b, l // tm, n // tn),
        in_specs=[pl.BlockSpec((1, tm, d), lambda bi, i, j: (bi, i, 0)),
                  pl.BlockSpec((1, 1, d), mod_map),
                  pl.BlockSpec((1, 1, d), mod_map),
                  pl.BlockSpec((1, d), lambda bi, i, j: (0, 0)),
                  pl.BlockSpec((1, d, tn), lambda bi, i, j: (layer, 0, j))],
        out_specs=[pl.BlockSpec((1, tm, tn), lambda bi, i, j: (bi, i, j)),
                   pl.BlockSpec((1, tm, S5_WIDTH), lambda bi, i, j: (bi, i, 0))],
        out_shape=[jax.ShapeDtypeStruct((b, l, n), BF16),
                   jax.ShapeDtypeStruct((b, l, S5_WIDTH), F32)],
        scratch_shapes=[pltpu.VMEM((tm, d), BF16)],
        compiler_params=_cparams(3),
        name="inproj",
    )(x, shift, scale, g, w_bf)


def _dft_tables(l):
    r = FFT_RADIX
    l2 = l // r
    scale = 1.0 / math.sqrt(l * FOURIER_GROUP_DIM)
    k1 = np.arange(r, dtype=np.int64)[:, None]
    t2 = np.arange(l2, dtype=np.int64)[None, :]
    ang = 2.0 * np.pi * ((k1 * t2) % l) / l
    lanes = (l, FOURIER_GROUP_DIM)
    tw_re = jnp.broadcast_to(jnp.asarray(np.cos(ang).reshape(l, 1), F32), lanes)
    tw_im = jnp.broadcast_to(jnp.asarray(-np.sin(ang).reshape(l, 1), F32), lanes)
    k2 = np.arange(l2, dtype=np.int64)
    a3 = 2.0 * np.pi * ((k2[:, None] * k2[None, :]) % l2) / l2
    d3 = jnp.asarray(np.concatenate([np.cos(a3), np.sin(a3)], axis=1) * scale, F32).astype(BF16)
    c = np.arange(FOURIER_GROUP_DIM, dtype=np.int64)
    ac = 2.0 * np.pi * ((c[:, None] * c[None, :]) % FOURIER_GROUP_DIM) / FOURIER_GROUP_DIM
    cs = jnp.asarray(np.concatenate([np.cos(ac), np.sin(ac)], axis=1), F32).astype(BF16)
    return (tw_re, tw_im, d3), cs


def _cadd(a, b):
    return a[0] + b[0], a[1] + b[1]


def _csub(a, b):
    return a[0] - b[0], a[1] - b[1]


def _cmul_neg_i(a):
    return a[1], -a[0]


def _dft4(y0, y1, y2, y3):
    c0, c1 = _cadd(y0, y2), _cadd(y1, y3)
    d0, d1 = _csub(y0, y2), _cmul_neg_i(_csub(y1, y3))
    return _cadd(c0, c1), _cadd(d0, d1), _csub(c0, c1), _csub(d0, d1)


def _dft8(x):
    h = math.sqrt(0.5)
    a = [_cadd(x[n], x[n + 4]) for n in range(4)]
    d = [_csub(x[n], x[n + 4]) for n in range(4)]
    b = [d[0],
         ((d[1][0] + d[1][1]) * h, (d[1][1] - d[1][0]) * h),
         _cmul_neg_i(d[2]),
         ((d[3][1] - d[3][0]) * h, -(d[3][0] + d[3][1]) * h)]
    even, odd = _dft4(*a), _dft4(*b)
    return [even[0], odd[0], even[1], odd[1], even[2], odd[2], even[3], odd[3]]


def _fourier_kernel(u_ref, g_ref, cs_ref, twr_ref, twi_ref, d3_ref, w_ref, o_ref, b_scr, f_scr, *, l):
    r = FFT_RADIX
    l2 = l // r
    gd = FOURIER_GROUP_DIM
    t = _dot(u_ref[0].astype(BF16), cs_ref[...])
    blocks = [(t[j * l2:(j + 1) * l2, :gd], -t[j * l2:(j + 1) * l2, gd:]) for j in range(r)]
    for k1, (ar, ai) in enumerate(_dft8(blocks)):
        twr = twr_ref[k1 * l2:(k1 + 1) * l2, :]
        twi = twi_ref[k1 * l2:(k1 + 1) * l2, :]
        b_scr[k1, 0:l2, :] = (ar * twr - ai * twi).astype(BF16)
        b_scr[k1, l2:2 * l2, :] = (ar * twi + ai * twr).astype(BF16)
    for k1 in range(r):
        f_scr[pl.ds(k1, l2, stride=r), :] = _dot(d3_ref[...], b_scr[k1])
    y = _dot(f_scr[...].astype(BF16), w_ref[0].astype(BF16))
    o_ref[0] = (y * _silu(g_ref[0].astype(F32))).astype(BF16)


def _fourier_branch(proj, fourier_w, dft, cs):
    b, l, _ = proj.shape
    gd = FOURIER_GROUP_DIM
    l2 = l // FFT_RADIX
    tw_re, tw_im, d3 = dft
    return pl.pallas_call(
        functools.partial(_fourier_kernel, l=l),
        grid=(b, FOURIER_GROUPS),
        in_specs=[pl.BlockSpec((1, l, gd), lambda bi, g: (bi, 0, COL_UA // gd + g)),
                  pl.BlockSpec((1, l, gd), lambda bi, g: (bi, 0, COL_GA // gd + g)),
                  pl.BlockSpec((gd, 2 * gd), lambda bi, g: (0, 0)),
                  pl.BlockSpec((l, gd), lambda bi, g: (0, 0)),
                  pl.BlockSpec((l, gd), lambda bi, g: (0, 0)),
                  pl.BlockSpec((l2, 2 * l2), lambda bi, g: (0, 0)),
                  pl.BlockSpec((1, gd, gd), lambda bi, g: (g, 0, 0))],
        out_specs=pl.BlockSpec((1, l, gd), lambda bi, g: (bi, 0, g)),
        out_shape=jax.ShapeDtypeStruct((b, l, FOURIER_WIDTH), BF16),
        scratch_shapes=[pltpu.VMEM((FFT_RADIX, 2 * l2, gd), BF16), pltpu.VMEM((l, gd), F32)],
        compiler_params=_cparams(2),
        name="fourier",
    )(proj, proj, cs, tw_re, tw_im, d3, fourier_w)


def _s5_matrices(lam_re, lam_im, log_step, b_re, b_im, c_re, c_im, dskip):
    t = S5_CHUNK
    g, n, p = S5_GROUPS, S5_STATE, S5_GROUP_DIM
    hp = lax.Precision.HIGHEST
    lam_re = lam_re.astype(F32)
    lam_im = lam_im.astype(F32)
    step = jnp.exp(log_step.astype(F32))[..., None]
    rho = lam_re * step
    th = lam_im * step
    mag = jnp.exp(rho)
    lb_re = mag * jnp.cos(th)
    lb_im = mag * jnp.sin(th)
    nr = lb_re - 1.0
    den = lam_re * lam_re + lam_im * lam_im
    fr = (nr * lam_re + lb_im * lam_im) / den
    fi = (lb_im * lam_re - nr * lam_im) / den
    b_re = b_re.astype(F32)
    b_im = b_im.astype(F32)
    bb_re = fr[..., None] * b_re - fi[..., None] * b_im
    bb_im = fr[..., None] * b_im + fi[..., None] * b_re
    tau = jnp.arange(t + 1, dtype=F32)
    pw_mag = jnp.exp(rho[..., None] * tau)
    pw_re = pw_mag * jnp.cos(th[..., None] * tau)
    pw_im = pw_mag * jnp.sin(th[..., None] * tau)
    c_re = c_re.astype(F32)
    c_im = c_im.astype(F32)
    ca_re = c_re[..., None] * pw_re[:, :, None] - c_im[..., None] * pw_im[:, :, None]
    ca_im = c_re[..., None] * pw_im[:, :, None] + c_im[..., None] * pw_re[:, :, None]
    kern = (jnp.einsum('dgpnt,dgnq->dgtpq', ca_re, bb_re, precision=hp)
            - jnp.einsum('dgpnt,dgnq->dgtpq', ca_im, bb_im, precision=hp))[:, :, :t]
    sig = np.arange(t)[:, None]
    ta = np.arange(t)[None, :]
    idx_f = np.clip(ta - sig, 0, t - 1)
    idx_b = np.clip(sig - ta, 0, t - 1)
    m_f = jnp.asarray((ta >= sig).astype(np.float32))[None, :, :, None, None]
    m_b = jnp.asarray((sig >= ta).astype(np.float32))[None, :, :, None, None]
    tf = kern[0][:, idx_f] * m_f
    tb = kern[1][:, idx_b] * m_b
    eye_t = jnp.asarray(np.eye(t, dtype=np.float32))[None, :, :, None, None]
    dmat = dskip.astype(F32)[:, None, None, :, None] * jnp.asarray(np.eye(p, dtype=np.float32))[None, None, None]
    tt = tf + tb + eye_t * dmat
    nb, gb = S5_NBLK, S5_GB
    one = np.ones
    eye = lambda k: np.eye(k, dtype=np.float32)
    exp_t = jnp.asarray(np.einsum('tu,pq,b->tpubq', eye(t), eye(p), one(gb)).reshape(t * p, S5_K))
    exp_n = jnp.asarray(np.einsum('nm,b->nbm', eye(n), one(gb)).reshape(n, S5_PART))
    mask_t = jnp.asarray(np.einsum('ab,s,q,u,p->saqubp', eye(gb), one(t), one(p), one(t), one(p))
                         .reshape(S5_K, S5_K))
    mask_ba = jnp.asarray(np.einsum('ab,s,q,n->saqbn', eye(gb), one(t), one(p), one(n)).reshape(S5_K, S5_PART))
    t_small = tt.reshape(nb, gb, t, t, p, p).transpose(0, 2, 1, 5, 3, 4).reshape(nb, S5_K, t * p)
    tmat = jnp.matmul(t_small, exp_t, precision=hp) * mask_t

    def ba(d, powers):
        pr = pw_re[d][:, :, powers]
        pi = pw_im[d][:, :, powers]
        re = pr[..., None] * bb_re[d][:, :, None, :] - pi[..., None] * bb_im[d][:, :, None, :]
        im = pr[..., None] * bb_im[d][:, :, None, :] + pi[..., None] * bb_re[d][:, :, None, :]
        blk = lambda m: jnp.matmul(m.reshape(nb, gb, n, t, p).transpose(0, 3, 1, 4, 2).reshape(nb, S5_K, n),
                                   exp_n, precision=hp) * mask_ba
        return blk(re), blk(im)

    baf_re, baf_im = ba(0, np.arange(t - 1, -1, -1))
    bab_re, bab_im = ba(1, np.arange(t))
    bamat = jnp.concatenate([baf_re, bab_re, baf_im, bab_im], axis=-1)

    def ca(d, powers):
        blk = lambda m: jnp.matmul(exp_n.T, m[..., powers].reshape(nb, gb, p, n, t).transpose(0, 3, 4, 1, 2)
                                   .reshape(nb, n, S5_K), precision=hp) * mask_ba.T
        return blk(ca_re[d]), blk(ca_im[d])

    caf_re, caf_im = ca(0, np.arange(1, t + 1))
    cab_re, cab_im = ca(1, np.arange(t, 0, -1))
    camat = jnp.concatenate([caf_re, cab_re, -caf_im, -cab_im], axis=1)
    part = lambda m: m[..., t].reshape(nb, S5_PART)
    a_pow = jnp.stack([jnp.concatenate([part(pw_re[0]), part(pw_re[1])], axis=-1),
                       jnp.concatenate([part(pw_im[0]), part(pw_im[1])], axis=-1)], axis=1)
    return tmat.astype(BF16), bamat.astype(BF16), camat.astype(BF16), a_pow


def _s5_kernel(u_ref, t_ref, ba_ref, ca_ref, a_ref, h0_ref, y_ref, hf_ref, ub_scr, s_scr, hin_scr, y_scr,
               *, nc, bt):
    t, w, hp = S5_CHUNK, S5_GB * S5_GROUP_DIM, S5_PART
    for bi in range(bt):
        for tau in range(t):
            ub_scr[tau, pl.ds(bi, nc, stride=bt), :] = u_ref[bi, pl.ds(tau, nc, stride=t), :]
    ub = jnp.concatenate([ub_scr[tau] for tau in range(t)], axis=1).astype(BF16)
    s_scr[...] = _dot(ub, ba_ref[0, 0])
    ar_f, ar_b = a_ref[0, 0, 0:1, 0:hp], a_ref[0, 0, 0:1, hp:2 * hp]
    ai_f, ai_b = a_ref[0, 0, 1:2, 0:hp], a_ref[0, 0, 1:2, hp:2 * hp]

    def step(i, carry):
        re_f, im_f, re_b, im_b = carry
        rf = pl.ds(pl.multiple_of(i * bt, bt), bt)
        rb = pl.ds(pl.multiple_of((nc - 1 - i) * bt, bt), bt)
        hin_scr[rf, 0:hp] = re_f
        hin_scr[rf, 2 * hp:3 * hp] = im_f
        hin_scr[rb, hp:2 * hp] = re_b
        hin_scr[rb, 3 * hp:4 * hp] = im_b
        n_re_f = ar_f * re_f - ai_f * im_f + s_scr[rf, 0:hp]
        n_im_f = ar_f * im_f + ai_f * re_f + s_scr[rf, 2 * hp:3 * hp]
        n_re_b = ar_b * re_b - ai_b * im_b + s_scr[rb, hp:2 * hp]
        n_im_b = ar_b * im_b + ai_b * re_b + s_scr[rb, 3 * hp:4 * hp]
        return n_re_f, n_im_f, n_re_b, n_im_b

    h0 = h0_ref[...].reshape(bt, 4 * hp)
    init = (h0[:, 0:hp], h0[:, 2 * hp:3 * hp], h0[:, hp:2 * hp], h0[:, 3 * hp:4 * hp])
    fin = lax.fori_loop(0, nc, step, init, unroll=8)
    hf_ref[...] = jnp.concatenate([fin[0], fin[2], fin[1], fin[3]], axis=1).reshape(hf_ref.shape)
    y = _dot(ub, t_ref[0, 0]) + _dot(hin_scr[...].astype(BF16), ca_ref[0, 0])
    for tau in range(t):
        y_scr[tau] = y[:, tau * w:(tau + 1) * w]
    for bi in range(bt):
        for tau in range(t):
            y_ref[bi, pl.ds(tau, nc, stride=t), :] = y_scr[tau, pl.ds(bi, nc, stride=bt), :]


def _s5_core(ub, tmat, bamat, camat, a_pow, h0, layer):
    b, l, _ = ub.shape
    nc = l // S5_CHUNK
    w = S5_GB * S5_GROUP_DIM
    bt = b if nc * b <= S5_MAX_ROWS else 1
    if bt == 1:
        h0 = h0.reshape(S5_NBLK, b, 1, 4 * S5_PART)
        st_spec = pl.BlockSpec((1, 1, 1, 4 * S5_PART), lambda j, bi: (j, bi, 0, 0))
    else:
        st_spec = pl.BlockSpec((1, bt, 4 * S5_PART), lambda j, bi: (j, bi, 0))
    rows = nc * bt
    y, hf = pl.pallas_call(
        functools.partial(_s5_kernel, nc=nc, bt=bt),
        grid=(S5_NBLK, b // bt),
        in_specs=[pl.BlockSpec((bt, l, w), lambda j, bi: (bi, 0, j)),
                  pl.BlockSpec((1, 1, S5_K, S5_K), lambda j, bi: (layer, j, 0, 0)),
                  pl.BlockSpec((1, 1, S5_K, 4 * S5_PART), lambda j, bi: (layer, j, 0, 0)),
                  pl.BlockSpec((1, 1, 4 * S5_PART, S5_K), lambda j, bi: (layer, j, 0, 0)),
                  pl.BlockSpec((1, 1, 2, 2 * S5_PART), lambda j, bi: (layer, j, 0, 0)),
                  st_spec],
        out_specs=[pl.BlockSpec((bt, l, w), lambda j, bi: (bi, 0, j)), st_spec],
        out_shape=[jax.ShapeDtypeStruct((b, l, S5_WIDTH), F32),
                   jax.ShapeDtypeStruct(h0.shape, F32)],
        scratch_shapes=[pltpu.VMEM((S5_CHUNK, rows, w), F32), pltpu.VMEM((rows, 4 * S5_PART), F32),
                        pltpu.VMEM((rows, 4 * S5_PART), F32), pltpu.VMEM((S5_CHUNK, rows, w), F32)],
        compiler_params=_cparams(2),
        name="s5_core",
    )(ub, tmat, bamat, camat, a_pow, h0)
    return y, hf.reshape(S5_NBLK, b, 4 * S5_PART)


def _s5_glu_kernel(y_ref, g_ref, w_ref, b_ref, o_ref):
    y = y_ref[0]
    y = 0.5 * y * (1.0 + jnp.tanh(math.sqrt(2.0 / math.pi) * (y + 0.044715 * (y * y * y))))
    z = _dot(y.astype(BF16), w_ref[...].astype(BF16)) + b_ref[...]
    o_ref[0] = (y * _sigmoid(z) * _silu(g_ref[0].astype(F32))).astype(BF16)


def _s5_glu(y, proj, glu_w, glu_b, tm):
    b, l, w = y.shape
    return pl.pallas_call(
        _s5_glu_kernel,
        grid=(b, l // tm),
        in_specs=[pl.BlockSpec((1, tm, w), lambda bi, i: (bi, i, 0)),
                  pl.BlockSpec((1, tm, w), lambda bi, i: (bi, i, COL_GB // w)),
                  pl.BlockSpec((w, w), lambda bi, i: (0, 0)),
                  pl.BlockSpec((1, w), lambda bi, i: (0, 0))],
        out_specs=pl.BlockSpec((1, tm, w), lambda bi, i: (bi, i, 0)),
        out_shape=jax.ShapeDtypeStruct((b, l, w), BF16),
        compiler_params=_cparams(2),
        name="s5_glu",
    )(y, proj, glu_w, glu_b.reshape(1, w))


def _s5_branch(proj, ub, mats, layer, glu_w, glu_b, h0_re, h0_im):
    b, l, _ = proj.shape
    g, n = S5_GROUPS, S5_STATE
    if h0_re is None:
        h0 = jnp.zeros((S5_NBLK, b, 4 * S5_PART), F32)
    else:
        blk = lambda h: h.astype(F32).reshape(b, S5_NBLK, S5_PART).transpose(1, 0, 2)
        h0 = jnp.concatenate([blk(h0_re[:, 0]), blk(h0_re[:, 1]), blk(h0_im[:, 0]), blk(h0_im[:, 1])], axis=-1)
    y, hf = _s5_core(ub, *mats, h0, layer)
    out = _s5_glu(y, proj, glu_w, glu_b, min(l, 1024))
    part = lambda i: hf[..., i * S5_PART:(i + 1) * S5_PART].transpose(1, 0, 2).reshape(b, g, n)
    fin_re = jnp.stack([part(0), part(1)], axis=1)
    fin_im = jnp.stack([part(2), part(3)], axis=1)
    return out, fin_re, fin_im


def _rope_tables(l):
    rows = l // GRID_W
    row = jnp.broadcast_to(jnp.arange(rows, dtype=F32)[:, None], (rows, GRID_W)).reshape(-1)
    col = jnp.broadcast_to(jnp.arange(GRID_W, dtype=F32)[None, :], (rows, GRID_W)).reshape(-1)
    half = HEAD_DIM // 2
    inv = ROPE_THETA ** (-jnp.arange(0, half, 2, dtype=F32) / half)
    ar = row[:, None] * inv
    ac = col[:, None] * inv
    cos = jnp.concatenate([jnp.cos(ar), jnp.cos(ar), jnp.cos(ac), jnp.cos(ac)], axis=1)
    sin = jnp.concatenate([-jnp.sin(ar), jnp.sin(ar), -jnp.sin(ac), jnp.sin(ac)], axis=1)
    return cos, sin


def _qk_kernel(*refs, rope, want_kf):
    q_ref, k_ref, v_ref, qn_ref, kn_ref = refs[:5]
    pos = 5
    if rope:
        cos_ref, sin_ref = refs[pos:pos + 2]
        pos += 2
    qo_ref, ko_ref, vo_ref = refs[pos:pos + 3]
    pos += 3
    if want_kf:
        kf_ref = refs[pos]
    hd = HEAD_DIM
    tm = q_ref.shape[1]
    quarter = hd // 4
    if rope:
        cos = cos_ref[...]
        sin = sin_ref[...]
        first = (lax.broadcasted_iota(jnp.int32, (tm, hd), 1) % (2 * quarter)) < quarter

    def norm(x, gain):
        ms = jnp.mean(x * x, axis=-1, keepdims=True)
        return x * lax.rsqrt(ms + NORM_EPS) * gain

    def rot(x):
        if not rope:
            return x
        nxt = pltpu.roll(x, hd - quarter, 1)
        prv = pltpu.roll(x, quarter, 1)
        return x * cos + jnp.where(first, nxt, prv) * sin

    qscale = (hd ** -0.5) * LOG2E
    for h in range(N_HEADS):
        x = norm(q_ref[0, :, h * hd:(h + 1) * hd].astype(F32), qn_ref[...])
        qo_ref[0, :, h * hd:(h + 1) * hd] = (rot(x) * qscale).astype(BF16)
    for h in range(N_KV_HEADS):
        x = norm(k_ref[0, :, h * hd:(h + 1) * hd].astype(F32), kn_ref[...])
        if want_kf:
            kf_ref[0, :, h * hd:(h + 1) * hd] = x
        ko_ref[0, :, h * hd:(h + 1) * hd] = rot(x).astype(BF16)
        vo_ref[0, h] = v_ref[0, :, h * hd:(h + 1) * hd].astype(F32).T.astype(BF16)


def _qk_prep(proj, q_norm, k_norm, rope_tabs, want_kf, tm):
    b, l, _ = proj.shape
    rope = rope_tabs is not None
    in_specs = [pl.BlockSpec((1, tm, ATTN_WIDTH), lambda bi, i: (bi, i, COL_QC // ATTN_WIDTH)),
                pl.BlockSpec((1, tm, KV_WIDTH), lambda bi, i: (bi, i, COL_KC // KV_WIDTH)),
                pl.BlockSpec((1, tm, KV_WIDTH), lambda bi, i: (bi, i, COL_VC // KV_WIDTH)),
                pl.BlockSpec((1, HEAD_DIM), lambda bi, i: (0, 0)),
                pl.BlockSpec((1, HEAD_DIM), lambda bi, i: (0, 0))]
    args = [proj, proj, proj, q_norm.reshape(1, HEAD_DIM), k_norm.reshape(1, HEAD_DIM)]
    if rope:
        in_specs += [pl.BlockSpec((tm, HEAD_DIM), lambda bi, i: (i, 0))] * 2
        args += list(rope_tabs)
    out_specs = [pl.BlockSpec((1, tm, ATTN_WIDTH), lambda bi, i: (bi, i, 0)),
                 pl.BlockSpec((1, tm, KV_WIDTH), lambda bi, i: (bi, i, 0)),
                 pl.BlockSpec((1, N_KV_HEADS, HEAD_DIM, tm), lambda bi, i: (bi, 0, 0, i))]
    out_shape = [jax.ShapeDtypeStruct((b, l, ATTN_WIDTH), BF16),
                 jax.ShapeDtypeStruct((b, l, KV_WIDTH), BF16),
                 jax.ShapeDtypeStruct((b, N_KV_HEADS, HEAD_DIM, l), BF16)]
    if want_kf:
        out_specs.append(pl.BlockSpec((1, tm, KV_WIDTH), lambda bi, i: (bi, i, 0)))
        out_shape.append(jax.ShapeDtypeStruct((b, l, KV_WIDTH), F32))
    return pl.pallas_call(
        functools.partial(_qk_kernel, rope=rope, want_kf=want_kf),
        grid=(b, l // tm),
        in_specs=in_specs, out_specs=out_specs, out_shape=out_shape,
        compiler_params=_cparams(2),
        name="qk_prep",
    )(*args)


def _attn_kernel(*refs, has_ctx):
    s_scr = refs[-1]
    q_ref, k_ref, v_ref = refs[:3]
    if has_ctx:
        ck_ref, cv_ref, g_ref, o_ref = refs[3:7]
        ck = ck_ref[0, 0].astype(BF16)
        cvt = cv_ref[0, 0, 0].astype(BF16)
        past = ck.shape[0]
    else:
        g_ref, o_ref = refs[3:5]
    hd = HEAD_DIM
    tq = q_ref.shape[1]
    lk = k_ref.shape[1]
    cr = min(lk, ATTN_KEY_CHUNK)
    chunks = [(c * cr, cr, (lambda c=c: k_ref[0, c * cr:(c + 1) * cr, :]),
               (lambda c=c: v_ref[0, 0, :, c * cr:(c + 1) * cr])) for c in range(lk // cr)]
    if has_ctx:
        chunks.append((lk, past, lambda: ck, lambda: cvt))
    qs = [q_ref[0, :, g * hd:(g + 1) * hd] for g in range(Q_PER_KV)]
    gate = _silu(g_ref[0].astype(F32))

    def score(g, chunk, m):
        off, rows, keys, _ = chunk
        s = _dot_nt(keys(), qs[g])
        s_scr[g % 2, off:off + rows, :] = s
        return jnp.maximum(m, jnp.max(s, axis=0, keepdims=True))

    def accumulate(g, chunk, m, den, ot):
        off, rows, _, values_t = chunk
        p = jnp.exp2(s_scr[g % 2, off:off + rows, :] - m)
        return den + jnp.sum(p, axis=0, keepdims=True), ot + _dot(values_t(), p.astype(BF16))

    neg = jnp.full((1, tq), -jnp.inf, F32)
    m_cur = neg
    for chunk in chunks:
        m_cur = score(0, chunk, m_cur)
    for g in range(Q_PER_KV):
        m_next = neg
        den = jnp.zeros((1, tq), F32)
        ot = jnp.zeros((hd, tq), F32)
        for chunk in chunks:
            if g + 1 < Q_PER_KV:
                m_next = score(g + 1, chunk, m_next)
            den, ot = accumulate(g, chunk, m_cur, den, ot)
        m_cur = m_next
        o = (ot / den).T
        o_ref[0, :, g * hd:(g + 1) * hd] = (o * gate[:, g * hd:(g + 1) * hd]).astype(BF16)


def _attention(qs, kr, vt, proj, ctx_k, ctx_v, layer, tq):
    b, l, _ = qs.shape
    gw = Q_PER_KV * HEAD_DIM
    has_ctx = ctx_k is not None
    in_specs = [pl.BlockSpec((1, tq, gw), lambda bi, h, i: (bi, i, h)),
                pl.BlockSpec((1, l, HEAD_DIM), lambda bi, h, i: (bi, 0, h)),
                pl.BlockSpec((1, 1, HEAD_DIM, l), lambda bi, h, i: (bi, h, 0, 0))]
    args = [qs, kr, vt]
    if has_ctx:
        past = ctx_k.shape[2]
        in_specs += [pl.BlockSpec((1, 1, past, HEAD_DIM), lambda bi, h, i: (bi, layer, 0, h)),
                     pl.BlockSpec((1, 1, 1, HEAD_DIM, past), lambda bi, h, i: (bi, layer, h, 0, 0))]
        args += [ctx_k, ctx_v]
    in_specs.append(pl.BlockSpec((1, tq, gw), lambda bi, h, i: (bi, i, COL_GC // gw + h)))
    args.append(proj)
    return pl.pallas_call(
        functools.partial(_attn_kernel, has_ctx=has_ctx),
        grid=(b, N_KV_HEADS, l // tq),
        in_specs=in_specs,
        out_specs=pl.BlockSpec((1, tq, gw), lambda bi, h, i: (bi, i, h)),
        out_shape=jax.ShapeDtypeStruct((b, l, ATTN_WIDTH), BF16),
        scratch_shapes=[pltpu.VMEM((2, l + (ctx_k.shape[2] if has_ctx else 0), tq), F32)],
        compiler_params=_cparams(3),
        name="attention",
    )(*args)


HGRN_COARSE = (32, 16, 8, 4)
HGRN_FINE = (2, 1)
HGRN_LEVELS = HGRN_COARSE + HGRN_FINE
HGRN_CHUNKS_PER_TRIP = 4


def _hgrn_tables():
    c = HGRN_CHUNK
    t = np.arange(c)
    w01, sm = [], []
    for d in range(2):
        cum = (t[None, :] <= t[:, None]) if d == 0 else (t[None, :] >= t[:, None])
        cum = cum.astype(np.float32)
        w01.append(np.concatenate([cum] + [cum[_hgrn_ref_row(t, hs, d)] for hs in HGRN_FINE], axis=0))
        sd = []
        for hs in HGRN_LEVELS:
            is_q = _hgrn_is_query(t, hs, d)
            same = (t[:, None] // (2 * hs)) == (t[None, :] // (2 * hs))
            sd.append((same & is_q[:, None] & ~is_q[None, :]).astype(np.float32))
        sd.append(np.eye(c, dtype=np.float32))
        sm.append(np.stack(sd))
    return jnp.asarray(np.stack(w01), BF16), jnp.asarray(np.stack(sm), F32)


def _hgrn_ref_row(t, hs, d):
    base = t & ~(2 * hs - 1)
    return base + hs - 1 if d == 0 else base + hs


def _hgrn_is_query(t, hs, d):
    return ((t & hs) != 0) if d == 0 else ((t & hs) == 0)


def _hgrn_cums(z, lbv, w01):
    f = lbv + (1.0 - lbv) * _sigmoid(z)
    p1, p2, p3 = _split3(jnp.log2(jnp.maximum(f, F_FLOOR)))
    return 1.0 - f, _dot(w01, p1) + (_dot(w01, p2) + _dot(w01, p3))


def _hgrn_products(q, k, allc, rows, d):
    c = HGRN_CHUNK
    sub = 8
    cum = allc[0:c]
    refs = {}
    for hs in HGRN_COARSE:
        pieces = []
        for j in range(c // sub):
            rr = int(_hgrn_ref_row(np.int64(j * sub), hs, d))
            pieces.append(jnp.broadcast_to(cum[rr:rr + 1, :], (sub, HGRN_DK)))
        refs[hs] = jnp.concatenate(pieces, axis=0)
    for i, hs in enumerate(HGRN_FINE):
        refs[hs] = allc[(i + 1) * c:(i + 2) * c]
    lrow = c - 1 if d == 0 else 0
    last = cum[lrow:lrow + 1, :]
    raws = [_dot_nt(q.astype(BF16), k.astype(BF16))]
    for hs in HGRN_LEVELS:
        is_q = ((rows & hs) != 0) if d == 0 else ((rows & hs) == 0)
        x = (jnp.where(is_q, q, k) * jnp.exp2((cum - refs[hs]) * jnp.where(is_q, 1.0, -1.0))).astype(BF16)
        raws.append(_dot_nt(x, x))
    qi = (q * jnp.exp2(cum)).astype(BF16)
    ks = (k * jnp.exp2(last - cum)).astype(BF16)
    return raws, qi, ks, jnp.exp2(last)


def _hgrn_kernel(*refs, nc, has_s0, want_state):
    q_ref, v_ref, zf_ref, zb_ref, g_ref, lb_ref, ng_ref, w01_ref, sm_ref = refs[:9]
    pos = 9
    if has_s0:
        s0_ref = refs[pos]
        pos += 1
    o_ref = refs[pos]
    pos += 1
    if want_state:
        sf_ref = refs[pos]
        pos += 1
    o_scr, st_scr = refs[pos:pos + 2]
    c = HGRN_CHUNK
    l = nc * c
    for d in range(2):
        if has_s0:
            st_scr[d] = s0_ref[0, d, 0]
        else:
            st_scr[d] = jnp.zeros((HGRN_DV, HGRN_DK), F32)
    rows = lax.broadcasted_iota(jnp.int32, (c, HGRN_DK), 0)

    per_trip = math.gcd(nc, HGRN_CHUNKS_PER_TRIP)

    def body(i, carry):
        insts = []
        for u in range(per_trip):
            for d in range(2):
                ci = per_trip * i + u if d == 0 else nc - 1 - (per_trip * i + u)
                insts.append((d, pl.multiple_of(ci * c, c)))
        gates = [_hgrn_cums((zf_ref, zb_ref)[d][0, pl.ds(r, c), :].astype(F32), lb_ref[d, 0], w01_ref[d])
                 for d, r in insts]
        prods = [_hgrn_products(q_ref[0, pl.ds(r, c), :].astype(F32), k, allc, rows, d)
                 for (d, r), (k, allc) in zip(insts, gates)]
        nl = len(HGRN_LEVELS)
        for (d, r), (raws, qi, ks, dec) in zip(insts, prods):
            scores = sm_ref[d, nl] * raws[0]
            for li in range(nl):
                scores = scores + sm_ref[d, li] * raws[li + 1]
            vb = v_ref[0, pl.ds(r, c), :].astype(BF16)
            st = st_scr[d]
            o_scr[d, pl.ds(r, c), :] = _dot(scores.astype(BF16), vb) + _dot_nt(qi, st.astype(BF16))
            st_scr[d] = dec * st + _dot_tn(vb, ks)
        return carry

    lax.fori_loop(0, nc // per_trip, body, 0)
    if want_state:
        for d in range(2):
            sf_ref[0, d, 0] = st_scr[d]

    fb = min(l, 256)

    def finish(i, carry):
        r = pl.multiple_of(i * fb, fb)
        tot = o_scr[0, pl.ds(r, fb), :] + o_scr[1, pl.ds(r, fb), :]
        ms = jnp.mean(tot * tot, axis=-1, keepdims=True)
        y = tot * lax.rsqrt(ms + NORM_EPS) * ng_ref[...]
        o_ref[0, pl.ds(r, fb), :] = (y * _silu(g_ref[0, pl.ds(r, fb), :].astype(F32))).astype(BF16)
        return carry

    lax.fori_loop(0, l // fb, finish, 0)


def _hgrn_branch(proj, lower_bound, norm_g, tables, s0_t, want_state):
    b, l, _ = proj.shape
    nc = l // HGRN_CHUNK
    w = HGRN_DK
    has_s0 = s0_t is not None
    w01, sm = tables

    def col(off):
        return lambda bi, h: (bi, 0, off // w + h)

    in_specs = [pl.BlockSpec((1, l, w), col(COL_QD)),
                pl.BlockSpec((1, l, w), col(COL_ID)),
                pl.BlockSpec((1, l, w), col(COL_ZF)),
                pl.BlockSpec((1, l, w), col(COL_ZB)),
                pl.BlockSpec((1, l, w), col(COL_GD)),
                pl.BlockSpec((2, 1, 1, w), lambda bi, h: (0, h, 0, 0)),
                pl.BlockSpec((1, w), lambda bi, h: (0, 0)),
                pl.BlockSpec(w01.shape, lambda bi, h: (0, 0, 0)),
                pl.BlockSpec(sm.shape, lambda bi, h: (0, 0, 0, 0))]
    args = [proj, proj, proj, proj, proj, lower_bound.reshape(2, HGRN_HEADS, 1, w), norm_g.reshape(1, w),
            w01, sm]
    st_spec = pl.BlockSpec((1, 2, 1, HGRN_DV, HGRN_DK), lambda bi, h: (bi, 0, h, 0, 0))
    if has_s0:
        in_specs.append(st_spec)
        args.append(s0_t)
    out_specs = [pl.BlockSpec((1, l, w), lambda bi, h: (bi, 0, h))]
    out_shape = [jax.ShapeDtypeStruct((b, l, HGRN_HEADS * HGRN_DV), BF16)]
    if want_state:
        out_specs.append(st_spec)
        out_shape.append(jax.ShapeDtypeStruct((b, 2, HGRN_HEADS, HGRN_DV, HGRN_DK), F32))
    res = pl.pallas_call(
        functools.partial(_hgrn_kernel, nc=nc, has_s0=has_s0, want_state=want_state),
        grid=(b, HGRN_HEADS),
        in_specs=in_specs, out_specs=out_specs, out_shape=out_shape,
        scratch_shapes=[pltpu.VMEM((2, l, HGRN_DV), F32), pltpu.VMEM((2, HGRN_DV, HGRN_DK), F32)],
        compiler_params=_cparams(2),
        name="hgrn",
    )(*args)
    return res if want_state else (res[0], None)


def _final_kernel(x_ref, gt_ref, ma_ref, mb_ref, mc_ref, md_ref, ya_ref, yb_ref, yc_ref, yd_ref,
                  wa_ref, wb_ref, wc_ref, wd_ref, wo_ref, gp_ref, o_ref, acc, *, nct):
    ci = pl.program_id(2)
    tm = acc.shape[0]
    subs = [slice(r0, r0 + FINAL_ROWS) for r0 in range(0, tm, FINAL_ROWS)]
    m_refs = (ma_ref, mb_ref, mc_ref, md_ref)
    y_refs = (ya_ref, yb_ref, yc_ref, yd_ref)
    w_refs = (wa_ref, wb_ref, wc_ref, wd_ref)

    def step(first, last):
        dots = [[_dot(y_ref[0, rs, :], w_ref[0]) for y_ref, w_ref in zip(y_refs, w_refs)] for rs in subs]
        for rs, branch in zip(subs, dots):
            mixed = None
            for m_ref, dot in zip(m_refs, branch):
                term = _gate_sigmoid(m_ref[0, rs, :].astype(F32)) * dot
                mixed = term if mixed is None else mixed + term
            contrib = _dot(mixed.astype(BF16), wo_ref[0])
            out = contrib if first else acc[rs, :] + contrib
            if last:
                ms = jnp.mean(out * out, axis=-1, keepdims=True)
                y = out * lax.rsqrt(ms + NORM_EPS) * gp_ref[...]
                o_ref[0, rs, :] = x_ref[0, rs, :] + gt_ref[0] * y
            else:
                acc[rs, :] = out

    if nct == 1:
        step(True, True)
    else:
        pl.when(ci == 0)(lambda: step(True, False))
        pl.when(jnp.logical_and(ci > 0, ci < nct - 1))(lambda: step(False, False))
        pl.when(ci == nct - 1)(lambda: step(False, True))


def _final(x, gate, proj, ys, w_projs, w_out, norm_post, layer, tm, tc):
    b, l, d = x.shape
    nct = d // tc
    per_batch = gate.shape[0] > 1
    gate_map = (lambda bi, i, c: (bi, 0, 0)) if per_batch else (lambda bi, i, c: (0, 0, 0))

    def mspec(j):
        return pl.BlockSpec((1, tm, tc), lambda bi, i, c: (bi, i, (COL_M + j * d) // tc + c))

    in_specs = [pl.BlockSpec((1, tm, d), lambda bi, i, c: (bi, i, 0)),
                pl.BlockSpec((1, 1, d), gate_map)]
    in_specs += [mspec(j) for j in range(N_BRANCHES)]
    in_specs += [pl.BlockSpec((1, tm, y.shape[-1]), lambda bi, i, c: (bi, i, 0)) for y in ys]
    in_specs += [pl.BlockSpec((1, w.shape[1], tc), lambda bi, i, c: (layer, 0, c)) for w in w_projs]
    in_specs += [pl.BlockSpec((1, tc, d), lambda bi, i, c: (layer, c, 0)),
                 pl.BlockSpec((1, d), lambda bi, i, c: (0, 0))]
    return pl.pallas_call(
        functools.partial(_final_kernel, nct=nct),
        grid=(b, l // tm, nct),
        in_specs=in_specs,
        out_specs=pl.BlockSpec((1, tm, d), lambda bi, i, c: (bi, i, 0)),
        out_shape=jax.ShapeDtypeStruct((b, l, d), F32),
        scratch_shapes=[pltpu.VMEM((tm, d), F32)],
        compiler_params=_cparams(3),
        name="merge_out",
    )(x, gate, proj, proj, proj, proj, *ys, *w_projs, w_out, norm_post.reshape(1, d))


def _trunk_layer(x, shift, scale, gate, p, ctx, layer, consts):
    b, l, d = x.shape
    shared_mod = shift.shape[0] == 1
    if shared_mod:
        proj, ub = _inproj(x.reshape(1, b * l, d), shift, scale, p['norm_pre'], p['w_in'], layer,
                           INPROJ_TM, INPROJ_TN)
        proj = proj.reshape(b, l, IN_COLS)
        ub = ub.reshape(b, l, S5_WIDTH)
    else:
        proj, ub = _inproj(x, shift, scale, p['norm_pre'], p['w_in'], layer, INPROJ_TM, INPROJ_TN)

    dl, cs = consts['dft'][l]
    y_a = _fourier_branch(proj, p['fourier_w'], dl, cs)

    if ctx is None:
        y_b, s5_re, s5_im = _s5_branch(proj, ub, p['s5_mats'], layer, p['s5_glu_w'], p['s5_glu_b'], None, None)
        qs, kr, vt, kf = _qk_prep(proj, p['q_norm'], p['k_norm'], None, True, min(l, 512))
        y_c = _attention(qs, kr, vt, proj, None, None, layer, min(l, ATTN_TQ))
        y_d, hg = _hgrn_branch(proj, p['lower_bound'], p['hgrn_norm'], consts['hgrn'], None, True)
    else:
        cache_k, cache_v, st_re, st_im, s0_t = ctx
        y_b, _, _ = _s5_branch(proj, ub, p['s5_mats'], layer, p['s5_glu_w'], p['s5_glu_b'],
                               st_re[:, layer], st_im[:, layer])
        qs, kr, vt = _qk_prep(proj, p['q_norm'], p['k_norm'], consts['rope'], False, min(l, 512))
        y_c = _attention(qs, kr, vt, proj, cache_k, cache_v, layer, min(l, ATTN_TQ))
        y_d, _ = _hgrn_branch(proj, p['lower_bound'], p['hgrn_norm'], consts['hgrn'], s0_t[:, layer], False)

    ys = (y_a, y_b, y_c, y_d)
    wps = (p['w_proj_a'], p['w_proj_b'], p['w_proj_c'], p['w_proj_d'])
    if shared_mod:
        n = b * l
        x_new = _final(x.reshape(1, n, d), gate, proj.reshape(1, n, IN_COLS),
                       tuple(y.reshape(1, n, y.shape[-1]) for y in ys), wps, p['w_out'], p['norm_post'], layer,
                       512, 512)
        x_new = x_new.reshape(b, l, d)
    else:
        x_new = _final(x, gate, proj, ys, wps, p['w_out'], p['norm_post'], layer, 512, 512)

    if ctx is None:
        v_f = proj[:, :, COL_VC:COL_VC + KV_WIDTH].astype(F32)
        return x_new, (kf, v_f, s5_re, s5_im, hg)
    return x_new, None


def kernel(x_prompt, x_sample, c, cache_k, cache_v, state_s5_re, state_s5_im, state_hgrn, c_ctx,
           norm_pre, norm_post, w_mod, b_mod, w_in, fourier_w,
           s5_lambda_re, s5_lambda_im, s5_log_step, s5_b_re, s5_b_im, s5_c_re, s5_c_im,
           s5_d, s5_glu_w, s5_glu_b, q_norm, k_norm, hgrn_lb_logits, hgrn_norm,
           w_proj_a, w_proj_b, w_proj_c, w_proj_d, w_out):
    depth = w_in.shape[0]
    bp, lp, d = x_prompt.shape
    bs, ls, _ = x_sample.shape

    lb_w = jax.nn.softmax(hgrn_lb_logits.astype(F32), axis=0)
    lower_bounds = jnp.cumsum(lb_w, axis=0) - lb_w[0]

    rows = ((bs + 1 + 7) // 8) * 8
    c_all = jnp.zeros((rows, d), F32).at[:bs].set(c).at[bs].set(c_ctx)
    mod = _modulation(c_all, w_mod, b_mod)

    w_in_bf = w_in.astype(BF16)
    wpa, wpb, wpc, wpd = (w.astype(BF16) for w in (w_proj_a, w_proj_b, w_proj_c, w_proj_d))
    w_out_bf = w_out.astype(BF16)

    consts = {'dft': {lp: _dft_tables(lp), ls: _dft_tables(ls)},
              'rope': _rope_tables(ls),
              'hgrn': _hgrn_tables()}
    past = cache_k.shape[2]
    ctx_k = cache_k.reshape(bs, depth, past, KV_WIDTH)
    ctx_v = cache_v.transpose(0, 1, 3, 4, 2)
    s0_t = jnp.swapaxes(state_hgrn.astype(F32), -1, -2)

    s5_mats = jax.vmap(_s5_matrices)(s5_lambda_re, s5_lambda_im, s5_log_step, s5_b_re, s5_b_im, s5_c_re, s5_c_im,
                                     s5_d.reshape(depth, S5_GROUPS, S5_GROUP_DIM))

    y_p, y_s = x_prompt, x_sample
    ks, vs, s5r, s5i, hg = [], [], [], [], []
    for l in range(depth):
        p = {'norm_pre': norm_pre[l].reshape(1, d), 'norm_post': norm_post[l], 'w_in': w_in_bf,
             'fourier_w': fourier_w[l],
             's5_mats': s5_mats,
             's5_glu_w': s5_glu_w[l], 's5_glu_b': s5_glu_b[l],
             'q_norm': q_norm[l], 'k_norm': k_norm[l],
             'lower_bound': lower_bounds[l], 'hgrn_norm': hgrn_norm[l],
             'w_proj_a': wpa, 'w_proj_b': wpb, 'w_proj_c': wpc, 'w_proj_d': wpd,
             'w_out': w_out_bf}
        m_ctx = mod[l, bs].reshape(1, 1, 3 * d)
        sh, sc, gt = (m_ctx[..., i * d:(i + 1) * d] for i in range(3))
        y_p, (k_l, v_l, sr_l, si_l, hg_l) = _trunk_layer(y_p, sh, sc, gt, p, None, l, consts)
        ks.append(k_l.reshape(bp, lp, N_KV_HEADS, HEAD_DIM))
        vs.append(v_l.reshape(bp, lp, N_KV_HEADS, HEAD_DIM))
        s5r.append(sr_l)
        s5i.append(si_l)
        hg.append(jnp.swapaxes(hg_l, -1, -2))
        m_s = mod[l, :bs].reshape(bs, 1, 3 * d)
        sh, sc, gt = (m_s[..., i * d:(i + 1) * d] for i in range(3))
        ctx = (ctx_k, ctx_v, state_s5_re, state_s5_im, s0_t)
        y_s, _ = _trunk_layer(y_s, sh, sc, gt, p, ctx, l, consts)

    return (y_p, y_s, jnp.stack(ks, axis=1), jnp.stack(vs, axis=1),
            jnp.stack(s5r, axis=1), jnp.stack(s5i, axis=1), jnp.stack(hg, axis=1))
```

```python
import functools
import math

import jax
import jax.numpy as jnp
import numpy as np
from jax import lax
from jax.experimental import pallas as pl
from jax.experimental.pallas import tpu as pltpu

F32 = jnp.float32
BF16 = jnp.bfloat16

NORM_EPS = 1e-6
F_FLOOR = 1e-30
LOG2E = 1.4426950408889634

D_MODEL = 2048
N_BRANCHES = 4
FOURIER_WIDTH = 512
FOURIER_GROUPS = 4
FOURIER_GROUP_DIM = 128
S5_WIDTH = 512
S5_GROUP_DIM = 16
S5_GROUPS = 32
S5_STATE = 64
S5_CHUNK = 8
S5_GB = 8
S5_NBLK = S5_GROUPS // S5_GB
S5_K = S5_CHUNK * S5_GB * S5_GROUP_DIM
S5_PART = S5_GB * S5_STATE
S5_MAX_ROWS = 512
N_HEADS = 8
N_KV_HEADS = 2
HEAD_DIM = 128
Q_PER_KV = 4
ATTN_WIDTH = 1024
KV_WIDTH = 256
ROPE_THETA = 10000.0
GRID_W = 64
HGRN_HEADS = 4
HGRN_DK = 128
HGRN_DV = 128
HGRN_WIDTH = 512
HGRN_CHUNK = 64

COL_UA, COL_GA, COL_UB, COL_GB = 0, 512, 1024, 1536
COL_QC, COL_KC, COL_VC, COL_GC = 2048, 3072, 3328, 3584
COL_QD, COL_ID, COL_ZF, COL_ZB, COL_GD = 4608, 5120, 5632, 6144, 6656
COL_M = 7168
IN_COLS = 15360

FFT_RADIX = 8
FINAL_ROWS = 256
ATTN_KEY_CHUNK = 1024
ATTN_TQ = 512
INPROJ_TM = 1024
INPROJ_TN = 1536
INPROJ_ROWS = 256
FINAL_TM = 1024
FINAL_TC = 256
VMEM_LIMIT = 56 * 1024 * 1024
FINAL_VMEM_LIMIT = 58 * 1024 * 1024


def _cparams(n_grid):
    return pltpu.CompilerParams(dimension_semantics=("arbitrary",) * n_grid,
                                vmem_limit_bytes=VMEM_LIMIT)


def _sigmoid(x):
    return 1.0 / (1.0 + jnp.exp(-x))


def _gate_sigmoid(x):
    return 0.5 * jnp.tanh(0.5 * x) + 0.5


def _silu(x):
    return x * _gate_sigmoid(x)


def _dot(a, b):
    return jnp.dot(a, b, preferred_element_type=F32)


def _dot_nt(a, b):
    return lax.dot_general(a, b, (((1,), (1,)), ((), ())), preferred_element_type=F32)


def _dot_tn(a, b):
    return lax.dot_general(a, b, (((0,), (0,)), ((), ())), preferred_element_type=F32)


def _split2(x):
    hi = x.astype(BF16)
    lo = (x - hi.astype(F32)).astype(BF16)
    return hi, lo


def _split3(x):
    hi = x.astype(BF16)
    r = x - hi.astype(F32)
    mid = r.astype(BF16)
    lo = (r - mid.astype(F32)).astype(BF16)
    return hi, mid, lo


def _dot3(a, b):
    ah, al = _split2(a)
    bh, bl = _split2(b)
    return _dot(ah, bh) + (_dot(ah, bl) + _dot(al, bh))


def _mod_kernel(c_ref, w_ref, b_ref, o_ref):
    a = _silu(c_ref[...])
    o_ref[0] = _dot3(a, w_ref[0]) + b_ref[0]


def _modulation(c_all, w_mod, b_mod):
    depth, d, n = w_mod.shape
    rows = c_all.shape[0]
    tn = 1024
    return pl.pallas_call(
        _mod_kernel,
        grid=(depth, n // tn),
        in_specs=[pl.BlockSpec((rows, d), lambda l, j: (0, 0)),
                  pl.BlockSpec((1, d, tn), lambda l, j: (l, 0, j)),
                  pl.BlockSpec((1, 1, tn), lambda l, j: (l, 0, j))],
        out_specs=pl.BlockSpec((1, rows, tn), lambda l, j: (l, 0, j)),
        out_shape=jax.ShapeDtypeStruct((depth, rows, n), F32),
        compiler_params=_cparams(2),
        name="modulation",
    )(c_all, w_mod, b_mod.reshape(depth, 1, n))


def _inproj_kernel(x_ref, sh_ref, sc_ref, g_ref, w_ref, o_ref, ub_ref, h_scr):
    first = pl.program_id(2) == 0
    tm = h_scr.shape[0]

    @pl.when(first)
    def _():
        for r0 in range(0, tm, INPROJ_ROWS):
            rs = slice(r0, r0 + INPROJ_ROWS)
            x = x_ref[0, rs, :]
            ms = jnp.mean(x * x, axis=-1, keepdims=True)
            y = x * lax.rsqrt(ms + NORM_EPS) * g_ref[...]
            h = (y * (1.0 + sc_ref[0]) + sh_ref[0]).astype(BF16)
            h_scr[rs, :] = h
            res = _dot(h, w_ref[0])
            o_ref[0, rs, :] = res.astype(BF16)
            ub_ref[0, rs, :] = res[:, COL_UB:COL_UB + S5_WIDTH]

    @pl.when(jnp.logical_not(first))
    def _():
        o_ref[0] = _dot(h_scr[...], w_ref[0]).astype(BF16)


def _inproj(x, shift, scale, g, w_bf, layer, tm, tn):
    b, l, d = x.shape
    n = w_bf.shape[2]
    assert COL_UB + S5_WIDTH <= tn
    per_batch = shift.shape[0] > 1
    mod_map = (lambda bi, i, j: (bi, 0, 0)) if per_batch else (lambda bi, i, j: (0, 0, 0))
    return pl.pallas_call(
        _inproj_kernel,
        grid=(b, l // tm, n // tn),
        in_specs=[pl.BlockSpec((1, tm, d), lambda bi, i, j: (bi, i, 0)),
                  pl.BlockSpec((1, 1, d), mod_map),
                  pl.BlockSpec((1, 1, d), mod_map),
                  pl.BlockSpec((1, d), lambda bi, i, j: (0, 0)),
                  pl.BlockSpec((1, d, tn), lambda bi, i, j: (layer, 0, j))],
        out_specs=[pl.BlockSpec((1, tm, tn), lambda bi, i, j: (bi, i, j)),
                   pl.BlockSpec((1, tm, S5_WIDTH), lambda bi, i, j: (bi, i, 0))],
        out_shape=[jax.ShapeDtypeStruct((b, l, n), BF16),
                   jax.ShapeDtypeStruct((b, l, S5_WIDTH), F32)],
        scratch_shapes=[pltpu.VMEM((tm, d), BF16)],
        compiler_params=_cparams(3),
        name="inproj",
    )(x, shift, scale, g, w_bf)


def _dft_tables(l):
    r = FFT_RADIX
    l2 = l // r
    scale = 1.0 / math.sqrt(l * FOURIER_GROUP_DIM)
    k1 = np.arange(r, dtype=np.int64)[:, None]
    t2 = np.arange(l2, dtype=np.int64)[None, :]
    ang = 2.0 * np.pi * ((k1 * t2) % l) / l
    lanes = (l, FOURIER_GROUP_DIM)
    tw_re = jnp.broadcast_to(jnp.asarray(np.cos(ang).reshape(l, 1), F32), lanes)
    tw_im = jnp.broadcast_to(jnp.asarray(-np.sin(ang).reshape(l, 1), F32), lanes)
    k2 = np.arange(l2, dtype=np.int64)
    a3 = 2.0 * np.pi * ((k2[:, None] * k2[None, :]) % l2) / l2
    d3 = jnp.asarray(np.concatenate([np.cos(a3), np.sin(a3)], axis=1) * scale, F32).astype(BF16)
    c = np.arange(FOURIER_GROUP_DIM, dtype=np.int64)
    ac = 2.0 * np.pi * ((c[:, None] * c[None, :]) % FOURIER_GROUP_DIM) / FOURIER_GROUP_DIM
    cs = jnp.asarray(np.concatenate([np.cos(ac), np.sin(ac)], axis=1), F32).astype(BF16)
    return (tw_re, tw_im, d3), cs


def _cadd(a, b):
    return a[0] + b[0], a[1] + b[1]


def _csub(a, b):
    return a[0] - b[0], a[1] - b[1]


def _cmul_neg_i(a):
    return a[1], -a[0]


def _dft4(y0, y1, y2, y3):
    c0, c1 = _cadd(y0, y2), _cadd(y1, y3)
    d0, d1 = _csub(y0, y2), _cmul_neg_i(_csub(y1, y3))
    return _cadd(c0, c1), _cadd(d0, d1), _csub(c0, c1), _csub(d0, d1)


def _dft8(x):
    h = math.sqrt(0.5)
    a = [_cadd(x[n], x[n + 4]) for n in range(4)]
    d = [_csub(x[n], x[n + 4]) for n in range(4)]
    b = [d[0],
         ((d[1][0] + d[1][1]) * h, (d[1][1] - d[1][0]) * h),
         _cmul_neg_i(d[2]),
         ((d[3][1] - d[3][0]) * h, -(d[3][0] + d[3][1]) * h)]
    even, odd = _dft4(*a), _dft4(*b)
    return [even[0], odd[0], even[1], odd[1], even[2], odd[2], even[3], odd[3]]


def _fourier_kernel(u_ref, g_ref, cs_ref, twr_ref, twi_ref, d3_ref, w_ref, o_ref, b_scr, f_scr, *, l):
    r = FFT_RADIX
    l2 = l // r
    gd = FOURIER_GROUP_DIM
    t = _dot(u_ref[0].astype(BF16), cs_ref[...])
    blocks = [(t[j * l2:(j + 1) * l2, :gd], -t[j * l2:(j + 1) * l2, gd:]) for j in range(r)]
    for k1, (ar, ai) in enumerate(_dft8(blocks)):
        twr = twr_ref[k1 * l2:(k1 + 1) * l2, :]
        twi = twi_ref[k1 * l2:(k1 + 1) * l2, :]
        b_scr[k1, 0:l2, :] = (ar * twr - ai * twi).astype(BF16)
        b_scr[k1, l2:2 * l2, :] = (ar * twi + ai * twr).astype(BF16)
    for k1 in range(r):
        f_scr[pl.ds(k1, l2, stride=r), :] = _dot(d3_ref[...], b_scr[k1])
    y = _dot(f_scr[...].astype(BF16), w_ref[0].astype(BF16))
    o_ref[0] = (y * _silu(g_ref[0].astype(F32))).astype(BF16)


def _fourier_branch(proj, fourier_w, dft, cs):
    b, l, _ = proj.shape
    gd = FOURIER_GROUP_DIM
    l2 = l // FFT_RADIX
    tw_re, tw_im, d3 = dft
    return pl.pallas_call(
        functools.partial(_fourier_kernel, l=l),
        grid=(b, FOURIER_GROUPS),
        in_specs=[pl.BlockSpec((1, l, gd), lambda bi, g: (bi, 0, COL_UA // gd + g)),
                  pl.BlockSpec((1, l, gd), lambda bi, g: (bi, 0, COL_GA // gd + g)),
                  pl.BlockSpec((gd, 2 * gd), lambda bi, g: (0, 0)),
                  pl.BlockSpec((l, gd), lambda bi, g: (0, 0)),
                  pl.BlockSpec((l, gd), lambda bi, g: (0, 0)),
                  pl.BlockSpec((l2, 2 * l2), lambda bi, g: (0, 0)),
                  pl.BlockSpec((1, gd, gd), lambda bi, g: (g, 0, 0))],
        out_specs=pl.BlockSpec((1, l, gd), lambda bi, g: (bi, 0, g)),
        out_shape=jax.ShapeDtypeStruct((b, l, FOURIER_WIDTH), BF16),
        scratch_shapes=[pltpu.VMEM((FFT_RADIX, 2 * l2, gd), BF16), pltpu.VMEM((l, gd), F32)],
        compiler_params=_cparams(2),
        name="fourier",
    )(proj, proj, cs, tw_re, tw_im, d3, fourier_w)


def _s5_matrices(lam_re, lam_im, log_step, b_re, b_im, c_re, c_im, dskip):
    t = S5_CHUNK
    g, n, p = S5_GROUPS, S5_STATE, S5_GROUP_DIM
    hp = lax.Precision.HIGHEST
    lam_re = lam_re.astype(F32)
    lam_im = lam_im.astype(F32)
    step = jnp.exp(log_step.astype(F32))[..., None]
    rho = lam_re * step
    th = lam_im * step
    mag = jnp.exp(rho)
    lb_re = mag * jnp.cos(th)
    lb_im = mag * jnp.sin(th)
    nr = lb_re - 1.0
    den = lam_re * lam_re + lam_im * lam_im
    fr = (nr * lam_re + lb_im * lam_im) / den
    fi = (lb_im * lam_re - nr * lam_im) / den
    b_re = b_re.astype(F32)
    b_im = b_im.astype(F32)
    bb_re = fr[..., None] * b_re - fi[..., None] * b_im
    bb_im = fr[..., None] * b_im + fi[..., None] * b_re
    tau = jnp.arange(t + 1, dtype=F32)
    pw_mag = jnp.exp(rho[..., None] * tau)
    pw_re = pw_mag * jnp.cos(th[..., None] * tau)
    pw_im = pw_mag * jnp.sin(th[..., None] * tau)
    c_re = c_re.astype(F32)
    c_im = c_im.astype(F32)
    ca_re = c_re[..., None] * pw_re[:, :, None] - c_im[..., None] * pw_im[:, :, None]
    ca_im = c_re[..., None] * pw_im[:, :, None] + c_im[..., None] * pw_re[:, :, None]
    kern = jnp.einsum('dgpnt,dgnq->dgtpq', jnp.concatenate([ca_re[..., :t], -ca_im[..., :t]], axis=3),
                      jnp.concatenate([bb_re, bb_im], axis=2), precision=hp)
    sig = np.arange(t)[:, None]
    ta = np.arange(t)[None, :]
    idx_f = np.clip(ta - sig, 0, t - 1)
    idx_b = np.clip(sig - ta, 0, t - 1)
    m_f = jnp.asarray((ta >= sig).astype(np.float32))[None, :, :, None, None]
    m_b = jnp.asarray((sig >= ta).astype(np.float32))[None, :, :, None, None]
    tf = kern[0][:, idx_f] * m_f
    tb = kern[1][:, idx_b] * m_b
    eye_t = jnp.asarray(np.eye(t, dtype=np.float32))[None, :, :, None, None]
    dmat = dskip.astype(F32)[:, None, None, :, None] * jnp.asarray(np.eye(p, dtype=np.float32))[None, None, None]
    tt = tf + tb + eye_t * dmat
    nb, gb = S5_NBLK, S5_GB
    one = np.ones
    eye = lambda k: np.eye(k, dtype=np.float32)
    exp_t = jnp.asarray(np.einsum('tu,pq,b->tpubq', eye(t), eye(p), one(gb)).reshape(t * p, S5_K))
    exp_n = jnp.asarray(np.einsum('nm,b->nbm', eye(n), one(gb)).reshape(n, S5_PART))
    mask_t = jnp.asarray(np.einsum('ab,s,q,u,p->saqubp', eye(gb), one(t), one(p), one(t), one(p))
                         .reshape(S5_K, S5_K))
    mask_ba = jnp.asarray(np.einsum('ab,s,q,n->saqbn', eye(gb), one(t), one(p), one(n)).reshape(S5_K, S5_PART))
    t_small = tt.reshape(nb, gb, t, t, p, p).transpose(0, 2, 1, 5, 3, 4).reshape(nb, S5_K, t * p)
    tmat = jnp.matmul(t_small, exp_t, precision=hp) * mask_t

    def ba(d, powers):
        pr = pw_re[d][:, :, powers]
        pi = pw_im[d][:, :, powers]
        re = pr[..., None] * bb_re[d][:, :, None, :] - pi[..., None] * bb_im[d][:, :, None, :]
        im = pr[..., None] * bb_im[d][:, :, None, :] + pi[..., None] * bb_re[d][:, :, None, :]
        blk = lambda m: jnp.matmul(m.reshape(nb, gb, n, t, p).transpose(0, 3, 1, 4, 2).reshape(nb, S5_K, n),
                                   exp_n, precision=hp) * mask_ba
        return blk(re), blk(im)

    baf_re, baf_im = ba(0, np.arange(t - 1, -1, -1))
    bab_re, bab_im = ba(1, np.arange(t))
    bamat = jnp.concatenate([baf_re, bab_re, baf_im, bab_im], axis=-1)

    def ca(d, powers):
        blk = lambda m: jnp.matmul(exp_n.T, m[..., powers].reshape(nb, gb, p, n, t).transpose(0, 3, 4, 1, 2)
                                   .reshape(nb, n, S5_K), precision=hp) * mask_ba.T
        return blk(ca_re[d]), blk(ca_im[d])

    caf_re, caf_im = ca(0, np.arange(1, t + 1))
    cab_re, cab_im = ca(1, np.arange(t, 0, -1))
    camat = jnp.concatenate([caf_re, cab_re, -caf_im, -cab_im], axis=1)
    part = lambda m: m[..., t].reshape(nb, S5_PART)
    a_pow = jnp.stack([jnp.concatenate([part(pw_re[0]), part(pw_re[1])], axis=-1),
                       jnp.concatenate([part(pw_im[0]), part(pw_im[1])], axis=-1)], axis=1)
    return tmat.astype(BF16), bamat.astype(BF16), camat.astype(BF16), a_pow


def _s5_kernel(u_ref, t_ref, ba_ref, ca_ref, a_ref, h0_ref, y_ref, hf_ref, ub_scr, s_scr, hin_scr, y_scr,
               *, nc, bt):
    t, w, hp = S5_CHUNK, S5_GB * S5_GROUP_DIM, S5_PART
    for bi in range(bt):
        for tau in range(t):
            ub_scr[tau, pl.ds(bi, nc, stride=bt), :] = u_ref[bi, pl.ds(tau, nc, stride=t), :]
    ub = jnp.concatenate([ub_scr[tau] for tau in range(t)], axis=1).astype(BF16)
    s_scr[...] = _dot(ub, ba_ref[0, 0])
    ar_f, ar_b = a_ref[0, 0, 0:1, 0:hp], a_ref[0, 0, 0:1, hp:2 * hp]
    ai_f, ai_b = a_ref[0, 0, 1:2, 0:hp], a_ref[0, 0, 1:2, hp:2 * hp]

    def step(i, carry):
        re_f, im_f, re_b, im_b = carry
        rf = pl.ds(pl.multiple_of(i * bt, bt), bt)
        rb = pl.ds(pl.multiple_of((nc - 1 - i) * bt, bt), bt)
        hin_scr[rf, 0:hp] = re_f
        hin_scr[rf, 2 * hp:3 * hp] = im_f
        hin_scr[rb, hp:2 * hp] = re_b
        hin_scr[rb, 3 * hp:4 * hp] = im_b
        n_re_f = ar_f * re_f - ai_f * im_f + s_scr[rf, 0:hp]
        n_im_f = ar_f * im_f + ai_f * re_f + s_scr[rf, 2 * hp:3 * hp]
        n_re_b = ar_b * re_b - ai_b * im_b + s_scr[rb, hp:2 * hp]
        n_im_b = ar_b * im_b + ai_b * re_b + s_scr[rb, 3 * hp:4 * hp]
        return n_re_f, n_im_f, n_re_b, n_im_b

    h0 = h0_ref[...].reshape(bt, 4 * hp)
    init = (h0[:, 0:hp], h0[:, 2 * hp:3 * hp], h0[:, hp:2 * hp], h0[:, 3 * hp:4 * hp])
    fin = lax.fori_loop(0, nc, step, init, unroll=8)
    hf_ref[...] = jnp.concatenate([fin[0], fin[2], fin[1], fin[3]], axis=1).reshape(hf_ref.shape)
    y = _dot(ub, t_ref[0, 0]) + _dot(hin_scr[...].astype(BF16), ca_ref[0, 0])
    for tau in range(t):
        y_scr[tau] = y[:, tau * w:(tau + 1) * w]
    for bi in range(bt):
        for tau in range(t):
            y_ref[bi, pl.ds(tau, nc, stride=t), :] = y_scr[tau, pl.ds(bi, nc, stride=bt), :]


def _s5_core(ub, tmat, bamat, camat, a_pow, h0, layer):
    b, l, _ = ub.shape
    nc = l // S5_CHUNK
    w = S5_GB * S5_GROUP_DIM
    bt = b if nc * b <= S5_MAX_ROWS else 1
    if bt == 1:
        h0 = h0.reshape(S5_NBLK, b, 1, 4 * S5_PART)
        st_spec = pl.BlockSpec((1, 1, 1, 4 * S5_PART), lambda j, bi: (j, bi, 0, 0))
    else:
        st_spec = pl.BlockSpec((1, bt, 4 * S5_PART), lambda j, bi: (j, bi, 0))
    rows = nc * bt
    y, hf = pl.pallas_call(
        functools.partial(_s5_kernel, nc=nc, bt=bt),
        grid=(S5_NBLK, b // bt),
        in_specs=[pl.BlockSpec((bt, l, w), lambda j, bi: (bi, 0, j)),
                  pl.BlockSpec((1, 1, S5_K, S5_K), lambda j, bi: (layer, j, 0, 0)),
                  pl.BlockSpec((1, 1, S5_K, 4 * S5_PART), lambda j, bi: (layer, j, 0, 0)),
                  pl.BlockSpec((1, 1, 4 * S5_PART, S5_K), lambda j, bi: (layer, j, 0, 0)),
                  pl.BlockSpec((1, 1, 2, 2 * S5_PART), lambda j, bi: (layer, j, 0, 0)),
                  st_spec],
        out_specs=[pl.BlockSpec((bt, l, w), lambda j, bi: (bi, 0, j)), st_spec],
        out_shape=[jax.ShapeDtypeStruct((b, l, S5_WIDTH), F32),
                   jax.ShapeDtypeStruct(h0.shape, F32)],
        scratch_shapes=[pltpu.VMEM((S5_CHUNK, rows, w), F32), pltpu.VMEM((rows, 4 * S5_PART), F32),
                        pltpu.VMEM((rows, 4 * S5_PART), F32), pltpu.VMEM((S5_CHUNK, rows, w), F32)],
        compiler_params=_cparams(2),
        name="s5_core",
    )(ub, tmat, bamat, camat, a_pow, h0)
    return y, hf.reshape(S5_NBLK, b, 4 * S5_PART)


def _s5_glu_kernel(y_ref, g_ref, w_ref, b_ref, o_ref):
    y = y_ref[0]
    y = 0.5 * y * (1.0 + jnp.tanh(math.sqrt(2.0 / math.pi) * (y + 0.044715 * (y * y * y))))
    z = _dot(y.astype(BF16), w_ref[...].astype(BF16)) + b_ref[...]
    o_ref[0] = (y * _sigmoid(z) * _silu(g_ref[0].astype(F32))).astype(BF16)


def _s5_glu(y, proj, glu_w, glu_b, tm):
    b, l, w = y.shape
    return pl.pallas_call(
        _s5_glu_kernel,
        grid=(b, l // tm),
        in_specs=[pl.BlockSpec((1, tm, w), lambda bi, i: (bi, i, 0)),
                  pl.BlockSpec((1, tm, w), lambda bi, i: (bi, i, COL_GB // w)),
                  pl.BlockSpec((w, w), lambda bi, i: (0, 0)),
                  pl.BlockSpec((1, w), lambda bi, i: (0, 0))],
        out_specs=pl.BlockSpec((1, tm, w), lambda bi, i: (bi, i, 0)),
        out_shape=jax.ShapeDtypeStruct((b, l, w), BF16),
        compiler_params=_cparams(2),
        name="s5_glu",
    )(y, proj, glu_w, glu_b.reshape(1, w))


def _s5_branch(proj, ub, mats, layer, glu_w, glu_b, h0_re, h0_im):
    b, l, _ = proj.shape
    g, n = S5_GROUPS, S5_STATE
    if h0_re is None:
        h0 = jnp.zeros((S5_NBLK, b, 4 * S5_PART), F32)
    else:
        blk = lambda h: h.astype(F32).reshape(b, S5_NBLK, S5_PART).transpose(1, 0, 2)
        h0 = jnp.concatenate([blk(h0_re[:, 0]), blk(h0_re[:, 1]), blk(h0_im[:, 0]), blk(h0_im[:, 1])], axis=-1)
    y, hf = _s5_core(ub, *mats, h0, layer)
    out = _s5_glu(y, proj, glu_w, glu_b, min(l, 1024))
    part = lambda i: hf[..., i * S5_PART:(i + 1) * S5_PART].transpose(1, 0, 2).reshape(b, g, n)
    fin_re = jnp.stack([part(0), part(1)], axis=1)
    fin_im = jnp.stack([part(2), part(3)], axis=1)
    return out, fin_re, fin_im


def _rope_tables(l):
    rows = l // GRID_W
    row = jnp.broadcast_to(jnp.arange(rows, dtype=F32)[:, None], (rows, GRID_W)).reshape(-1)
    col = jnp.broadcast_to(jnp.arange(GRID_W, dtype=F32)[None, :], (rows, GRID_W)).reshape(-1)
    half = HEAD_DIM // 2
    inv = ROPE_THETA ** (-jnp.arange(0, half, 2, dtype=F32) / half)
    ar = row[:, None] * inv
    ac = col[:, None] * inv
    cos = jnp.concatenate([jnp.cos(ar), jnp.cos(ar), jnp.cos(ac), jnp.cos(ac)], axis=1)
    sin = jnp.concatenate([-jnp.sin(ar), jnp.sin(ar), -jnp.sin(ac), jnp.sin(ac)], axis=1)
    return cos, sin


def _qk_kernel(*refs, rope, want_kf):
    q_ref, k_ref, v_ref, qn_ref, kn_ref = refs[:5]
    pos = 5
    if rope:
        cos_ref, sin_ref = refs[pos:pos + 2]
        pos += 2
    qo_ref, ko_ref, vo_ref = refs[pos:pos + 3]
    pos += 3
    if want_kf:
        kf_ref = refs[pos]
    hd = HEAD_DIM
    tm = q_ref.shape[1]
    quarter = hd // 4
    if rope:
        cos = cos_ref[...]
        sin = sin_ref[...]
        first = (lax.broadcasted_iota(jnp.int32, (tm, hd), 1) % (2 * quarter)) < quarter

    def norm(x, gain):
        ms = jnp.mean(x * x, axis=-1, keepdims=True)
        return x * lax.rsqrt(ms + NORM_EPS) * gain

    def rot(x):
        if not rope:
            return x
        nxt = pltpu.roll(x, hd - quarter, 1)
        prv = pltpu.roll(x, quarter, 1)
        return x * cos + jnp.where(first, nxt, prv) * sin

    qscale = (hd ** -0.5) * LOG2E
    for h in range(N_HEADS):
        x = norm(q_ref[0, :, h * hd:(h + 1) * hd].astype(F32), qn_ref[...])
        qo_ref[0, :, h * hd:(h + 1) * hd] = (rot(x) * qscale).astype(BF16)
    for h in range(N_KV_HEADS):
        x = norm(k_ref[0, :, h * hd:(h + 1) * hd].astype(F32), kn_ref[...])
        if want_kf:
            kf_ref[0, :, h * hd:(h + 1) * hd] = x
        ko_ref[0, :, h * hd:(h + 1) * hd] = rot(x).astype(BF16)
        vo_ref[0, h] = v_ref[0, :, h * hd:(h + 1) * hd].astype(F32).T.astype(BF16)


def _qk_prep(proj, q_norm, k_norm, rope_tabs, want_kf, tm):
    b, l, _ = proj.shape
    rope = rope_tabs is not None
    in_specs = [pl.BlockSpec((1, tm, ATTN_WIDTH), lambda bi, i: (bi, i, COL_QC // ATTN_WIDTH)),
                pl.BlockSpec((1, tm, KV_WIDTH), lambda bi, i: (bi, i, COL_KC // KV_WIDTH)),
                pl.BlockSpec((1, tm, KV_WIDTH), lambda bi, i: (bi, i, COL_VC // KV_WIDTH)),
                pl.BlockSpec((1, HEAD_DIM), lambda bi, i: (0, 0)),
                pl.BlockSpec((1, HEAD_DIM), lambda bi, i: (0, 0))]
    args = [proj, proj, proj, q_norm.reshape(1, HEAD_DIM), k_norm.reshape(1, HEAD_DIM)]
    if rope:
        in_specs += [pl.BlockSpec((tm, HEAD_DIM), lambda bi, i: (i, 0))] * 2
        args += list(rope_tabs)
    out_specs = [pl.BlockSpec((1, tm, ATTN_WIDTH), lambda bi, i: (bi, i, 0)),
                 pl.BlockSpec((1, tm, KV_WIDTH), lambda bi, i: (bi, i, 0)),
                 pl.BlockSpec((1, N_KV_HEADS, HEAD_DIM, tm), lambda bi, i: (bi, 0, 0, i))]
    out_shape = [jax.ShapeDtypeStruct((b, l, ATTN_WIDTH), BF16),
                 jax.ShapeDtypeStruct((b, l, KV_WIDTH), BF16),
                 jax.ShapeDtypeStruct((b, N_KV_HEADS, HEAD_DIM, l), BF16)]
    if want_kf:
        out_specs.append(pl.BlockSpec((1, tm, KV_WIDTH), lambda bi, i: (bi, i, 0)))
        out_shape.append(jax.ShapeDtypeStruct((b, l, KV_WIDTH), F32))
    return pl.pallas_call(
        functools.partial(_qk_kernel, rope=rope, want_kf=want_kf),
        grid=(b, l // tm),
        in_specs=in_specs, out_specs=out_specs, out_shape=out_shape,
        compiler_params=_cparams(2),
        name="qk_prep",
    )(*args)


def _attn_kernel(*refs, has_ctx):
    s_scr = refs[-1]
    q_ref, k_ref, v_ref = refs[:3]
    if has_ctx:
        ck_ref, cv_ref, g_ref, o_ref = refs[3:7]
        ck = ck_ref[0, 0].astype(BF16)
        cvt = cv_ref[0, 0, 0].astype(BF16)
        past = ck.shape[0]
    else:
        g_ref, o_ref = refs[3:5]
    hd = HEAD_DIM
    tq = q_ref.shape[1]
    lk = k_ref.shape[1]
    cr = min(lk, ATTN_KEY_CHUNK)
    chunks = [(c * cr, cr, (lambda c=c: k_ref[0, c * cr:(c + 1) * cr, :]),
               (lambda c=c: v_ref[0, 0, :, c * cr:(c + 1) * cr])) for c in range(lk // cr)]
    if has_ctx:
        chunks.append((lk, past, lambda: ck, lambda: cvt))
    qs = [q_ref[0, :, g * hd:(g + 1) * hd] for g in range(Q_PER_KV)]
    gate = _silu(g_ref[0].astype(F32))

    def score(g, chunk, m):
        off, rows, keys, _ = chunk
        s = _dot_nt(keys(), qs[g])
        s_scr[g % 2, off:off + rows, :] = s
        return jnp.maximum(m, jnp.max(s, axis=0, keepdims=True))

    def accumulate(g, chunk, m, den, ot):
        off, rows, _, values_t = chunk
        p = jnp.exp2(s_scr[g % 2, off:off + rows, :] - m)
        return den + jnp.sum(p, axis=0, keepdims=True), ot + _dot(values_t(), p.astype(BF16))

    neg = jnp.full((1, tq), -jnp.inf, F32)
    m_cur = neg
    for chunk in chunks:
        m_cur = score(0, chunk, m_cur)
    for g in range(Q_PER_KV):
        m_next = neg
        den = jnp.zeros((1, tq), F32)
        ot = jnp.zeros((hd, tq), F32)
        for chunk in chunks:
            if g + 1 < Q_PER_KV:
                m_next = score(g + 1, chunk, m_next)
            den, ot = accumulate(g, chunk, m_cur, den, ot)
        m_cur = m_next
        o = (ot / den).T
        o_ref[0, :, g * hd:(g + 1) * hd] = (o * gate[:, g * hd:(g + 1) * hd]).astype(BF16)


def _attention(qs, kr, vt, proj, ctx_k, ctx_v, layer, tq):
    b, l, _ = qs.shape
    gw = Q_PER_KV * HEAD_DIM
    has_ctx = ctx_k is not None
    in_specs = [pl.BlockSpec((1, tq, gw), lambda bi, h, i: (bi, i, h)),
                pl.BlockSpec((1, l, HEAD_DIM), lambda bi, h, i: (bi, 0, h)),
                pl.BlockSpec((1, 1, HEAD_DIM, l), lambda bi, h, i: (bi, h, 0, 0))]
    args = [qs, kr, vt]
    if has_ctx:
        past = ctx_k.shape[2]
        in_specs += [pl.BlockSpec((1, 1, past, HEAD_DIM), lambda bi, h, i: (bi, layer, 0, h)),
                     pl.BlockSpec((1, 1, 1, HEAD_DIM, past), lambda bi, h, i: (bi, layer, h, 0, 0))]
        args += [ctx_k, ctx_v]
    in_specs.append(pl.BlockSpec((1, tq, gw), lambda bi, h, i: (bi, i, COL_GC // gw + h)))
    args.append(proj)
    return pl.pallas_call(
        functools.partial(_attn_kernel, has_ctx=has_ctx),
        grid=(b, N_KV_HEADS, l // tq),
        in_specs=in_specs,
        out_specs=pl.BlockSpec((1, tq, gw), lambda bi, h, i: (bi, i, h)),
        out_shape=jax.ShapeDtypeStruct((b, l, ATTN_WIDTH), BF16),
        scratch_shapes=[pltpu.VMEM((2, l + (ctx_k.shape[2] if has_ctx else 0), tq), F32)],
        compiler_params=_cparams(3),
        name="attention",
    )(*args)


HGRN_COARSE = (32, 16, 8, 4)
HGRN_FINE = (2, 1)
HGRN_LEVELS = HGRN_COARSE + HGRN_FINE
HGRN_CHUNKS_PER_TRIP = 4


def _hgrn_tables():
    c = HGRN_CHUNK
    t = np.arange(c)
    w01, sm = [], []
    for d in range(2):
        cum = (t[None, :] <= t[:, None]) if d == 0 else (t[None, :] >= t[:, None])
        cum = cum.astype(np.float32)
        w01.append(np.concatenate([cum] + [cum[_hgrn_ref_row(t, hs, d)] for hs in HGRN_FINE], axis=0))
        sd = []
        for hs in HGRN_LEVELS:
            is_q = _hgrn_is_query(t, hs, d)
            same = (t[:, None] // (2 * hs)) == (t[None, :] // (2 * hs))
            sd.append((same & is_q[:, None] & ~is_q[None, :]).astype(np.float32))
        sd.append(np.eye(c, dtype=np.float32))
        sm.append(np.stack(sd))
    return jnp.asarray(np.stack(w01), BF16), jnp.asarray(np.stack(sm), F32)


def _hgrn_ref_row(t, hs, d):
    base = t & ~(2 * hs - 1)
    return base + hs - 1 if d == 0 else base + hs


def _hgrn_is_query(t, hs, d):
    return ((t & hs) != 0) if d == 0 else ((t & hs) == 0)


def _hgrn_cums(z, lbv, w01):
    f = lbv + (1.0 - lbv) * _sigmoid(z)
    p1, p2, p3 = _split3(jnp.log2(jnp.maximum(f, F_FLOOR)))
    return 1.0 - f, _dot(w01, p1) + (_dot(w01, p2) + _dot(w01, p3))


def _hgrn_products(q, k, allc, rows, d):
    c = HGRN_CHUNK
    sub = 8
    cum = allc[0:c]
    refs = {}
    for hs in HGRN_COARSE:
        pieces = []
        for j in range(c // sub):
            rr = int(_hgrn_ref_row(np.int64(j * sub), hs, d))
            pieces.append(jnp.broadcast_to(cum[rr:rr + 1, :], (sub, HGRN_DK)))
        refs[hs] = jnp.concatenate(pieces, axis=0)
    for i, hs in enumerate(HGRN_FINE):
        refs[hs] = allc[(i + 1) * c:(i + 2) * c]
    lrow = c - 1 if d == 0 else 0
    last = cum[lrow:lrow + 1, :]
    raws = [_dot_nt(q.astype(BF16), k.astype(BF16))]
    for hs in HGRN_LEVELS:
        is_q = ((rows & hs) != 0) if d == 0 else ((rows & hs) == 0)
        x = (jnp.where(is_q, q, k) * jnp.exp2((cum - refs[hs]) * jnp.where(is_q, 1.0, -1.0))).astype(BF16)
        raws.append(_dot_nt(x, x))
    qi = (q * jnp.exp2(cum)).astype(BF16)
    ks = (k * jnp.exp2(last - cum)).astype(BF16)
    return raws, qi, ks, jnp.exp2(last)


def _hgrn_kernel(*refs, nc, has_s0, want_state):
    q_ref, v_ref, zf_ref, zb_ref, g_ref, lb_ref, ng_ref, w01_ref, sm_ref = refs[:9]
    pos = 9
    if has_s0:
        s0_ref = refs[pos]
        pos += 1
    o_ref = refs[pos]
    pos += 1
    if want_state:
        sf_ref = refs[pos]
        pos += 1
    o_scr, st_scr = refs[pos:pos + 2]
    c = HGRN_CHUNK
    l = nc * c
    for d in range(2):
        if has_s0:
            st_scr[d] = s0_ref[0, d, 0]
        else:
            st_scr[d] = jnp.zeros((HGRN_DV, HGRN_DK), F32)
    rows = lax.broadcasted_iota(jnp.int32, (c, HGRN_DK), 0)

    per_trip = math.gcd(nc, HGRN_CHUNKS_PER_TRIP)

    def body(i, carry):
        insts = []
        for u in range(per_trip):
            for d in range(2):
                ci = per_trip * i + u if d == 0 else nc - 1 - (per_trip * i + u)
                insts.append((d, pl.multiple_of(ci * c, c)))
        gates = [_hgrn_cums((zf_ref, zb_ref)[d][0, pl.ds(r, c), :].astype(F32), lb_ref[d, 0], w01_ref[d])
                 for d, r in insts]
        prods = [_hgrn_products(q_ref[0, pl.ds(r, c), :].astype(F32), k, allc, rows, d)
                 for (d, r), (k, allc) in zip(insts, gates)]
        nl = len(HGRN_LEVELS)
        for (d, r), (raws, qi, ks, dec) in zip(insts, prods):
            scores = sm_ref[d, nl] * raws[0]
            for li in range(nl):
                scores = scores + sm_ref[d, li] * raws[li + 1]
            vb = v_ref[0, pl.ds(r, c), :].astype(BF16)
            st = st_scr[d]
            o_scr[d, pl.ds(r, c), :] = _dot(scores.astype(BF16), vb) + _dot_nt(qi, st.astype(BF16))
            st_scr[d] = dec * st + _dot_tn(vb, ks)
        return carry

    lax.fori_loop(0, nc // per_trip, body, 0)
    if want_state:
        for d in range(2):
            sf_ref[0, d, 0] = st_scr[d]

    fb = min(l, 256)

    def finish(i, carry):
        r = pl.multiple_of(i * fb, fb)
        tot = o_scr[0, pl.ds(r, fb), :] + o_scr[1, pl.ds(r, fb), :]
        ms = jnp.mean(tot * tot, axis=-1, keepdims=True)
        y = tot * lax.rsqrt(ms + NORM_EPS) * ng_ref[...]
        o_ref[0, pl.ds(r, fb), :] = (y * _silu(g_ref[0, pl.ds(r, fb), :].astype(F32))).astype(BF16)
        return carry

    lax.fori_loop(0, l // fb, finish, 0)


def _hgrn_branch(proj, lower_bound, norm_g, tables, s0_t, want_state):
    b, l, _ = proj.shape
    nc = l // HGRN_CHUNK
    w = HGRN_DK
    has_s0 = s0_t is not None
    w01, sm = tables

    def col(off):
        return lambda bi, h: (bi, 0, off // w + h)

    in_specs = [pl.BlockSpec((1, l, w), col(COL_QD)),
                pl.BlockSpec((1, l, w), col(COL_ID)),
                pl.BlockSpec((1, l, w), col(COL_ZF)),
                pl.BlockSpec((1, l, w), col(COL_ZB)),
                pl.BlockSpec((1, l, w), col(COL_GD)),
                pl.BlockSpec((2, 1, 1, w), lambda bi, h: (0, h, 0, 0)),
                pl.BlockSpec((1, w), lambda bi, h: (0, 0)),
                pl.BlockSpec(w01.shape, lambda bi, h: (0, 0, 0)),
                pl.BlockSpec(sm.shape, lambda bi, h: (0, 0, 0, 0))]
    args = [proj, proj, proj, proj, proj, lower_bound.reshape(2, HGRN_HEADS, 1, w), norm_g.reshape(1, w),
            w01, sm]
    st_spec = pl.BlockSpec((1, 2, 1, HGRN_DV, HGRN_DK), lambda bi, h: (bi, 0, h, 0, 0))
    if has_s0:
        in_specs.append(st_spec)
        args.append(s0_t)
    out_specs = [pl.BlockSpec((1, l, w), lambda bi, h: (bi, 0, h))]
    out_shape = [jax.ShapeDtypeStruct((b, l, HGRN_HEADS * HGRN_DV), BF16)]
    if want_state:
        out_specs.append(st_spec)
        out_shape.append(jax.ShapeDtypeStruct((b, 2, HGRN_HEADS, HGRN_DV, HGRN_DK), F32))
    res = pl.pallas_call(
        functools.partial(_hgrn_kernel, nc=nc, has_s0=has_s0, want_state=want_state),
        grid=(b, HGRN_HEADS),
        in_specs=in_specs, out_specs=out_specs, out_shape=out_shape,
        scratch_shapes=[pltpu.VMEM((2, l, HGRN_DV), F32), pltpu.VMEM((2, HGRN_DV, HGRN_DK), F32)],
        compiler_params=_cparams(2),
        name="hgrn",
    )(*args)
    return res if want_state else (res[0], None)


def _final_kernel(x_ref, gt_ref, ma_ref, mb_ref, mc_ref, md_ref, ya_ref, yb_ref, yc_ref, yd_ref,
                  wa_ref, wb_ref, wc_ref, wd_ref, wo_ref, gp_ref, o_ref, *, nct):
    ci = pl.program_id(2)
    tm = o_ref.shape[1]
    subs = [slice(r0, r0 + FINAL_ROWS) for r0 in range(0, tm, FINAL_ROWS)]
    m_refs = (ma_ref, mb_ref, mc_ref, md_ref)
    y_refs = (ya_ref, yb_ref, yc_ref, yd_ref)
    w_refs = (wa_ref, wb_ref, wc_ref, wd_ref)

    def branch_dots(rs):
        return [_dot(y_ref[0, rs, :], w_ref[0]) for y_ref, w_ref in zip(y_refs, w_refs)]

    def step(first, last):
        nxt = branch_dots(subs[0])
        for si, rs in enumerate(subs):
            branch = nxt
            if si + 1 < len(subs):
                nxt = branch_dots(subs[si + 1])
            mixed = None
            for m_ref, dot in zip(m_refs, branch):
                term = _gate_sigmoid(m_ref[0, rs, :].astype(F32)) * dot
                mixed = term if mixed is None else mixed + term
            contrib = _dot(mixed.astype(BF16), wo_ref[0])
            out = contrib if first else o_ref[0, rs, :] + contrib
            if last:
                ms = jnp.mean(out * out, axis=-1, keepdims=True)
                y = out * lax.rsqrt(ms + NORM_EPS) * gp_ref[...]
                o_ref[0, rs, :] = x_ref[0, rs, :] + gt_ref[0] * y
            else:
                o_ref[0, rs, :] = out

    if nct == 1:
        step(True, True)
    else:
        pl.when(ci == 0)(lambda: step(True, False))
        pl.when(jnp.logical_and(ci > 0, ci < nct - 1))(lambda: step(False, False))
        pl.when(ci == nct - 1)(lambda: step(False, True))


def _final(x, gate, proj, ys, w_projs, w_out, norm_post, layer, tm, tc):
    b, l, d = x.shape
    nct = d // tc
    per_batch = gate.shape[0] > 1
    gate_map = (lambda bi, i, c: (bi, 0, 0)) if per_batch else (lambda bi, i, c: (0, 0, 0))

    def mspec(j):
        return pl.BlockSpec((1, tm, tc), lambda bi, i, c: (bi, i, (COL_M + j * d) // tc + c))

    in_specs = [pl.BlockSpec((1, tm, d), lambda bi, i, c: (bi, i, 0)),
                pl.BlockSpec((1, 1, d), gate_map)]
    in_specs += [mspec(j) for j in range(N_BRANCHES)]
    in_specs += [pl.BlockSpec((1, tm, y.shape[-1]), lambda bi, i, c: (bi, i, 0)) for y in ys]
    in_specs += [pl.BlockSpec((1, w.shape[1], tc), lambda bi, i, c: (layer, 0, c)) for w in w_projs]
    in_specs += [pl.BlockSpec((1, tc, d), lambda bi, i, c: (layer, c, 0)),
                 pl.BlockSpec((1, d), lambda bi, i, c: (0, 0))]
    return pl.pallas_call(
        functools.partial(_final_kernel, nct=nct),
        grid=(b, l // tm, nct),
        in_specs=in_specs,
        out_specs=pl.BlockSpec((1, tm, d), lambda bi, i, c: (bi, i, 0)),
        out_shape=jax.ShapeDtypeStruct((b, l, d), F32),
        compiler_params=pltpu.CompilerParams(dimension_semantics=("arbitrary",) * 3,
                                             vmem_limit_bytes=FINAL_VMEM_LIMIT),
        name="merge_out",
    )(x, gate, proj, proj, proj, proj, *ys, *w_projs, w_out, norm_post.reshape(1, d))


def _trunk_layer(x, shift, scale, gate, p, ctx, layer, consts):
    b, l, d = x.shape
    shared_mod = shift.shape[0] == 1
    if shared_mod:
        proj, ub = _inproj(x.reshape(1, b * l, d), shift, scale, p['norm_pre'], p['w_in'], layer,
                           INPROJ_TM, INPROJ_TN)
        proj = proj.reshape(b, l, IN_COLS)
        ub = ub.reshape(b, l, S5_WIDTH)
    else:
        proj, ub = _inproj(x, shift, scale, p['norm_pre'], p['w_in'], layer, INPROJ_TM, INPROJ_TN)

    dl, cs = consts['dft'][l]
    y_a = _fourier_branch(proj, p['fourier_w'], dl, cs)

    if ctx is None:
        y_b, s5_re, s5_im = _s5_branch(proj, ub, p['s5_mats'], layer, p['s5_glu_w'], p['s5_glu_b'], None, None)
        qs, kr, vt, kf = _qk_prep(proj, p['q_norm'], p['k_norm'], None, True, min(l, 512))
        y_c = _attention(qs, kr, vt, proj, None, None, layer, min(l, ATTN_TQ))
        y_d, hg = _hgrn_branch(proj, p['lower_bound'], p['hgrn_norm'], consts['hgrn'], None, True)
    else:
        cache_k, cache_v, st_re, st_im, s0_t = ctx
        y_b, _, _ = _s5_branch(proj, ub, p['s5_mats'], layer, p['s5_glu_w'], p['s5_glu_b'],
                               st_re[:, layer], st_im[:, layer])
        qs, kr, vt = _qk_prep(proj, p['q_norm'], p['k_norm'], consts['rope'], False, min(l, 512))
        y_c = _attention(qs, kr, vt, proj, cache_k, cache_v, layer, min(l, ATTN_TQ))
        y_d, _ = _hgrn_branch(proj, p['lower_bound'], p['hgrn_norm'], consts['hgrn'], s0_t[:, layer], False)

    ys = (y_a, y_b, y_c, y_d)
    wps = (p['w_proj_a'], p['w_proj_b'], p['w_proj_c'], p['w_proj_d'])
    if shared_mod:
        n = b * l
        x_new = _final(x.reshape(1, n, d), gate, proj.reshape(1, n, IN_COLS),
                       tuple(y.reshape(1, n, y.shape[-1]) for y in ys), wps, p['w_out'], p['norm_post'], layer,
                       FINAL_TM, FINAL_TC)
        x_new = x_new.reshape(b, l, d)
    else:
        x_new = _final(x, gate, proj, ys, wps, p['w_out'], p['norm_post'], layer, FINAL_TM, FINAL_TC)

    if ctx is None:
        v_f = proj[:, :, COL_VC:COL_VC + KV_WIDTH].astype(F32)
        return x_new, (kf, v_f, s5_re, s5_im, hg)
    return x_new, None


def kernel(x_prompt, x_sample, c, cache_k, cache_v, state_s5_re, state_s5_im, state_hgrn, c_ctx,
           norm_pre, norm_post, w_mod, b_mod, w_in, fourier_w,
           s5_lambda_re, s5_lambda_im, s5_log_step, s5_b_re, s5_b_im, s5_c_re, s5_c_im,
           s5_d, s5_glu_w, s5_glu_b, q_norm, k_norm, hgrn_lb_logits, hgrn_norm,
           w_proj_a, w_proj_b, w_proj_c, w_proj_d, w_out):
    depth = w_in.shape[0]
    bp, lp, d = x_prompt.shape
    bs, ls, _ = x_sample.shape

    lb_w = jax.nn.softmax(hgrn_lb_logits.astype(F32), axis=0)
    lower_bounds = jnp.cumsum(lb_w, axis=0) - lb_w[0]

    rows = ((bs + 1 + 7) // 8) * 8
    c_all = jnp.zeros((rows, d), F32).at[:bs].set(c).at[bs].set(c_ctx)
    mod = _modulation(c_all, w_mod, b_mod)

    w_in_bf = w_in.astype(BF16)
    wpa, wpb, wpc, wpd = (w.astype(BF16) for w in (w_proj_a, w_proj_b, w_proj_c, w_proj_d))
    w_out_bf = w_out.astype(BF16)

    consts = {'dft': {lp: _dft_tables(lp), ls: _dft_tables(ls)},
              'rope': _rope_tables(ls),
              'hgrn': _hgrn_tables()}
    past = cache_k.shape[2]
    ctx_k = cache_k.reshape(bs, depth, past, KV_WIDTH)
    ctx_v = cache_v.transpose(0, 1, 3, 4, 2)
    s0_t = jnp.swapaxes(state_hgrn.astype(F32), -1, -2)

    s5_mats = jax.vmap(_s5_matrices)(s5_lambda_re, s5_lambda_im, s5_log_step, s5_b_re, s5_b_im, s5_c_re, s5_c_im,
                                     s5_d.reshape(depth, S5_GROUPS, S5_GROUP_DIM))

    y_p, y_s = x_prompt, x_sample
    ks, vs, s5r, s5i, hg = [], [], [], [], []
    for l in range(depth):
        p = {'norm_pre': norm_pre[l].reshape(1, d), 'norm_post': norm_post[l], 'w_in': w_in_bf,
             'fourier_w': fourier_w[l],
             's5_mats': s5_mats,
             's5_glu_w': s5_glu_w[l], 's5_glu_b': s5_glu_b[l],
             'q_norm': q_norm[l], 'k_norm': k_norm[l],
             'lower_bound': lower_bounds[l], 'hgrn_norm': hgrn_norm[l],
             'w_proj_a': wpa, 'w_proj_b': wpb, 'w_proj_c': wpc, 'w_proj_d': wpd,
             'w_out': w_out_bf}
        m_ctx = mod[l, bs].reshape(1, 1, 3 * d)
        sh, sc, gt = (m_ctx[..., i * d:(i + 1) * d] for i in range(3))
        y_p, (k_l, v_l, sr_l, si_l, hg_l) = _trunk_layer(y_p, sh, sc, gt, p, None, l, consts)
        ks.append(k_l.reshape(bp, lp, N_KV_HEADS, HEAD_DIM))
        vs.append(v_l.reshape(bp, lp, N_KV_HEADS, HEAD_DIM))
        s5r.append(sr_l)
        s5i.append(si_l)
        hg.append(jnp.swapaxes(hg_l, -1, -2))
        m_s = mod[l, :bs].reshape(bs, 1, 3 * d)
        sh, sc, gt = (m_s[..., i * d:(i + 1) * d] for i in range(3))
        ctx = (ctx_k, ctx_v, state_s5_re, state_s5_im, s0_t)
        y_s, _ = _trunk_layer(y_s, sh, sc, gt, p, ctx, l, consts)

    return (y_p, y_s, jnp.stack(ks, axis=1), jnp.stack(vs, axis=1),
            jnp.stack(s5r, axis=1), jnp.stack(s5i, axis=1), jnp.stack(hg, axis=1))
```

```python
import functools
import math

import jax
import jax.numpy as jnp
import numpy as np
from jax import lax
from jax.experimental import pallas as pl
from jax.experimental.pallas import tpu as pltpu

F32 = jnp.float32
BF16 = jnp.bfloat16

NORM_EPS = 1e-6
F_FLOOR = 1e-30
LOG2E = 1.4426950408889634

N_BRANCHES = 4
FOURIER_WIDTH = 512
FOURIER_GROUPS = 4
FOURIER_GROUP_DIM = 128
S5_WIDTH = 512
S5_GROUP_DIM = 16
S5_GROUPS = 32
S5_STATE = 64
S5_CHUNK = 8
S5_GB = 8
S5_NBLK = S5_GROUPS // S5_GB
S5_K = S5_CHUNK * S5_GB * S5_GROUP_DIM
S5_PART = S5_GB * S5_STATE
S5_MAX_ROWS = 512
N_HEADS = 8
N_KV_HEADS = 2
HEAD_DIM = 128
Q_PER_KV = 4
ATTN_WIDTH = 1024
KV_WIDTH = 256
ROPE_THETA = 10000.0
GRID_W = 64
HGRN_HEADS = 4
HGRN_DK = 128
HGRN_DV = 128
HGRN_CHUNK = 64

COL_UA, COL_GA, COL_UB, COL_GB = 0, 512, 1024, 1536
COL_QC, COL_KC, COL_VC, COL_GC = 2048, 3072, 3328, 3584
COL_QD, COL_ID, COL_ZF, COL_ZB, COL_GD = 4608, 5120, 5632, 6144, 6656
COL_M = 7168
IN_COLS = 15360

FFT_RADIX = 8
FINAL_ROWS = 256
ATTN_KEY_CHUNK = 1024
ATTN_TQ = 512
INPROJ_TM = 1024
INPROJ_TN = 1536
INPROJ_ROWS = 256
FINAL_TM = 1024
FINAL_TC = 256
MOD_TN = 1024
QK_PREP_TM = 512
S5_GLU_TM = 1024
VMEM_LIMIT = 56 * 1024 * 1024
FINAL_VMEM_LIMIT = 58 * 1024 * 1024


def _cparams(n_grid):
    return pltpu.CompilerParams(dimension_semantics=("arbitrary",) * n_grid,
                                vmem_limit_bytes=VMEM_LIMIT)


def _sigmoid(x):
    return 1.0 / (1.0 + jnp.exp(-x))


def _gate_sigmoid(x):
    return 0.5 * jnp.tanh(0.5 * x) + 0.5


def _silu(x):
    return x * _gate_sigmoid(x)


def _dot(a, b):
    return jnp.dot(a, b, preferred_element_type=F32)


def _dot_nt(a, b):
    return lax.dot_general(a, b, (((1,), (1,)), ((), ())), preferred_element_type=F32)


def _dot_tn(a, b):
    return lax.dot_general(a, b, (((0,), (0,)), ((), ())), preferred_element_type=F32)


def _split2(x):
    hi = x.astype(BF16)
    lo = (x - hi.astype(F32)).astype(BF16)
    return hi, lo


def _dot3(a, b):
    ah, al = _split2(a)
    bh, bl = _split2(b)
    return _dot(ah, bh) + (_dot(ah, bl) + _dot(al, bh))


def _mod_kernel(c_ref, w_ref, b_ref, o_ref):
    a = _silu(c_ref[...])
    o_ref[0] = _dot3(a, w_ref[0]) + b_ref[0]


def _modulation(c_all, w_mod, b_mod):
    depth, d, n = w_mod.shape
    rows = c_all.shape[0]
    tn = MOD_TN
    return pl.pallas_call(
        _mod_kernel,
        grid=(depth, n // tn),
        in_specs=[pl.BlockSpec((rows, d), lambda l, j: (0, 0)),
                  pl.BlockSpec((1, d, tn), lambda l, j: (l, 0, j)),
                  pl.BlockSpec((1, 1, tn), lambda l, j: (l, 0, j))],
        out_specs=pl.BlockSpec((1, rows, tn), lambda l, j: (l, 0, j)),
        out_shape=jax.ShapeDtypeStruct((depth, rows, n), F32),
        compiler_params=_cparams(2),
        name="modulation",
    )(c_all, w_mod, b_mod.reshape(depth, 1, n))


def _inproj_kernel(x_ref, sh_ref, sc_ref, g_ref, w_ref, o_ref, ub_ref, h_scr):
    first = pl.program_id(2) == 0
    tm = h_scr.shape[0]

    @pl.when(first)
    def _():
        for r0 in range(0, tm, INPROJ_ROWS):
            rs = slice(r0, r0 + INPROJ_ROWS)
            x = x_ref[0, rs, :]
            ms = jnp.mean(x * x, axis=-1, keepdims=True)
            y = x * lax.rsqrt(ms + NORM_EPS) * g_ref[...]
            h = (y * (1.0 + sc_ref[0]) + sh_ref[0]).astype(BF16)
            h_scr[rs, :] = h
            res = _dot(h, w_ref[0])
            o_ref[0, rs, :] = res.astype(BF16)
            ub_ref[0, rs, :] = res[:, COL_UB:COL_UB + S5_WIDTH]

    @pl.when(jnp.logical_not(first))
    def _():
        o_ref[0] = _dot(h_scr[...], w_ref[0]).astype(BF16)


def _inproj(x, shift, scale, g, w_bf, layer, tm, tn):
    b, l, d = x.shape
    n = w_bf.shape[2]
    assert COL_UB + S5_WIDTH <= tn
    per_batch = shift.shape[0] > 1
    mod_map = (lambda bi, i, j: (bi, 0, 0)) if per_batch else (lambda bi, i, j: (0, 0, 0))
    return pl.pallas_call(
        _inproj_kernel,
        grid=(b, l // tm, n // tn),
        in_specs=[pl.BlockSpec((1, tm, d), lambda bi, i, j: (bi, i, 0)),
                  pl.BlockSpec((1, 1, d), mod_map),
                  pl.BlockSpec((1, 1, d), mod_map),
                  pl.BlockSpec((1, d), lambda bi, i, j: (0, 0)),
                  pl.BlockSpec((1, d, tn), lambda bi, i, j: (layer, 0, j))],
        out_specs=[pl.BlockSpec((1, tm, tn), lambda bi, i, j: (bi, i, j)),
                   pl.BlockSpec((1, tm, S5_WIDTH), lambda bi, i, j: (bi, i, 0))],
        out_shape=[jax.ShapeDtypeStruct((b, l, n), BF16),
                   jax.ShapeDtypeStruct((b, l, S5_WIDTH), F32)],
        scratch_shapes=[pltpu.VMEM((tm, d), BF16)],
        compiler_params=_cparams(3),
        name="inproj",
    )(x, shift, scale, g, w_bf)


def _dft_tables(l):
    r = FFT_RADIX
    l2 = l // r
    scale = 1.0 / math.sqrt(l * FOURIER_GROUP_DIM)
    k1 = np.arange(r, dtype=np.int64)[:, None]
    t2 = np.arange(l2, dtype=np.int64)[None, :]
    ang = 2.0 * np.pi * ((k1 * t2) % l) / l
    lanes = (l, FOURIER_GROUP_DIM)
    tw_re = jnp.broadcast_to(jnp.asarray(np.cos(ang).reshape(l, 1), F32), lanes)
    tw_im = jnp.broadcast_to(jnp.asarray(-np.sin(ang).reshape(l, 1), F32), lanes)
    k2 = np.arange(l2, dtype=np.int64)
    a3 = 2.0 * np.pi * ((k2[:, None] * k2[None, :]) % l2) / l2
    d3 = jnp.asarray(np.concatenate([np.cos(a3), np.sin(a3)], axis=1) * scale, F32).astype(BF16)
    c = np.arange(FOURIER_GROUP_DIM, dtype=np.int64)
    ac = 2.0 * np.pi * ((c[:, None] * c[None, :]) % FOURIER_GROUP_DIM) / FOURIER_GROUP_DIM
    cs = jnp.asarray(np.concatenate([np.cos(ac), np.sin(ac)], axis=1), F32).astype(BF16)
    return (tw_re, tw_im, d3), cs


def _cadd(a, b):
    return a[0] + b[0], a[1] + b[1]


def _csub(a, b):
    return a[0] - b[0], a[1] - b[1]


def _cmul_neg_i(a):
    return a[1], -a[0]


def _dft4(y0, y1, y2, y3):
    c0, c1 = _cadd(y0, y2), _cadd(y1, y3)
    d0, d1 = _csub(y0, y2), _cmul_neg_i(_csub(y1, y3))
    return _cadd(c0, c1), _cadd(d0, d1), _csub(c0, c1), _csub(d0, d1)


def _dft8(x):
    h = math.sqrt(0.5)
    a = [_cadd(x[n], x[n + 4]) for n in range(4)]
    d = [_csub(x[n], x[n + 4]) for n in range(4)]
    b = [d[0],
         ((d[1][0] + d[1][1]) * h, (d[1][1] - d[1][0]) * h),
         _cmul_neg_i(d[2]),
         ((d[3][1] - d[3][0]) * h, -(d[3][0] + d[3][1]) * h)]
    even, odd = _dft4(*a), _dft4(*b)
    return [even[0], odd[0], even[1], odd[1], even[2], odd[2], even[3], odd[3]]


def _fourier_kernel(u_ref, g_ref, cs_ref, twr_ref, twi_ref, d3_ref, w_ref, o_ref, b_scr, f_scr, *, l):
    r = FFT_RADIX
    l2 = l // r
    gd = FOURIER_GROUP_DIM
    t = _dot(u_ref[0].astype(BF16), cs_ref[...])
    blocks = [(t[j * l2:(j + 1) * l2, :gd], -t[j * l2:(j + 1) * l2, gd:]) for j in range(r)]
    for k1, (ar, ai) in enumerate(_dft8(blocks)):
        twr = twr_ref[k1 * l2:(k1 + 1) * l2, :]
        twi = twi_ref[k1 * l2:(k1 + 1) * l2, :]
        b_scr[k1, 0:l2, :] = (ar * twr - ai * twi).astype(BF16)
        b_scr[k1, l2:2 * l2, :] = (ar * twi + ai * twr).astype(BF16)
    for k1 in range(r):
        f_scr[pl.ds(k1, l2, stride=r), :] = _dot(d3_ref[...], b_scr[k1])
    y = _dot(f_scr[...].astype(BF16), w_ref[0].astype(BF16))
    o_ref[0] = (y * _silu(g_ref[0].astype(F32))).astype(BF16)


def _fourier_branch(proj, fourier_w, dft, cs):
    b, l, _ = proj.shape
    gd = FOURIER_GROUP_DIM
    l2 = l // FFT_RADIX
    tw_re, tw_im, d3 = dft
    return pl.pallas_call(
        functools.partial(_fourier_kernel, l=l),
        grid=(b, FOURIER_GROUPS),
        in_specs=[pl.BlockSpec((1, l, gd), lambda bi, g: (bi, 0, COL_UA // gd + g)),
                  pl.BlockSpec((1, l, gd), lambda bi, g: (bi, 0, COL_GA // gd + g)),
                  pl.BlockSpec((gd, 2 * gd), lambda bi, g: (0, 0)),
                  pl.BlockSpec((l, gd), lambda bi, g: (0, 0)),
                  pl.BlockSpec((l, gd), lambda bi, g: (0, 0)),
                  pl.BlockSpec((l2, 2 * l2), lambda bi, g: (0, 0)),
                  pl.BlockSpec((1, gd, gd), lambda bi, g: (g, 0, 0))],
        out_specs=pl.BlockSpec((1, l, gd), lambda bi, g: (bi, 0, g)),
        out_shape=jax.ShapeDtypeStruct((b, l, FOURIER_WIDTH), BF16),
        scratch_shapes=[pltpu.VMEM((FFT_RADIX, 2 * l2, gd), BF16), pltpu.VMEM((l, gd), F32)],
        compiler_params=_cparams(2),
        name="fourier",
    )(proj, proj, cs, tw_re, tw_im, d3, fourier_w)


def _s5_matrices(lam_re, lam_im, log_step, b_re, b_im, c_re, c_im, dskip):
    t = S5_CHUNK
    g, n, p = S5_GROUPS, S5_STATE, S5_GROUP_DIM
    hp = lax.Precision.HIGHEST
    lam_re = lam_re.astype(F32)
    lam_im = lam_im.astype(F32)
    step = jnp.exp(log_step.astype(F32))[..., None]
    rho = lam_re * step
    th = lam_im * step
    mag = jnp.exp(rho)
    lb_re = mag * jnp.cos(th)
    lb_im = mag * jnp.sin(th)
    nr = lb_re - 1.0
    den = lam_re * lam_re + lam_im * lam_im
    fr = (nr * lam_re + lb_im * lam_im) / den
    fi = (lb_im * lam_re - nr * lam_im) / den
    b_re = b_re.astype(F32)
    b_im = b_im.astype(F32)
    bb_re = fr[..., None] * b_re - fi[..., None] * b_im
    bb_im = fr[..., None] * b_im + fi[..., None] * b_re
    tau = jnp.arange(t + 1, dtype=F32)
    pw_mag = jnp.exp(rho[..., None] * tau)
    pw_re = pw_mag * jnp.cos(th[..., None] * tau)
    pw_im = pw_mag * jnp.sin(th[..., None] * tau)
    c_re = c_re.astype(F32)
    c_im = c_im.astype(F32)
    ca_re = c_re[..., None] * pw_re[:, :, None] - c_im[..., None] * pw_im[:, :, None]
    ca_im = c_re[..., None] * pw_im[:, :, None] + c_im[..., None] * pw_re[:, :, None]
    kern = jnp.einsum('dgpnt,dgnq->dgtpq', jnp.concatenate([ca_re[..., :t], -ca_im[..., :t]], axis=3),
                      jnp.concatenate([bb_re, bb_im], axis=2), precision=hp)
    sig = np.arange(t)[:, None]
    ta = np.arange(t)[None, :]
    idx_f = np.clip(ta - sig, 0, t - 1)
    idx_b = np.clip(sig - ta, 0, t - 1)
    m_f = jnp.asarray((ta >= sig).astype(np.float32))[None, :, :, None, None]
    m_b = jnp.asarray((sig >= ta).astype(np.float32))[None, :, :, None, None]
    tf = kern[0][:, idx_f] * m_f
    tb = kern[1][:, idx_b] * m_b
    eye_t = jnp.asarray(np.eye(t, dtype=np.float32))[None, :, :, None, None]
    dmat = dskip.astype(F32)[:, None, None, :, None] * jnp.asarray(np.eye(p, dtype=np.float32))[None, None, None]
    tt = tf + tb + eye_t * dmat
    nb, gb = S5_NBLK, S5_GB
    one = np.ones
    eye = lambda k: np.eye(k, dtype=np.float32)
    exp_t = jnp.asarray(np.einsum('tu,pq,b->tpubq', eye(t), eye(p), one(gb)).reshape(t * p, S5_K))
    exp_n = jnp.asarray(np.einsum('nm,b->nbm', eye(n), one(gb)).reshape(n, S5_PART))
    mask_t = jnp.asarray(np.einsum('ab,s,q,u,p->saqubp', eye(gb), one(t), one(p), one(t), one(p))
                         .reshape(S5_K, S5_K))
    mask_ba = jnp.asarray(np.einsum('ab,s,q,n->saqbn', eye(gb), one(t), one(p), one(n)).reshape(S5_K, S5_PART))
    t_small = tt.reshape(nb, gb, t, t, p, p).transpose(0, 2, 1, 5, 3, 4).reshape(nb, S5_K, t * p)
    tmat = jnp.matmul(t_small, exp_t, precision=hp) * mask_t

    def ba(d, powers):
        pr = pw_re[d][:, :, powers]
        pi = pw_im[d][:, :, powers]
        re = pr[..., None] * bb_re[d][:, :, None, :] - pi[..., None] * bb_im[d][:, :, None, :]
        im = pr[..., None] * bb_im[d][:, :, None, :] + pi[..., None] * bb_re[d][:, :, None, :]
        blk = lambda m: jnp.matmul(m.reshape(nb, gb, n, t, p).transpose(0, 3, 1, 4, 2).reshape(nb, S5_K, n),
                                   exp_n, precision=hp) * mask_ba
        return blk(re), blk(im)

    baf_re, baf_im = ba(0, np.arange(t - 1, -1, -1))
    bab_re, bab_im = ba(1, np.arange(t))
    bamat = jnp.concatenate([baf_re, bab_re, baf_im, bab_im], axis=-1)

    def ca(d, powers):
        blk = lambda m: jnp.matmul(exp_n.T, m[..., powers].reshape(nb, gb, p, n, t).transpose(0, 3, 4, 1, 2)
                                   .reshape(nb, n, S5_K), precision=hp) * mask_ba.T
        return blk(ca_re[d]), blk(ca_im[d])

    caf_re, caf_im = ca(0, np.arange(1, t + 1))
    cab_re, cab_im = ca(1, np.arange(t, 0, -1))
    camat = jnp.concatenate([caf_re, cab_re, -caf_im, -cab_im], axis=1)
    part = lambda m: m[..., t].reshape(nb, S5_PART)
    a_pow = jnp.stack([jnp.concatenate([part(pw_re[0]), part(pw_re[1])], axis=-1),
                       jnp.concatenate([part(pw_im[0]), part(pw_im[1])], axis=-1)], axis=1)
    return tmat.astype(BF16), bamat.astype(BF16), camat.astype(BF16), a_pow


def _s5_kernel(u_ref, t_ref, ba_ref, ca_ref, a_ref, h0_ref, y_ref, hf_ref, ub_scr, s_scr, hin_scr, y_scr,
               *, nc, bt):
    t, w, hp = S5_CHUNK, S5_GB * S5_GROUP_DIM, S5_PART
    for bi in range(bt):
        for tau in range(t):
            ub_scr[tau, pl.ds(bi, nc, stride=bt), :] = u_ref[bi, pl.ds(tau, nc, stride=t), :]
    ub = jnp.concatenate([ub_scr[tau] for tau in range(t)], axis=1).astype(BF16)
    s_scr[...] = _dot(ub, ba_ref[0, 0])
    ar_f, ar_b = a_ref[0, 0, 0:1, 0:hp], a_ref[0, 0, 0:1, hp:2 * hp]
    ai_f, ai_b = a_ref[0, 0, 1:2, 0:hp], a_ref[0, 0, 1:2, hp:2 * hp]

    def step(i, carry):
        re_f, im_f, re_b, im_b = carry
        rf = pl.ds(pl.multiple_of(i * bt, bt), bt)
        rb = pl.ds(pl.multiple_of((nc - 1 - i) * bt, bt), bt)
        hin_scr[rf, 0:hp] = re_f
        hin_scr[rf, 2 * hp:3 * hp] = im_f
        hin_scr[rb, hp:2 * hp] = re_b
        hin_scr[rb, 3 * hp:4 * hp] = im_b
        n_re_f = ar_f * re_f - ai_f * im_f + s_scr[rf, 0:hp]
        n_im_f = ar_f * im_f + ai_f * re_f + s_scr[rf, 2 * hp:3 * hp]
        n_re_b = ar_b * re_b - ai_b * im_b + s_scr[rb, hp:2 * hp]
        n_im_b = ar_b * im_b + ai_b * re_b + s_scr[rb, 3 * hp:4 * hp]
        return n_re_f, n_im_f, n_re_b, n_im_b

    h0 = h0_ref[...].reshape(bt, 4 * hp)
    init = (h0[:, 0:hp], h0[:, 2 * hp:3 * hp], h0[:, hp:2 * hp], h0[:, 3 * hp:4 * hp])
    fin = lax.fori_loop(0, nc, step, init, unroll=8)
    hf_ref[...] = jnp.concatenate([fin[0], fin[2], fin[1], fin[3]], axis=1).reshape(hf_ref.shape)
    y = _dot(ub, t_ref[0, 0]) + _dot(hin_scr[...].astype(BF16), ca_ref[0, 0])
    for tau in range(t):
        y_scr[tau] = y[:, tau * w:(tau + 1) * w]
    for bi in range(bt):
        for tau in range(t):
            y_ref[bi, pl.ds(tau, nc, stride=t), :] = y_scr[tau, pl.ds(bi, nc, stride=bt), :]


def _s5_core(ub, tmat, bamat, camat, a_pow, h0, layer):
    b, l, _ = ub.shape
    nc = l // S5_CHUNK
    w = S5_GB * S5_GROUP_DIM
    bt = b if nc * b <= S5_MAX_ROWS else 1
    if bt == 1:
        h0 = h0.reshape(S5_NBLK, b, 1, 4 * S5_PART)
        st_spec = pl.BlockSpec((1, 1, 1, 4 * S5_PART), lambda j, bi: (j, bi, 0, 0))
    else:
        st_spec = pl.BlockSpec((1, bt, 4 * S5_PART), lambda j, bi: (j, bi, 0))
    rows = nc * bt
    y, hf = pl.pallas_call(
        functools.partial(_s5_kernel, nc=nc, bt=bt),
        grid=(S5_NBLK, b // bt),
        in_specs=[pl.BlockSpec((bt, l, w), lambda j, bi: (bi, 0, j)),
                  pl.BlockSpec((1, 1, S5_K, S5_K), lambda j, bi: (layer, j, 0, 0)),
                  pl.BlockSpec((1, 1, S5_K, 4 * S5_PART), lambda j, bi: (layer, j, 0, 0)),
                  pl.BlockSpec((1, 1, 4 * S5_PART, S5_K), lambda j, bi: (layer, j, 0, 0)),
                  pl.BlockSpec((1, 1, 2, 2 * S5_PART), lambda j, bi: (layer, j, 0, 0)),
                  st_spec],
        out_specs=[pl.BlockSpec((bt, l, w), lambda j, bi: (bi, 0, j)), st_spec],
        out_shape=[jax.ShapeDtypeStruct((b, l, S5_WIDTH), F32),
                   jax.ShapeDtypeStruct(h0.shape, F32)],
        scratch_shapes=[pltpu.VMEM((S5_CHUNK, rows, w), F32), pltpu.VMEM((rows, 4 * S5_PART), F32),
                        pltpu.VMEM((rows, 4 * S5_PART), F32), pltpu.VMEM((S5_CHUNK, rows, w), F32)],
        compiler_params=_cparams(2),
        name="s5_core",
    )(ub, tmat, bamat, camat, a_pow, h0)
    return y, hf.reshape(S5_NBLK, b, 4 * S5_PART)


def _s5_glu_kernel(y_ref, g_ref, w_ref, b_ref, o_ref):
    y = y_ref[0]
    y = 0.5 * y * (1.0 + jnp.tanh(math.sqrt(2.0 / math.pi) * (y + 0.044715 * (y * y * y))))
    z = _dot(y.astype(BF16), w_ref[...].astype(BF16)) + b_ref[...]
    o_ref[0] = (y * _sigmoid(z) * _silu(g_ref[0].astype(F32))).astype(BF16)


def _s5_glu(y, proj, glu_w, glu_b, tm):
    b, l, w = y.shape
    return pl.pallas_call(
        _s5_glu_kernel,
        grid=(b, l // tm),
        in_specs=[pl.BlockSpec((1, tm, w), lambda bi, i: (bi, i, 0)),
                  pl.BlockSpec((1, tm, w), lambda bi, i: (bi, i, COL_GB // w)),
                  pl.BlockSpec((w, w), lambda bi, i: (0, 0)),
                  pl.BlockSpec((1, w), lambda bi, i: (0, 0))],
        out_specs=pl.BlockSpec((1, tm, w), lambda bi, i: (bi, i, 0)),
        out_shape=jax.ShapeDtypeStruct((b, l, w), BF16),
        compiler_params=_cparams(2),
        name="s5_glu",
    )(y, proj, glu_w, glu_b.reshape(1, w))


def _s5_branch(proj, ub, mats, layer, glu_w, glu_b, h0_re, h0_im):
    b, l, _ = proj.shape
    g, n = S5_GROUPS, S5_STATE
    if h0_re is None:
        h0 = jnp.zeros((S5_NBLK, b, 4 * S5_PART), F32)
    else:
        blk = lambda h: h.astype(F32).reshape(b, S5_NBLK, S5_PART).transpose(1, 0, 2)
        h0 = jnp.concatenate([blk(h0_re[:, 0]), blk(h0_re[:, 1]), blk(h0_im[:, 0]), blk(h0_im[:, 1])], axis=-1)
    y, hf = _s5_core(ub, *mats, h0, layer)
    out = _s5_glu(y, proj, glu_w, glu_b, min(l, S5_GLU_TM))
    part = lambda i: hf[..., i * S5_PART:(i + 1) * S5_PART].transpose(1, 0, 2).reshape(b, g, n)
    fin_re = jnp.stack([part(0), part(1)], axis=1)
    fin_im = jnp.stack([part(2), part(3)], axis=1)
    return out, fin_re, fin_im


def _rope_tables(l):
    rows = l // GRID_W
    row = jnp.broadcast_to(jnp.arange(rows, dtype=F32)[:, None], (rows, GRID_W)).reshape(-1)
    col = jnp.broadcast_to(jnp.arange(GRID_W, dtype=F32)[None, :], (rows, GRID_W)).reshape(-1)
    half = HEAD_DIM // 2
    inv = ROPE_THETA ** (-jnp.arange(0, half, 2, dtype=F32) / half)
    ar = row[:, None] * inv
    ac = col[:, None] * inv
    cos = jnp.concatenate([jnp.cos(ar), jnp.cos(ar), jnp.cos(ac), jnp.cos(ac)], axis=1)
    sin = jnp.concatenate([-jnp.sin(ar), jnp.sin(ar), -jnp.sin(ac), jnp.sin(ac)], axis=1)
    return cos, sin


def _qk_kernel(*refs, rope, want_kf):
    q_ref, k_ref, v_ref, qn_ref, kn_ref = refs[:5]
    pos = 5
    if rope:
        cos_ref, sin_ref = refs[pos:pos + 2]
        pos += 2
    qo_ref, ko_ref, vo_ref = refs[pos:pos + 3]
    pos += 3
    if want_kf:
        kf_ref = refs[pos]
    hd = HEAD_DIM
    tm = q_ref.shape[1]
    quarter = hd // 4
    if rope:
        cos = cos_ref[...]
        sin = sin_ref[...]
        first = (lax.broadcasted_iota(jnp.int32, (tm, hd), 1) % (2 * quarter)) < quarter

    def norm(x, gain):
        ms = jnp.mean(x * x, axis=-1, keepdims=True)
        return x * lax.rsqrt(ms + NORM_EPS) * gain

    def rot(x):
        if not rope:
            return x
        nxt = pltpu.roll(x, hd - quarter, 1)
        prv = pltpu.roll(x, quarter, 1)
        return x * cos + jnp.where(first, nxt, prv) * sin

    qscale = (hd ** -0.5) * LOG2E
    for h in range(N_HEADS):
        x = norm(q_ref[0, :, h * hd:(h + 1) * hd].astype(F32), qn_ref[...])
        qo_ref[0, :, h * hd:(h + 1) * hd] = (rot(x) * qscale).astype(BF16)
    for h in range(N_KV_HEADS):
        x = norm(k_ref[0, :, h * hd:(h + 1) * hd].astype(F32), kn_ref[...])
        if want_kf:
            kf_ref[0, :, h * hd:(h + 1) * hd] = x
        ko_ref[0, :, h * hd:(h + 1) * hd] = rot(x).astype(BF16)
        vo_ref[0, h] = v_ref[0, :, h * hd:(h + 1) * hd].astype(F32).T.astype(BF16)


def _qk_prep(proj, q_norm, k_norm, rope_tabs, want_kf, tm):
    b, l, _ = proj.shape
    rope = rope_tabs is not None
    in_specs = [pl.BlockSpec((1, tm, ATTN_WIDTH), lambda bi, i: (bi, i, COL_QC // ATTN_WIDTH)),
                pl.BlockSpec((1, tm, KV_WIDTH), lambda bi, i: (bi, i, COL_KC // KV_WIDTH)),
                pl.BlockSpec((1, tm, KV_WIDTH), lambda bi, i: (bi, i, COL_VC // KV_WIDTH)),
                pl.BlockSpec((1, HEAD_DIM), lambda bi, i: (0, 0)),
                pl.BlockSpec((1, HEAD_DIM), lambda bi, i: (0, 0))]
    args = [proj, proj, proj, q_norm.reshape(1, HEAD_DIM), k_norm.reshape(1, HEAD_DIM)]
    if rope:
        in_specs += [pl.BlockSpec((tm, HEAD_DIM), lambda bi, i: (i, 0))] * 2
        args += list(rope_tabs)
    out_specs = [pl.BlockSpec((1, tm, ATTN_WIDTH), lambda bi, i: (bi, i, 0)),
                 pl.BlockSpec((1, tm, KV_WIDTH), lambda bi, i: (bi, i, 0)),
                 pl.BlockSpec((1, N_KV_HEADS, HEAD_DIM, tm), lambda bi, i: (bi, 0, 0, i))]
    out_shape = [jax.ShapeDtypeStruct((b, l, ATTN_WIDTH), BF16),
                 jax.ShapeDtypeStruct((b, l, KV_WIDTH), BF16),
                 jax.ShapeDtypeStruct((b, N_KV_HEADS, HEAD_DIM, l), BF16)]
    if want_kf:
        out_specs.append(pl.BlockSpec((1, tm, KV_WIDTH), lambda bi, i: (bi, i, 0)))
        out_shape.append(jax.ShapeDtypeStruct((b, l, KV_WIDTH), F32))
    return pl.pallas_call(
        functools.partial(_qk_kernel, rope=rope, want_kf=want_kf),
        grid=(b, l // tm),
        in_specs=in_specs, out_specs=out_specs, out_shape=out_shape,
        compiler_params=_cparams(2),
        name="qk_prep",
    )(*args)


def _attn_kernel(*refs, has_ctx):
    s_scr = refs[-1]
    q_ref, k_ref, v_ref = refs[:3]
    if has_ctx:
        ck_ref, cv_ref, g_ref, o_ref = refs[3:7]
        ck = ck_ref[0, 0].astype(BF16)
        cvt = cv_ref[0, 0, 0].astype(BF16)
        past = ck.shape[0]
    else:
        g_ref, o_ref = refs[3:5]
    hd = HEAD_DIM
    tq = q_ref.shape[1]
    lk = k_ref.shape[1]
    cr = min(lk, ATTN_KEY_CHUNK)
    chunks = [(c * cr, cr, (lambda c=c: k_ref[0, c * cr:(c + 1) * cr, :]),
               (lambda c=c: v_ref[0, 0, :, c * cr:(c + 1) * cr])) for c in range(lk // cr)]
    if has_ctx:
        chunks.append((lk, past, lambda: ck, lambda: cvt))
    qs = [q_ref[0, :, g * hd:(g + 1) * hd] for g in range(Q_PER_KV)]
    gate = _silu(g_ref[0].astype(F32))

    def score(g, chunk, m):
        off, rows, keys, _ = chunk
        s = _dot_nt(keys(), qs[g])
        s_scr[g % 2, off:off + rows, :] = s
        return jnp.maximum(m, jnp.max(s, axis=0, keepdims=True))

    def accumulate(g, chunk, m, den, ot):
        off, rows, _, values_t = chunk
        p = jnp.exp2(s_scr[g % 2, off:off + rows, :] - m)
        return den + jnp.sum(p, axis=0, keepdims=True), ot + _dot(values_t(), p.astype(BF16))

    neg = jnp.full((1, tq), -jnp.inf, F32)
    m_cur = neg
    for chunk in chunks:
        m_cur = score(0, chunk, m_cur)
    for g in range(Q_PER_KV):
        m_next = neg
        den = jnp.zeros((1, tq), F32)
        ot = jnp.zeros((hd, tq), F32)
        for chunk in chunks:
            if g + 1 < Q_PER_KV:
                m_next = score(g + 1, chunk, m_next)
            den, ot = accumulate(g, chunk, m_cur, den, ot)
        m_cur = m_next
        o = (ot / den).T
        o_ref[0, :, g * hd:(g + 1) * hd] = (o * gate[:, g * hd:(g + 1) * hd]).astype(BF16)


def _attention(qs, kr, vt, proj, ctx_k, ctx_v, layer, tq):
    b, l, _ = qs.shape
    gw = Q_PER_KV * HEAD_DIM
    has_ctx = ctx_k is not None
    in_specs = [pl.BlockSpec((1, tq, gw), lambda bi, h, i: (bi, i, h)),
                pl.BlockSpec((1, l, HEAD_DIM), lambda bi, h, i: (bi, 0, h)),
                pl.BlockSpec((1, 1, HEAD_DIM, l), lambda bi, h, i: (bi, h, 0, 0))]
    args = [qs, kr, vt]
    if has_ctx:
        past = ctx_k.shape[2]
        in_specs += [pl.BlockSpec((1, 1, past, HEAD_DIM), lambda bi, h, i: (bi, layer, 0, h)),
                     pl.BlockSpec((1, 1, 1, HEAD_DIM, past), lambda bi, h, i: (bi, layer, h, 0, 0))]
        args += [ctx_k, ctx_v]
    in_specs.append(pl.BlockSpec((1, tq, gw), lambda bi, h, i: (bi, i, COL_GC // gw + h)))
    args.append(proj)
    return pl.pallas_call(
        functools.partial(_attn_kernel, has_ctx=has_ctx),
        grid=(b, N_KV_HEADS, l // tq),
        in_specs=in_specs,
        out_specs=pl.BlockSpec((1, tq, gw), lambda bi, h, i: (bi, i, h)),
        out_shape=jax.ShapeDtypeStruct((b, l, ATTN_WIDTH), BF16),
        scratch_shapes=[pltpu.VMEM((2, l + (ctx_k.shape[2] if has_ctx else 0), tq), F32)],
        compiler_params=_cparams(3),
        name="attention",
    )(*args)


HGRN_COARSE = (32, 16, 8, 4)
HGRN_FINE = (2,)
HGRN_LEVELS = HGRN_COARSE + HGRN_FINE + (1,)
HGRN_CHUNKS_PER_TRIP = 4


def _hgrn_tables():
    c = HGRN_CHUNK
    t = np.arange(c)
    w01, sm = [], []
    for d in range(2):
        cum = (t[None, :] <= t[:, None]) if d == 0 else (t[None, :] >= t[:, None])
        cum = cum.astype(np.float32)
        w01.append(np.concatenate([cum] + [cum[_hgrn_ref_row(t, hs, d)] for hs in HGRN_FINE], axis=0))
        sd = []
        for hs in HGRN_LEVELS:
            is_q = _hgrn_is_query(t, hs, d)
            same = (t[:, None] // (2 * hs)) == (t[None, :] // (2 * hs))
            sd.append((same & is_q[:, None] & ~is_q[None, :]).astype(np.float32))
        sd.append(np.eye(c, dtype=np.float32))
        sm.append(np.stack(sd))
    return jnp.asarray(np.stack(w01), BF16), jnp.asarray(np.stack(sm), F32)


def _hgrn_ref_row(t, hs, d):
    base = t & ~(2 * hs - 1)
    return base + hs - 1 if d == 0 else base + hs


def _hgrn_is_query(t, hs, d):
    return ((t & hs) != 0) if d == 0 else ((t & hs) == 0)


def _hgrn_cums(z, lbv, w01):
    f = lbv + (1.0 - lbv) * _sigmoid(z)
    fm = jnp.maximum(f, F_FLOOR)
    hi, lo = _split2(jnp.log2(fm))
    return fm, 1.0 - f, _dot(w01, hi) + _dot(w01, lo)


def _hgrn_products(q, k, fm, allc, rows, d):
    c = HGRN_CHUNK
    sub = 8
    cum = allc[0:c]
    refs = {}
    for hs in HGRN_COARSE:
        pieces = []
        for j in range(c // sub):
            rr = int(_hgrn_ref_row(np.int64(j * sub), hs, d))
            pieces.append(jnp.broadcast_to(cum[rr:rr + 1, :], (sub, HGRN_DK)))
        refs[hs] = jnp.concatenate(pieces, axis=0)
    for i, hs in enumerate(HGRN_FINE):
        refs[hs] = allc[(i + 1) * c:(i + 2) * c]
    lrow = c - 1 if d == 0 else 0
    last = cum[lrow:lrow + 1, :]
    raws = [_dot_nt(q.astype(BF16), k.astype(BF16))]
    for hs in HGRN_LEVELS:
        is_q = ((rows & hs) != 0) if d == 0 else ((rows & hs) == 0)
        if hs == 1:
            x = jnp.where(is_q, q * fm, k).astype(BF16)
        else:
            x = (jnp.where(is_q, q, k) * jnp.exp2((cum - refs[hs]) * jnp.where(is_q, 1.0, -1.0))).astype(BF16)
        raws.append(_dot_nt(x, x))
    qi = (q * jnp.exp2(cum)).astype(BF16)
    ks = (k * jnp.exp2(last - cum)).astype(BF16)
    return raws, qi, ks, jnp.exp2(last)


def _hgrn_kernel(*refs, nc, has_s0, want_state):
    q_ref, v_ref, zf_ref, zb_ref, g_ref, lb_ref, ng_ref, w01_ref, sm_ref = refs[:9]
    pos = 9
    if has_s0:
        s0_ref = refs[pos]
        pos += 1
    o_ref = refs[pos]
    pos += 1
    if want_state:
        sf_ref = refs[pos]
        pos += 1
    o_scr, st_scr = refs[pos:pos + 2]
    c = HGRN_CHUNK
    l = nc * c
    for d in range(2):
        if has_s0:
            st_scr[d] = s0_ref[0, d, 0]
        else:
            st_scr[d] = jnp.zeros((HGRN_DV, HGRN_DK), F32)
    rows = lax.broadcasted_iota(jnp.int32, (c, HGRN_DK), 0)

    per_trip = math.gcd(nc, HGRN_CHUNKS_PER_TRIP)

    def body(i, carry):
        insts = []
        for u in range(per_trip):
            for d in range(2):
                ci = per_trip * i + u if d == 0 else nc - 1 - (per_trip * i + u)
                insts.append((d, pl.multiple_of(ci * c, c)))
        gates = [_hgrn_cums((zf_ref, zb_ref)[d][0, pl.ds(r, c), :].astype(F32), lb_ref[d, 0], w01_ref[d])
                 for d, r in insts]
        prods = [_hgrn_products(q_ref[0, pl.ds(r, c), :].astype(F32), k, fm, allc, rows, d)
                 for (d, r), (fm, k, allc) in zip(insts, gates)]
        nl = len(HGRN_LEVELS)
        for (d, r), (raws, qi, ks, dec) in zip(insts, prods):
            scores = sm_ref[d, nl] * raws[0]
            for li in range(nl):
                scores = scores + sm_ref[d, li] * raws[li + 1]
            vb = v_ref[0, pl.ds(r, c), :].astype(BF16)
            st = st_scr[d]
            o_scr[d, pl.ds(r, c), :] = _dot(scores.astype(BF16), vb) + _dot_nt(qi, st.astype(BF16))
            st_scr[d] = dec * st + _dot_tn(vb, ks)
        return carry

    lax.fori_loop(0, nc // per_trip, body, 0)
    if want_state:
        for d in range(2):
            sf_ref[0, d, 0] = st_scr[d]

    fb = min(l, 256)

    def finish(i, carry):
        r = pl.multiple_of(i * fb, fb)
        tot = o_scr[0, pl.ds(r, fb), :] + o_scr[1, pl.ds(r, fb), :]
        ms = jnp.mean(tot * tot, axis=-1, keepdims=True)
        y = tot * lax.rsqrt(ms + NORM_EPS) * ng_ref[...]
        o_ref[0, pl.ds(r, fb), :] = (y * _silu(g_ref[0, pl.ds(r, fb), :].astype(F32))).astype(BF16)
        return carry

    lax.fori_loop(0, l // fb, finish, 0)


def _hgrn_branch(proj, lower_bound, norm_g, tables, s0_t, want_state):
    b, l, _ = proj.shape
    nc = l // HGRN_CHUNK
    w = HGRN_DK
    has_s0 = s0_t is not None
    w01, sm = tables

    def col(off):
        return lambda bi, h: (bi, 0, off // w + h)

    in_specs = [pl.BlockSpec((1, l, w), col(COL_QD)),
                pl.BlockSpec((1, l, w), col(COL_ID)),
                pl.BlockSpec((1, l, w), col(COL_ZF)),
                pl.BlockSpec((1, l, w), col(COL_ZB)),
                pl.BlockSpec((1, l, w), col(COL_GD)),
                pl.BlockSpec((2, 1, 1, w), lambda bi, h: (0, h, 0, 0)),
                pl.BlockSpec((1, w), lambda bi, h: (0, 0)),
                pl.BlockSpec(w01.shape, lambda bi, h: (0, 0, 0)),
                pl.BlockSpec(sm.shape, lambda bi, h: (0, 0, 0, 0))]
    args = [proj, proj, proj, proj, proj, lower_bound.reshape(2, HGRN_HEADS, 1, w), norm_g.reshape(1, w),
            w01, sm]
    st_spec = pl.BlockSpec((1, 2, 1, HGRN_DV, HGRN_DK), lambda bi, h: (bi, 0, h, 0, 0))
    if has_s0:
        in_specs.append(st_spec)
        args.append(s0_t)
    out_specs = [pl.BlockSpec((1, l, w), lambda bi, h: (bi, 0, h))]
    out_shape = [jax.ShapeDtypeStruct((b, l, HGRN_HEADS * HGRN_DV), BF16)]
    if want_state:
        out_specs.append(st_spec)
        out_shape.append(jax.ShapeDtypeStruct((b, 2, HGRN_HEADS, HGRN_DV, HGRN_DK), F32))
    res = pl.pallas_call(
        functools.partial(_hgrn_kernel, nc=nc, has_s0=has_s0, want_state=want_state),
        grid=(b, HGRN_HEADS),
        in_specs=in_specs, out_specs=out_specs, out_shape=out_shape,
        scratch_shapes=[pltpu.VMEM((2, l, HGRN_DV), F32), pltpu.VMEM((2, HGRN_DV, HGRN_DK), F32)],
        compiler_params=_cparams(2),
        name="hgrn",
    )(*args)
    return res if want_state else (res[0], None)


def _final_kernel(x_ref, gt_ref, ma_ref, mb_ref, mc_ref, md_ref, ya_ref, yb_ref, yc_ref, yd_ref,
                  wa_ref, wb_ref, wc_ref, wd_ref, wo_ref, gp_ref, o_ref, *, nct):
    ci = pl.program_id(2)
    tm = o_ref.shape[1]
    subs = [slice(r0, r0 + FINAL_ROWS) for r0 in range(0, tm, FINAL_ROWS)]
    m_refs = (ma_ref, mb_ref, mc_ref, md_ref)
    y_refs = (ya_ref, yb_ref, yc_ref, yd_ref)
    w_refs = (wa_ref, wb_ref, wc_ref, wd_ref)

    def branch_dots(rs):
        return [_dot(y_ref[0, rs, :], w_ref[0]) for y_ref, w_ref in zip(y_refs, w_refs)]

    def step(first, last):
        nxt = branch_dots(subs[0])
        for si, rs in enumerate(subs):
            branch = nxt
            if si + 1 < len(subs):
                nxt = branch_dots(subs[si + 1])
            mixed = None
            for m_ref, dot in zip(m_refs, branch):
                term = _gate_sigmoid(m_ref[0, rs, :].astype(F32)) * dot
                mixed = term if mixed is None else mixed + term
            contrib = _dot(mixed.astype(BF16), wo_ref[0])
            out = contrib if first else o_ref[0, rs, :] + contrib
            if last:
                ms = jnp.mean(out * out, axis=-1, keepdims=True)
                y = out * lax.rsqrt(ms + NORM_EPS) * gp_ref[...]
                o_ref[0, rs, :] = x_ref[0, rs, :] + gt_ref[0] * y
            else:
                o_ref[0, rs, :] = out

    if nct == 1:
        step(True, True)
    else:
        pl.when(ci == 0)(lambda: step(True, False))
        pl.when(jnp.logical_and(ci > 0, ci < nct - 1))(lambda: step(False, False))
        pl.when(ci == nct - 1)(lambda: step(False, True))


def _final(x, gate, proj, ys, w_projs, w_out, norm_post, layer, tm, tc):
    b, l, d = x.shape
    nct = d // tc
    per_batch = gate.shape[0] > 1
    gate_map = (lambda bi, i, c: (bi, 0, 0)) if per_batch else (lambda bi, i, c: (0, 0, 0))

    def mspec(j):
        return pl.BlockSpec((1, tm, tc), lambda bi, i, c: (bi, i, (COL_M + j * d) // tc + c))

    in_specs = [pl.BlockSpec((1, tm, d), lambda bi, i, c: (bi, i, 0)),
                pl.BlockSpec((1, 1, d), gate_map)]
    in_specs += [mspec(j) for j in range(N_BRANCHES)]
    in_specs += [pl.BlockSpec((1, tm, y.shape[-1]), lambda bi, i, c: (bi, i, 0)) for y in ys]
    in_specs += [pl.BlockSpec((1, w.shape[1], tc), lambda bi, i, c: (layer, 0, c)) for w in w_projs]
    in_specs += [pl.BlockSpec((1, tc, d), lambda bi, i, c: (layer, c, 0)),
                 pl.BlockSpec((1, d), lambda bi, i, c: (0, 0))]
    return pl.pallas_call(
        functools.partial(_final_kernel, nct=nct),
        grid=(b, l // tm, nct),
        in_specs=in_specs,
        out_specs=pl.BlockSpec((1, tm, d), lambda bi, i, c: (bi, i, 0)),
        out_shape=jax.ShapeDtypeStruct((b, l, d), F32),
        compiler_params=pltpu.CompilerParams(dimension_semantics=("arbitrary",) * 3,
                                             vmem_limit_bytes=FINAL_VMEM_LIMIT),
        name="merge_out",
    )(x, gate, proj, proj, proj, proj, *ys, *w_projs, w_out, norm_post.reshape(1, d))


def _trunk_layer(x, shift, scale, gate, p, ctx, layer, consts):
    b, l, d = x.shape
    shared_mod = shift.shape[0] == 1
    if shared_mod:
        proj, ub = _inproj(x.reshape(1, b * l, d), shift, scale, p['norm_pre'], p['w_in'], layer,
                           INPROJ_TM, INPROJ_TN)
        proj = proj.reshape(b, l, IN_COLS)
        ub = ub.reshape(b, l, S5_WIDTH)
    else:
        proj, ub = _inproj(x, shift, scale, p['norm_pre'], p['w_in'], layer, INPROJ_TM, INPROJ_TN)

    dl, cs = consts['dft'][l]
    y_a = _fourier_branch(proj, p['fourier_w'], dl, cs)

    if ctx is None:
        y_b, s5_re, s5_im = _s5_branch(proj, ub, p['s5_mats'], layer, p['s5_glu_w'], p['s5_glu_b'], None, None)
        qs, kr, vt, kf = _qk_prep(proj, p['q_norm'], p['k_norm'], None, True, min(l, QK_PREP_TM))
        y_c = _attention(qs, kr, vt, proj, None, None, layer, min(l, ATTN_TQ))
        y_d, hg = _hgrn_branch(proj, p['lower_bound'], p['hgrn_norm'], consts['hgrn'], None, True)
    else:
        cache_k, cache_v, st_re, st_im, s0_t = ctx
        y_b, _, _ = _s5_branch(proj, ub, p['s5_mats'], layer, p['s5_glu_w'], p['s5_glu_b'],
                               st_re[:, layer], st_im[:, layer])
        qs, kr, vt = _qk_prep(proj, p['q_norm'], p['k_norm'], consts['rope'], False, min(l, QK_PREP_TM))
        y_c = _attention(qs, kr, vt, proj, cache_k, cache_v, layer, min(l, ATTN_TQ))
        y_d, _ = _hgrn_branch(proj, p['lower_bound'], p['hgrn_norm'], consts['hgrn'], s0_t[:, layer], False)

    ys = (y_a, y_b, y_c, y_d)
    wps = (p['w_proj_a'], p['w_proj_b'], p['w_proj_c'], p['w_proj_d'])
    if shared_mod:
        n = b * l
        x_new = _final(x.reshape(1, n, d), gate, proj.reshape(1, n, IN_COLS),
                       tuple(y.reshape(1, n, y.shape[-1]) for y in ys), wps, p['w_out'], p['norm_post'], layer,
                       FINAL_TM, FINAL_TC)
        x_new = x_new.reshape(b, l, d)
    else:
        x_new = _final(x, gate, proj, ys, wps, p['w_out'], p['norm_post'], layer, FINAL_TM, FINAL_TC)

    if ctx is None:
        v_f = proj[:, :, COL_VC:COL_VC + KV_WIDTH].astype(F32)
        return x_new, (kf, v_f, s5_re, s5_im, hg)
    return x_new, None


def kernel(x_prompt, x_sample, c, cache_k, cache_v, state_s5_re, state_s5_im, state_hgrn, c_ctx,
           norm_pre, norm_post, w_mod, b_mod, w_in, fourier_w,
           s5_lambda_re, s5_lambda_im, s5_log_step, s5_b_re, s5_b_im, s5_c_re, s5_c_im,
           s5_d, s5_glu_w, s5_glu_b, q_norm, k_norm, hgrn_lb_logits, hgrn_norm,
           w_proj_a, w_proj_b, w_proj_c, w_proj_d, w_out):
    depth = w_in.shape[0]
    bp, lp, d = x_prompt.shape
    bs, ls, _ = x_sample.shape

    lb_w = jax.nn.softmax(hgrn_lb_logits.astype(F32), axis=0)
    lower_bounds = jnp.cumsum(lb_w, axis=0) - lb_w[0]

    rows = ((bs + 1 + 7) // 8) * 8
    c_all = jnp.zeros((rows, d), F32).at[:bs].set(c).at[bs].set(c_ctx)
    mod = _modulation(c_all, w_mod, b_mod)

    w_in_bf = w_in.astype(BF16)
    wpa, wpb, wpc, wpd = (w.astype(BF16) for w in (w_proj_a, w_proj_b, w_proj_c, w_proj_d))
    w_out_bf = w_out.astype(BF16)

    consts = {'dft': {lp: _dft_tables(lp), ls: _dft_tables(ls)},
              'rope': _rope_tables(ls),
              'hgrn': _hgrn_tables()}
    past = cache_k.shape[2]
    ctx_k = cache_k.reshape(bs, depth, past, KV_WIDTH)
    ctx_v = cache_v.transpose(0, 1, 3, 4, 2)
    s0_t = jnp.swapaxes(state_hgrn.astype(F32), -1, -2)

    s5_mats = jax.vmap(_s5_matrices)(s5_lambda_re, s5_lambda_im, s5_log_step, s5_b_re, s5_b_im, s5_c_re, s5_c_im,
                                     s5_d.reshape(depth, S5_GROUPS, S5_GROUP_DIM))

    y_p, y_s = x_prompt, x_sample
    ks, vs, s5r, s5i, hg = [], [], [], [], []
    for l in range(depth):
        p = {'norm_pre': norm_pre[l].reshape(1, d), 'norm_post': norm_post[l], 'w_in': w_in_bf,
             'fourier_w': fourier_w[l],
             's5_mats': s5_mats,
             's5_glu_w': s5_glu_w[l], 's5_glu_b': s5_glu_b[l],
             'q_norm': q_norm[l], 'k_norm': k_norm[l],
             'lower_bound': lower_bounds[l], 'hgrn_norm': hgrn_norm[l],
             'w_proj_a': wpa, 'w_proj_b': wpb, 'w_proj_c': wpc, 'w_proj_d': wpd,
             'w_out': w_out_bf}
        m_ctx = mod[l, bs].reshape(1, 1, 3 * d)
        sh, sc, gt = (m_ctx[..., i * d:(i + 1) * d] for i in range(3))
        y_p, (k_l, v_l, sr_l, si_l, hg_l) = _trunk_layer(y_p, sh, sc, gt, p, None, l, consts)
        ks.append(k_l.reshape(bp, lp, N_KV_HEADS, HEAD_DIM))
        vs.append(v_l.reshape(bp, lp, N_KV_HEADS, HEAD_DIM))
        s5r.append(sr_l)
        s5i.append(si_l)
        hg.append(jnp.swapaxes(hg_l, -1, -2))
        m_s = mod[l, :bs].reshape(bs, 1, 3 * d)
        sh, sc, gt = (m_s[..., i * d:(i + 1) * d] for i in range(3))
        ctx = (ctx_k, ctx_v, state_s5_re, state_s5_im, s0_t)
        y_s, _ = _trunk_layer(y_s, sh, sc, gt, p, ctx, l, consts)

    return (y_p, y_s, jnp.stack(ks, axis=1), jnp.stack(vs, axis=1),
            jnp.stack(s5r, axis=1), jnp.stack(s5i, axis=1), jnp.stack(hg, axis=1))
```

```python
import functools
import math

import jax
import jax.numpy as jnp
import numpy as np
from jax import lax
from jax.experimental import pallas as pl
from jax.experimental.pallas import tpu as pltpu

F32 = jnp.float32
BF16 = jnp.bfloat16

NORM_EPS = 1e-6
F_FLOOR = 1e-30
LOG2E = 1.4426950408889634

N_BRANCHES = 4
FOURIER_WIDTH = 512
FOURIER_GROUPS = 4
FOURIER_GROUP_DIM = 128
S5_WIDTH = 512
S5_GROUP_DIM = 16
S5_GROUPS = 32
S5_STATE = 64
S5_CHUNK = 8
S5_GB = 8
S5_NBLK = S5_GROUPS // S5_GB
S5_K = S5_CHUNK * S5_GB * S5_GROUP_DIM
S5_PART = S5_GB * S5_STATE
S5_MAX_ROWS = 512
N_HEADS = 8
N_KV_HEADS = 2
HEAD_DIM = 128
Q_PER_KV = 4
ATTN_WIDTH = 1024
KV_WIDTH = 256
ROPE_THETA = 10000.0
GRID_W = 64
HGRN_HEADS = 4
HGRN_DK = 128
HGRN_DV = 128
HGRN_CHUNK = 64

COL_UA, COL_GA, COL_UB, COL_GB = 0, 512, 1024, 1536
COL_QC, COL_KC, COL_VC, COL_GC = 2048, 3072, 3328, 3584
COL_QD, COL_ID, COL_ZF, COL_ZB, COL_GD = 4608, 5120, 5632, 6144, 6656
COL_M = 7168
IN_COLS = 15360

FFT_RADIX = 8
FINAL_ROWS = 256
ATTN_KEY_CHUNK = 1024
ATTN_TQ = 512
INPROJ_TM = 1024
INPROJ_TN = 1536
INPROJ_ROWS = 256
FINAL_TM = 1024
FINAL_TC = 256
MOD_TN = 1024
QK_PREP_TM = 512
S5_GLU_TM = 1024
VMEM_LIMIT = 56 * 1024 * 1024
FINAL_VMEM_LIMIT = 58 * 1024 * 1024


def _cparams(n_grid):
    return pltpu.CompilerParams(dimension_semantics=("arbitrary",) * n_grid,
                                vmem_limit_bytes=VMEM_LIMIT)


def _sigmoid(x):
    return 1.0 / (1.0 + jnp.exp(-x))


def _gate_sigmoid(x):
    return 0.5 * jnp.tanh(0.5 * x) + 0.5


def _silu(x):
    return x * _gate_sigmoid(x)


def _dot(a, b):
    return jnp.dot(a, b, preferred_element_type=F32)


def _dot_nt(a, b):
    return lax.dot_general(a, b, (((1,), (1,)), ((), ())), preferred_element_type=F32)


def _dot_tn(a, b):
    return lax.dot_general(a, b, (((0,), (0,)), ((), ())), preferred_element_type=F32)


def _split2(x):
    hi = x.astype(BF16)
    lo = (x - hi.astype(F32)).astype(BF16)
    return hi, lo


def _dot3(a, b):
    ah, al = _split2(a)
    bh, bl = _split2(b)
    return _dot(ah, bh) + (_dot(ah, bl) + _dot(al, bh))


def _mod_kernel(c_ref, w_ref, b_ref, o_ref):
    a = _silu(c_ref[...])
    o_ref[0] = _dot3(a, w_ref[0]) + b_ref[0]


def _modulation(c_all, w_mod, b_mod):
    depth, d, n = w_mod.shape
    rows = c_all.shape[0]
    tn = MOD_TN
    return pl.pallas_call(
        _mod_kernel,
        grid=(depth, n // tn),
        in_specs=[pl.BlockSpec((rows, d), lambda l, j: (0, 0)),
                  pl.BlockSpec((1, d, tn), lambda l, j: (l, 0, j)),
                  pl.BlockSpec((1, 1, tn), lambda l, j: (l, 0, j))],
        out_specs=pl.BlockSpec((1, rows, tn), lambda l, j: (l, 0, j)),
        out_shape=jax.ShapeDtypeStruct((depth, rows, n), F32),
        compiler_params=_cparams(2),
        name="modulation",
    )(c_all, w_mod, b_mod.reshape(depth, 1, n))


def _inproj_kernel(x_ref, sh_ref, sc_ref, g_ref, w_ref, o_ref, ub_ref, h_scr):
    first = pl.program_id(2) == 0
    tm = h_scr.shape[0]

    @pl.when(first)
    def _():
        for r0 in range(0, tm, INPROJ_ROWS):
            rs = slice(r0, r0 + INPROJ_ROWS)
            x = x_ref[0, rs, :]
            ms = jnp.mean(x * x, axis=-1, keepdims=True)
            y = x * lax.rsqrt(ms + NORM_EPS) * g_ref[...]
            h = (y * (1.0 + sc_ref[0]) + sh_ref[0]).astype(BF16)
            h_scr[rs, :] = h
            res = _dot(h, w_ref[0])
            o_ref[0, rs, :] = res.astype(BF16)
            ub_ref[0, rs, :] = res[:, COL_UB:COL_UB + S5_WIDTH]

    @pl.when(jnp.logical_not(first))
    def _():
        o_ref[0] = _dot(h_scr[...], w_ref[0]).astype(BF16)


def _inproj(x, shift, scale, g, w_bf, layer, tm, tn):
    b, l, d = x.shape
    n = w_bf.shape[2]
    assert COL_UB + S5_WIDTH <= tn
    per_batch = shift.shape[0] > 1
    mod_map = (lambda bi, i, j: (bi, 0, 0)) if per_batch else (lambda bi, i, j: (0, 0, 0))
    return pl.pallas_call(
        _inproj_kernel,
        grid=(b, l // tm, n // tn),
        in_specs=[pl.BlockSpec((1, tm, d), lambda bi, i, j: (bi, i, 0)),
                  pl.BlockSpec((1, 1, d), mod_map),
                  pl.BlockSpec((1, 1, d), mod_map),
                  pl.BlockSpec((1, d), lambda bi, i, j: (0, 0)),
                  pl.BlockSpec((1, d, tn), lambda bi, i, j: (layer, 0, j))],
        out_specs=[pl.BlockSpec((1, tm, tn), lambda bi, i, j: (bi, i, j)),
                   pl.BlockSpec((1, tm, S5_WIDTH), lambda bi, i, j: (bi, i, 0))],
        out_shape=[jax.ShapeDtypeStruct((b, l, n), BF16),
                   jax.ShapeDtypeStruct((b, l, S5_WIDTH), F32)],
        scratch_shapes=[pltpu.VMEM((tm, d), BF16)],
        compiler_params=_cparams(3),
        name="inproj",
    )(x, shift, scale, g, w_bf)


def _dft_tables(l):
    r = FFT_RADIX
    l2 = l // r
    scale = 1.0 / math.sqrt(l * FOURIER_GROUP_DIM)
    k1 = np.arange(r, dtype=np.int64)[:, None]
    t2 = np.arange(l2, dtype=np.int64)[None, :]
    ang = 2.0 * np.pi * ((k1 * t2) % l) / l
    lanes = (l, FOURIER_GROUP_DIM)
    tw_re = jnp.broadcast_to(jnp.asarray(np.cos(ang).reshape(l, 1), F32), lanes)
    tw_im = jnp.broadcast_to(jnp.asarray(-np.sin(ang).reshape(l, 1), F32), lanes)
    k2 = np.arange(l2, dtype=np.int64)
    a3 = 2.0 * np.pi * ((k2[:, None] * k2[None, :]) % l2) / l2
    d3 = jnp.asarray(np.concatenate([np.cos(a3), np.sin(a3)], axis=1) * scale, F32).astype(BF16)
    c = np.arange(FOURIER_GROUP_DIM, dtype=np.int64)
    ac = 2.0 * np.pi * ((c[:, None] * c[None, :]) % FOURIER_GROUP_DIM) / FOURIER_GROUP_DIM
    cs = jnp.asarray(np.concatenate([np.cos(ac), np.sin(ac)], axis=1), F32).astype(BF16)
    return (tw_re, tw_im, d3), cs


def _cadd(a, b):
    return a[0] + b[0], a[1] + b[1]


def _csub(a, b):
    return a[0] - b[0], a[1] - b[1]


def _cmul_neg_i(a):
    return a[1], -a[0]


def _dft4(y0, y1, y2, y3):
    c0, c1 = _cadd(y0, y2), _cadd(y1, y3)
    d0, d1 = _csub(y0, y2), _cmul_neg_i(_csub(y1, y3))
    return _cadd(c0, c1), _cadd(d0, d1), _csub(c0, c1), _csub(d0, d1)


def _dft8(x):
    h = math.sqrt(0.5)
    a = [_cadd(x[n], x[n + 4]) for n in range(4)]
    d = [_csub(x[n], x[n + 4]) for n in range(4)]
    b = [d[0],
         ((d[1][0] + d[1][1]) * h, (d[1][1] - d[1][0]) * h),
         _cmul_neg_i(d[2]),
         ((d[3][1] - d[3][0]) * h, -(d[3][0] + d[3][1]) * h)]
    even, odd = _dft4(*a), _dft4(*b)
    return [even[0], odd[0], even[1], odd[1], even[2], odd[2], even[3], odd[3]]


def _fourier_kernel(u_ref, g_ref, cs_ref, twr_ref, twi_ref, d3_ref, w_ref, o_ref, b_scr, f_scr, *, l):
    r = FFT_RADIX
    l2 = l // r
    gd = FOURIER_GROUP_DIM
    t = _dot(u_ref[0].astype(BF16), cs_ref[...])
    blocks = [(t[j * l2:(j + 1) * l2, :gd], -t[j * l2:(j + 1) * l2, gd:]) for j in range(r)]
    for k1, (ar, ai) in enumerate(_dft8(blocks)):
        twr = twr_ref[k1 * l2:(k1 + 1) * l2, :]
        twi = twi_ref[k1 * l2:(k1 + 1) * l2, :]
        b_scr[k1, 0:l2, :] = (ar * twr - ai * twi).astype(BF16)
        b_scr[k1, l2:2 * l2, :] = (ar * twi + ai * twr).astype(BF16)
    for k1 in range(r):
        f_scr[pl.ds(k1, l2, stride=r), :] = _dot(d3_ref[...], b_scr[k1])
    y = _dot(f_scr[...].astype(BF16), w_ref[0].astype(BF16))
    o_ref[0] = (y * _silu(g_ref[0].astype(F32))).astype(BF16)


def _fourier_branch(proj, fourier_w, dft, cs):
    b, l, _ = proj.shape
    gd = FOURIER_GROUP_DIM
    l2 = l // FFT_RADIX
    tw_re, tw_im, d3 = dft
    return pl.pallas_call(
        functools.partial(_fourier_kernel, l=l),
        grid=(b, FOURIER_GROUPS),
        in_specs=[pl.BlockSpec((1, l, gd), lambda bi, g: (bi, 0, COL_UA // gd + g)),
                  pl.BlockSpec((1, l, gd), lambda bi, g: (bi, 0, COL_GA // gd + g)),
                  pl.BlockSpec((gd, 2 * gd), lambda bi, g: (0, 0)),
                  pl.BlockSpec((l, gd), lambda bi, g: (0, 0)),
                  pl.BlockSpec((l, gd), lambda bi, g: (0, 0)),
                  pl.BlockSpec((l2, 2 * l2), lambda bi, g: (0, 0)),
                  pl.BlockSpec((1, gd, gd), lambda bi, g: (g, 0, 0))],
        out_specs=pl.BlockSpec((1, l, gd), lambda bi, g: (bi, 0, g)),
        out_shape=jax.ShapeDtypeStruct((b, l, FOURIER_WIDTH), BF16),
        scratch_shapes=[pltpu.VMEM((FFT_RADIX, 2 * l2, gd), BF16), pltpu.VMEM((l, gd), F32)],
        compiler_params=_cparams(2),
        name="fourier",
    )(proj, proj, cs, tw_re, tw_im, d3, fourier_w)


def _s5_matrices(lam_re, lam_im, log_step, b_re, b_im, c_re, c_im, dskip):
    t = S5_CHUNK
    g, n, p = S5_GROUPS, S5_STATE, S5_GROUP_DIM
    hp = lax.Precision.HIGHEST
    lam_re = lam_re.astype(F32)
    lam_im = lam_im.astype(F32)
    step = jnp.exp(log_step.astype(F32))[..., None]
    rho = lam_re * step
    th = lam_im * step
    mag = jnp.exp(rho)
    lb_re = mag * jnp.cos(th)
    lb_im = mag * jnp.sin(th)
    nr = lb_re - 1.0
    den = lam_re * lam_re + lam_im * lam_im
    fr = (nr * lam_re + lb_im * lam_im) / den
    fi = (lb_im * lam_re - nr * lam_im) / den
    b_re = b_re.astype(F32)
    b_im = b_im.astype(F32)
    bb_re = fr[..., None] * b_re - fi[..., None] * b_im
    bb_im = fr[..., None] * b_im + fi[..., None] * b_re
    tau = jnp.arange(t + 1, dtype=F32)
    pw_mag = jnp.exp(rho[..., None] * tau)
    pw_re = pw_mag * jnp.cos(th[..., None] * tau)
    pw_im = pw_mag * jnp.sin(th[..., None] * tau)
    c_re = c_re.astype(F32)
    c_im = c_im.astype(F32)
    ca_re = c_re[..., None] * pw_re[:, :, None] - c_im[..., None] * pw_im[:, :, None]
    ca_im = c_re[..., None] * pw_im[:, :, None] + c_im[..., None] * pw_re[:, :, None]
    kern = jnp.einsum('dgpnt,dgnq->dgtpq', jnp.concatenate([ca_re[..., :t], -ca_im[..., :t]], axis=3),
                      jnp.concatenate([bb_re, bb_im], axis=2), precision=hp)
    sig = np.arange(t)[:, None]
    ta = np.arange(t)[None, :]
    idx_f = np.clip(ta - sig, 0, t - 1)
    idx_b = np.clip(sig - ta, 0, t - 1)
    m_f = jnp.asarray((ta >= sig).astype(np.float32))[None, :, :, None, None]
    m_b = jnp.asarray((sig >= ta).astype(np.float32))[None, :, :, None, None]
    tf = kern[0][:, idx_f] * m_f
    tb = kern[1][:, idx_b] * m_b
    eye_t = jnp.asarray(np.eye(t, dtype=np.float32))[None, :, :, None, None]
    dmat = dskip.astype(F32)[:, None, None, :, None] * jnp.asarray(np.eye(p, dtype=np.float32))[None, None, None]
    tt = tf + tb + eye_t * dmat
    nb, gb = S5_NBLK, S5_GB
    one = np.ones
    eye = lambda k: np.eye(k, dtype=np.float32)
    exp_t = jnp.asarray(np.einsum('tu,pq,b->tpubq', eye(t), eye(p), one(gb)).reshape(t * p, S5_K))
    exp_n = jnp.asarray(np.einsum('nm,b->nbm', eye(n), one(gb)).reshape(n, S5_PART))
    mask_t = jnp.asarray(np.einsum('ab,s,q,u,p->saqubp', eye(gb), one(t), one(p), one(t), one(p))
                         .reshape(S5_K, S5_K))
    mask_ba = jnp.asarray(np.einsum('ab,s,q,n->saqbn', eye(gb), one(t), one(p), one(n)).reshape(S5_K, S5_PART))
    t_small = tt.reshape(nb, gb, t, t, p, p).transpose(0, 2, 1, 5, 3, 4).reshape(nb, S5_K, t * p)
    tmat = jnp.matmul(t_small, exp_t, precision=hp) * mask_t

    def ba(d, powers):
        pr = pw_re[d][:, :, powers]
        pi = pw_im[d][:, :, powers]
        re = pr[..., None] * bb_re[d][:, :, None, :] - pi[..., None] * bb_im[d][:, :, None, :]
        im = pr[..., None] * bb_im[d][:, :, None, :] + pi[..., None] * bb_re[d][:, :, None, :]
        blk = lambda m: jnp.matmul(m.reshape(nb, gb, n, t, p).transpose(0, 3, 1, 4, 2).reshape(nb, S5_K, n),
                                   exp_n, precision=hp) * mask_ba
        return blk(re), blk(im)

    baf_re, baf_im = ba(0, np.arange(t - 1, -1, -1))
    bab_re, bab_im = ba(1, np.arange(t))
    bamat = jnp.concatenate([baf_re, bab_re, baf_im, bab_im], axis=-1)

    def ca(d, powers):
        blk = lambda m: jnp.matmul(exp_n.T, m[..., powers].reshape(nb, gb, p, n, t).transpose(0, 3, 4, 1, 2)
                                   .reshape(nb, n, S5_K), precision=hp) * mask_ba.T
        return blk(ca_re[d]), blk(ca_im[d])

    caf_re, caf_im = ca(0, np.arange(1, t + 1))
    cab_re, cab_im = ca(1, np.arange(t, 0, -1))
    camat = jnp.concatenate([caf_re, cab_re, -caf_im, -cab_im], axis=1)
    part = lambda m: m[..., t].reshape(nb, S5_PART)
    a_pow = jnp.stack([jnp.concatenate([part(pw_re[0]), part(pw_re[1])], axis=-1),
                       jnp.concatenate([part(pw_im[0]), part(pw_im[1])], axis=-1)], axis=1)
    return tmat.astype(BF16), bamat.astype(BF16), camat.astype(BF16), a_pow


def _s5_kernel(u_ref, t_ref, ba_ref, ca_ref, a_ref, h0_ref, y_ref, hf_ref, ub_scr, s_scr, hin_scr, y_scr,
               *, nc, bt):
    t, w, hp = S5_CHUNK, S5_GB * S5_GROUP_DIM, S5_PART
    for bi in range(bt):
        for tau in range(t):
            ub_scr[tau, pl.ds(bi, nc, stride=bt), :] = u_ref[bi, pl.ds(tau, nc, stride=t), :]
    ub = jnp.concatenate([ub_scr[tau] for tau in range(t)], axis=1).astype(BF16)
    s_scr[...] = _dot(ub, ba_ref[0, 0])
    ar_f, ar_b = a_ref[0, 0, 0:1, 0:hp], a_ref[0, 0, 0:1, hp:2 * hp]
    ai_f, ai_b = a_ref[0, 0, 1:2, 0:hp], a_ref[0, 0, 1:2, hp:2 * hp]

    def step(i, carry):
        re_f, im_f, re_b, im_b = carry
        rf = pl.ds(pl.multiple_of(i * bt, bt), bt)
        rb = pl.ds(pl.multiple_of((nc - 1 - i) * bt, bt), bt)
        hin_scr[rf, 0:hp] = re_f
        hin_scr[rf, 2 * hp:3 * hp] = im_f
        hin_scr[rb, hp:2 * hp] = re_b
        hin_scr[rb, 3 * hp:4 * hp] = im_b
        n_re_f = ar_f * re_f - ai_f * im_f + s_scr[rf, 0:hp]
        n_im_f = ar_f * im_f + ai_f * re_f + s_scr[rf, 2 * hp:3 * hp]
        n_re_b = ar_b * re_b - ai_b * im_b + s_scr[rb, hp:2 * hp]
        n_im_b = ar_b * im_b + ai_b * re_b + s_scr[rb, 3 * hp:4 * hp]
        return n_re_f, n_im_f, n_re_b, n_im_b

    h0 = h0_ref[...].reshape(bt, 4 * hp)
    init = (h0[:, 0:hp], h0[:, 2 * hp:3 * hp], h0[:, hp:2 * hp], h0[:, 3 * hp:4 * hp])
    fin = lax.fori_loop(0, nc, step, init, unroll=8)
    hf_ref[...] = jnp.concatenate([fin[0], fin[2], fin[1], fin[3]], axis=1).reshape(hf_ref.shape)
    y = _dot(ub, t_ref[0, 0]) + _dot(hin_scr[...].astype(BF16), ca_ref[0, 0])
    for tau in range(t):
        y_scr[tau] = y[:, tau * w:(tau + 1) * w]
    for bi in range(bt):
        for tau in range(t):
            y_ref[bi, pl.ds(tau, nc, stride=t), :] = y_scr[tau, pl.ds(bi, nc, stride=bt), :]


def _s5_core(ub, tmat, bamat, camat, a_pow, h0, layer):
    b, l, _ = ub.shape
    nc = l // S5_CHUNK
    w = S5_GB * S5_GROUP_DIM
    bt = b if nc * b <= S5_MAX_ROWS else 1
    if bt == 1:
        h0 = h0.reshape(S5_NBLK, b, 1, 4 * S5_PART)
        st_spec = pl.BlockSpec((1, 1, 1, 4 * S5_PART), lambda j, bi: (j, bi, 0, 0))
    else:
        st_spec = pl.BlockSpec((1, bt, 4 * S5_PART), lambda j, bi: (j, bi, 0))
    rows = nc * bt
    y, hf = pl.pallas_call(
        functools.partial(_s5_kernel, nc=nc, bt=bt),
        grid=(S5_NBLK, b // bt),
        in_specs=[pl.BlockSpec((bt, l, w), lambda j, bi: (bi, 0, j)),
                  pl.BlockSpec((1, 1, S5_K, S5_K), lambda j, bi: (layer, j, 0, 0)),
                  pl.BlockSpec((1, 1, S5_K, 4 * S5_PART), lambda j, bi: (layer, j, 0, 0)),
                  pl.BlockSpec((1, 1, 4 * S5_PART, S5_K), lambda j, bi: (layer, j, 0, 0)),
                  pl.BlockSpec((1, 1, 2, 2 * S5_PART), lambda j, bi: (layer, j, 0, 0)),
                  st_spec],
        out_specs=[pl.BlockSpec((bt, l, w), lambda j, bi: (bi, 0, j)), st_spec],
        out_shape=[jax.ShapeDtypeStruct((b, l, S5_WIDTH), F32),
                   jax.ShapeDtypeStruct(h0.shape, F32)],
        scratch_shapes=[pltpu.VMEM((S5_CHUNK, rows, w), F32), pltpu.VMEM((rows, 4 * S5_PART), F32),
                        pltpu.VMEM((rows, 4 * S5_PART), F32), pltpu.VMEM((S5_CHUNK, rows, w), F32)],
        compiler_params=_cparams(2),
        name="s5_core",
    )(ub, tmat, bamat, camat, a_pow, h0)
    return y, hf.reshape(S5_NBLK, b, 4 * S5_PART)


def _s5_glu_kernel(y_ref, g_ref, w_ref, b_ref, o_ref):
    y = y_ref[0]
    y = 0.5 * y * (1.0 + jnp.tanh(math.sqrt(2.0 / math.pi) * (y + 0.044715 * (y * y * y))))
    z = _dot(y.astype(BF16), w_ref[...].astype(BF16)) + b_ref[...]
    o_ref[0] = (y * _sigmoid(z) * _silu(g_ref[0].astype(F32))).astype(BF16)


def _s5_glu(y, proj, glu_w, glu_b, tm):
    b, l, w = y.shape
    return pl.pallas_call(
        _s5_glu_kernel,
        grid=(b, l // tm),
        in_specs=[pl.BlockSpec((1, tm, w), lambda bi, i: (bi, i, 0)),
                  pl.BlockSpec((1, tm, w), lambda bi, i: (bi, i, COL_GB // w)),
                  pl.BlockSpec((w, w), lambda bi, i: (0, 0)),
                  pl.BlockSpec((1, w), lambda bi, i: (0, 0))],
        out_specs=pl.BlockSpec((1, tm, w), lambda bi, i: (bi, i, 0)),
        out_shape=jax.ShapeDtypeStruct((b, l, w), BF16),
        compiler_params=_cparams(2),
        name="s5_glu",
    )(y, proj, glu_w, glu_b.reshape(1, w))


def _s5_branch(proj, ub, mats, layer, glu_w, glu_b, h0_re, h0_im):
    b, l, _ = proj.shape
    g, n = S5_GROUPS, S5_STATE
    if h0_re is None:
        h0 = jnp.zeros((S5_NBLK, b, 4 * S5_PART), F32)
    else:
        blk = lambda h: h.astype(F32).reshape(b, S5_NBLK, S5_PART).transpose(1, 0, 2)
        h0 = jnp.concatenate([blk(h0_re[:, 0]), blk(h0_re[:, 1]), blk(h0_im[:, 0]), blk(h0_im[:, 1])], axis=-1)
    y, hf = _s5_core(ub, *mats, h0, layer)
    out = _s5_glu(y, proj, glu_w, glu_b, min(l, S5_GLU_TM))
    part = lambda i: hf[..., i * S5_PART:(i + 1) * S5_PART].transpose(1, 0, 2).reshape(b, g, n)
    fin_re = jnp.stack([part(0), part(1)], axis=1)
    fin_im = jnp.stack([part(2), part(3)], axis=1)
    return out, fin_re, fin_im


def _rope_tables(l):
    rows = l // GRID_W
    row = jnp.broadcast_to(jnp.arange(rows, dtype=F32)[:, None], (rows, GRID_W)).reshape(-1)
    col = jnp.broadcast_to(jnp.arange(GRID_W, dtype=F32)[None, :], (rows, GRID_W)).reshape(-1)
    half = HEAD_DIM // 2
    inv = ROPE_THETA ** (-jnp.arange(0, half, 2, dtype=F32) / half)
    ar = row[:, None] * inv
    ac = col[:, None] * inv
    cos = jnp.concatenate([jnp.cos(ar), jnp.cos(ar), jnp.cos(ac), jnp.cos(ac)], axis=1)
    sin = jnp.concatenate([-jnp.sin(ar), jnp.sin(ar), -jnp.sin(ac), jnp.sin(ac)], axis=1)
    return cos, sin


def _qk_kernel(*refs, rope, want_kf):
    q_ref, k_ref, v_ref, qn_ref, kn_ref = refs[:5]
    pos = 5
    if rope:
        cos_ref, sin_ref = refs[pos:pos + 2]
        pos += 2
    qo_ref, ko_ref, vo_ref = refs[pos:pos + 3]
    pos += 3
    if want_kf:
        kf_ref = refs[pos]
    hd = HEAD_DIM
    tm = q_ref.shape[1]
    quarter = hd // 4
    if rope:
        cos = cos_ref[...]
        sin = sin_ref[...]
        first = (lax.broadcasted_iota(jnp.int32, (tm, hd), 1) % (2 * quarter)) < quarter

    def norm(x, gain):
        ms = jnp.mean(x * x, axis=-1, keepdims=True)
        return x * lax.rsqrt(ms + NORM_EPS) * gain

    def rot(x):
        if not rope:
            return x
        nxt = pltpu.roll(x, hd - quarter, 1)
        prv = pltpu.roll(x, quarter, 1)
        return x * cos + jnp.where(first, nxt, prv) * sin

    qscale = (hd ** -0.5) * LOG2E
    for h in range(N_HEADS):
        x = norm(q_ref[0, :, h * hd:(h + 1) * hd].astype(F32), qn_ref[...])
        qo_ref[0, :, h * hd:(h + 1) * hd] = (rot(x) * qscale).astype(BF16)
    for h in range(N_KV_HEADS):
        x = norm(k_ref[0, :, h * hd:(h + 1) * hd].astype(F32), kn_ref[...])
        if want_kf:
            kf_ref[0, :, h * hd:(h + 1) * hd] = x
        ko_ref[0, :, h * hd:(h + 1) * hd] = rot(x).astype(BF16)
        vo_ref[0, h] = v_ref[0, :, h * hd:(h + 1) * hd].astype(F32).T.astype(BF16)


def _qk_prep(proj, q_norm, k_norm, rope_tabs, want_kf, tm):
    b, l, _ = proj.shape
    rope = rope_tabs is not None
    in_specs = [pl.BlockSpec((1, tm, ATTN_WIDTH), lambda bi, i: (bi, i, COL_QC // ATTN_WIDTH)),
                pl.BlockSpec((1, tm, KV_WIDTH), lambda bi, i: (bi, i, COL_KC // KV_WIDTH)),
                pl.BlockSpec((1, tm, KV_WIDTH), lambda bi, i: (bi, i, COL_VC // KV_WIDTH)),
                pl.BlockSpec((1, HEAD_DIM), lambda bi, i: (0, 0)),
                pl.BlockSpec((1, HEAD_DIM), lambda bi, i: (0, 0))]
    args = [proj, proj, proj, q_norm.reshape(1, HEAD_DIM), k_norm.reshape(1, HEAD_DIM)]
    if rope:
        in_specs += [pl.BlockSpec((tm, HEAD_DIM), lambda bi, i: (i, 0))] * 2
        args += list(rope_tabs)
    out_specs = [pl.BlockSpec((1, tm, ATTN_WIDTH), lambda bi, i: (bi, i, 0)),
                 pl.BlockSpec((1, tm, KV_WIDTH), lambda bi, i: (bi, i, 0)),
                 pl.BlockSpec((1, N_KV_HEADS, HEAD_DIM, tm), lambda bi, i: (bi, 0, 0, i))]
    out_shape = [jax.ShapeDtypeStruct((b, l, ATTN_WIDTH), BF16),
                 jax.ShapeDtypeStruct((b, l, KV_WIDTH), BF16),
                 jax.ShapeDtypeStruct((b, N_KV_HEADS, HEAD_DIM, l), BF16)]
    if want_kf:
        out_specs.append(pl.BlockSpec((1, tm, KV_WIDTH), lambda bi, i: (bi, i, 0)))
        out_shape.append(jax.ShapeDtypeStruct((b, l, KV_WIDTH), F32))
    return pl.pallas_call(
        functools.partial(_qk_kernel, rope=rope, want_kf=want_kf),
        grid=(b, l // tm),
        in_specs=in_specs, out_specs=out_specs, out_shape=out_shape,
        compiler_params=_cparams(2),
        name="qk_prep",
    )(*args)


def _attn_kernel(*refs, has_ctx):
    s_scr = refs[-1]
    q_ref, k_ref, v_ref = refs[:3]
    if has_ctx:
        ck_ref, cv_ref, g_ref, o_ref = refs[3:7]
        ck = ck_ref[0, 0].astype(BF16)
        cvt = cv_ref[0, 0, 0].astype(BF16)
        past = ck.shape[0]
    else:
        g_ref, o_ref = refs[3:5]
    hd = HEAD_DIM
    tq = q_ref.shape[1]
    lk = k_ref.shape[1]
    cr = min(lk, ATTN_KEY_CHUNK)
    chunks = [(c * cr, cr, (lambda c=c: k_ref[0, c * cr:(c + 1) * cr, :]),
               (lambda c=c: v_ref[0, 0, :, c * cr:(c + 1) * cr])) for c in range(lk // cr)]
    if has_ctx:
        chunks.append((lk, past, lambda: ck, lambda: cvt))
    qs = [q_ref[0, :, g * hd:(g + 1) * hd] for g in range(Q_PER_KV)]
    gate = _silu(g_ref[0].astype(F32))

    def score(g, chunk, m):
        off, rows, keys, _ = chunk
        s = _dot_nt(keys(), qs[g])
        s_scr[g % 2, off:off + rows, :] = s
        return jnp.maximum(m, jnp.max(s, axis=0, keepdims=True))

    def accumulate(g, chunk, m, den, ot):
        off, rows, _, values_t = chunk
        p = jnp.exp2(s_scr[g % 2, off:off + rows, :] - m)
        return den + jnp.sum(p, axis=0, keepdims=True), ot + _dot(values_t(), p.astype(BF16))

    neg = jnp.full((1, tq), -jnp.inf, F32)
    m_cur = neg
    for chunk in chunks:
        m_cur = score(0, chunk, m_cur)
    for g in range(Q_PER_KV):
        m_next = neg
        den = jnp.zeros((1, tq), F32)
        ot = jnp.zeros((hd, tq), F32)
        for chunk in chunks:
            if g + 1 < Q_PER_KV:
                m_next = score(g + 1, chunk, m_next)
            den, ot = accumulate(g, chunk, m_cur, den, ot)
        m_cur = m_next
        o = (ot / den).T
        o_ref[0, :, g * hd:(g + 1) * hd] = (o * gate[:, g * hd:(g + 1) * hd]).astype(BF16)


def _attention(qs, kr, vt, proj, ctx_k, ctx_v, layer, tq):
    b, l, _ = qs.shape
    gw = Q_PER_KV * HEAD_DIM
    has_ctx = ctx_k is not None
    in_specs = [pl.BlockSpec((1, tq, gw), lambda bi, h, i: (bi, i, h)),
                pl.BlockSpec((1, l, HEAD_DIM), lambda bi, h, i: (bi, 0, h)),
                pl.BlockSpec((1, 1, HEAD_DIM, l), lambda bi, h, i: (bi, h, 0, 0))]
    args = [qs, kr, vt]
    if has_ctx:
        past = ctx_k.shape[2]
        in_specs += [pl.BlockSpec((1, 1, past, HEAD_DIM), lambda bi, h, i: (bi, layer, 0, h)),
                     pl.BlockSpec((1, 1, 1, HEAD_DIM, past), lambda bi, h, i: (bi, layer, h, 0, 0))]
        args += [ctx_k, ctx_v]
    in_specs.append(pl.BlockSpec((1, tq, gw), lambda bi, h, i: (bi, i, COL_GC // gw + h)))
    args.append(proj)
    return pl.pallas_call(
        functools.partial(_attn_kernel, has_ctx=has_ctx),
        grid=(b, N_KV_HEADS, l // tq),
        in_specs=in_specs,
        out_specs=pl.BlockSpec((1, tq, gw), lambda bi, h, i: (bi, i, h)),
        out_shape=jax.ShapeDtypeStruct((b, l, ATTN_WIDTH), BF16),
        scratch_shapes=[pltpu.VMEM((2, l + (ctx_k.shape[2] if has_ctx else 0), tq), F32)],
        compiler_params=_cparams(3),
        name="attention",
    )(*args)


HGRN_COARSE = (32, 16, 8, 4)
HGRN_FINE = (2,)
HGRN_LEVELS = HGRN_COARSE + HGRN_FINE + (1,)
HGRN_CHUNKS_PER_TRIP = 4


def _hgrn_tables():
    c = HGRN_CHUNK
    t = np.arange(c)
    w01, sm = [], []
    for d in range(2):
        cum = (t[None, :] <= t[:, None]) if d == 0 else (t[None, :] >= t[:, None])
        cum = cum.astype(np.float32)
        w01.append(np.concatenate([cum] + [cum[_hgrn_ref_row(t, hs, d)] for hs in HGRN_FINE], axis=0))
        sd = []
        for hs in HGRN_LEVELS:
            is_q = _hgrn_is_query(t, hs, d)
            same = (t[:, None] // (2 * hs)) == (t[None, :] // (2 * hs))
            sd.append((same & is_q[:, None] & ~is_q[None, :]).astype(np.float32))
        sd.append(np.eye(c, dtype=np.float32))
        sm.append(np.stack(sd))
    return jnp.asarray(np.stack(w01), BF16), jnp.asarray(np.stack(sm), F32)


def _hgrn_ref_row(t, hs, d):
    base = t & ~(2 * hs - 1)
    return base + hs - 1 if d == 0 else base + hs


def _hgrn_is_query(t, hs, d):
    return ((t & hs) != 0) if d == 0 else ((t & hs) == 0)


def _hgrn_cums(z, lbv, w01):
    f = lbv + (1.0 - lbv) * _sigmoid(z)
    fm = jnp.maximum(f, F_FLOOR)
    hi, lo = _split2(jnp.log2(fm))
    return fm, 1.0 - f, _dot(w01, hi) + _dot(w01, lo)


def _hgrn_products(q, k, fm, allc, rows, d):
    c = HGRN_CHUNK
    sub = 8
    cum = allc[0:c]
    refs = {}
    for hs in HGRN_COARSE:
        pieces = []
        for j in range(c // sub):
            rr = int(_hgrn_ref_row(np.int64(j * sub), hs, d))
            pieces.append(jnp.broadcast_to(cum[rr:rr + 1, :], (sub, HGRN_DK)))
        refs[hs] = jnp.concatenate(pieces, axis=0)
    for i, hs in enumerate(HGRN_FINE):
        refs[hs] = allc[(i + 1) * c:(i + 2) * c]
    lrow = c - 1 if d == 0 else 0
    last = cum[lrow:lrow + 1, :]
    raws = [_dot_nt(q.astype(BF16), k.astype(BF16))]
    for hs in HGRN_LEVELS:
        is_q = ((rows & hs) != 0) if d == 0 else ((rows & hs) == 0)
        if hs == 1:
            x = jnp.where(is_q, q * fm, k).astype(BF16)
        else:
            x = (jnp.where(is_q, q, k) * jnp.exp2((cum - refs[hs]) * jnp.where(is_q, 1.0, -1.0))).astype(BF16)
        raws.append(_dot_nt(x, x))
    qi = (q * jnp.exp2(cum)).astype(BF16)
    ks = (k * jnp.exp2(last - cum)).astype(BF16)
    return raws, qi, ks, jnp.exp2(last)


def _hgrn_kernel(*refs, nc, has_s0, want_state):
    q_ref, v_ref, zf_ref, zb_ref, g_ref, lb_ref, ng_ref, w01_ref, sm_ref = refs[:9]
    pos = 9
    if has_s0:
        s0_ref = refs[pos]
        pos += 1
    o_ref = refs[pos]
    pos += 1
    if want_state:
        sf_ref = refs[pos]
        pos += 1
    o_scr, st_scr = refs[pos:pos + 2]
    c = HGRN_CHUNK
    l = nc * c
    for d in range(2):
        if has_s0:
            st_scr[d] = s0_ref[0, d, 0]
        else:
            st_scr[d] = jnp.zeros((HGRN_DV, HGRN_DK), F32)
    rows = lax.broadcasted_iota(jnp.int32, (c, HGRN_DK), 0)

    per_trip = math.gcd(nc, HGRN_CHUNKS_PER_TRIP)
    ntrips = nc // per_trip
    nl = len(HGRN_LEVELS)

    def instances(i):
        insts = []
        for u in range(per_trip):
            for d in range(2):
                ci = per_trip * i + u if d == 0 else nc - 1 - (per_trip * i + u)
                r = ci * c
                insts.append((d, r if isinstance(r, int) else pl.multiple_of(r, c)))
        return insts

    def state_step(inst, local):
        (d, r), (scores, qi, ks, dec) = inst, local
        vb = v_ref[0, pl.ds(r, c), :].astype(BF16)
        st = st_scr[d]
        o_scr[d, pl.ds(r, c), :] = _dot(scores, vb) + _dot_nt(qi, st.astype(BF16))
        st_scr[d] = dec * st + _dot_tn(vb, ks)

    def trip(i, prev):
        insts = instances(i)
        prev_insts = instances(i - 1) if prev is not None else None
        gates = [_hgrn_cums((zf_ref, zb_ref)[d][0, pl.ds(r, c), :].astype(F32), lb_ref[d, 0], w01_ref[d])
                 for d, r in insts]
        out = []
        for idx, ((d, r), (fm, k, allc)) in enumerate(zip(insts, gates)):
            raws, qi, ks, dec = _hgrn_products(q_ref[0, pl.ds(r, c), :].astype(F32), k, fm, allc, rows, d)
            if prev is not None:
                state_step(prev_insts[idx], prev[idx])
            scores = sm_ref[d, nl] * raws[0]
            for li in range(nl):
                scores = scores + sm_ref[d, li] * raws[li + 1]
            out.append((scores.astype(BF16), qi, ks, dec))
        return tuple(out)

    last = lax.fori_loop(1, ntrips, trip, trip(0, None))
    for inst, local in zip(instances(ntrips - 1), last):
        state_step(inst, local)
    if want_state:
        for d in range(2):
            sf_ref[0, d, 0] = st_scr[d]

    fb = min(l, 256)

    def finish(i, carry):
        r = pl.multiple_of(i * fb, fb)
        tot = o_scr[0, pl.ds(r, fb), :] + o_scr[1, pl.ds(r, fb), :]
        ms = jnp.mean(tot * tot, axis=-1, keepdims=True)
        y = tot * lax.rsqrt(ms + NORM_EPS) * ng_ref[...]
        o_ref[0, pl.ds(r, fb), :] = (y * _silu(g_ref[0, pl.ds(r, fb), :].astype(F32))).astype(BF16)
        return carry

    lax.fori_loop(0, l // fb, finish, 0)


def _hgrn_branch(proj, lower_bound, norm_g, tables, s0_t, want_state):
    b, l, _ = proj.shape
    nc = l // HGRN_CHUNK
    w = HGRN_DK
    has_s0 = s0_t is not None
    w01, sm = tables

    def col(off):
        return lambda bi, h: (bi, 0, off // w + h)

    in_specs = [pl.BlockSpec((1, l, w), col(COL_QD)),
                pl.BlockSpec((1, l, w), col(COL_ID)),
                pl.BlockSpec((1, l, w), col(COL_ZF)),
                pl.BlockSpec((1, l, w), col(COL_ZB)),
                pl.BlockSpec((1, l, w), col(COL_GD)),
                pl.BlockSpec((2, 1, 1, w), lambda bi, h: (0, h, 0, 0)),
                pl.BlockSpec((1, w), lambda bi, h: (0, 0)),
                pl.BlockSpec(w01.shape, lambda bi, h: (0, 0, 0)),
                pl.BlockSpec(sm.shape, lambda bi, h: (0, 0, 0, 0))]
    args = [proj, proj, proj, proj, proj, lower_bound.reshape(2, HGRN_HEADS, 1, w), norm_g.reshape(1, w),
            w01, sm]
    st_spec = pl.BlockSpec((1, 2, 1, HGRN_DV, HGRN_DK), lambda bi, h: (bi, 0, h, 0, 0))
    if has_s0:
        in_specs.append(st_spec)
        args.append(s0_t)
    out_specs = [pl.BlockSpec((1, l, w), lambda bi, h: (bi, 0, h))]
    out_shape = [jax.ShapeDtypeStruct((b, l, HGRN_HEADS * HGRN_DV), BF16)]
    if want_state:
        out_specs.append(st_spec)
        out_shape.append(jax.ShapeDtypeStruct((b, 2, HGRN_HEADS, HGRN_DV, HGRN_DK), F32))
    res = pl.pallas_call(
        functools.partial(_hgrn_kernel, nc=nc, has_s0=has_s0, want_state=want_state),
        grid=(b, HGRN_HEADS),
        in_specs=in_specs, out_specs=out_specs, out_shape=out_shape,
        scratch_shapes=[pltpu.VMEM((2, l, HGRN_DV), F32), pltpu.VMEM((2, HGRN_DV, HGRN_DK), F32)],
        compiler_params=_cparams(2),
        name="hgrn",
    )(*args)
    return res if want_state else (res[0], None)


def _final_kernel(x_ref, gt_ref, ma_ref, mb_ref, mc_ref, md_ref, ya_ref, yb_ref, yc_ref, yd_ref,
                  wa_ref, wb_ref, wc_ref, wd_ref, wo_ref, gp_ref, o_ref, *, nct):
    ci = pl.program_id(2)
    tm = o_ref.shape[1]
    subs = [slice(r0, r0 + FINAL_ROWS) for r0 in range(0, tm, FINAL_ROWS)]
    m_refs = (ma_ref, mb_ref, mc_ref, md_ref)
    y_refs = (ya_ref, yb_ref, yc_ref, yd_ref)
    w_refs = (wa_ref, wb_ref, wc_ref, wd_ref)

    def branch_dots(rs):
        return [_dot(y_ref[0, rs, :], w_ref[0]) for y_ref, w_ref in zip(y_refs, w_refs)]

    def step(first, last):
        nxt = branch_dots(subs[0])
        for si, rs in enumerate(subs):
            branch = nxt
            if si + 1 < len(subs):
                nxt = branch_dots(subs[si + 1])
            mixed = None
            for m_ref, dot in zip(m_refs, branch):
                term = _gate_sigmoid(m_ref[0, rs, :].astype(F32)) * dot
                mixed = term if mixed is None else mixed + term
            contrib = _dot(mixed.astype(BF16), wo_ref[0])
            out = contrib if first else o_ref[0, rs, :] + contrib
            if last:
                ms = jnp.mean(out * out, axis=-1, keepdims=True)
                y = out * lax.rsqrt(ms + NORM_EPS) * gp_ref[...]
                o_ref[0, rs, :] = x_ref[0, rs, :] + gt_ref[0] * y
            else:
                o_ref[0, rs, :] = out

    if nct == 1:
        step(True, True)
    else:
        pl.when(ci == 0)(lambda: step(True, False))
        pl.when(jnp.logical_and(ci > 0, ci < nct - 1))(lambda: step(False, False))
        pl.when(ci == nct - 1)(lambda: step(False, True))


def _final(x, gate, proj, ys, w_projs, w_out, norm_post, layer, tm, tc):
    b, l, d = x.shape
    nct = d // tc
    per_batch = gate.shape[0] > 1
    gate_map = (lambda bi, i, c: (bi, 0, 0)) if per_batch else (lambda bi, i, c: (0, 0, 0))

    def mspec(j):
        return pl.BlockSpec((1, tm, tc), lambda bi, i, c: (bi, i, (COL_M + j * d) // tc + c))

    in_specs = [pl.BlockSpec((1, tm, d), lambda bi, i, c: (bi, i, 0)),
                pl.BlockSpec((1, 1, d), gate_map)]
    in_specs += [mspec(j) for j in range(N_BRANCHES)]
    in_specs += [pl.BlockSpec((1, tm, y.shape[-1]), lambda bi, i, c: (bi, i, 0)) for y in ys]
    in_specs += [pl.BlockSpec((1, w.shape[1], tc), lambda bi, i, c: (layer, 0, c)) for w in w_projs]
    in_specs += [pl.BlockSpec((1, tc, d), lambda bi, i, c: (layer, c, 0)),
                 pl.BlockSpec((1, d), lambda bi, i, c: (0, 0))]
    return pl.pallas_call(
        functools.partial(_final_kernel, nct=nct),
        grid=(b, l // tm, nct),
        in_specs=in_specs,
        out_specs=pl.BlockSpec((1, tm, d), lambda bi, i, c: (bi, i, 0)),
        out_shape=jax.ShapeDtypeStruct((b, l, d), F32),
        compiler_params=pltpu.CompilerParams(dimension_semantics=("arbitrary",) * 3,
                                             vmem_limit_bytes=FINAL_VMEM_LIMIT),
        name="merge_out",
    )(x, gate, proj, proj, proj, proj, *ys, *w_projs, w_out, norm_post.reshape(1, d))


def _trunk_layer(x, shift, scale, gate, p, ctx, layer, consts):
    b, l, d = x.shape
    shared_mod = shift.shape[0] == 1
    if shared_mod:
        proj, ub = _inproj(x.reshape(1, b * l, d), shift, scale, p['norm_pre'], p['w_in'], layer,
                           INPROJ_TM, INPROJ_TN)
        proj = proj.reshape(b, l, IN_COLS)
        ub = ub.reshape(b, l, S5_WIDTH)
    else:
        proj, ub = _inproj(x, shift, scale, p['norm_pre'], p['w_in'], layer, INPROJ_TM, INPROJ_TN)

    dl, cs = consts['dft'][l]
    y_a = _fourier_branch(proj, p['fourier_w'], dl, cs)

    if ctx is None:
        y_b, s5_re, s5_im = _s5_branch(proj, ub, p['s5_mats'], layer, p['s5_glu_w'], p['s5_glu_b'], None, None)
        qs, kr, vt, kf = _qk_prep(proj, p['q_norm'], p['k_norm'], None, True, min(l, QK_PREP_TM))
        y_c = _attention(qs, kr, vt, proj, None, None, layer, min(l, ATTN_TQ))
        y_d, hg = _hgrn_branch(proj, p['lower_bound'], p['hgrn_norm'], consts['hgrn'], None, True)
    else:
        cache_k, cache_v, st_re, st_im, s0_t = ctx
        y_b, _, _ = _s5_branch(proj, ub, p['s5_mats'], layer, p['s5_glu_w'], p['s5_glu_b'],
                               st_re[:, layer], st_im[:, layer])
        qs, kr, vt = _qk_prep(proj, p['q_norm'], p['k_norm'], consts['rope'], False, min(l, QK_PREP_TM))
        y_c = _attention(qs, kr, vt, proj, cache_k, cache_v, layer, min(l, ATTN_TQ))
        y_d, _ = _hgrn_branch(proj, p['lower_bound'], p['hgrn_norm'], consts['hgrn'], s0_t[:, layer], False)

    ys = (y_a, y_b, y_c, y_d)
    wps = (p['w_proj_a'], p['w_proj_b'], p['w_proj_c'], p['w_proj_d'])
    if shared_mod:
        n = b * l
        x_new = _final(x.reshape(1, n, d), gate, proj.reshape(1, n, IN_COLS),
                       tuple(y.reshape(1, n, y.shape[-1]) for y in ys), wps, p['w_out'], p['norm_post'], layer,
                       FINAL_TM, FINAL_TC)
        x_new = x_new.reshape(b, l, d)
    else:
        x_new = _final(x, gate, proj, ys, wps, p['w_out'], p['norm_post'], layer, FINAL_TM, FINAL_TC)

    if ctx is None:
        v_f = proj[:, :, COL_VC:COL_VC + KV_WIDTH].astype(F32)
        return x_new, (kf, v_f, s5_re, s5_im, hg)
    return x_new, None


def kernel(x_prompt, x_sample, c, cache_k, cache_v, state_s5_re, state_s5_im, state_hgrn, c_ctx,
           norm_pre, norm_post, w_mod, b_mod, w_in, fourier_w,
           s5_lambda_re, s5_lambda_im, s5_log_step, s5_b_re, s5_b_im, s5_c_re, s5_c_im,
           s5_d, s5_glu_w, s5_glu_b, q_norm, k_norm, hgrn_lb_logits, hgrn_norm,
           w_proj_a, w_proj_b, w_proj_c, w_proj_d, w_out):
    depth = w_in.shape[0]
    bp, lp, d = x_prompt.shape
    bs, ls, _ = x_sample.shape

    lb_w = jax.nn.softmax(hgrn_lb_logits.astype(F32), axis=0)
    lower_bounds = jnp.cumsum(lb_w, axis=0) - lb_w[0]

    rows = ((bs + 1 + 7) // 8) * 8
    c_all = jnp.zeros((rows, d), F32).at[:bs].set(c).at[bs].set(c_ctx)
    mod = _modulation(c_all, w_mod, b_mod)

    w_in_bf = w_in.astype(BF16)
    wpa, wpb, wpc, wpd = (w.astype(BF16) for w in (w_proj_a, w_proj_b, w_proj_c, w_proj_d))
    w_out_bf = w_out.astype(BF16)

    consts = {'dft': {lp: _dft_tables(lp), ls: _dft_tables(ls)},
              'rope': _rope_tables(ls),
              'hgrn': _hgrn_tables()}
    past = cache_k.shape[2]
    ctx_k = cache_k.reshape(bs, depth, past, KV_WIDTH)
    ctx_v = cache_v.transpose(0, 1, 3, 4, 2)
    s0_t = jnp.swapaxes(state_hgrn.astype(F32), -1, -2)

    s5_mats = jax.vmap(_s5_matrices)(s5_lambda_re, s5_lambda_im, s5_log_step, s5_b_re, s5_b_im, s5_c_re, s5_c_im,
                                     s5_d.reshape(depth, S5_GROUPS, S5_GROUP_DIM))

    y_p, y_s = x_prompt, x_sample
    ks, vs, s5r, s5i, hg = [], [], [], [], []
    for l in range(depth):
        p = {'norm_pre': norm_pre[l].reshape(1, d), 'norm_post': norm_post[l], 'w_in': w_in_bf,
             'fourier_w': fourier_w[l],
             's5_mats': s5_mats,
             's5_glu_w': s5_glu_w[l], 's5_glu_b': s5_glu_b[l],
             'q_norm': q_norm[l], 'k_norm': k_norm[l],
             'lower_bound': lower_bounds[l], 'hgrn_norm': hgrn_norm[l],
             'w_proj_a': wpa, 'w_proj_b': wpb, 'w_proj_c': wpc, 'w_proj_d': wpd,
             'w_out': w_out_bf}
        m_ctx = mod[l, bs].reshape(1, 1, 3 * d)
        sh, sc, gt = (m_ctx[..., i * d:(i + 1) * d] for i in range(3))
        y_p, (k_l, v_l, sr_l, si_l, hg_l) = _trunk_layer(y_p, sh, sc, gt, p, None, l, consts)
        ks.append(k_l.reshape(bp, lp, N_KV_HEADS, HEAD_DIM))
        vs.append(v_l.reshape(bp, lp, N_KV_HEADS, HEAD_DIM))
        s5r.append(sr_l)
        s5i.append(si_l)
        hg.append(jnp.swapaxes(hg_l, -1, -2))
        m_s = mod[l, :bs].reshape(bs, 1, 3 * d)
        sh, sc, gt = (m_s[..., i * d:(i + 1) * d] for i in range(3))
        ctx = (ctx_k, ctx_v, state_s5_re, state_s5_im, s0_t)
        y_s, _ = _trunk_layer(y_s, sh, sc, gt, p, ctx, l, consts)

    return (y_p, y_s, jnp.stack(ks, axis=1), jnp.stack(vs, axis=1),
            jnp.stack(s5r, axis=1), jnp.stack(s5i, axis=1), jnp.stack(hg, axis=1))
```

```python
import functools
import math

import jax
import jax.numpy as jnp
import numpy as np
from jax import lax
from jax.experimental import pallas as pl
from jax.experimental.pallas import tpu as pltpu

F32 = jnp.float32
BF16 = jnp.bfloat16

NORM_EPS = 1e-6
F_FLOOR = 1e-30
LOG2E = 1.4426950408889634

N_BRANCHES = 4
FOURIER_WIDTH = 512
FOURIER_GROUPS = 4
FOURIER_GROUP_DIM = 128
S5_WIDTH = 512
S5_GROUP_DIM = 16
S5_GROUPS = 32
S5_STATE = 64
S5_CHUNK = 8
S5_GB = 8
S5_NBLK = S5_GROUPS // S5_GB
S5_K = S5_CHUNK * S5_GB * S5_GROUP_DIM
S5_PART = S5_GB * S5_STATE
S5_MAX_ROWS = 512
N_HEADS = 8
N_KV_HEADS = 2
HEAD_DIM = 128
Q_PER_KV = 4
ATTN_WIDTH = 1024
KV_WIDTH = 256
ROPE_THETA = 10000.0
GRID_W = 64
HGRN_HEADS = 4
HGRN_DK = 128
HGRN_DV = 128
HGRN_CHUNK = 64

COL_UA, COL_GA, COL_UB, COL_GB = 0, 512, 1024, 1536
COL_QC, COL_KC, COL_VC, COL_GC = 2048, 3072, 3328, 3584
COL_QD, COL_ID, COL_ZF, COL_ZB, COL_GD = 4608, 5120, 5632, 6144, 6656
COL_M = 7168
IN_COLS = 15360

FFT_RADIX = 8
FINAL_ROWS = 256
ATTN_KEY_CHUNK = 1024
ATTN_TQ = 512
INPROJ_TM = 1024
INPROJ_TN = 1536
INPROJ_ROWS = 256
FINAL_TM = 1024
FINAL_TC = 256
MOD_TN = 1024
QK_PREP_TM = 512
S5_GLU_TM = 1024
VMEM_LIMIT = 56 * 1024 * 1024
FINAL_VMEM_LIMIT = 58 * 1024 * 1024


def _cparams(n_grid):
    return pltpu.CompilerParams(dimension_semantics=("arbitrary",) * n_grid,
                                vmem_limit_bytes=VMEM_LIMIT)


def _sigmoid(x):
    return 1.0 / (1.0 + jnp.exp(-x))


def _gate_sigmoid(x):
    return 0.5 * jnp.tanh(0.5 * x) + 0.5


def _silu(x):
    return x * _gate_sigmoid(x)


def _dot(a, b):
    return jnp.dot(a, b, preferred_element_type=F32)


def _dot_nt(a, b):
    return lax.dot_general(a, b, (((1,), (1,)), ((), ())), preferred_element_type=F32)


def _dot_tn(a, b):
    return lax.dot_general(a, b, (((0,), (0,)), ((), ())), preferred_element_type=F32)


def _split2(x):
    hi = x.astype(BF16)
    lo = (x - hi.astype(F32)).astype(BF16)
    return hi, lo


def _dot3(a, b):
    ah, al = _split2(a)
    bh, bl = _split2(b)
    return _dot(ah, bh) + (_dot(ah, bl) + _dot(al, bh))


def _mod_kernel(c_ref, w_ref, b_ref, o_ref):
    a = _silu(c_ref[...])
    o_ref[0] = _dot3(a, w_ref[0]) + b_ref[0]


def _modulation(c_all, w_mod, b_mod):
    depth, d, n = w_mod.shape
    rows = c_all.shape[0]
    tn = MOD_TN
    return pl.pallas_call(
        _mod_kernel,
        grid=(depth, n // tn),
        in_specs=[pl.BlockSpec((rows, d), lambda l, j: (0, 0)),
                  pl.BlockSpec((1, d, tn), lambda l, j: (l, 0, j)),
                  pl.BlockSpec((1, 1, tn), lambda l, j: (l, 0, j))],
        out_specs=pl.BlockSpec((1, rows, tn), lambda l, j: (l, 0, j)),
        out_shape=jax.ShapeDtypeStruct((depth, rows, n), F32),
        compiler_params=_cparams(2),
        name="modulation",
    )(c_all, w_mod, b_mod.reshape(depth, 1, n))


def _inproj_kernel(x_ref, sh_ref, sc_ref, g_ref, w_ref, o_ref, ub_ref, h_scr):
    first = pl.program_id(2) == 0
    tm = h_scr.shape[0]

    @pl.when(first)
    def _():
        for r0 in range(0, tm, INPROJ_ROWS):
            rs = slice(r0, r0 + INPROJ_ROWS)
            x = x_ref[0, rs, :]
            ms = jnp.mean(x * x, axis=-1, keepdims=True)
            y = x * lax.rsqrt(ms + NORM_EPS) * g_ref[...]
            h = (y * (1.0 + sc_ref[0]) + sh_ref[0]).astype(BF16)
            h_scr[rs, :] = h
            res = _dot(h, w_ref[0])
            o_ref[0, rs, :] = res.astype(BF16)
            ub_ref[0, rs, :] = res[:, COL_UB:COL_UB + S5_WIDTH]

    @pl.when(jnp.logical_not(first))
    def _():
        o_ref[0] = _dot(h_scr[...], w_ref[0]).astype(BF16)


def _inproj(x, shift, scale, g, w_bf, layer, tm, tn):
    b, l, d = x.shape
    n = w_bf.shape[2]
    assert COL_UB + S5_WIDTH <= tn
    per_batch = shift.shape[0] > 1
    mod_map = (lambda bi, i, j: (bi, 0, 0)) if per_batch else (lambda bi, i, j: (0, 0, 0))
    return pl.pallas_call(
        _inproj_kernel,
        grid=(b, l // tm, n // tn),
        in_specs=[pl.BlockSpec((1, tm, d), lambda bi, i, j: (bi, i, 0)),
                  pl.BlockSpec((1, 1, d), mod_map),
                  pl.BlockSpec((1, 1, d), mod_map),
                  pl.BlockSpec((1, d), lambda bi, i, j: (0, 0)),
                  pl.BlockSpec((1, d, tn), lambda bi, i, j: (layer, 0, j))],
        out_specs=[pl.BlockSpec((1, tm, tn), lambda bi, i, j: (bi, i, j)),
                   pl.BlockSpec((1, tm, S5_WIDTH), lambda bi, i, j: (bi, i, 0))],
        out_shape=[jax.ShapeDtypeStruct((b, l, n), BF16),
                   jax.ShapeDtypeStruct((b, l, S5_WIDTH), F32)],
        scratch_shapes=[pltpu.VMEM((tm, d), BF16)],
        compiler_params=_cparams(3),
        name="inproj",
    )(x, shift, scale, g, w_bf)


def _dft_tables(l):
    r = FFT_RADIX
    l2 = l // r
    scale = 1.0 / math.sqrt(l * FOURIER_GROUP_DIM)
    k1 = np.arange(r, dtype=np.int64)[:, None]
    t2 = np.arange(l2, dtype=np.int64)[None, :]
    ang = 2.0 * np.pi * ((k1 * t2) % l) / l
    lanes = (l, FOURIER_GROUP_DIM)
    tw_re = jnp.broadcast_to(jnp.asarray(np.cos(ang).reshape(l, 1), F32), lanes)
    tw_im = jnp.broadcast_to(jnp.asarray(-np.sin(ang).reshape(l, 1), F32), lanes)
    k2 = np.arange(l2, dtype=np.int64)
    a3 = 2.0 * np.pi * ((k2[:, None] * k2[None, :]) % l2) / l2
    d3 = jnp.asarray(np.concatenate([np.cos(a3), np.sin(a3)], axis=1) * scale, F32).astype(BF16)
    c = np.arange(FOURIER_GROUP_DIM, dtype=np.int64)
    ac = 2.0 * np.pi * ((c[:, None] * c[None, :]) % FOURIER_GROUP_DIM) / FOURIER_GROUP_DIM
    cs = jnp.asarray(np.concatenate([np.cos(ac), np.sin(ac)], axis=1), F32).astype(BF16)
    return (tw_re, tw_im, d3), cs


def _cadd(a, b):
    return a[0] + b[0], a[1] + b[1]


def _csub(a, b):
    return a[0] - b[0], a[1] - b[1]


def _cmul_neg_i(a):
    return a[1], -a[0]


def _dft4(y0, y1, y2, y3):
    c0, c1 = _cadd(y0, y2), _cadd(y1, y3)
    d0, d1 = _csub(y0, y2), _cmul_neg_i(_csub(y1, y3))
    return _cadd(c0, c1), _cadd(d0, d1), _csub(c0, c1), _csub(d0, d1)


def _dft8(x):
    h = math.sqrt(0.5)
    a = [_cadd(x[n], x[n + 4]) for n in range(4)]
    d = [_csub(x[n], x[n + 4]) for n in range(4)]
    b = [d[0],
         ((d[1][0] + d[1][1]) * h, (d[1][1] - d[1][0]) * h),
         _cmul_neg_i(d[2]),
         ((d[3][1] - d[3][0]) * h, -(d[3][0] + d[3][1]) * h)]
    even, odd = _dft4(*a), _dft4(*b)
    return [even[0], odd[0], even[1], odd[1], even[2], odd[2], even[3], odd[3]]


def _fourier_kernel(u_ref, g_ref, cs_ref, twr_ref, twi_ref, d3_ref, w_ref, o_ref, b_scr, f_scr, *, l):
    r = FFT_RADIX
    l2 = l // r
    gd = FOURIER_GROUP_DIM
    t = _dot(u_ref[0].astype(BF16), cs_ref[...])
    blocks = [(t[j * l2:(j + 1) * l2, :gd], -t[j * l2:(j + 1) * l2, gd:]) for j in range(r)]
    for k1, (ar, ai) in enumerate(_dft8(blocks)):
        twr = twr_ref[k1 * l2:(k1 + 1) * l2, :]
        twi = twi_ref[k1 * l2:(k1 + 1) * l2, :]
        b_scr[k1, 0:l2, :] = (ar * twr - ai * twi).astype(BF16)
        b_scr[k1, l2:2 * l2, :] = (ar * twi + ai * twr).astype(BF16)
    for k1 in range(r):
        f_scr[pl.ds(k1, l2, stride=r), :] = _dot(d3_ref[...], b_scr[k1])
    y = _dot(f_scr[...].astype(BF16), w_ref[0].astype(BF16))
    o_ref[0] = (y * _silu(g_ref[0].astype(F32))).astype(BF16)


def _fourier_branch(proj, fourier_w, dft, cs):
    b, l, _ = proj.shape
    gd = FOURIER_GROUP_DIM
    l2 = l // FFT_RADIX
    tw_re, tw_im, d3 = dft
    return pl.pallas_call(
        functools.partial(_fourier_kernel, l=l),
        grid=(b, FOURIER_GROUPS),
        in_specs=[pl.BlockSpec((1, l, gd), lambda bi, g: (bi, 0, COL_UA // gd + g)),
                  pl.BlockSpec((1, l, gd), lambda bi, g: (bi, 0, COL_GA // gd + g)),
                  pl.BlockSpec((gd, 2 * gd), lambda bi, g: (0, 0)),
                  pl.BlockSpec((l, gd), lambda bi, g: (0, 0)),
                  pl.BlockSpec((l, gd), lambda bi, g: (0, 0)),
                  pl.BlockSpec((l2, 2 * l2), lambda bi, g: (0, 0)),
                  pl.BlockSpec((1, gd, gd), lambda bi, g: (g, 0, 0))],
        out_specs=pl.BlockSpec((1, l, gd), lambda bi, g: (bi, 0, g)),
        out_shape=jax.ShapeDtypeStruct((b, l, FOURIER_WIDTH), BF16),
        scratch_shapes=[pltpu.VMEM((FFT_RADIX, 2 * l2, gd), BF16), pltpu.VMEM((l, gd), F32)],
        compiler_params=_cparams(2),
        name="fourier",
    )(proj, proj, cs, tw_re, tw_im, d3, fourier_w)


def _s5_matrices(lam_re, lam_im, log_step, b_re, b_im, c_re, c_im, dskip):
    t = S5_CHUNK
    g, n, p = S5_GROUPS, S5_STATE, S5_GROUP_DIM
    hp = lax.Precision.HIGHEST
    lam_re = lam_re.astype(F32)
    lam_im = lam_im.astype(F32)
    step = jnp.exp(log_step.astype(F32))[..., None]
    rho = lam_re * step
    th = lam_im * step
    mag = jnp.exp(rho)
    lb_re = mag * jnp.cos(th)
    lb_im = mag * jnp.sin(th)
    nr = lb_re - 1.0
    den = lam_re * lam_re + lam_im * lam_im
    fr = (nr * lam_re + lb_im * lam_im) / den
    fi = (lb_im * lam_re - nr * lam_im) / den
    b_re = b_re.astype(F32)
    b_im = b_im.astype(F32)
    bb_re = fr[..., None] * b_re - fi[..., None] * b_im
    bb_im = fr[..., None] * b_im + fi[..., None] * b_re
    tau = jnp.arange(t + 1, dtype=F32)
    pw_mag = jnp.exp(rho[..., None] * tau)
    pw_re = pw_mag * jnp.cos(th[..., None] * tau)
    pw_im = pw_mag * jnp.sin(th[..., None] * tau)
    c_re = c_re.astype(F32)
    c_im = c_im.astype(F32)
    ca_re = c_re[..., None] * pw_re[:, :, None] - c_im[..., None] * pw_im[:, :, None]
    ca_im = c_re[..., None] * pw_im[:, :, None] + c_im[..., None] * pw_re[:, :, None]
    kern = jnp.einsum('dgpnt,dgnq->dgtpq', jnp.concatenate([ca_re[..., :t], -ca_im[..., :t]], axis=3),
                      jnp.concatenate([bb_re, bb_im], axis=2), precision=hp)
    sig = np.arange(t)[:, None]
    ta = np.arange(t)[None, :]
    idx_f = np.clip(ta - sig, 0, t - 1)
    idx_b = np.clip(sig - ta, 0, t - 1)
    m_f = jnp.asarray((ta >= sig).astype(np.float32))[None, :, :, None, None]
    m_b = jnp.asarray((sig >= ta).astype(np.float32))[None, :, :, None, None]
    tf = kern[0][:, idx_f] * m_f
    tb = kern[1][:, idx_b] * m_b
    eye_t = jnp.asarray(np.eye(t, dtype=np.float32))[None, :, :, None, None]
    dmat = dskip.astype(F32)[:, None, None, :, None] * jnp.asarray(np.eye(p, dtype=np.float32))[None, None, None]
    tt = tf + tb + eye_t * dmat
    nb, gb = S5_NBLK, S5_GB
    one = np.ones
    eye = lambda k: np.eye(k, dtype=np.float32)
    exp_t = jnp.asarray(np.einsum('tu,pq,b->tpubq', eye(t), eye(p), one(gb)).reshape(t * p, S5_K))
    exp_n = jnp.asarray(np.einsum('nm,b->nbm', eye(n), one(gb)).reshape(n, S5_PART))
    mask_t = jnp.asarray(np.einsum('ab,s,q,u,p->saqubp', eye(gb), one(t), one(p), one(t), one(p))
                         .reshape(S5_K, S5_K))
    mask_ba = jnp.asarray(np.einsum('ab,s,q,n->saqbn', eye(gb), one(t), one(p), one(n)).reshape(S5_K, S5_PART))
    t_small = tt.reshape(nb, gb, t, t, p, p).transpose(0, 2, 1, 5, 3, 4).reshape(nb, S5_K, t * p)
    tmat = jnp.matmul(t_small, exp_t, precision=hp) * mask_t

    def ba(d, powers):
        pr = pw_re[d][:, :, powers]
        pi = pw_im[d][:, :, powers]
        re = pr[..., None] * bb_re[d][:, :, None, :] - pi[..., None] * bb_im[d][:, :, None, :]
        im = pr[..., None] * bb_im[d][:, :, None, :] + pi[..., None] * bb_re[d][:, :, None, :]
        blk = lambda m: jnp.matmul(m.reshape(nb, gb, n, t, p).transpose(0, 3, 1, 4, 2).reshape(nb, S5_K, n),
                                   exp_n, precision=hp) * mask_ba
        return blk(re), blk(im)

    baf_re, baf_im = ba(0, np.arange(t - 1, -1, -1))
    bab_re, bab_im = ba(1, np.arange(t))
    bamat = jnp.concatenate([baf_re, bab_re, baf_im, bab_im], axis=-1)

    def ca(d, powers):
        blk = lambda m: jnp.matmul(exp_n.T, m[..., powers].reshape(nb, gb, p, n, t).transpose(0, 3, 4, 1, 2)
                                   .reshape(nb, n, S5_K), precision=hp) * mask_ba.T
        return blk(ca_re[d]), blk(ca_im[d])

    caf_re, caf_im = ca(0, np.arange(1, t + 1))
    cab_re, cab_im = ca(1, np.arange(t, 0, -1))
    camat = jnp.concatenate([caf_re, cab_re, -caf_im, -cab_im], axis=1)
    part = lambda m: m[..., t].reshape(nb, S5_PART)
    a_pow = jnp.stack([jnp.concatenate([part(pw_re[0]), part(pw_re[1])], axis=-1),
                       jnp.concatenate([part(pw_im[0]), part(pw_im[1])], axis=-1)], axis=1)
    return tmat.astype(BF16), bamat.astype(BF16), camat.astype(BF16), a_pow


def _s5_kernel(u_ref, t_ref, ba_ref, ca_ref, a_ref, h0_ref, y_ref, hf_ref, ub_scr, s_scr, hin_scr, y_scr,
               *, nc, bt):
    t, w, hp = S5_CHUNK, S5_GB * S5_GROUP_DIM, S5_PART
    for bi in range(bt):
        for tau in range(t):
            ub_scr[tau, pl.ds(bi, nc, stride=bt), :] = u_ref[bi, pl.ds(tau, nc, stride=t), :]
    ub = jnp.concatenate([ub_scr[tau] for tau in range(t)], axis=1).astype(BF16)
    s_scr[...] = _dot(ub, ba_ref[0, 0])
    ar_f, ar_b = a_ref[0, 0, 0:1, 0:hp], a_ref[0, 0, 0:1, hp:2 * hp]
    ai_f, ai_b = a_ref[0, 0, 1:2, 0:hp], a_ref[0, 0, 1:2, hp:2 * hp]

    def step(i, carry):
        re_f, im_f, re_b, im_b = carry
        rf = pl.ds(pl.multiple_of(i * bt, bt), bt)
        rb = pl.ds(pl.multiple_of((nc - 1 - i) * bt, bt), bt)
        hin_scr[rf, 0:hp] = re_f
        hin_scr[rf, 2 * hp:3 * hp] = im_f
        hin_scr[rb, hp:2 * hp] = re_b
        hin_scr[rb, 3 * hp:4 * hp] = im_b
        n_re_f = ar_f * re_f - ai_f * im_f + s_scr[rf, 0:hp]
        n_im_f = ar_f * im_f + ai_f * re_f + s_scr[rf, 2 * hp:3 * hp]
        n_re_b = ar_b * re_b - ai_b * im_b + s_scr[rb, hp:2 * hp]
        n_im_b = ar_b * im_b + ai_b * re_b + s_scr[rb, 3 * hp:4 * hp]
        return n_re_f, n_im_f, n_re_b, n_im_b

    h0 = h0_ref[...].reshape(bt, 4 * hp)
    init = (h0[:, 0:hp], h0[:, 2 * hp:3 * hp], h0[:, hp:2 * hp], h0[:, 3 * hp:4 * hp])
    fin = lax.fori_loop(0, nc, step, init, unroll=8)
    hf_ref[...] = jnp.concatenate([fin[0], fin[2], fin[1], fin[3]], axis=1).reshape(hf_ref.shape)
    y = _dot(ub, t_ref[0, 0]) + _dot(hin_scr[...].astype(BF16), ca_ref[0, 0])
    for tau in range(t):
        y_scr[tau] = y[:, tau * w:(tau + 1) * w]
    for bi in range(bt):
        for tau in range(t):
            y_ref[bi, pl.ds(tau, nc, stride=t), :] = y_scr[tau, pl.ds(bi, nc, stride=bt), :]


def _s5_core(ub, tmat, bamat, camat, a_pow, h0, layer):
    b, l, _ = ub.shape
    nc = l // S5_CHUNK
    w = S5_GB * S5_GROUP_DIM
    bt = b if nc * b <= S5_MAX_ROWS else 1
    if bt == 1:
        h0 = h0.reshape(S5_NBLK, b, 1, 4 * S5_PART)
        st_spec = pl.BlockSpec((1, 1, 1, 4 * S5_PART), lambda j, bi: (j, bi, 0, 0))
    else:
        st_spec = pl.BlockSpec((1, bt, 4 * S5_PART), lambda j, bi: (j, bi, 0))
    rows = nc * bt
    y, hf = pl.pallas_call(
        functools.partial(_s5_kernel, nc=nc, bt=bt),
        grid=(S5_NBLK, b // bt),
        in_specs=[pl.BlockSpec((bt, l, w), lambda j, bi: (bi, 0, j)),
                  pl.BlockSpec((1, 1, S5_K, S5_K), lambda j, bi: (layer, j, 0, 0)),
                  pl.BlockSpec((1, 1, S5_K, 4 * S5_PART), lambda j, bi: (layer, j, 0, 0)),
                  pl.BlockSpec((1, 1, 4 * S5_PART, S5_K), lambda j, bi: (layer, j, 0, 0)),
                  pl.BlockSpec((1, 1, 2, 2 * S5_PART), lambda j, bi: (layer, j, 0, 0)),
                  st_spec],
        out_specs=[pl.BlockSpec((bt, l, w), lambda j, bi: (bi, 0, j)), st_spec],
        out_shape=[jax.ShapeDtypeStruct((b, l, S5_WIDTH), F32),
                   jax.ShapeDtypeStruct(h0.shape, F32)],
        scratch_shapes=[pltpu.VMEM((S5_CHUNK, rows, w), F32), pltpu.VMEM((rows, 4 * S5_PART), F32),
                        pltpu.VMEM((rows, 4 * S5_PART), F32), pltpu.VMEM((S5_CHUNK, rows, w), F32)],
        compiler_params=_cparams(2),
        name="s5_core",
    )(ub, tmat, bamat, camat, a_pow, h0)
    return y, hf.reshape(S5_NBLK, b, 4 * S5_PART)


def _s5_glu_kernel(y_ref, g_ref, w_ref, b_ref, o_ref):
    y = y_ref[0]
    y = 0.5 * y * (1.0 + jnp.tanh(math.sqrt(2.0 / math.pi) * (y + 0.044715 * (y * y * y))))
    z = _dot(y.astype(BF16), w_ref[...].astype(BF16)) + b_ref[...]
    o_ref[0] = (y * _sigmoid(z) * _silu(g_ref[0].astype(F32))).astype(BF16)


def _s5_glu(y, proj, glu_w, glu_b, tm):
    b, l, w = y.shape
    return pl.pallas_call(
        _s5_glu_kernel,
        grid=(b, l // tm),
        in_specs=[pl.BlockSpec((1, tm, w), lambda bi, i: (bi, i, 0)),
                  pl.BlockSpec((1, tm, w), lambda bi, i: (bi, i, COL_GB // w)),
                  pl.BlockSpec((w, w), lambda bi, i: (0, 0)),
                  pl.BlockSpec((1, w), lambda bi, i: (0, 0))],
        out_specs=pl.BlockSpec((1, tm, w), lambda bi, i: (bi, i, 0)),
        out_shape=jax.ShapeDtypeStruct((b, l, w), BF16),
        compiler_params=_cparams(2),
        name="s5_glu",
    )(y, proj, glu_w, glu_b.reshape(1, w))


def _s5_branch(proj, ub, mats, layer, glu_w, glu_b, h0_re, h0_im):
    b, l, _ = proj.shape
    g, n = S5_GROUPS, S5_STATE
    if h0_re is None:
        h0 = jnp.zeros((S5_NBLK, b, 4 * S5_PART), F32)
    else:
        blk = lambda h: h.astype(F32).reshape(b, S5_NBLK, S5_PART).transpose(1, 0, 2)
        h0 = jnp.concatenate([blk(h0_re[:, 0]), blk(h0_re[:, 1]), blk(h0_im[:, 0]), blk(h0_im[:, 1])], axis=-1)
    y, hf = _s5_core(ub, *mats, h0, layer)
    out = _s5_glu(y, proj, glu_w, glu_b, min(l, S5_GLU_TM))
    part = lambda i: hf[..., i * S5_PART:(i + 1) * S5_PART].transpose(1, 0, 2).reshape(b, g, n)
    fin_re = jnp.stack([part(0), part(1)], axis=1)
    fin_im = jnp.stack([part(2), part(3)], axis=1)
    return out, fin_re, fin_im


def _rope_tables(l):
    rows = l // GRID_W
    row = jnp.broadcast_to(jnp.arange(rows, dtype=F32)[:, None], (rows, GRID_W)).reshape(-1)
    col = jnp.broadcast_to(jnp.arange(GRID_W, dtype=F32)[None, :], (rows, GRID_W)).reshape(-1)
    half = HEAD_DIM // 2
    inv = ROPE_THETA ** (-jnp.arange(0, half, 2, dtype=F32) / half)
    ar = row[:, None] * inv
    ac = col[:, None] * inv
    cos = jnp.concatenate([jnp.cos(ar), jnp.cos(ar), jnp.cos(ac), jnp.cos(ac)], axis=1)
    sin = jnp.concatenate([-jnp.sin(ar), jnp.sin(ar), -jnp.sin(ac), jnp.sin(ac)], axis=1)
    return cos, sin


def _qk_kernel(*refs, rope, want_kf):
    q_ref, k_ref, v_ref, qn_ref, kn_ref = refs[:5]
    pos = 5
    if rope:
        cos_ref, sin_ref = refs[pos:pos + 2]
        pos += 2
    qo_ref, ko_ref, vo_ref = refs[pos:pos + 3]
    pos += 3
    if want_kf:
        kf_ref = refs[pos]
    hd = HEAD_DIM
    tm = q_ref.shape[1]
    quarter = hd // 4
    if rope:
        cos = cos_ref[...]
        sin = sin_ref[...]
        first = (lax.broadcasted_iota(jnp.int32, (tm, hd), 1) % (2 * quarter)) < quarter

    def norm(x, gain):
        ms = jnp.mean(x * x, axis=-1, keepdims=True)
        return x * lax.rsqrt(ms + NORM_EPS) * gain

    def rot(x):
        if not rope:
            return x
        nxt = pltpu.roll(x, hd - quarter, 1)
        prv = pltpu.roll(x, quarter, 1)
        return x * cos + jnp.where(first, nxt, prv) * sin

    qscale = (hd ** -0.5) * LOG2E
    for h in range(N_HEADS):
        x = norm(q_ref[0, :, h * hd:(h + 1) * hd].astype(F32), qn_ref[...])
        qo_ref[0, :, h * hd:(h + 1) * hd] = (rot(x) * qscale).astype(BF16)
    for h in range(N_KV_HEADS):
        x = norm(k_ref[0, :, h * hd:(h + 1) * hd].astype(F32), kn_ref[...])
        if want_kf:
            kf_ref[0, :, h * hd:(h + 1) * hd] = x
        ko_ref[0, :, h * hd:(h + 1) * hd] = rot(x).astype(BF16)
        vo_ref[0, h] = v_ref[0, :, h * hd:(h + 1) * hd].astype(F32).T.astype(BF16)


def _qk_prep(proj, q_norm, k_norm, rope_tabs, want_kf, tm):
    b, l, _ = proj.shape
    rope = rope_tabs is not None
    in_specs = [pl.BlockSpec((1, tm, ATTN_WIDTH), lambda bi, i: (bi, i, COL_QC // ATTN_WIDTH)),
                pl.BlockSpec((1, tm, KV_WIDTH), lambda bi, i: (bi, i, COL_KC // KV_WIDTH)),
                pl.BlockSpec((1, tm, KV_WIDTH), lambda bi, i: (bi, i, COL_VC // KV_WIDTH)),
                pl.BlockSpec((1, HEAD_DIM), lambda bi, i: (0, 0)),
                pl.BlockSpec((1, HEAD_DIM), lambda bi, i: (0, 0))]
    args = [proj, proj, proj, q_norm.reshape(1, HEAD_DIM), k_norm.reshape(1, HEAD_DIM)]
    if rope:
        in_specs += [pl.BlockSpec((tm, HEAD_DIM), lambda bi, i: (i, 0))] * 2
        args += list(rope_tabs)
    out_specs = [pl.BlockSpec((1, tm, ATTN_WIDTH), lambda bi, i: (bi, i, 0)),
                 pl.BlockSpec((1, tm, KV_WIDTH), lambda bi, i: (bi, i, 0)),
                 pl.BlockSpec((1, N_KV_HEADS, HEAD_DIM, tm), lambda bi, i: (bi, 0, 0, i))]
    out_shape = [jax.ShapeDtypeStruct((b, l, ATTN_WIDTH), BF16),
                 jax.ShapeDtypeStruct((b, l, KV_WIDTH), BF16),
                 jax.ShapeDtypeStruct((b, N_KV_HEADS, HEAD_DIM, l), BF16)]
    if want_kf:
        out_specs.append(pl.BlockSpec((1, tm, KV_WIDTH), lambda bi, i: (bi, i, 0)))
        out_shape.append(jax.ShapeDtypeStruct((b, l, KV_WIDTH), F32))
    return pl.pallas_call(
        functools.partial(_qk_kernel, rope=rope, want_kf=want_kf),
        grid=(b, l // tm),
        in_specs=in_specs, out_specs=out_specs, out_shape=out_shape,
        compiler_params=_cparams(2),
        name="qk_prep",
    )(*args)


def _attn_kernel(*refs, has_ctx):
    s_scr = refs[-1]
    q_ref, k_ref, v_ref = refs[:3]
    if has_ctx:
        ck_ref, cv_ref, g_ref, o_ref = refs[3:7]
        ck = ck_ref[0, 0].astype(BF16)
        cvt = cv_ref[0, 0, 0].astype(BF16)
        past = ck.shape[0]
    else:
        g_ref, o_ref = refs[3:5]
    hd = HEAD_DIM
    tq = q_ref.shape[1]
    lk = k_ref.shape[1]
    cr = min(lk, ATTN_KEY_CHUNK)
    chunks = [(c * cr, cr, (lambda c=c: k_ref[0, c * cr:(c + 1) * cr, :]),
               (lambda c=c: v_ref[0, 0, :, c * cr:(c + 1) * cr])) for c in range(lk // cr)]
    if has_ctx:
        chunks.append((lk, past, lambda: ck, lambda: cvt))
    qs = [q_ref[0, :, g * hd:(g + 1) * hd] for g in range(Q_PER_KV)]
    gate = _silu(g_ref[0].astype(F32))

    def score(g, chunk, m):
        off, rows, keys, _ = chunk
        s = _dot_nt(keys(), qs[g])
        s_scr[g % 2, off:off + rows, :] = s
        return jnp.maximum(m, jnp.max(s, axis=0, keepdims=True))

    def accumulate(g, chunk, m, den, ot):
        off, rows, _, values_t = chunk
        p = jnp.exp2(s_scr[g % 2, off:off + rows, :] - m)
        return den + jnp.sum(p, axis=0, keepdims=True), ot + _dot(values_t(), p.astype(BF16))

    neg = jnp.full((1, tq), -jnp.inf, F32)
    m_cur = neg
    for chunk in chunks:
        m_cur = score(0, chunk, m_cur)
    for g in range(Q_PER_KV):
        m_next = neg
        den = jnp.zeros((1, tq), F32)
        ot = jnp.zeros((hd, tq), F32)
        for chunk in chunks:
            if g + 1 < Q_PER_KV:
                m_next = score(g + 1, chunk, m_next)
            den, ot = accumulate(g, chunk, m_cur, den, ot)
        m_cur = m_next
        o = (ot / den).T
        o_ref[0, :, g * hd:(g + 1) * hd] = (o * gate[:, g * hd:(g + 1) * hd]).astype(BF16)


def _attention(qs, kr, vt, proj, ctx_k, ctx_v, layer, tq):
    b, l, _ = qs.shape
    gw = Q_PER_KV * HEAD_DIM
    has_ctx = ctx_k is not None
    in_specs = [pl.BlockSpec((1, tq, gw), lambda bi, h, i: (bi, i, h)),
                pl.BlockSpec((1, l, HEAD_DIM), lambda bi, h, i: (bi, 0, h)),
                pl.BlockSpec((1, 1, HEAD_DIM, l), lambda bi, h, i: (bi, h, 0, 0))]
    args = [qs, kr, vt]
    if has_ctx:
        past = ctx_k.shape[2]
        in_specs += [pl.BlockSpec((1, 1, past, HEAD_DIM), lambda bi, h, i: (bi, layer, 0, h)),
                     pl.BlockSpec((1, 1, 1, HEAD_DIM, past), lambda bi, h, i: (bi, layer, h, 0, 0))]
        args += [ctx_k, ctx_v]
    in_specs.append(pl.BlockSpec((1, tq, gw), lambda bi, h, i: (bi, i, COL_GC // gw + h)))
    args.append(proj)
    return pl.pallas_call(
        functools.partial(_attn_kernel, has_ctx=has_ctx),
        grid=(b, N_KV_HEADS, l // tq),
        in_specs=in_specs,
        out_specs=pl.BlockSpec((1, tq, gw), lambda bi, h, i: (bi, i, h)),
        out_shape=jax.ShapeDtypeStruct((b, l, ATTN_WIDTH), BF16),
        scratch_shapes=[pltpu.VMEM((2, l + (ctx_k.shape[2] if has_ctx else 0), tq), F32)],
        compiler_params=_cparams(3),
        name="attention",
    )(*args)


HGRN_COARSE = (32, 16, 8, 4)
HGRN_FINE = (2,)
HGRN_LEVELS = HGRN_COARSE + HGRN_FINE + (1,)
HGRN_CHUNKS_PER_TRIP = 4


def _hgrn_tables():
    c = HGRN_CHUNK
    t = np.arange(c)
    w01, sm = [], []
    for d in range(2):
        cum = (t[None, :] <= t[:, None]) if d == 0 else (t[None, :] >= t[:, None])
        cum = cum.astype(np.float32)
        w01.append(np.concatenate([cum] + [cum[_hgrn_ref_row(t, hs, d)] for hs in HGRN_FINE], axis=0))
        sd = []
        for hs in HGRN_LEVELS:
            is_q = _hgrn_is_query(t, hs, d)
            same = (t[:, None] // (2 * hs)) == (t[None, :] // (2 * hs))
            sd.append((same & is_q[:, None] & ~is_q[None, :]).astype(np.float32))
        sd.append(np.eye(c, dtype=np.float32))
        sm.append(np.stack(sd))
    return jnp.asarray(np.stack(w01), BF16), jnp.asarray(np.stack(sm), F32)


def _hgrn_ref_row(t, hs, d):
    base = t & ~(2 * hs - 1)
    return base + hs - 1 if d == 0 else base + hs


def _hgrn_is_query(t, hs, d):
    return ((t & hs) != 0) if d == 0 else ((t & hs) == 0)


def _hgrn_cums(z, lbv, w01):
    f = lbv + (1.0 - lbv) * _sigmoid(z)
    fm = jnp.maximum(f, F_FLOOR)
    hi, lo = _split2(jnp.log2(fm))
    return fm, 1.0 - f, _dot(w01, hi) + _dot(w01, lo)


def _hgrn_products(q, k, fm, allc, rows, d):
    c = HGRN_CHUNK
    sub = 8
    cum = allc[0:c]
    refs = {}
    for hs in HGRN_COARSE:
        pieces = []
        for j in range(c // sub):
            rr = int(_hgrn_ref_row(np.int64(j * sub), hs, d))
            pieces.append(jnp.broadcast_to(cum[rr:rr + 1, :], (sub, HGRN_DK)))
        refs[hs] = jnp.concatenate(pieces, axis=0)
    for i, hs in enumerate(HGRN_FINE):
        refs[hs] = allc[(i + 1) * c:(i + 2) * c]
    lrow = c - 1 if d == 0 else 0
    last = cum[lrow:lrow + 1, :]
    raws = [_dot_nt(q.astype(BF16), k.astype(BF16))]
    for hs in HGRN_LEVELS:
        is_q = ((rows & hs) != 0) if d == 0 else ((rows & hs) == 0)
        if hs == 1:
            x = jnp.where(is_q, q * fm, k).astype(BF16)
        else:
            x = (jnp.where(is_q, q, k) * jnp.exp2((cum - refs[hs]) * jnp.where(is_q, 1.0, -1.0))).astype(BF16)
        raws.append(_dot_nt(x, x))
    qi = (q * jnp.exp2(cum)).astype(BF16)
    ks = (k * jnp.exp2(last - cum)).astype(BF16)
    return raws, qi, ks, jnp.exp2(last)


def _hgrn_kernel(*refs, nc, has_s0, want_state):
    q_ref, v_ref, zf_ref, zb_ref, g_ref, lb_ref, ng_ref, w01_ref, sm_ref = refs[:9]
    pos = 9
    if has_s0:
        s0_ref = refs[pos]
        pos += 1
    o_ref = refs[pos]
    pos += 1
    if want_state:
        sf_ref = refs[pos]
        pos += 1
    o_scr, st_scr = refs[pos:pos + 2]
    c = HGRN_CHUNK
    l = nc * c
    for d in range(2):
        if has_s0:
            st_scr[d] = s0_ref[0, d, 0]
        else:
            st_scr[d] = jnp.zeros((HGRN_DV, HGRN_DK), F32)
    rows = lax.broadcasted_iota(jnp.int32, (c, HGRN_DK), 0)

    per_trip = math.gcd(nc, HGRN_CHUNKS_PER_TRIP)
    ntrips = nc // per_trip
    nl = len(HGRN_LEVELS)

    def instances(i):
        insts = []
        for u in range(per_trip):
            for d in range(2):
                ci = per_trip * i + u if d == 0 else nc - 1 - (per_trip * i + u)
                r = ci * c
                insts.append((d, r if isinstance(r, int) else pl.multiple_of(r, c)))
        return insts

    def emit(tot, r, n):
        ms = jnp.mean(tot * tot, axis=-1, keepdims=True)
        y = tot * lax.rsqrt(ms + NORM_EPS) * ng_ref[...]
        o_ref[0, pl.ds(r, n), :] = (y * _silu(g_ref[0, pl.ds(r, n), :].astype(F32))).astype(BF16)

    def state_step(inst, local, finalize):
        (d, r), (scores, qi, ks, dec) = inst, local
        vb = v_ref[0, pl.ds(r, c), :].astype(BF16)
        st = st_scr[d]
        out = _dot(scores, vb) + _dot_nt(qi, st.astype(BF16))
        st_scr[d] = dec * st + _dot_tn(vb, ks)
        if finalize:
            emit(out + o_scr[1 - d, pl.ds(r, c), :], r, c)
        else:
            o_scr[d, pl.ds(r, c), :] = out

    def trip(i, prev, finalize):
        insts = instances(i)
        prev_insts = instances(i - 1) if prev is not None else None
        gates = [_hgrn_cums((zf_ref, zb_ref)[d][0, pl.ds(r, c), :].astype(F32), lb_ref[d, 0], w01_ref[d])
                 for d, r in insts]
        out = []
        for idx, ((d, r), (fm, k, allc)) in enumerate(zip(insts, gates)):
            raws, qi, ks, dec = _hgrn_products(q_ref[0, pl.ds(r, c), :].astype(F32), k, fm, allc, rows, d)
            if prev is not None:
                state_step(prev_insts[idx], prev[idx], finalize)
            scores = sm_ref[d, nl] * raws[0]
            for li in range(nl):
                scores = scores + sm_ref[d, li] * raws[li + 1]
            out.append((scores.astype(BF16), qi, ks, dec))
        return tuple(out)

    fused = ntrips % 2 == 0
    half = ntrips // 2 if fused else ntrips
    local = lax.fori_loop(1, half + 1 if fused else ntrips, functools.partial(trip, finalize=False),
                          trip(0, None, False))
    if fused:
        local = lax.fori_loop(half + 1, ntrips, functools.partial(trip, finalize=True), local)
    for inst, loc in zip(instances(ntrips - 1), local):
        state_step(inst, loc, fused)
    if want_state:
        for d in range(2):
            sf_ref[0, d, 0] = st_scr[d]

    if not fused:
        fb = min(l, 256)

        def finish(i, carry):
            r = pl.multiple_of(i * fb, fb)
            emit(o_scr[0, pl.ds(r, fb), :] + o_scr[1, pl.ds(r, fb), :], r, fb)
            return carry

        lax.fori_loop(0, l // fb, finish, 0)


def _hgrn_branch(proj, lower_bound, norm_g, tables, s0_t, want_state):
    b, l, _ = proj.shape
    nc = l // HGRN_CHUNK
    w = HGRN_DK
    has_s0 = s0_t is not None
    w01, sm = tables

    def col(off):
        return lambda bi, h: (bi, 0, off // w + h)

    in_specs = [pl.BlockSpec((1, l, w), col(COL_QD)),
                pl.BlockSpec((1, l, w), col(COL_ID)),
                pl.BlockSpec((1, l, w), col(COL_ZF)),
                pl.BlockSpec((1, l, w), col(COL_ZB)),
                pl.BlockSpec((1, l, w), col(COL_GD)),
                pl.BlockSpec((2, 1, 1, w), lambda bi, h: (0, h, 0, 0)),
                pl.BlockSpec((1, w), lambda bi, h: (0, 0)),
                pl.BlockSpec(w01.shape, lambda bi, h: (0, 0, 0)),
                pl.BlockSpec(sm.shape, lambda bi, h: (0, 0, 0, 0))]
    args = [proj, proj, proj, proj, proj, lower_bound.reshape(2, HGRN_HEADS, 1, w), norm_g.reshape(1, w),
            w01, sm]
    st_spec = pl.BlockSpec((1, 2, 1, HGRN_DV, HGRN_DK), lambda bi, h: (bi, 0, h, 0, 0))
    if has_s0:
        in_specs.append(st_spec)
        args.append(s0_t)
    out_specs = [pl.BlockSpec((1, l, w), lambda bi, h: (bi, 0, h))]
    out_shape = [jax.ShapeDtypeStruct((b, l, HGRN_HEADS * HGRN_DV), BF16)]
    if want_state:
        out_specs.append(st_spec)
        out_shape.append(jax.ShapeDtypeStruct((b, 2, HGRN_HEADS, HGRN_DV, HGRN_DK), F32))
    res = pl.pallas_call(
        functools.partial(_hgrn_kernel, nc=nc, has_s0=has_s0, want_state=want_state),
        grid=(b, HGRN_HEADS),
        in_specs=in_specs, out_specs=out_specs, out_shape=out_shape,
        scratch_shapes=[pltpu.VMEM((2, l, HGRN_DV), F32), pltpu.VMEM((2, HGRN_DV, HGRN_DK), F32)],
        compiler_params=_cparams(2),
        name="hgrn",
    )(*args)
    return res if want_state else (res[0], None)


def _final_kernel(x_ref, gt_ref, ma_ref, mb_ref, mc_ref, md_ref, ya_ref, yb_ref, yc_ref, yd_ref,
                  wa_ref, wb_ref, wc_ref, wd_ref, wo_ref, gp_ref, o_ref, *, nct):
    ci = pl.program_id(2)
    tm = o_ref.shape[1]
    subs = [slice(r0, r0 + FINAL_ROWS) for r0 in range(0, tm, FINAL_ROWS)]
    m_refs = (ma_ref, mb_ref, mc_ref, md_ref)
    y_refs = (ya_ref, yb_ref, yc_ref, yd_ref)
    w_refs = (wa_ref, wb_ref, wc_ref, wd_ref)

    def branch_dots(rs):
        return [_dot(y_ref[0, rs, :], w_ref[0]) for y_ref, w_ref in zip(y_refs, w_refs)]

    def step(first, last):
        nxt = branch_dots(subs[0])
        for si, rs in enumerate(subs):
            branch = nxt
            if si + 1 < len(subs):
                nxt = branch_dots(subs[si + 1])
            mixed = None
            for m_ref, dot in zip(m_refs, branch):
                term = _gate_sigmoid(m_ref[0, rs, :].astype(F32)) * dot
                mixed = term if mixed is None else mixed + term
            contrib = _dot(mixed.astype(BF16), wo_ref[0])
            out = contrib if first else o_ref[0, rs, :] + contrib
            if last:
                ms = jnp.mean(out * out, axis=-1, keepdims=True)
                y = out * lax.rsqrt(ms + NORM_EPS) * gp_ref[...]
                o_ref[0, rs, :] = x_ref[0, rs, :] + gt_ref[0] * y
            else:
                o_ref[0, rs, :] = out

    if nct == 1:
        step(True, True)
    else:
        pl.when(ci == 0)(lambda: step(True, False))
        pl.when(jnp.logical_and(ci > 0, ci < nct - 1))(lambda: step(False, False))
        pl.when(ci == nct - 1)(lambda: step(False, True))


def _final(x, gate, proj, ys, w_projs, w_out, norm_post, layer, tm, tc):
    b, l, d = x.shape
    nct = d // tc
    per_batch = gate.shape[0] > 1
    gate_map = (lambda bi, i, c: (bi, 0, 0)) if per_batch else (lambda bi, i, c: (0, 0, 0))

    def mspec(j):
        return pl.BlockSpec((1, tm, tc), lambda bi, i, c: (bi, i, (COL_M + j * d) // tc + c))

    in_specs = [pl.BlockSpec((1, tm, d), lambda bi, i, c: (bi, i, 0)),
                pl.BlockSpec((1, 1, d), gate_map)]
    in_specs += [mspec(j) for j in range(N_BRANCHES)]
    in_specs += [pl.BlockSpec((1, tm, y.shape[-1]), lambda bi, i, c: (bi, i, 0)) for y in ys]
    in_specs += [pl.BlockSpec((1, w.shape[1], tc), lambda bi, i, c: (layer, 0, c)) for w in w_projs]
    in_specs += [pl.BlockSpec((1, tc, d), lambda bi, i, c: (layer, c, 0)),
                 pl.BlockSpec((1, d), lambda bi, i, c: (0, 0))]
    return pl.pallas_call(
        functools.partial(_final_kernel, nct=nct),
        grid=(b, l // tm, nct),
        in_specs=in_specs,
        out_specs=pl.BlockSpec((1, tm, d), lambda bi, i, c: (bi, i, 0)),
        out_shape=jax.ShapeDtypeStruct((b, l, d), F32),
        compiler_params=pltpu.CompilerParams(dimension_semantics=("arbitrary",) * 3,
                                             vmem_limit_bytes=FINAL_VMEM_LIMIT),
        name="merge_out",
    )(x, gate, proj, proj, proj, proj, *ys, *w_projs, w_out, norm_post.reshape(1, d))


def _trunk_layer(x, shift, scale, gate, p, ctx, layer, consts):
    b, l, d = x.shape
    shared_mod = shift.shape[0] == 1
    if shared_mod:
        proj, ub = _inproj(x.reshape(1, b * l, d), shift, scale, p['norm_pre'], p['w_in'], layer,
                           INPROJ_TM, INPROJ_TN)
        proj = proj.reshape(b, l, IN_COLS)
        ub = ub.reshape(b, l, S5_WIDTH)
    else:
        proj, ub = _inproj(x, shift, scale, p['norm_pre'], p['w_in'], layer, INPROJ_TM, INPROJ_TN)

    dl, cs = consts['dft'][l]
    y_a = _fourier_branch(proj, p['fourier_w'], dl, cs)

    if ctx is None:
        y_b, s5_re, s5_im = _s5_branch(proj, ub, p['s5_mats'], layer, p['s5_glu_w'], p['s5_glu_b'], None, None)
        qs, kr, vt, kf = _qk_prep(proj, p['q_norm'], p['k_norm'], None, True, min(l, QK_PREP_TM))
        y_c = _attention(qs, kr, vt, proj, None, None, layer, min(l, ATTN_TQ))
        y_d, hg = _hgrn_branch(proj, p['lower_bound'], p['hgrn_norm'], consts['hgrn'], None, True)
    else:
        cache_k, cache_v, st_re, st_im, s0_t = ctx
        y_b, _, _ = _s5_branch(proj, ub, p['s5_mats'], layer, p['s5_glu_w'], p['s5_glu_b'],
                               st_re[:, layer], st_im[:, layer])
        qs, kr, vt = _qk_prep(proj, p['q_norm'], p['k_norm'], consts['rope'], False, min(l, QK_PREP_TM))
        y_c = _attention(qs, kr, vt, proj, cache_k, cache_v, layer, min(l, ATTN_TQ))
        y_d, _ = _hgrn_branch(proj, p['lower_bound'], p['hgrn_norm'], consts['hgrn'], s0_t[:, layer], False)

    ys = (y_a, y_b, y_c, y_d)
    wps = (p['w_proj_a'], p['w_proj_b'], p['w_proj_c'], p['w_proj_d'])
    if shared_mod:
        n = b * l
        x_new = _final(x.reshape(1, n, d), gate, proj.reshape(1, n, IN_COLS),
                       tuple(y.reshape(1, n, y.shape[-1]) for y in ys), wps, p['w_out'], p['norm_post'], layer,
                       FINAL_TM, FINAL_TC)
        x_new = x_new.reshape(b, l, d)
    else:
        x_new = _final(x, gate, proj, ys, wps, p['w_out'], p['norm_post'], layer, FINAL_TM, FINAL_TC)

    if ctx is None:
        v_f = proj[:, :, COL_VC:COL_VC + KV_WIDTH].astype(F32)
        return x_new, (kf, v_f, s5_re, s5_im, hg)
    return x_new, None


def kernel(x_prompt, x_sample, c, cache_k, cache_v, state_s5_re, state_s5_im, state_hgrn, c_ctx,
           norm_pre, norm_post, w_mod, b_mod, w_in, fourier_w,
           s5_lambda_re, s5_lambda_im, s5_log_step, s5_b_re, s5_b_im, s5_c_re, s5_c_im,
           s5_d, s5_glu_w, s5_glu_b, q_norm, k_norm, hgrn_lb_logits, hgrn_norm,
           w_proj_a, w_proj_b, w_proj_c, w_proj_d, w_out):
    depth = w_in.shape[0]
    bp, lp, d = x_prompt.shape
    bs, ls, _ = x_sample.shape

    lb_w = jax.nn.softmax(hgrn_lb_logits.astype(F32), axis=0)
    lower_bounds = jnp.cumsum(lb_w, axis=0) - lb_w[0]

    rows = ((bs + 1 + 7) // 8) * 8
    c_all = jnp.zeros((rows, d), F32).at[:bs].set(c).at[bs].set(c_ctx)
    mod = _modulation(c_all, w_mod, b_mod)

    w_in_bf = w_in.astype(BF16)
    wpa, wpb, wpc, wpd = (w.astype(BF16) for w in (w_proj_a, w_proj_b, w_proj_c, w_proj_d))
    w_out_bf = w_out.astype(BF16)

    consts = {'dft': {lp: _dft_tables(lp), ls: _dft_tables(ls)},
              'rope': _rope_tables(ls),
              'hgrn': _hgrn_tables()}
    past = cache_k.shape[2]
    ctx_k = cache_k.reshape(bs, depth, past, KV_WIDTH)
    ctx_v = cache_v.transpose(0, 1, 3, 4, 2)
    s0_t = jnp.swapaxes(state_hgrn.astype(F32), -1, -2)

    s5_mats = jax.vmap(_s5_matrices)(s5_lambda_re, s5_lambda_im, s5_log_step, s5_b_re, s5_b_im, s5_c_re, s5_c_im,
                                     s5_d.reshape(depth, S5_GROUPS, S5_GROUP_DIM))

    y_p, y_s = x_prompt, x_sample
    ks, vs, s5r, s5i, hg = [], [], [], [], []
    for l in range(depth):
        p = {'norm_pre': norm_pre[l].reshape(1, d), 'norm_post': norm_post[l], 'w_in': w_in_bf,
             'fourier_w': fourier_w[l],
             's5_mats': s5_mats,
             's5_glu_w': s5_glu_w[l], 's5_glu_b': s5_glu_b[l],
             'q_norm': q_norm[l], 'k_norm': k_norm[l],
             'lower_bound': lower_bounds[l], 'hgrn_norm': hgrn_norm[l],
             'w_proj_a': wpa, 'w_proj_b': wpb, 'w_proj_c': wpc, 'w_proj_d': wpd,
             'w_out': w_out_bf}
        m_ctx = mod[l, bs].reshape(1, 1, 3 * d)
        sh, sc, gt = (m_ctx[..., i * d:(i + 1) * d] for i in range(3))
        y_p, (k_l, v_l, sr_l, si_l, hg_l) = _trunk_layer(y_p, sh, sc, gt, p, None, l, consts)
        ks.append(k_l.reshape(bp, lp, N_KV_HEADS, HEAD_DIM))
        vs.append(v_l.reshape(bp, lp, N_KV_HEADS, HEAD_DIM))
        s5r.append(sr_l)
        s5i.append(si_l)
        hg.append(jnp.swapaxes(hg_l, -1, -2))
        m_s = mod[l, :bs].reshape(bs, 1, 3 * d)
        sh, sc, gt = (m_s[..., i * d:(i + 1) * d] for i in range(3))
        ctx = (ctx_k, ctx_v, state_s5_re, state_s5_im, s0_t)
        y_s, _ = _trunk_layer(y_s, sh, sc, gt, p, ctx, l, consts)

    return (y_p, y_s, jnp.stack(ks, axis=1), jnp.stack(vs, axis=1),
            jnp.stack(s5r, axis=1), jnp.stack(s5i, axis=1), jnp.stack(hg, axis=1))
```

```python
import functools
import math

import jax
import jax.numpy as jnp
import numpy as np
from jax import lax
from jax.experimental import pallas as pl
from jax.experimental.pallas import tpu as pltpu

F32 = jnp.float32
BF16 = jnp.bfloat16

NORM_EPS = 1e-6
F_FLOOR = 1e-30
LOG2E = 1.4426950408889634

N_BRANCHES = 4
FOURIER_WIDTH = 512
FOURIER_GROUPS = 4
FOURIER_GROUP_DIM = 128
S5_WIDTH = 512
S5_GROUP_DIM = 16
S5_GROUPS = 32
S5_STATE = 64
S5_CHUNK = 8
S5_GB = 8
S5_NBLK = S5_GROUPS // S5_GB
S5_K = S5_CHUNK * S5_GB * S5_GROUP_DIM
S5_PART = S5_GB * S5_STATE
S5_MAX_ROWS = 512
N_HEADS = 8
N_KV_HEADS = 2
HEAD_DIM = 128
Q_PER_KV = 4
ATTN_WIDTH = 1024
KV_WIDTH = 256
ROPE_THETA = 10000.0
GRID_W = 64
HGRN_HEADS = 4
HGRN_DK = 128
HGRN_DV = 128
HGRN_CHUNK = 64

COL_UA, COL_GA, COL_UB, COL_GB = 0, 512, 1024, 1536
COL_QC, COL_KC, COL_VC, COL_GC = 2048, 3072, 3328, 3584
COL_QD, COL_ID, COL_ZF, COL_ZB, COL_GD = 4608, 5120, 5632, 6144, 6656
COL_M = 7168
IN_COLS = 15360

FFT_RADIX = 8
FINAL_ROWS = 256
ATTN_KEY_CHUNK = 1024
ATTN_TQ = 512
INPROJ_TM = 1024
INPROJ_TN = 1536
INPROJ_ROWS = 256
FINAL_TM = 1024
FINAL_TC = 256
MOD_TN = 1024
QK_PREP_TM = 512
S5_GLU_TM = 1024
VMEM_LIMIT = 56 * 1024 * 1024
FINAL_VMEM_LIMIT = 58 * 1024 * 1024


def _cparams(n_grid):
    return pltpu.CompilerParams(dimension_semantics=("arbitrary",) * n_grid,
                                vmem_limit_bytes=VMEM_LIMIT)


def _sigmoid(x):
    return 1.0 / (1.0 + jnp.exp(-x))


def _gate_sigmoid(x):
    return 0.5 * jnp.tanh(0.5 * x) + 0.5


def _silu(x):
    return x * _gate_sigmoid(x)


def _dot(a, b):
    return jnp.dot(a, b, preferred_element_type=F32)


def _dot_nt(a, b):
    return lax.dot_general(a, b, (((1,), (1,)), ((), ())), preferred_element_type=F32)


def _dot_tn(a, b):
    return lax.dot_general(a, b, (((0,), (0,)), ((), ())), preferred_element_type=F32)


def _split2(x):
    hi = x.astype(BF16)
    lo = (x - hi.astype(F32)).astype(BF16)
    return hi, lo


def _dot3(a, b):
    ah, al = _split2(a)
    bh, bl = _split2(b)
    return _dot(ah, bh) + (_dot(ah, bl) + _dot(al, bh))


def _mod_kernel(c_ref, w_ref, b_ref, o_ref):
    a = _silu(c_ref[...])
    o_ref[0] = _dot3(a, w_ref[0]) + b_ref[0]


def _modulation(c_all, w_mod, b_mod):
    depth, d, n = w_mod.shape
    rows = c_all.shape[0]
    tn = MOD_TN
    return pl.pallas_call(
        _mod_kernel,
        grid=(depth, n // tn),
        in_specs=[pl.BlockSpec((rows, d), lambda l, j: (0, 0)),
                  pl.BlockSpec((1, d, tn), lambda l, j: (l, 0, j)),
                  pl.BlockSpec((1, 1, tn), lambda l, j: (l, 0, j))],
        out_specs=pl.BlockSpec((1, rows, tn), lambda l, j: (l, 0, j)),
        out_shape=jax.ShapeDtypeStruct((depth, rows, n), F32),
        compiler_params=_cparams(2),
        name="modulation",
    )(c_all, w_mod, b_mod.reshape(depth, 1, n))


def _inproj_kernel(x_ref, sh_ref, sc_ref, g_ref, w_ref, o_ref, ub_ref, h_scr):
    first = pl.program_id(2) == 0
    tm = h_scr.shape[0]

    @pl.when(first)
    def _():
        for r0 in range(0, tm, INPROJ_ROWS):
            rs = slice(r0, r0 + INPROJ_ROWS)
            x = x_ref[0, rs, :]
            ms = jnp.mean(x * x, axis=-1, keepdims=True)
            y = x * lax.rsqrt(ms + NORM_EPS) * g_ref[...]
            h = (y * (1.0 + sc_ref[0]) + sh_ref[0]).astype(BF16)
            h_scr[rs, :] = h
            res = _dot(h, w_ref[0])
            o_ref[0, rs, :] = res.astype(BF16)
            ub_ref[0, rs, :] = res[:, COL_UB:COL_UB + S5_WIDTH]

    @pl.when(jnp.logical_not(first))
    def _():
        o_ref[0] = _dot(h_scr[...], w_ref[0]).astype(BF16)


def _inproj(x, shift, scale, g, w_bf, layer, tm, tn):
    b, l, d = x.shape
    n = w_bf.shape[2]
    assert COL_UB + S5_WIDTH <= tn
    per_batch = shift.shape[0] > 1
    mod_map = (lambda bi, i, j: (bi, 0, 0)) if per_batch else (lambda bi, i, j: (0, 0, 0))
    return pl.pallas_call(
        _inproj_kernel,
        grid=(b, l // tm, n // tn),
        in_specs=[pl.BlockSpec((1, tm, d), lambda bi, i, j: (bi, i, 0)),
                  pl.BlockSpec((1, 1, d), mod_map),
                  pl.BlockSpec((1, 1, d), mod_map),
                  pl.BlockSpec((1, d), lambda bi, i, j: (0, 0)),
                  pl.BlockSpec((1, d, tn), lambda bi, i, j: (layer, 0, j))],
        out_specs=[pl.BlockSpec((1, tm, tn), lambda bi, i, j: (bi, i, j)),
                   pl.BlockSpec((1, tm, S5_WIDTH), lambda bi, i, j: (bi, i, 0))],
        out_shape=[jax.ShapeDtypeStruct((b, l, n), BF16),
                   jax.ShapeDtypeStruct((b, l, S5_WIDTH), F32)],
        scratch_shapes=[pltpu.VMEM((tm, d), BF16)],
        compiler_params=_cparams(3),
        name="inproj",
    )(x, shift, scale, g, w_bf)


def _dft_tables(l):
    r = FFT_RADIX
    l2 = l // r
    scale = 1.0 / math.sqrt(l * FOURIER_GROUP_DIM)
    k1 = np.arange(r, dtype=np.int64)[:, None]
    t2 = np.arange(l2, dtype=np.int64)[None, :]
    ang = 2.0 * np.pi * ((k1 * t2) % l) / l
    lanes = (l, FOURIER_GROUP_DIM)
    tw_re = jnp.broadcast_to(jnp.asarray(np.cos(ang).reshape(l, 1), F32), lanes)
    tw_im = jnp.broadcast_to(jnp.asarray(-np.sin(ang).reshape(l, 1), F32), lanes)
    k2 = np.arange(l2, dtype=np.int64)
    a3 = 2.0 * np.pi * ((k2[:, None] * k2[None, :]) % l2) / l2
    d3 = jnp.asarray(np.concatenate([np.cos(a3), np.sin(a3)], axis=1) * scale, F32).astype(BF16)
    c = np.arange(FOURIER_GROUP_DIM, dtype=np.int64)
    ac = 2.0 * np.pi * ((c[:, None] * c[None, :]) % FOURIER_GROUP_DIM) / FOURIER_GROUP_DIM
    cs = jnp.asarray(np.concatenate([np.cos(ac), np.sin(ac)], axis=1), F32).astype(BF16)
    return (tw_re, tw_im, d3), cs


def _cadd(a, b):
    return a[0] + b[0], a[1] + b[1]


def _csub(a, b):
    return a[0] - b[0], a[1] - b[1]


def _cmul_neg_i(a):
    return a[1], -a[0]


def _dft4(y0, y1, y2, y3):
    c0, c1 = _cadd(y0, y2), _cadd(y1, y3)
    d0, d1 = _csub(y0, y2), _cmul_neg_i(_csub(y1, y3))
    return _cadd(c0, c1), _cadd(d0, d1), _csub(c0, c1), _csub(d0, d1)


def _dft8(x):
    h = math.sqrt(0.5)
    a = [_cadd(x[n], x[n + 4]) for n in range(4)]
    d = [_csub(x[n], x[n + 4]) for n in range(4)]
    b = [d[0],
         ((d[1][0] + d[1][1]) * h, (d[1][1] - d[1][0]) * h),
         _cmul_neg_i(d[2]),
         ((d[3][1] - d[3][0]) * h, -(d[3][0] + d[3][1]) * h)]
    even, odd = _dft4(*a), _dft4(*b)
    return [even[0], odd[0], even[1], odd[1], even[2], odd[2], even[3], odd[3]]


def _fourier_kernel(u_ref, g_ref, cs_ref, twr_ref, twi_ref, d3_ref, w_ref, o_ref, b_scr, f_scr, *, l):
    r = FFT_RADIX
    l2 = l // r
    gd = FOURIER_GROUP_DIM
    t = _dot(u_ref[0].astype(BF16), cs_ref[...])
    blocks = [(t[j * l2:(j + 1) * l2, :gd], -t[j * l2:(j + 1) * l2, gd:]) for j in range(r)]
    for k1, (ar, ai) in enumerate(_dft8(blocks)):
        twr = twr_ref[k1 * l2:(k1 + 1) * l2, :]
        twi = twi_ref[k1 * l2:(k1 + 1) * l2, :]
        b_scr[k1, 0:l2, :] = (ar * twr - ai * twi).astype(BF16)
        b_scr[k1, l2:2 * l2, :] = (ar * twi + ai * twr).astype(BF16)
    for k1 in range(r):
        f_scr[pl.ds(k1, l2, stride=r), :] = _dot(d3_ref[...], b_scr[k1])
    y = _dot(f_scr[...].astype(BF16), w_ref[0].astype(BF16))
    o_ref[0] = (y * _silu(g_ref[0].astype(F32))).astype(BF16)


def _fourier_branch(proj, fourier_w, dft, cs):
    b, l, _ = proj.shape
    gd = FOURIER_GROUP_DIM
    l2 = l // FFT_RADIX
    tw_re, tw_im, d3 = dft
    return pl.pallas_call(
        functools.partial(_fourier_kernel, l=l),
        grid=(b, FOURIER_GROUPS),
        in_specs=[pl.BlockSpec((1, l, gd), lambda bi, g: (bi, 0, COL_UA // gd + g)),
                  pl.BlockSpec((1, l, gd), lambda bi, g: (bi, 0, COL_GA // gd + g)),
                  pl.BlockSpec((gd, 2 * gd), lambda bi, g: (0, 0)),
                  pl.BlockSpec((l, gd), lambda bi, g: (0, 0)),
                  pl.BlockSpec((l, gd), lambda bi, g: (0, 0)),
                  pl.BlockSpec((l2, 2 * l2), lambda bi, g: (0, 0)),
                  pl.BlockSpec((1, gd, gd), lambda bi, g: (g, 0, 0))],
        out_specs=pl.BlockSpec((1, l, gd), lambda bi, g: (bi, 0, g)),
        out_shape=jax.ShapeDtypeStruct((b, l, FOURIER_WIDTH), BF16),
        scratch_shapes=[pltpu.VMEM((FFT_RADIX, 2 * l2, gd), BF16), pltpu.VMEM((l, gd), F32)],
        compiler_params=_cparams(2),
        name="fourier",
    )(proj, proj, cs, tw_re, tw_im, d3, fourier_w)


def _s5_matrices(lam_re, lam_im, log_step, b_re, b_im, c_re, c_im, dskip):
    t = S5_CHUNK
    g, n, p = S5_GROUPS, S5_STATE, S5_GROUP_DIM
    hp = lax.Precision.HIGHEST
    lam_re = lam_re.astype(F32)
    lam_im = lam_im.astype(F32)
    step = jnp.exp(log_step.astype(F32))[..., None]
    rho = lam_re * step
    th = lam_im * step
    mag = jnp.exp(rho)
    lb_re = mag * jnp.cos(th)
    lb_im = mag * jnp.sin(th)
    nr = lb_re - 1.0
    den = lam_re * lam_re + lam_im * lam_im
    fr = (nr * lam_re + lb_im * lam_im) / den
    fi = (lb_im * lam_re - nr * lam_im) / den
    b_re = b_re.astype(F32)
    b_im = b_im.astype(F32)
    bb_re = fr[..., None] * b_re - fi[..., None] * b_im
    bb_im = fr[..., None] * b_im + fi[..., None] * b_re
    tau = jnp.arange(t + 1, dtype=F32)
    pw_mag = jnp.exp(rho[..., None] * tau)
    pw_re = pw_mag * jnp.cos(th[..., None] * tau)
    pw_im = pw_mag * jnp.sin(th[..., None] * tau)
    c_re = c_re.astype(F32)
    c_im = c_im.astype(F32)
    ca_re = c_re[..., None] * pw_re[:, :, None] - c_im[..., None] * pw_im[:, :, None]
    ca_im = c_re[..., None] * pw_im[:, :, None] + c_im[..., None] * pw_re[:, :, None]
    kern = jnp.einsum('dgpnt,dgnq->dgtpq', jnp.concatenate([ca_re[..., :t], -ca_im[..., :t]], axis=3),
                      jnp.concatenate([bb_re, bb_im], axis=2), precision=hp)
    sig = np.arange(t)[:, None]
    ta = np.arange(t)[None, :]
    idx_f = np.clip(ta - sig, 0, t - 1)
    idx_b = np.clip(sig - ta, 0, t - 1)
    m_f = jnp.asarray((ta >= sig).astype(np.float32))[None, :, :, None, None]
    m_b = jnp.asarray((sig >= ta).astype(np.float32))[None, :, :, None, None]
    tf = kern[0][:, idx_f] * m_f
    tb = kern[1][:, idx_b] * m_b
    eye_t = jnp.asarray(np.eye(t, dtype=np.float32))[None, :, :, None, None]
    dmat = dskip.astype(F32)[:, None, None, :, None] * jnp.asarray(np.eye(p, dtype=np.float32))[None, None, None]
    tt = tf + tb + eye_t * dmat
    nb, gb = S5_NBLK, S5_GB
    one = np.ones
    eye = lambda k: np.eye(k, dtype=np.float32)
    exp_t = jnp.asarray(np.einsum('tu,pq,b->tpubq', eye(t), eye(p), one(gb)).reshape(t * p, S5_K))
    exp_n = jnp.asarray(np.einsum('nm,b->nbm', eye(n), one(gb)).reshape(n, S5_PART))
    mask_t = jnp.asarray(np.einsum('ab,s,q,u,p->saqubp', eye(gb), one(t), one(p), one(t), one(p))
                         .reshape(S5_K, S5_K))
    mask_ba = jnp.asarray(np.einsum('ab,s,q,n->saqbn', eye(gb), one(t), one(p), one(n)).reshape(S5_K, S5_PART))
    t_small = tt.reshape(nb, gb, t, t, p, p).transpose(0, 2, 1, 5, 3, 4).reshape(nb, S5_K, t * p)
    tmat = jnp.matmul(t_small, exp_t, precision=hp) * mask_t

    def ba(d, powers):
        pr = pw_re[d][:, :, powers]
        pi = pw_im[d][:, :, powers]
        re = pr[..., None] * bb_re[d][:, :, None, :] - pi[..., None] * bb_im[d][:, :, None, :]
        im = pr[..., None] * bb_im[d][:, :, None, :] + pi[..., None] * bb_re[d][:, :, None, :]
        blk = lambda m: jnp.matmul(m.reshape(nb, gb, n, t, p).transpose(0, 3, 1, 4, 2).reshape(nb, S5_K, n),
                                   exp_n, precision=hp) * mask_ba
        return blk(re), blk(im)

    baf_re, baf_im = ba(0, np.arange(t - 1, -1, -1))
    bab_re, bab_im = ba(1, np.arange(t))
    bamat = jnp.concatenate([baf_re, bab_re, baf_im, bab_im], axis=-1)

    def ca(d, powers):
        blk = lambda m: jnp.matmul(exp_n.T, m[..., powers].reshape(nb, gb, p, n, t).transpose(0, 3, 4, 1, 2)
                                   .reshape(nb, n, S5_K), precision=hp) * mask_ba.T
        return blk(ca_re[d]), blk(ca_im[d])

    caf_re, caf_im = ca(0, np.arange(1, t + 1))
    cab_re, cab_im = ca(1, np.arange(t, 0, -1))
    camat = jnp.concatenate([caf_re, cab_re, -caf_im, -cab_im], axis=1)
    part = lambda m: m[..., t].reshape(nb, S5_PART)
    a_pow = jnp.stack([jnp.concatenate([part(pw_re[0]), part(pw_re[1])], axis=-1),
                       jnp.concatenate([part(pw_im[0]), part(pw_im[1])], axis=-1)], axis=1)
    return tmat.astype(BF16), bamat.astype(BF16), camat.astype(BF16), a_pow


def _s5_kernel(u_ref, t_ref, ba_ref, ca_ref, a_ref, h0_ref, y_ref, hf_ref, ub_scr, s_scr, hin_scr, y_scr,
               *, nc, bt):
    t, w, hp = S5_CHUNK, S5_GB * S5_GROUP_DIM, S5_PART
    for bi in range(bt):
        for tau in range(t):
            ub_scr[tau, pl.ds(bi, nc, stride=bt), :] = u_ref[bi, pl.ds(tau, nc, stride=t), :]
    ub = jnp.concatenate([ub_scr[tau] for tau in range(t)], axis=1).astype(BF16)
    s_scr[...] = _dot(ub, ba_ref[0, 0])
    ar_f, ar_b = a_ref[0, 0, 0:1, 0:hp], a_ref[0, 0, 0:1, hp:2 * hp]
    ai_f, ai_b = a_ref[0, 0, 1:2, 0:hp], a_ref[0, 0, 1:2, hp:2 * hp]

    def step(i, carry):
        re_f, im_f, re_b, im_b = carry
        rf = pl.ds(pl.multiple_of(i * bt, bt), bt)
        rb = pl.ds(pl.multiple_of((nc - 1 - i) * bt, bt), bt)
        hin_scr[rf, 0:hp] = re_f
        hin_scr[rf, 2 * hp:3 * hp] = im_f
        hin_scr[rb, hp:2 * hp] = re_b
        hin_scr[rb, 3 * hp:4 * hp] = im_b
        n_re_f = ar_f * re_f - ai_f * im_f + s_scr[rf, 0:hp]
        n_im_f = ar_f * im_f + ai_f * re_f + s_scr[rf, 2 * hp:3 * hp]
        n_re_b = ar_b * re_b - ai_b * im_b + s_scr[rb, hp:2 * hp]
        n_im_b = ar_b * im_b + ai_b * re_b + s_scr[rb, 3 * hp:4 * hp]
        return n_re_f, n_im_f, n_re_b, n_im_b

    h0 = h0_ref[...].reshape(bt, 4 * hp)
    init = (h0[:, 0:hp], h0[:, 2 * hp:3 * hp], h0[:, hp:2 * hp], h0[:, 3 * hp:4 * hp])
    fin = lax.fori_loop(0, nc, step, init, unroll=8)
    hf_ref[...] = jnp.concatenate([fin[0], fin[2], fin[1], fin[3]], axis=1).reshape(hf_ref.shape)
    y = _dot(ub, t_ref[0, 0]) + _dot(hin_scr[...].astype(BF16), ca_ref[0, 0])
    for tau in range(t):
        y_scr[tau] = y[:, tau * w:(tau + 1) * w]
    for bi in range(bt):
        for tau in range(t):
            y_ref[bi, pl.ds(tau, nc, stride=t), :] = y_scr[tau, pl.ds(bi, nc, stride=bt), :]


def _s5_core(ub, tmat, bamat, camat, a_pow, h0, layer):
    b, l, _ = ub.shape
    nc = l // S5_CHUNK
    w = S5_GB * S5_GROUP_DIM
    bt = b if nc * b <= S5_MAX_ROWS else 1
    if bt == 1:
        h0 = h0.reshape(S5_NBLK, b, 1, 4 * S5_PART)
        st_spec = pl.BlockSpec((1, 1, 1, 4 * S5_PART), lambda j, bi: (j, bi, 0, 0))
    else:
        st_spec = pl.BlockSpec((1, bt, 4 * S5_PART), lambda j, bi: (j, bi, 0))
    rows = nc * bt
    y, hf = pl.pallas_call(
        functools.partial(_s5_kernel, nc=nc, bt=bt),
        grid=(S5_NBLK, b // bt),
        in_specs=[pl.BlockSpec((bt, l, w), lambda j, bi: (bi, 0, j)),
                  pl.BlockSpec((1, 1, S5_K, S5_K), lambda j, bi: (layer, j, 0, 0)),
                  pl.BlockSpec((1, 1, S5_K, 4 * S5_PART), lambda j, bi: (layer, j, 0, 0)),
                  pl.BlockSpec((1, 1, 4 * S5_PART, S5_K), lambda j, bi: (layer, j, 0, 0)),
                  pl.BlockSpec((1, 1, 2, 2 * S5_PART), lambda j, bi: (layer, j, 0, 0)),
                  st_spec],
        out_specs=[pl.BlockSpec((bt, l, w), lambda j, bi: (bi, 0, j)), st_spec],
        out_shape=[jax.ShapeDtypeStruct((b, l, S5_WIDTH), F32),
                   jax.ShapeDtypeStruct(h0.shape, F32)],
        scratch_shapes=[pltpu.VMEM((S5_CHUNK, rows, w), F32), pltpu.VMEM((rows, 4 * S5_PART), F32),
                        pltpu.VMEM((rows, 4 * S5_PART), F32), pltpu.VMEM((S5_CHUNK, rows, w), F32)],
        compiler_params=_cparams(2),
        name="s5_core",
    )(ub, tmat, bamat, camat, a_pow, h0)
    return y, hf.reshape(S5_NBLK, b, 4 * S5_PART)


def _s5_glu_kernel(y_ref, g_ref, w_ref, b_ref, o_ref):
    y = y_ref[0]
    y = 0.5 * y * (1.0 + jnp.tanh(math.sqrt(2.0 / math.pi) * (y + 0.044715 * (y * y * y))))
    z = _dot(y.astype(BF16), w_ref[...].astype(BF16)) + b_ref[...]
    o_ref[0] = (y * _sigmoid(z) * _silu(g_ref[0].astype(F32))).astype(BF16)


def _s5_glu(y, proj, glu_w, glu_b, tm):
    b, l, w = y.shape
    return pl.pallas_call(
        _s5_glu_kernel,
        grid=(b, l // tm),
        in_specs=[pl.BlockSpec((1, tm, w), lambda bi, i: (bi, i, 0)),
                  pl.BlockSpec((1, tm, w), lambda bi, i: (bi, i, COL_GB // w)),
                  pl.BlockSpec((w, w), lambda bi, i: (0, 0)),
                  pl.BlockSpec((1, w), lambda bi, i: (0, 0))],
        out_specs=pl.BlockSpec((1, tm, w), lambda bi, i: (bi, i, 0)),
        out_shape=jax.ShapeDtypeStruct((b, l, w), BF16),
        compiler_params=_cparams(2),
        name="s5_glu",
    )(y, proj, glu_w, glu_b.reshape(1, w))


def _s5_branch(proj, ub, mats, layer, glu_w, glu_b, h0_re, h0_im):
    b, l, _ = proj.shape
    g, n = S5_GROUPS, S5_STATE
    if h0_re is None:
        h0 = jnp.zeros((S5_NBLK, b, 4 * S5_PART), F32)
    else:
        blk = lambda h: h.astype(F32).reshape(b, S5_NBLK, S5_PART).transpose(1, 0, 2)
        h0 = jnp.concatenate([blk(h0_re[:, 0]), blk(h0_re[:, 1]), blk(h0_im[:, 0]), blk(h0_im[:, 1])], axis=-1)
    y, hf = _s5_core(ub, *mats, h0, layer)
    out = _s5_glu(y, proj, glu_w, glu_b, min(l, S5_GLU_TM))
    part = lambda i: hf[..., i * S5_PART:(i + 1) * S5_PART].transpose(1, 0, 2).reshape(b, g, n)
    fin_re = jnp.stack([part(0), part(1)], axis=1)
    fin_im = jnp.stack([part(2), part(3)], axis=1)
    return out, fin_re, fin_im


def _rope_tables(l):
    rows = l // GRID_W
    row = jnp.broadcast_to(jnp.arange(rows, dtype=F32)[:, None], (rows, GRID_W)).reshape(-1)
    col = jnp.broadcast_to(jnp.arange(GRID_W, dtype=F32)[None, :], (rows, GRID_W)).reshape(-1)
    half = HEAD_DIM // 2
    inv = ROPE_THETA ** (-jnp.arange(0, half, 2, dtype=F32) / half)
    ar = row[:, None] * inv
    ac = col[:, None] * inv
    cos = jnp.concatenate([jnp.cos(ar), jnp.cos(ar), jnp.cos(ac), jnp.cos(ac)], axis=1)
    sin = jnp.concatenate([-jnp.sin(ar), jnp.sin(ar), -jnp.sin(ac), jnp.sin(ac)], axis=1)
    return cos, sin


def _qk_kernel(*refs, rope, want_kf):
    q_ref, k_ref, v_ref, qn_ref, kn_ref = refs[:5]
    pos = 5
    if rope:
        cos_ref, sin_ref = refs[pos:pos + 2]
        pos += 2
    qo_ref, ko_ref, vo_ref = refs[pos:pos + 3]
    pos += 3
    if want_kf:
        kf_ref = refs[pos]
    hd = HEAD_DIM
    tm = q_ref.shape[1]
    quarter = hd // 4
    if rope:
        cos = cos_ref[...]
        sin = sin_ref[...]
        first = (lax.broadcasted_iota(jnp.int32, (tm, hd), 1) % (2 * quarter)) < quarter

    def norm(x, gain):
        ms = jnp.mean(x * x, axis=-1, keepdims=True)
        return x * lax.rsqrt(ms + NORM_EPS) * gain

    def rot(x):
        if not rope:
            return x
        nxt = pltpu.roll(x, hd - quarter, 1)
        prv = pltpu.roll(x, quarter, 1)
        return x * cos + jnp.where(first, nxt, prv) * sin

    qscale = (hd ** -0.5) * LOG2E
    for h in range(N_HEADS):
        x = norm(q_ref[0, :, h * hd:(h + 1) * hd].astype(F32), qn_ref[...])
        qo_ref[0, :, h * hd:(h + 1) * hd] = (rot(x) * qscale).astype(BF16)
    for h in range(N_KV_HEADS):
        x = norm(k_ref[0, :, h * hd:(h + 1) * hd].astype(F32), kn_ref[...])
        if want_kf:
            kf_ref[0, :, h * hd:(h + 1) * hd] = x
        ko_ref[0, :, h * hd:(h + 1) * hd] = rot(x).astype(BF16)
        vo_ref[0, h] = v_ref[0, :, h * hd:(h + 1) * hd].astype(F32).T.astype(BF16)


def _qk_prep(proj, q_norm, k_norm, rope_tabs, want_kf, tm):
    b, l, _ = proj.shape
    rope = rope_tabs is not None
    in_specs = [pl.BlockSpec((1, tm, ATTN_WIDTH), lambda bi, i: (bi, i, COL_QC // ATTN_WIDTH)),
                pl.BlockSpec((1, tm, KV_WIDTH), lambda bi, i: (bi, i, COL_KC // KV_WIDTH)),
                pl.BlockSpec((1, tm, KV_WIDTH), lambda bi, i: (bi, i, COL_VC // KV_WIDTH)),
                pl.BlockSpec((1, HEAD_DIM), lambda bi, i: (0, 0)),
                pl.BlockSpec((1, HEAD_DIM), lambda bi, i: (0, 0))]
    args = [proj, proj, proj, q_norm.reshape(1, HEAD_DIM), k_norm.reshape(1, HEAD_DIM)]
    if rope:
        in_specs += [pl.BlockSpec((tm, HEAD_DIM), lambda bi, i: (i, 0))] * 2
        args += list(rope_tabs)
    out_specs = [pl.BlockSpec((1, tm, ATTN_WIDTH), lambda bi, i: (bi, i, 0)),
                 pl.BlockSpec((1, tm, KV_WIDTH), lambda bi, i: (bi, i, 0)),
                 pl.BlockSpec((1, N_KV_HEADS, HEAD_DIM, tm), lambda bi, i: (bi, 0, 0, i))]
    out_shape = [jax.ShapeDtypeStruct((b, l, ATTN_WIDTH), BF16),
                 jax.ShapeDtypeStruct((b, l, KV_WIDTH), BF16),
                 jax.ShapeDtypeStruct((b, N_KV_HEADS, HEAD_DIM, l), BF16)]
    if want_kf:
        out_specs.append(pl.BlockSpec((1, tm, KV_WIDTH), lambda bi, i: (bi, i, 0)))
        out_shape.append(jax.ShapeDtypeStruct((b, l, KV_WIDTH), F32))
    return pl.pallas_call(
        functools.partial(_qk_kernel, rope=rope, want_kf=want_kf),
        grid=(b, l // tm),
        in_specs=in_specs, out_specs=out_specs, out_shape=out_shape,
        compiler_params=_cparams(2),
        name="qk_prep",
    )(*args)


def _attn_kernel(*refs, has_ctx):
    s_scr = refs[-1]
    q_ref, k_ref, v_ref = refs[:3]
    if has_ctx:
        ck_ref, cv_ref, g_ref, o_ref = refs[3:7]
        ck = ck_ref[0, 0].astype(BF16)
        cvt = cv_ref[0, 0, 0].astype(BF16)
        past = ck.shape[0]
    else:
        g_ref, o_ref = refs[3:5]
    hd = HEAD_DIM
    tq = q_ref.shape[1]
    lk = k_ref.shape[1]
    cr = min(lk, ATTN_KEY_CHUNK)
    chunks = [(c * cr, cr, (lambda c=c: k_ref[0, c * cr:(c + 1) * cr, :]),
               (lambda c=c: v_ref[0, 0, :, c * cr:(c + 1) * cr])) for c in range(lk // cr)]
    if has_ctx:
        chunks.append((lk, past, lambda: ck, lambda: cvt))
    qs = [q_ref[0, :, g * hd:(g + 1) * hd] for g in range(Q_PER_KV)]
    gate = _silu(g_ref[0].astype(F32))

    def score(g, chunk, m):
        off, rows, keys, _ = chunk
        s = _dot_nt(keys(), qs[g])
        s_scr[g % 2, off:off + rows, :] = s
        return jnp.maximum(m, jnp.max(s, axis=0, keepdims=True))

    def accumulate(g, chunk, m, den, ot):
        off, rows, _, values_t = chunk
        p = jnp.exp2(s_scr[g % 2, off:off + rows, :] - m)
        return den + jnp.sum(p, axis=0, keepdims=True), ot + _dot(values_t(), p.astype(BF16))

    neg = jnp.full((1, tq), -jnp.inf, F32)
    m_cur = neg
    for chunk in chunks:
        m_cur = score(0, chunk, m_cur)
    for g in range(Q_PER_KV):
        m_next = neg
        den = jnp.zeros((1, tq), F32)
        ot = jnp.zeros((hd, tq), F32)
        for chunk in chunks:
            den, ot = accumulate(g, chunk, m_cur, den, ot)
            if g + 1 < Q_PER_KV:
                m_next = score(g + 1, chunk, m_next)
        m_cur = m_next
        o = (ot / den).T
        o_ref[0, :, g * hd:(g + 1) * hd] = (o * gate[:, g * hd:(g + 1) * hd]).astype(BF16)


def _attention(qs, kr, vt, proj, ctx_k, ctx_v, layer, tq):
    b, l, _ = qs.shape
    gw = Q_PER_KV * HEAD_DIM
    has_ctx = ctx_k is not None
    in_specs = [pl.BlockSpec((1, tq, gw), lambda bi, h, i: (bi, i, h)),
                pl.BlockSpec((1, l, HEAD_DIM), lambda bi, h, i: (bi, 0, h)),
                pl.BlockSpec((1, 1, HEAD_DIM, l), lambda bi, h, i: (bi, h, 0, 0))]
    args = [qs, kr, vt]
    if has_ctx:
        past = ctx_k.shape[2]
        in_specs += [pl.BlockSpec((1, 1, past, HEAD_DIM), lambda bi, h, i: (bi, layer, 0, h)),
                     pl.BlockSpec((1, 1, 1, HEAD_DIM, past), lambda bi, h, i: (bi, layer, h, 0, 0))]
        args += [ctx_k, ctx_v]
    in_specs.append(pl.BlockSpec((1, tq, gw), lambda bi, h, i: (bi, i, COL_GC // gw + h)))
    args.append(proj)
    return pl.pallas_call(
        functools.partial(_attn_kernel, has_ctx=has_ctx),
        grid=(b, N_KV_HEADS, l // tq),
        in_specs=in_specs,
        out_specs=pl.BlockSpec((1, tq, gw), lambda bi, h, i: (bi, i, h)),
        out_shape=jax.ShapeDtypeStruct((b, l, ATTN_WIDTH), BF16),
        scratch_shapes=[pltpu.VMEM((2, l + (ctx_k.shape[2] if has_ctx else 0), tq), F32)],
        compiler_params=_cparams(3),
        name="attention",
    )(*args)


HGRN_COARSE = (32, 16, 8, 4)
HGRN_FINE = (2,)
HGRN_LEVELS = HGRN_COARSE + HGRN_FINE + (1,)
HGRN_CHUNKS_PER_TRIP = 4


def _hgrn_tables():
    c = HGRN_CHUNK
    t = np.arange(c)
    w01, sm = [], []
    for d in range(2):
        cum = (t[None, :] <= t[:, None]) if d == 0 else (t[None, :] >= t[:, None])
        cum = cum.astype(np.float32)
        w01.append(np.concatenate([cum] + [cum[_hgrn_ref_row(t, hs, d)] for hs in HGRN_FINE], axis=0))
        sd = []
        for hs in HGRN_LEVELS:
            is_q = _hgrn_is_query(t, hs, d)
            same = (t[:, None] // (2 * hs)) == (t[None, :] // (2 * hs))
            sd.append((same & is_q[:, None] & ~is_q[None, :]).astype(np.float32))
        sd.append(np.eye(c, dtype=np.float32))
        sm.append(np.stack(sd))
    return jnp.asarray(np.stack(w01), BF16), jnp.asarray(np.stack(sm), F32)


def _hgrn_ref_row(t, hs, d):
    base = t & ~(2 * hs - 1)
    return base + hs - 1 if d == 0 else base + hs


def _hgrn_is_query(t, hs, d):
    return ((t & hs) != 0) if d == 0 else ((t & hs) == 0)


def _hgrn_cums(z, lbv, w01):
    f = lbv + (1.0 - lbv) * _sigmoid(z)
    fm = jnp.maximum(f, F_FLOOR)
    hi, lo = _split2(jnp.log2(fm))
    return fm, 1.0 - f, _dot(w01, hi) + _dot(w01, lo)


def _hgrn_products(q, k, fm, allc, rows, d):
    c = HGRN_CHUNK
    sub = 8
    cum = allc[0:c]
    refs = {}
    for hs in HGRN_COARSE:
        pieces = []
        for j in range(c // sub):
            rr = int(_hgrn_ref_row(np.int64(j * sub), hs, d))
            pieces.append(jnp.broadcast_to(cum[rr:rr + 1, :], (sub, HGRN_DK)))
        refs[hs] = jnp.concatenate(pieces, axis=0)
    for i, hs in enumerate(HGRN_FINE):
        refs[hs] = allc[(i + 1) * c:(i + 2) * c]
    lrow = c - 1 if d == 0 else 0
    last = cum[lrow:lrow + 1, :]
    raws = [_dot_nt(q.astype(BF16), k.astype(BF16))]
    for hs in HGRN_LEVELS:
        is_q = ((rows & hs) != 0) if d == 0 else ((rows & hs) == 0)
        if hs == 1:
            x = jnp.where(is_q, q * fm, k).astype(BF16)
        else:
            x = (jnp.where(is_q, q, k) * jnp.exp2((cum - refs[hs]) * jnp.where(is_q, 1.0, -1.0))).astype(BF16)
        raws.append(_dot_nt(x, x))
    qi = (q * jnp.exp2(cum)).astype(BF16)
    ks = (k * jnp.exp2(last - cum)).astype(BF16)
    return raws, qi, ks, jnp.exp2(last)


def _hgrn_kernel(*refs, nc, has_s0, want_state):
    q_ref, v_ref, zf_ref, zb_ref, g_ref, lb_ref, ng_ref, w01_ref, sm_ref = refs[:9]
    pos = 9
    if has_s0:
        s0_ref = refs[pos]
        pos += 1
    o_ref = refs[pos]
    pos += 1
    if want_state:
        sf_ref = refs[pos]
        pos += 1
    o_scr, st_scr = refs[pos:pos + 2]
    c = HGRN_CHUNK
    l = nc * c
    for d in range(2):
        if has_s0:
            st_scr[d] = s0_ref[0, d, 0]
        else:
            st_scr[d] = jnp.zeros((HGRN_DV, HGRN_DK), F32)
    rows = lax.broadcasted_iota(jnp.int32, (c, HGRN_DK), 0)

    per_trip = math.gcd(nc, HGRN_CHUNKS_PER_TRIP)
    ntrips = nc // per_trip
    nl = len(HGRN_LEVELS)

    def instances(i):
        insts = []
        for u in range(per_trip):
            for d in range(2):
                ci = per_trip * i + u if d == 0 else nc - 1 - (per_trip * i + u)
                r = ci * c
                insts.append((d, r if isinstance(r, int) else pl.multiple_of(r, c)))
        return insts

    def emit(tot, r, n):
        ms = jnp.mean(tot * tot, axis=-1, keepdims=True)
        y = tot * lax.rsqrt(ms + NORM_EPS) * ng_ref[...]
        o_ref[0, pl.ds(r, n), :] = (y * _silu(g_ref[0, pl.ds(r, n), :].astype(F32))).astype(BF16)

    def state_step(inst, local, finalize):
        (d, r), (scores, qi, ks, dec) = inst, local
        vb = v_ref[0, pl.ds(r, c), :].astype(BF16)
        st = st_scr[d]
        out = _dot(scores, vb) + _dot_nt(qi, st.astype(BF16))
        st_scr[d] = dec * st + _dot_tn(vb, ks)
        if finalize:
            emit(out + o_scr[1 - d, pl.ds(r, c), :], r, c)
        else:
            o_scr[d, pl.ds(r, c), :] = out

    def trip(i, prev, finalize):
        insts = instances(i)
        prev_insts = instances(i - 1) if prev is not None else None
        gates = [_hgrn_cums((zf_ref, zb_ref)[d][0, pl.ds(r, c), :].astype(F32), lb_ref[d, 0], w01_ref[d])
                 for d, r in insts]
        out = []
        for idx, ((d, r), (fm, k, allc)) in enumerate(zip(insts, gates)):
            raws, qi, ks, dec = _hgrn_products(q_ref[0, pl.ds(r, c), :].astype(F32), k, fm, allc, rows, d)
            if prev is not None:
                state_step(prev_insts[idx], prev[idx], finalize)
            scores = sm_ref[d, nl] * raws[0]
            for li in range(nl):
                scores = scores + sm_ref[d, li] * raws[li + 1]
            out.append((scores.astype(BF16), qi, ks, dec))
        return tuple(out)

    fused = ntrips % 2 == 0
    half = ntrips // 2 if fused else ntrips
    local = lax.fori_loop(1, half + 1 if fused else ntrips, functools.partial(trip, finalize=False),
                          trip(0, None, False))
    if fused:
        local = lax.fori_loop(half + 1, ntrips, functools.partial(trip, finalize=True), local)
    for inst, loc in zip(instances(ntrips - 1), local):
        state_step(inst, loc, fused)
    if want_state:
        for d in range(2):
            sf_ref[0, d, 0] = st_scr[d]

    if not fused:
        fb = min(l, 256)

        def finish(i, carry):
            r = pl.multiple_of(i * fb, fb)
            emit(o_scr[0, pl.ds(r, fb), :] + o_scr[1, pl.ds(r, fb), :], r, fb)
            return carry

        lax.fori_loop(0, l // fb, finish, 0)


def _hgrn_branch(proj, lower_bound, norm_g, tables, s0_t, want_state):
    b, l, _ = proj.shape
    nc = l // HGRN_CHUNK
    w = HGRN_DK
    has_s0 = s0_t is not None
    w01, sm = tables

    def col(off):
        return lambda bi, h: (bi, 0, off // w + h)

    in_specs = [pl.BlockSpec((1, l, w), col(COL_QD)),
                pl.BlockSpec((1, l, w), col(COL_ID)),
                pl.BlockSpec((1, l, w), col(COL_ZF)),
                pl.BlockSpec((1, l, w), col(COL_ZB)),
                pl.BlockSpec((1, l, w), col(COL_GD)),
                pl.BlockSpec((2, 1, 1, w), lambda bi, h: (0, h, 0, 0)),
                pl.BlockSpec((1, w), lambda bi, h: (0, 0)),
                pl.BlockSpec(w01.shape, lambda bi, h: (0, 0, 0)),
                pl.BlockSpec(sm.shape, lambda bi, h: (0, 0, 0, 0))]
    args = [proj, proj, proj, proj, proj, lower_bound.reshape(2, HGRN_HEADS, 1, w), norm_g.reshape(1, w),
            w01, sm]
    st_spec = pl.BlockSpec((1, 2, 1, HGRN_DV, HGRN_DK), lambda bi, h: (bi, 0, h, 0, 0))
    if has_s0:
        in_specs.append(st_spec)
        args.append(s0_t)
    out_specs = [pl.BlockSpec((1, l, w), lambda bi, h: (bi, 0, h))]
    out_shape = [jax.ShapeDtypeStruct((b, l, HGRN_HEADS * HGRN_DV), BF16)]
    if want_state:
        out_specs.append(st_spec)
        out_shape.append(jax.ShapeDtypeStruct((b, 2, HGRN_HEADS, HGRN_DV, HGRN_DK), F32))
    res = pl.pallas_call(
        functools.partial(_hgrn_kernel, nc=nc, has_s0=has_s0, want_state=want_state),
        grid=(b, HGRN_HEADS),
        in_specs=in_specs, out_specs=out_specs, out_shape=out_shape,
        scratch_shapes=[pltpu.VMEM((2, l, HGRN_DV), F32), pltpu.VMEM((2, HGRN_DV, HGRN_DK), F32)],
        compiler_params=_cparams(2),
        name="hgrn",
    )(*args)
    return res if want_state else (res[0], None)


def _final_kernel(x_ref, gt_ref, ma_ref, mb_ref, mc_ref, md_ref, ya_ref, yb_ref, yc_ref, yd_ref,
                  wa_ref, wb_ref, wc_ref, wd_ref, wo_ref, gp_ref, o_ref, *, nct):
    ci = pl.program_id(2)
    tm = o_ref.shape[1]
    subs = [slice(r0, r0 + FINAL_ROWS) for r0 in range(0, tm, FINAL_ROWS)]
    m_refs = (ma_ref, mb_ref, mc_ref, md_ref)
    y_refs = (ya_ref, yb_ref, yc_ref, yd_ref)
    w_refs = (wa_ref, wb_ref, wc_ref, wd_ref)

    def branch_dots(rs):
        return [_dot(y_ref[0, rs, :], w_ref[0]) for y_ref, w_ref in zip(y_refs, w_refs)]

    def step(first, last):
        nxt = branch_dots(subs[0])
        for si, rs in enumerate(subs):
            branch = nxt
            if si + 1 < len(subs):
                nxt = branch_dots(subs[si + 1])
            mixed = None
            for m_ref, dot in zip(m_refs, branch):
                term = _gate_sigmoid(m_ref[0, rs, :].astype(F32)) * dot
                mixed = term if mixed is None else mixed + term
            contrib = _dot(mixed.astype(BF16), wo_ref[0])
            out = contrib if first else o_ref[0, rs, :] + contrib
            if last:
                ms = jnp.mean(out * out, axis=-1, keepdims=True)
                y = out * lax.rsqrt(ms + NORM_EPS) * gp_ref[...]
                o_ref[0, rs, :] = x_ref[0, rs, :] + gt_ref[0] * y
            else:
                o_ref[0, rs, :] = out

    if nct == 1:
        step(True, True)
    else:
        pl.when(ci == 0)(lambda: step(True, False))
        pl.when(jnp.logical_and(ci > 0, ci < nct - 1))(lambda: step(False, False))
        pl.when(ci == nct - 1)(lambda: step(False, True))


def _final(x, gate, proj, ys, w_projs, w_out, norm_post, layer, tm, tc):
    b, l, d = x.shape
    nct = d // tc
    per_batch = gate.shape[0] > 1
    gate_map = (lambda bi, i, c: (bi, 0, 0)) if per_batch else (lambda bi, i, c: (0, 0, 0))

    def mspec(j):
        return pl.BlockSpec((1, tm, tc), lambda bi, i, c: (bi, i, (COL_M + j * d) // tc + c))

    in_specs = [pl.BlockSpec((1, tm, d), lambda bi, i, c: (bi, i, 0)),
                pl.BlockSpec((1, 1, d), gate_map)]
    in_specs += [mspec(j) for j in range(N_BRANCHES)]
    in_specs += [pl.BlockSpec((1, tm, y.shape[-1]), lambda bi, i, c: (bi, i, 0)) for y in ys]
    in_specs += [pl.BlockSpec((1, w.shape[1], tc), lambda bi, i, c: (layer, 0, c)) for w in w_projs]
    in_specs += [pl.BlockSpec((1, tc, d), lambda bi, i, c: (layer, c, 0)),
                 pl.BlockSpec((1, d), lambda bi, i, c: (0, 0))]
    return pl.pallas_call(
        functools.partial(_final_kernel, nct=nct),
        grid=(b, l // tm, nct),
        in_specs=in_specs,
        out_specs=pl.BlockSpec((1, tm, d), lambda bi, i, c: (bi, i, 0)),
        out_shape=jax.ShapeDtypeStruct((b, l, d), F32),
        compiler_params=pltpu.CompilerParams(dimension_semantics=("arbitrary",) * 3,
                                             vmem_limit_bytes=FINAL_VMEM_LIMIT),
        name="merge_out",
    )(x, gate, proj, proj, proj, proj, *ys, *w_projs, w_out, norm_post.reshape(1, d))


def _trunk_layer(x, shift, scale, gate, p, ctx, layer, consts):
    b, l, d = x.shape
    shared_mod = shift.shape[0] == 1
    if shared_mod:
        proj, ub = _inproj(x.reshape(1, b * l, d), shift, scale, p['norm_pre'], p['w_in'], layer,
                           INPROJ_TM, INPROJ_TN)
        proj = proj.reshape(b, l, IN_COLS)
        ub = ub.reshape(b, l, S5_WIDTH)
    else:
        proj, ub = _inproj(x, shift, scale, p['norm_pre'], p['w_in'], layer, INPROJ_TM, INPROJ_TN)

    dl, cs = consts['dft'][l]
    y_a = _fourier_branch(proj, p['fourier_w'], dl, cs)

    if ctx is None:
        y_b, s5_re, s5_im = _s5_branch(proj, ub, p['s5_mats'], layer, p['s5_glu_w'], p['s5_glu_b'], None, None)
        qs, kr, vt, kf = _qk_prep(proj, p['q_norm'], p['k_norm'], None, True, min(l, QK_PREP_TM))
        y_c = _attention(qs, kr, vt, proj, None, None, layer, min(l, ATTN_TQ))
        y_d, hg = _hgrn_branch(proj, p['lower_bound'], p['hgrn_norm'], consts['hgrn'], None, True)
    else:
        cache_k, cache_v, st_re, st_im, s0_t = ctx
        y_b, _, _ = _s5_branch(proj, ub, p['s5_mats'], layer, p['s5_glu_w'], p['s5_glu_b'],
                               st_re[:, layer], st_im[:, layer])
        qs, kr, vt = _qk_prep(proj, p['q_norm'], p['k_norm'], consts['rope'], False, min(l, QK_PREP_TM))
        y_c = _attention(qs, kr, vt, proj, cache_k, cache_v, layer, min(l, ATTN_TQ))
        y_d, _ = _hgrn_branch(proj, p['lower_bound'], p['hgrn_norm'], consts['hgrn'], s0_t[:, layer], False)

    ys = (y_a, y_b, y_c, y_d)
    wps = (p['w_proj_a'], p['w_proj_b'], p['w_proj_c'], p['w_proj_d'])
    if shared_mod:
        n = b * l
        x_new = _final(x.reshape(1, n, d), gate, proj.reshape(1, n, IN_COLS),
                       tuple(y.reshape(1, n, y.shape[-1]) for y in ys), wps, p['w_out'], p['norm_post'], layer,
                       FINAL_TM, FINAL_TC)
        x_new = x_new.reshape(b, l, d)
    else:
        x_new = _final(x, gate, proj, ys, wps, p['w_out'], p['norm_post'], layer, FINAL_TM, FINAL_TC)

    if ctx is None:
        v_f = proj[:, :, COL_VC:COL_VC + KV_WIDTH].astype(F32)
        return x_new, (kf, v_f, s5_re, s5_im, hg)
    return x_new, None


def kernel(x_prompt, x_sample, c, cache_k, cache_v, state_s5_re, state_s5_im, state_hgrn, c_ctx,
           norm_pre, norm_post, w_mod, b_mod, w_in, fourier_w,
           s5_lambda_re, s5_lambda_im, s5_log_step, s5_b_re, s5_b_im, s5_c_re, s5_c_im,
           s5_d, s5_glu_w, s5_glu_b, q_norm, k_norm, hgrn_lb_logits, hgrn_norm,
           w_proj_a, w_proj_b, w_proj_c, w_proj_d, w_out):
    depth = w_in.shape[0]
    bp, lp, d = x_prompt.shape
    bs, ls, _ = x_sample.shape

    lb_w = jax.nn.softmax(hgrn_lb_logits.astype(F32), axis=0)
    lower_bounds = jnp.cumsum(lb_w, axis=0) - lb_w[0]

    rows = ((bs + 1 + 7) // 8) * 8
    c_all = jnp.zeros((rows, d), F32).at[:bs].set(c).at[bs].set(c_ctx)
    mod = _modulation(c_all, w_mod, b_mod)

    w_in_bf = w_in.astype(BF16)
    wpa, wpb, wpc, wpd = (w.astype(BF16) for w in (w_proj_a, w_proj_b, w_proj_c, w_proj_d))
    w_out_bf = w_out.astype(BF16)

    consts = {'dft': {lp: _dft_tables(lp), ls: _dft_tables(ls)},
              'rope': _rope_tables(ls),
              'hgrn': _hgrn_tables()}
    past = cache_k.shape[2]
    ctx_k = cache_k.reshape(bs, depth, past, KV_WIDTH)
    ctx_v = cache_v.transpose(0, 1, 3, 4, 2)
    s0_t = jnp.swapaxes(state_hgrn.astype(F32), -1, -2)

    s5_mats = jax.vmap(_s5_matrices)(s5_lambda_re, s5_lambda_im, s5_log_step, s5_b_re, s5_b_im, s5_c_re, s5_c_im,
                                     s5_d.reshape(depth, S5_GROUPS, S5_GROUP_DIM))

    y_p, y_s = x_prompt, x_sample
    ks, vs, s5r, s5i, hg = [], [], [], [], []
    for l in range(depth):
        p = {'norm_pre': norm_pre[l].reshape(1, d), 'norm_post': norm_post[l], 'w_in': w_in_bf,
             'fourier_w': fourier_w[l],
             's5_mats': s5_mats,
             's5_glu_w': s5_glu_w[l], 's5_glu_b': s5_glu_b[l],
             'q_norm': q_norm[l], 'k_norm': k_norm[l],
             'lower_bound': lower_bounds[l], 'hgrn_norm': hgrn_norm[l],
             'w_proj_a': wpa, 'w_proj_b': wpb, 'w_proj_c': wpc, 'w_proj_d': wpd,
             'w_out': w_out_bf}
        m_ctx = mod[l, bs].reshape(1, 1, 3 * d)
        sh, sc, gt = (m_ctx[..., i * d:(i + 1) * d] for i in range(3))
        y_p, (k_l, v_l, sr_l, si_l, hg_l) = _trunk_layer(y_p, sh, sc, gt, p, None, l, consts)
        ks.append(k_l.reshape(bp, lp, N_KV_HEADS, HEAD_DIM))
        vs.append(v_l.reshape(bp, lp, N_KV_HEADS, HEAD_DIM))
        s5r.append(sr_l)
        s5i.append(si_l)
        hg.append(jnp.swapaxes(hg_l, -1, -2))
        m_s = mod[l, :bs].reshape(bs, 1, 3 * d)
        sh, sc, gt = (m_s[..., i * d:(i + 1) * d] for i in range(3))
        ctx = (ctx_k, ctx_v, state_s5_re, state_s5_im, s0_t)
        y_s, _ = _trunk_layer(y_s, sh, sc, gt, p, ctx, l, consts)

    return (y_p, y_s, jnp.stack(ks, axis=1), jnp.stack(vs, axis=1),
            jnp.stack(s5r, axis=1), jnp.stack(s5i, axis=1), jnp.stack(hg, axis=1))
```

```python
import functools
import math

import jax
import jax.numpy as jnp
import numpy as np
from jax import lax
from jax.experimental import pallas as pl
from jax.experimental.pallas import tpu as pltpu

F32 = jnp.float32
BF16 = jnp.bfloat16

NORM_EPS = 1e-6
F_FLOOR = 1e-30
LOG2E = 1.4426950408889634

N_BRANCHES = 4
FOURIER_WIDTH = 512
FOURIER_GROUPS = 4
FOURIER_GROUP_DIM = 128
S5_WIDTH = 512
S5_GROUP_DIM = 16
S5_GROUPS = 32
S5_STATE = 64
S5_CHUNK = 8
S5_GB = 8
S5_NBLK = S5_GROUPS // S5_GB
S5_K = S5_CHUNK * S5_GB * S5_GROUP_DIM
S5_PART = S5_GB * S5_STATE
S5_MAX_ROWS = 512
S5_SCAN_SLABS = 4
N_HEADS = 8
N_KV_HEADS = 2
HEAD_DIM = 128
Q_PER_KV = 4
ATTN_WIDTH = 1024
KV_WIDTH = 256
ROPE_THETA = 10000.0
GRID_W = 64
HGRN_HEADS = 4
HGRN_DK = 128
HGRN_DV = 128
HGRN_CHUNK = 64

COL_UA, COL_GA, COL_UB, COL_GB = 0, 512, 1024, 1536
COL_QC, COL_KC, COL_VC, COL_GC = 2048, 3072, 3328, 3584
COL_QD, COL_ID, COL_ZF, COL_ZB, COL_GD = 4608, 5120, 5632, 6144, 6656
COL_M = 7168
IN_COLS = 15360

FFT_RADIX = 8
FINAL_ROWS = 256
ATTN_KEY_CHUNK = 1024
ATTN_TQ = 512
INPROJ_TM = 1024
INPROJ_TN = 1536
INPROJ_ROWS = 256
FINAL_TM = 1024
FINAL_TC = 256
MOD_TN = 1024
QK_PREP_TM = 512
S5_GLU_TM = 1024
VMEM_LIMIT = 56 * 1024 * 1024
FINAL_VMEM_LIMIT = 58 * 1024 * 1024


def _cparams(n_grid):
    return pltpu.CompilerParams(dimension_semantics=("arbitrary",) * n_grid,
                                vmem_limit_bytes=VMEM_LIMIT)


def _sigmoid(x):
    return 1.0 / (1.0 + jnp.exp(-x))


def _gate_sigmoid(x):
    return 0.5 * jnp.tanh(0.5 * x) + 0.5


def _silu(x):
    return x * _gate_sigmoid(x)


def _dot(a, b):
    return jnp.dot(a, b, preferred_element_type=F32)


def _dot_nt(a, b):
    return lax.dot_general(a, b, (((1,), (1,)), ((), ())), preferred_element_type=F32)


def _dot_tn(a, b):
    return lax.dot_general(a, b, (((0,), (0,)), ((), ())), preferred_element_type=F32)


def _split2(x):
    hi = x.astype(BF16)
    lo = (x - hi.astype(F32)).astype(BF16)
    return hi, lo


def _dot3(a, b):
    ah, al = _split2(a)
    bh, bl = _split2(b)
    return _dot(ah, bh) + (_dot(ah, bl) + _dot(al, bh))


def _mod_kernel(c_ref, w_ref, b_ref, o_ref):
    a = _silu(c_ref[...])
    o_ref[0] = _dot3(a, w_ref[0]) + b_ref[0]


def _modulation(c_all, w_mod, b_mod):
    depth, d, n = w_mod.shape
    rows = c_all.shape[0]
    tn = MOD_TN
    return pl.pallas_call(
        _mod_kernel,
        grid=(depth, n // tn),
        in_specs=[pl.BlockSpec((rows, d), lambda l, j: (0, 0)),
                  pl.BlockSpec((1, d, tn), lambda l, j: (l, 0, j)),
                  pl.BlockSpec((1, 1, tn), lambda l, j: (l, 0, j))],
        out_specs=pl.BlockSpec((1, rows, tn), lambda l, j: (l, 0, j)),
        out_shape=jax.ShapeDtypeStruct((depth, rows, n), F32),
        compiler_params=_cparams(2),
        name="modulation",
    )(c_all, w_mod, b_mod.reshape(depth, 1, n))


def _inproj_kernel(x_ref, sh_ref, sc_ref, g_ref, w_ref, o_ref, ub_ref, h_scr):
    first = pl.program_id(2) == 0
    tm = h_scr.shape[0]

    @pl.when(first)
    def _():
        for r0 in range(0, tm, INPROJ_ROWS):
            rs = slice(r0, r0 + INPROJ_ROWS)
            x = x_ref[0, rs, :]
            ms = jnp.mean(x * x, axis=-1, keepdims=True)
            y = x * lax.rsqrt(ms + NORM_EPS) * g_ref[...]
            h = (y * (1.0 + sc_ref[0]) + sh_ref[0]).astype(BF16)
            h_scr[rs, :] = h
            res = _dot(h, w_ref[0])
            o_ref[0, rs, :] = res.astype(BF16)
            ub_ref[0, rs, :] = res[:, COL_UB:COL_UB + S5_WIDTH]

    @pl.when(jnp.logical_not(first))
    def _():
        o_ref[0] = _dot(h_scr[...], w_ref[0]).astype(BF16)


def _inproj(x, shift, scale, g, w_bf, layer, tm, tn):
    b, l, d = x.shape
    n = w_bf.shape[2]
    assert COL_UB + S5_WIDTH <= tn
    per_batch = shift.shape[0] > 1
    mod_map = (lambda bi, i, j: (bi, 0, 0)) if per_batch else (lambda bi, i, j: (0, 0, 0))
    return pl.pallas_call(
        _inproj_kernel,
        grid=(b, l // tm, n // tn),
        in_specs=[pl.BlockSpec((1, tm, d), lambda bi, i, j: (bi, i, 0)),
                  pl.BlockSpec((1, 1, d), mod_map),
                  pl.BlockSpec((1, 1, d), mod_map),
                  pl.BlockSpec((1, d), lambda bi, i, j: (0, 0)),
                  pl.BlockSpec((1, d, tn), lambda bi, i, j: (layer, 0, j))],
        out_specs=[pl.BlockSpec((1, tm, tn), lambda bi, i, j: (bi, i, j)),
                   pl.BlockSpec((1, tm, S5_WIDTH), lambda bi, i, j: (bi, i, 0))],
        out_shape=[jax.ShapeDtypeStruct((b, l, n), BF16),
                   jax.ShapeDtypeStruct((b, l, S5_WIDTH), F32)],
        scratch_shapes=[pltpu.VMEM((tm, d), BF16)],
        compiler_params=_cparams(3),
        name="inproj",
    )(x, shift, scale, g, w_bf)


def _dft_tables(l):
    r = FFT_RADIX
    l2 = l // r
    scale = 1.0 / math.sqrt(l * FOURIER_GROUP_DIM)
    k1 = np.arange(r, dtype=np.int64)[:, None]
    t2 = np.arange(l2, dtype=np.int64)[None, :]
    ang = 2.0 * np.pi * ((k1 * t2) % l) / l
    lanes = (l, FOURIER_GROUP_DIM)
    tw_re = jnp.broadcast_to(jnp.asarray(np.cos(ang).reshape(l, 1), F32), lanes)
    tw_im = jnp.broadcast_to(jnp.asarray(-np.sin(ang).reshape(l, 1), F32), lanes)
    k2 = np.arange(l2, dtype=np.int64)
    a3 = 2.0 * np.pi * ((k2[:, None] * k2[None, :]) % l2) / l2
    d3 = jnp.asarray(np.concatenate([np.cos(a3), np.sin(a3)], axis=1) * scale, F32).astype(BF16)
    c = np.arange(FOURIER_GROUP_DIM, dtype=np.int64)
    ac = 2.0 * np.pi * ((c[:, None] * c[None, :]) % FOURIER_GROUP_DIM) / FOURIER_GROUP_DIM
    cs = jnp.asarray(np.concatenate([np.cos(ac), np.sin(ac)], axis=1), F32).astype(BF16)
    return (tw_re, tw_im, d3), cs


def _cadd(a, b):
    return a[0] + b[0], a[1] + b[1]


def _csub(a, b):
    return a[0] - b[0], a[1] - b[1]


def _cmul_neg_i(a):
    return a[1], -a[0]


def _dft4(y0, y1, y2, y3):
    c0, c1 = _cadd(y0, y2), _cadd(y1, y3)
    d0, d1 = _csub(y0, y2), _cmul_neg_i(_csub(y1, y3))
    return _cadd(c0, c1), _cadd(d0, d1), _csub(c0, c1), _csub(d0, d1)


def _dft8(x):
    h = math.sqrt(0.5)
    a = [_cadd(x[n], x[n + 4]) for n in range(4)]
    d = [_csub(x[n], x[n + 4]) for n in range(4)]
    b = [d[0],
         ((d[1][0] + d[1][1]) * h, (d[1][1] - d[1][0]) * h),
         _cmul_neg_i(d[2]),
         ((d[3][1] - d[3][0]) * h, -(d[3][0] + d[3][1]) * h)]
    even, odd = _dft4(*a), _dft4(*b)
    return [even[0], odd[0], even[1], odd[1], even[2], odd[2], even[3], odd[3]]


def _fourier_kernel(u_ref, g_ref, cs_ref, twr_ref, twi_ref, d3_ref, w_ref, o_ref, b_scr, f_scr, *, l):
    r = FFT_RADIX
    l2 = l // r
    gd = FOURIER_GROUP_DIM
    t = _dot(u_ref[0].astype(BF16), cs_ref[...])
    blocks = [(t[j * l2:(j + 1) * l2, :gd], -t[j * l2:(j + 1) * l2, gd:]) for j in range(r)]
    for k1, (ar, ai) in enumerate(_dft8(blocks)):
        twr = twr_ref[k1 * l2:(k1 + 1) * l2, :]
        twi = twi_ref[k1 * l2:(k1 + 1) * l2, :]
        b_scr[k1, 0:l2, :] = (ar * twr - ai * twi).astype(BF16)
        b_scr[k1, l2:2 * l2, :] = (ar * twi + ai * twr).astype(BF16)
    for k1 in range(r):
        f_scr[pl.ds(k1, l2, stride=r), :] = _dot(d3_ref[...], b_scr[k1])
    y = _dot(f_scr[...].astype(BF16), w_ref[0].astype(BF16))
    o_ref[0] = (y * _silu(g_ref[0].astype(F32))).astype(BF16)


def _fourier_branch(proj, fourier_w, dft, cs):
    b, l, _ = proj.shape
    gd = FOURIER_GROUP_DIM
    l2 = l // FFT_RADIX
    tw_re, tw_im, d3 = dft
    return pl.pallas_call(
        functools.partial(_fourier_kernel, l=l),
        grid=(b, FOURIER_GROUPS),
        in_specs=[pl.BlockSpec((1, l, gd), lambda bi, g: (bi, 0, COL_UA // gd + g)),
                  pl.BlockSpec((1, l, gd), lambda bi, g: (bi, 0, COL_GA // gd + g)),
                  pl.BlockSpec((gd, 2 * gd), lambda bi, g: (0, 0)),
                  pl.BlockSpec((l, gd), lambda bi, g: (0, 0)),
                  pl.BlockSpec((l, gd), lambda bi, g: (0, 0)),
                  pl.BlockSpec((l2, 2 * l2), lambda bi, g: (0, 0)),
                  pl.BlockSpec((1, gd, gd), lambda bi, g: (g, 0, 0))],
        out_specs=pl.BlockSpec((1, l, gd), lambda bi, g: (bi, 0, g)),
        out_shape=jax.ShapeDtypeStruct((b, l, FOURIER_WIDTH), BF16),
        scratch_shapes=[pltpu.VMEM((FFT_RADIX, 2 * l2, gd), BF16), pltpu.VMEM((l, gd), F32)],
        compiler_params=_cparams(2),
        name="fourier",
    )(proj, proj, cs, tw_re, tw_im, d3, fourier_w)


def _s5_matrices(lam_re, lam_im, log_step, b_re, b_im, c_re, c_im, dskip):
    t = S5_CHUNK
    g, n, p = S5_GROUPS, S5_STATE, S5_GROUP_DIM
    hp = lax.Precision.HIGHEST
    lam_re = lam_re.astype(F32)
    lam_im = lam_im.astype(F32)
    step = jnp.exp(log_step.astype(F32))[..., None]
    rho = lam_re * step
    th = lam_im * step
    mag = jnp.exp(rho)
    lb_re = mag * jnp.cos(th)
    lb_im = mag * jnp.sin(th)
    nr = lb_re - 1.0
    den = lam_re * lam_re + lam_im * lam_im
    fr = (nr * lam_re + lb_im * lam_im) / den
    fi = (lb_im * lam_re - nr * lam_im) / den
    b_re = b_re.astype(F32)
    b_im = b_im.astype(F32)
    bb_re = fr[..., None] * b_re - fi[..., None] * b_im
    bb_im = fr[..., None] * b_im + fi[..., None] * b_re
    tau = jnp.arange(t + 1, dtype=F32)
    pw_mag = jnp.exp(rho[..., None] * tau)
    pw_re = pw_mag * jnp.cos(th[..., None] * tau)
    pw_im = pw_mag * jnp.sin(th[..., None] * tau)
    c_re = c_re.astype(F32)
    c_im = c_im.astype(F32)
    ca_re = c_re[..., None] * pw_re[:, :, None] - c_im[..., None] * pw_im[:, :, None]
    ca_im = c_re[..., None] * pw_im[:, :, None] + c_im[..., None] * pw_re[:, :, None]
    kern = jnp.einsum('dgpnt,dgnq->dgtpq', jnp.concatenate([ca_re[..., :t], -ca_im[..., :t]], axis=3),
                      jnp.concatenate([bb_re, bb_im], axis=2), precision=hp)
    sig = np.arange(t)[:, None]
    ta = np.arange(t)[None, :]
    idx_f = np.clip(ta - sig, 0, t - 1)
    idx_b = np.clip(sig - ta, 0, t - 1)
    m_f = jnp.asarray((ta >= sig).astype(np.float32))[None, :, :, None, None]
    m_b = jnp.asarray((sig >= ta).astype(np.float32))[None, :, :, None, None]
    tf = kern[0][:, idx_f] * m_f
    tb = kern[1][:, idx_b] * m_b
    eye_t = jnp.asarray(np.eye(t, dtype=np.float32))[None, :, :, None, None]
    dmat = dskip.astype(F32)[:, None, None, :, None] * jnp.asarray(np.eye(p, dtype=np.float32))[None, None, None]
    tt = tf + tb + eye_t * dmat
    nb, gb = S5_NBLK, S5_GB
    one = np.ones
    eye = lambda k: np.eye(k, dtype=np.float32)
    exp_t = jnp.asarray(np.einsum('tu,pq,b->tpubq', eye(t), eye(p), one(gb)).reshape(t * p, S5_K))
    exp_n = jnp.asarray(np.einsum('nm,b->nbm', eye(n), one(gb)).reshape(n, S5_PART))
    mask_t = jnp.asarray(np.einsum('ab,s,q,u,p->saqubp', eye(gb), one(t), one(p), one(t), one(p))
                         .reshape(S5_K, S5_K))
    mask_ba = jnp.asarray(np.einsum('ab,s,q,n->saqbn', eye(gb), one(t), one(p), one(n)).reshape(S5_K, S5_PART))
    t_small = tt.reshape(nb, gb, t, t, p, p).transpose(0, 2, 1, 5, 3, 4).reshape(nb, S5_K, t * p)
    tmat = jnp.matmul(t_small, exp_t, precision=hp) * mask_t

    def ba(d, powers):
        pr = pw_re[d][:, :, powers]
        pi = pw_im[d][:, :, powers]
        re = pr[..., None] * bb_re[d][:, :, None, :] - pi[..., None] * bb_im[d][:, :, None, :]
        im = pr[..., None] * bb_im[d][:, :, None, :] + pi[..., None] * bb_re[d][:, :, None, :]
        blk = lambda m: jnp.matmul(m.reshape(nb, gb, n, t, p).transpose(0, 3, 1, 4, 2).reshape(nb, S5_K, n),
                                   exp_n, precision=hp) * mask_ba
        return blk(re), blk(im)

    baf_re, baf_im = ba(0, np.arange(t - 1, -1, -1))
    bab_re, bab_im = ba(1, np.arange(t))
    bamat = jnp.concatenate([baf_re, bab_re, baf_im, bab_im], axis=-1)

    def ca(d, powers):
        blk = lambda m: jnp.matmul(exp_n.T, m[..., powers].reshape(nb, gb, p, n, t).transpose(0, 3, 4, 1, 2)
                                   .reshape(nb, n, S5_K), precision=hp) * mask_ba.T
        return blk(ca_re[d]), blk(ca_im[d])

    caf_re, caf_im = ca(0, np.arange(1, t + 1))
    cab_re, cab_im = ca(1, np.arange(t, 0, -1))
    camat = jnp.concatenate([caf_re, cab_re, -caf_im, -cab_im], axis=1)
    part = lambda m: m[..., t].reshape(nb, S5_PART)
    a_pow = jnp.stack([jnp.concatenate([part(pw_re[0]), part(pw_re[1])], axis=-1),
                       jnp.concatenate([part(pw_im[0]), part(pw_im[1])], axis=-1)], axis=1)
    return tmat.astype(BF16), bamat.astype(BF16), camat.astype(BF16), a_pow


def _s5_kernel(u_ref, t_ref, ba_ref, ca_ref, a_ref, h0_ref, y_ref, hf_ref, ub_scr, s_scr, hin_scr, y_scr,
               *, nc, bt):
    t, w, hp = S5_CHUNK, S5_GB * S5_GROUP_DIM, S5_PART
    for bi in range(bt):
        for tau in range(t):
            ub_scr[tau, pl.ds(bi, nc, stride=bt), :] = u_ref[bi, pl.ds(tau, nc, stride=t), :]
    ub = jnp.concatenate([ub_scr[tau] for tau in range(t)], axis=1).astype(BF16)
    s_scr[...] = _dot(ub, ba_ref[0, 0])
    ar_f, ar_b = a_ref[0, 0, 0:1, 0:hp], a_ref[0, 0, 0:1, hp:2 * hp]
    ai_f, ai_b = a_ref[0, 0, 1:2, 0:hp], a_ref[0, 0, 1:2, hp:2 * hp]

    def step(i, carry):
        re_f, im_f, re_b, im_b = carry
        rf = pl.ds(i * bt, bt)
        rb = pl.ds((nc - 1 - i) * bt, bt)
        hin_scr[rf, 0:hp] = re_f
        hin_scr[rf, 2 * hp:3 * hp] = im_f
        hin_scr[rb, hp:2 * hp] = re_b
        hin_scr[rb, 3 * hp:4 * hp] = im_b
        n_re_f = ar_f * re_f - ai_f * im_f + s_scr[rf, 0:hp]
        n_im_f = ar_f * im_f + ai_f * re_f + s_scr[rf, 2 * hp:3 * hp]
        n_re_b = ar_b * re_b - ai_b * im_b + s_scr[rb, hp:2 * hp]
        n_im_b = ar_b * im_b + ai_b * re_b + s_scr[rb, 3 * hp:4 * hp]
        return n_re_f, n_im_f, n_re_b, n_im_b

    h0 = h0_ref[...].reshape(bt, 4 * hp)
    carry = (h0[:, 0:hp], h0[:, 2 * hp:3 * hp], h0[:, hp:2 * hp], h0[:, 3 * hp:4 * hp])
    slab = S5_K // S5_SCAN_SLABS
    y_parts = []
    for sl in range(S5_SCAN_SLABS):
        y_parts.append(_dot(ub, t_ref[0, 0, :, sl * slab:(sl + 1) * slab]))
        for i in range(sl * nc // S5_SCAN_SLABS, (sl + 1) * nc // S5_SCAN_SLABS):
            carry = step(i, carry)
    hf_ref[...] = jnp.concatenate([carry[0], carry[2], carry[1], carry[3]], axis=1).reshape(hf_ref.shape)
    y = jnp.concatenate(y_parts, axis=1) + _dot(hin_scr[...].astype(BF16), ca_ref[0, 0])
    for tau in range(t):
        y_scr[tau] = y[:, tau * w:(tau + 1) * w]
    for bi in range(bt):
        for tau in range(t):
            y_ref[bi, pl.ds(tau, nc, stride=t), :] = y_scr[tau, pl.ds(bi, nc, stride=bt), :]


def _s5_core(ub, tmat, bamat, camat, a_pow, h0, layer):
    b, l, _ = ub.shape
    nc = l // S5_CHUNK
    w = S5_GB * S5_GROUP_DIM
    bt = b if nc * b <= S5_MAX_ROWS else 1
    if bt == 1:
        h0 = h0.reshape(S5_NBLK, b, 1, 4 * S5_PART)
        st_spec = pl.BlockSpec((1, 1, 1, 4 * S5_PART), lambda j, bi: (j, bi, 0, 0))
    else:
        st_spec = pl.BlockSpec((1, bt, 4 * S5_PART), lambda j, bi: (j, bi, 0))
    rows = nc * bt
    y, hf = pl.pallas_call(
        functools.partial(_s5_kernel, nc=nc, bt=bt),
        grid=(S5_NBLK, b // bt),
        in_specs=[pl.BlockSpec((bt, l, w), lambda j, bi: (bi, 0, j)),
                  pl.BlockSpec((1, 1, S5_K, S5_K), lambda j, bi: (layer, j, 0, 0)),
                  pl.BlockSpec((1, 1, S5_K, 4 * S5_PART), lambda j, bi: (layer, j, 0, 0)),
                  pl.BlockSpec((1, 1, 4 * S5_PART, S5_K), lambda j, bi: (layer, j, 0, 0)),
                  pl.BlockSpec((1, 1, 2, 2 * S5_PART), lambda j, bi: (layer, j, 0, 0)),
                  st_spec],
        out_specs=[pl.BlockSpec((bt, l, w), lambda j, bi: (bi, 0, j)), st_spec],
        out_shape=[jax.ShapeDtypeStruct((b, l, S5_WIDTH), F32),
                   jax.ShapeDtypeStruct(h0.shape, F32)],
        scratch_shapes=[pltpu.VMEM((S5_CHUNK, rows, w), F32), pltpu.VMEM((rows, 4 * S5_PART), F32),
                        pltpu.VMEM((rows, 4 * S5_PART), F32), pltpu.VMEM((S5_CHUNK, rows, w), F32)],
        compiler_params=_cparams(2),
        name="s5_core",
    )(ub, tmat, bamat, camat, a_pow, h0)
    return y, hf.reshape(S5_NBLK, b, 4 * S5_PART)


def _s5_glu_kernel(y_ref, g_ref, w_ref, b_ref, o_ref):
    y = y_ref[0]
    y = 0.5 * y * (1.0 + jnp.tanh(math.sqrt(2.0 / math.pi) * (y + 0.044715 * (y * y * y))))
    z = _dot(y.astype(BF16), w_ref[...].astype(BF16)) + b_ref[...]
    o_ref[0] = (y * _sigmoid(z) * _silu(g_ref[0].astype(F32))).astype(BF16)


def _s5_glu(y, proj, glu_w, glu_b, tm):
    b, l, w = y.shape
    return pl.pallas_call(
        _s5_glu_kernel,
        grid=(b, l // tm),
        in_specs=[pl.BlockSpec((1, tm, w), lambda bi, i: (bi, i, 0)),
                  pl.BlockSpec((1, tm, w), lambda bi, i: (bi, i, COL_GB // w)),
                  pl.BlockSpec((w, w), lambda bi, i: (0, 0)),
                  pl.BlockSpec((1, w), lambda bi, i: (0, 0))],
        out_specs=pl.BlockSpec((1, tm, w), lambda bi, i: (bi, i, 0)),
        out_shape=jax.ShapeDtypeStruct((b, l, w), BF16),
        compiler_params=_cparams(2),
        name="s5_glu",
    )(y, proj, glu_w, glu_b.reshape(1, w))


def _s5_branch(proj, ub, mats, layer, glu_w, glu_b, h0_re, h0_im):
    b, l, _ = proj.shape
    g, n = S5_GROUPS, S5_STATE
    if h0_re is None:
        h0 = jnp.zeros((S5_NBLK, b, 4 * S5_PART), F32)
    else:
        blk = lambda h: h.astype(F32).reshape(b, S5_NBLK, S5_PART).transpose(1, 0, 2)
        h0 = jnp.concatenate([blk(h0_re[:, 0]), blk(h0_re[:, 1]), blk(h0_im[:, 0]), blk(h0_im[:, 1])], axis=-1)
    y, hf = _s5_core(ub, *mats, h0, layer)
    out = _s5_glu(y, proj, glu_w, glu_b, min(l, S5_GLU_TM))
    part = lambda i: hf[..., i * S5_PART:(i + 1) * S5_PART].transpose(1, 0, 2).reshape(b, g, n)
    fin_re = jnp.stack([part(0), part(1)], axis=1)
    fin_im = jnp.stack([part(2), part(3)], axis=1)
    return out, fin_re, fin_im


def _rope_tables(l):
    rows = l // GRID_W
    row = jnp.broadcast_to(jnp.arange(rows, dtype=F32)[:, None], (rows, GRID_W)).reshape(-1)
    col = jnp.broadcast_to(jnp.arange(GRID_W, dtype=F32)[None, :], (rows, GRID_W)).reshape(-1)
    half = HEAD_DIM // 2
    inv = ROPE_THETA ** (-jnp.arange(0, half, 2, dtype=F32) / half)
    ar = row[:, None] * inv
    ac = col[:, None] * inv
    cos = jnp.concatenate([jnp.cos(ar), jnp.cos(ar), jnp.cos(ac), jnp.cos(ac)], axis=1)
    sin = jnp.concatenate([-jnp.sin(ar), jnp.sin(ar), -jnp.sin(ac), jnp.sin(ac)], axis=1)
    return cos, sin


def _qk_kernel(*refs, rope, want_kf):
    q_ref, k_ref, v_ref, qn_ref, kn_ref = refs[:5]
    pos = 5
    if rope:
        cos_ref, sin_ref = refs[pos:pos + 2]
        pos += 2
    qo_ref, ko_ref, vo_ref = refs[pos:pos + 3]
    pos += 3
    if want_kf:
        kf_ref = refs[pos]
    hd = HEAD_DIM
    tm = q_ref.shape[1]
    quarter = hd // 4
    if rope:
        cos = cos_ref[...]
        sin = sin_ref[...]
        first = (lax.broadcasted_iota(jnp.int32, (tm, hd), 1) % (2 * quarter)) < quarter

    def norm(x, gain):
        ms = jnp.mean(x * x, axis=-1, keepdims=True)
        return x * lax.rsqrt(ms + NORM_EPS) * gain

    def rot(x):
        if not rope:
            return x
        nxt = pltpu.roll(x, hd - quarter, 1)
        prv = pltpu.roll(x, quarter, 1)
        return x * cos + jnp.where(first, nxt, prv) * sin

    qscale = (hd ** -0.5) * LOG2E
    for h in range(N_HEADS):
        x = norm(q_ref[0, :, h * hd:(h + 1) * hd].astype(F32), qn_ref[...])
        qo_ref[0, :, h * hd:(h + 1) * hd] = (rot(x) * qscale).astype(BF16)
    for h in range(N_KV_HEADS):
        x = norm(k_ref[0, :, h * hd:(h + 1) * hd].astype(F32), kn_ref[...])
        if want_kf:
            kf_ref[0, :, h * hd:(h + 1) * hd] = x
        ko_ref[0, :, h * hd:(h + 1) * hd] = rot(x).astype(BF16)
        vo_ref[0, h] = v_ref[0, :, h * hd:(h + 1) * hd].astype(F32).T.astype(BF16)


def _qk_prep(proj, q_norm, k_norm, rope_tabs, want_kf, tm):
    b, l, _ = proj.shape
    rope = rope_tabs is not None
    in_specs = [pl.BlockSpec((1, tm, ATTN_WIDTH), lambda bi, i: (bi, i, COL_QC // ATTN_WIDTH)),
                pl.BlockSpec((1, tm, KV_WIDTH), lambda bi, i: (bi, i, COL_KC // KV_WIDTH)),
                pl.BlockSpec((1, tm, KV_WIDTH), lambda bi, i: (bi, i, COL_VC // KV_WIDTH)),
                pl.BlockSpec((1, HEAD_DIM), lambda bi, i: (0, 0)),
                pl.BlockSpec((1, HEAD_DIM), lambda bi, i: (0, 0))]
    args = [proj, proj, proj, q_norm.reshape(1, HEAD_DIM), k_norm.reshape(1, HEAD_DIM)]
    if rope:
        in_specs += [pl.BlockSpec((tm, HEAD_DIM), lambda bi, i: (i, 0))] * 2
        args += list(rope_tabs)
    out_specs = [pl.BlockSpec((1, tm, ATTN_WIDTH), lambda bi, i: (bi, i, 0)),
                 pl.BlockSpec((1, tm, KV_WIDTH), lambda bi, i: (bi, i, 0)),
                 pl.BlockSpec((1, N_KV_HEADS, HEAD_DIM, tm), lambda bi, i: (bi, 0, 0, i))]
    out_shape = [jax.ShapeDtypeStruct((b, l, ATTN_WIDTH), BF16),
                 jax.ShapeDtypeStruct((b, l, KV_WIDTH), BF16),
                 jax.ShapeDtypeStruct((b, N_KV_HEADS, HEAD_DIM, l), BF16)]
    if want_kf:
        out_specs.append(pl.BlockSpec((1, tm, KV_WIDTH), lambda bi, i: (bi, i, 0)))
        out_shape.append(jax.ShapeDtypeStruct((b, l, KV_WIDTH), F32))
    return pl.pallas_call(
        functools.partial(_qk_kernel, rope=rope, want_kf=want_kf),
        grid=(b, l // tm),
        in_specs=in_specs, out_specs=out_specs, out_shape=out_shape,
        compiler_params=_cparams(2),
        name="qk_prep",
    )(*args)


def _attn_kernel(*refs, has_ctx):
    s_scr = refs[-1]
    q_ref, k_ref, v_ref = refs[:3]
    if has_ctx:
        ck_ref, cv_ref, g_ref, o_ref = refs[3:7]
        ck = ck_ref[0, 0].astype(BF16)
        cvt = cv_ref[0, 0, 0].astype(BF16)
        past = ck.shape[0]
    else:
        g_ref, o_ref = refs[3:5]
    hd = HEAD_DIM
    tq = q_ref.shape[1]
    lk = k_ref.shape[1]
    cr = min(lk, ATTN_KEY_CHUNK)
    chunks = [(c * cr, cr, (lambda c=c: k_ref[0, c * cr:(c + 1) * cr, :]),
               (lambda c=c: v_ref[0, 0, :, c * cr:(c + 1) * cr])) for c in range(lk // cr)]
    if has_ctx:
        chunks.append((lk, past, lambda: ck, lambda: cvt))
    qs = [q_ref[0, :, g * hd:(g + 1) * hd] for g in range(Q_PER_KV)]
    gate = _silu(g_ref[0].astype(F32))

    def score(g, chunk, m):
        off, rows, keys, _ = chunk
        s = _dot_nt(keys(), qs[g])
        s_scr[g % 2, off:off + rows, :] = s
        return jnp.maximum(m, jnp.max(s, axis=0, keepdims=True))

    def accumulate(g, chunk, m, den, ot):
        off, rows, _, values_t = chunk
        p = jnp.exp2(s_scr[g % 2, off:off + rows, :] - m)
        return den + jnp.sum(p, axis=0, keepdims=True), ot + _dot(values_t(), p.astype(BF16))

    neg = jnp.full((1, tq), -jnp.inf, F32)
    m_cur = neg
    for chunk in chunks:
        m_cur = score(0, chunk, m_cur)
    for g in range(Q_PER_KV):
        m_next = neg
        den = jnp.zeros((1, tq), F32)
        ot = jnp.zeros((hd, tq), F32)
        for chunk in chunks:
            den, ot = accumulate(g, chunk, m_cur, den, ot)
            if g + 1 < Q_PER_KV:
                m_next = score(g + 1, chunk, m_next)
        m_cur = m_next
        o = (ot / den).T
        o_ref[0, :, g * hd:(g + 1) * hd] = (o * gate[:, g * hd:(g + 1) * hd]).astype(BF16)


def _attention(qs, kr, vt, proj, ctx_k, ctx_v, layer, tq):
    b, l, _ = qs.shape
    gw = Q_PER_KV * HEAD_DIM
    has_ctx = ctx_k is not None
    in_specs = [pl.BlockSpec((1, tq, gw), lambda bi, h, i: (bi, i, h)),
                pl.BlockSpec((1, l, HEAD_DIM), lambda bi, h, i: (bi, 0, h)),
                pl.BlockSpec((1, 1, HEAD_DIM, l), lambda bi, h, i: (bi, h, 0, 0))]
    args = [qs, kr, vt]
    if has_ctx:
        past = ctx_k.shape[2]
        in_specs += [pl.BlockSpec((1, 1, past, HEAD_DIM), lambda bi, h, i: (bi, layer, 0, h)),
                     pl.BlockSpec((1, 1, 1, HEAD_DIM, past), lambda bi, h, i: (bi, layer, h, 0, 0))]
        args += [ctx_k, ctx_v]
    in_specs.append(pl.BlockSpec((1, tq, gw), lambda bi, h, i: (bi, i, COL_GC // gw + h)))
    args.append(proj)
    return pl.pallas_call(
        functools.partial(_attn_kernel, has_ctx=has_ctx),
        grid=(b, N_KV_HEADS, l // tq),
        in_specs=in_specs,
        out_specs=pl.BlockSpec((1, tq, gw), lambda bi, h, i: (bi, i, h)),
        out_shape=jax.ShapeDtypeStruct((b, l, ATTN_WIDTH), BF16),
        scratch_shapes=[pltpu.VMEM((2, l + (ctx_k.shape[2] if has_ctx else 0), tq), F32)],
        compiler_params=_cparams(3),
        name="attention",
    )(*args)


HGRN_COARSE = (32, 16, 8, 4)
HGRN_FINE = (2,)
HGRN_LEVELS = HGRN_COARSE + HGRN_FINE + (1,)
HGRN_CHUNKS_PER_TRIP = 4


def _hgrn_tables():
    c = HGRN_CHUNK
    t = np.arange(c)
    w01, sm = [], []
    for d in range(2):
        cum = (t[None, :] <= t[:, None]) if d == 0 else (t[None, :] >= t[:, None])
        cum = cum.astype(np.float32)
        w01.append(np.concatenate([cum] + [cum[_hgrn_ref_row(t, hs, d)] for hs in HGRN_FINE], axis=0))
        sd = []
        for hs in HGRN_LEVELS:
            is_q = _hgrn_is_query(t, hs, d)
            same = (t[:, None] // (2 * hs)) == (t[None, :] // (2 * hs))
            sd.append((same & is_q[:, None] & ~is_q[None, :]).astype(np.float32))
        sd.append(np.eye(c, dtype=np.float32))
        sm.append(np.stack(sd))
    return jnp.asarray(np.stack(w01), BF16), jnp.asarray(np.stack(sm), F32)


def _hgrn_ref_row(t, hs, d):
    base = t & ~(2 * hs - 1)
    return base + hs - 1 if d == 0 else base + hs


def _hgrn_is_query(t, hs, d):
    return ((t & hs) != 0) if d == 0 else ((t & hs) == 0)


def _hgrn_cums(z, lbv, w01):
    f = lbv + (1.0 - lbv) * _sigmoid(z)
    fm = jnp.maximum(f, F_FLOOR)
    hi, lo = _split2(jnp.log2(fm))
    return fm, 1.0 - f, _dot(w01, hi) + _dot(w01, lo)


def _hgrn_products(q, k, fm, allc, rows, d):
    c = HGRN_CHUNK
    sub = 8
    cum = allc[0:c]
    refs = {}
    for hs in HGRN_COARSE:
        pieces = []
        for j in range(c // sub):
            rr = int(_hgrn_ref_row(np.int64(j * sub), hs, d))
            pieces.append(jnp.broadcast_to(cum[rr:rr + 1, :], (sub, HGRN_DK)))
        refs[hs] = jnp.concatenate(pieces, axis=0)
    for i, hs in enumerate(HGRN_FINE):
        refs[hs] = allc[(i + 1) * c:(i + 2) * c]
    lrow = c - 1 if d == 0 else 0
    last = cum[lrow:lrow + 1, :]
    raws = [_dot_nt(q.astype(BF16), k.astype(BF16))]
    for hs in HGRN_LEVELS:
        is_q = ((rows & hs) != 0) if d == 0 else ((rows & hs) == 0)
        if hs == 1:
            x = jnp.where(is_q, q * fm, k).astype(BF16)
        else:
            x = (jnp.where(is_q, q, k) * jnp.exp2((cum - refs[hs]) * jnp.where(is_q, 1.0, -1.0))).astype(BF16)
        raws.append(_dot_nt(x, x))
    qi = (q * jnp.exp2(cum)).astype(BF16)
    ks = (k * jnp.exp2(last - cum)).astype(BF16)
    return raws, qi, ks, jnp.exp2(last)


def _hgrn_kernel(*refs, nc, has_s0, want_state):
    q_ref, v_ref, zf_ref, zb_ref, g_ref, lb_ref, ng_ref, w01_ref, sm_ref = refs[:9]
    pos = 9
    if has_s0:
        s0_ref = refs[pos]
        pos += 1
    o_ref = refs[pos]
    pos += 1
    if want_state:
        sf_ref = refs[pos]
        pos += 1
    o_scr, st_scr = refs[pos:pos + 2]
    c = HGRN_CHUNK
    l = nc * c
    for d in range(2):
        if has_s0:
            st_scr[d] = s0_ref[0, d, 0]
        else:
            st_scr[d] = jnp.zeros((HGRN_DV, HGRN_DK), F32)
    rows = lax.broadcasted_iota(jnp.int32, (c, HGRN_DK), 0)

    per_trip = math.gcd(nc, HGRN_CHUNKS_PER_TRIP)
    ntrips = nc // per_trip
    nl = len(HGRN_LEVELS)

    def instances(i):
        insts = []
        for u in range(per_trip):
            for d in range(2):
                ci = per_trip * i + u if d == 0 else nc - 1 - (per_trip * i + u)
                r = ci * c
                insts.append((d, r if isinstance(r, int) else pl.multiple_of(r, c)))
        return insts

    def emit(tot, r, n):
        ms = jnp.mean(tot * tot, axis=-1, keepdims=True)
        y = tot * lax.rsqrt(ms + NORM_EPS) * ng_ref[...]
        o_ref[0, pl.ds(r, n), :] = (y * _silu(g_ref[0, pl.ds(r, n), :].astype(F32))).astype(BF16)

    def state_step(inst, local, finalize):
        (d, r), (scores, qi, ks, dec) = inst, local
        vb = v_ref[0, pl.ds(r, c), :].astype(BF16)
        st = st_scr[d]
        out = _dot(scores, vb) + _dot_nt(qi, st.astype(BF16))
        st_scr[d] = dec * st + _dot_tn(vb, ks)
        if finalize:
            emit(out + o_scr[1 - d, pl.ds(r, c), :], r, c)
        else:
            o_scr[d, pl.ds(r, c), :] = out

    def trip(i, prev, finalize):
        insts = instances(i)
        prev_insts = instances(i - 1) if prev is not None else None
        gates = [_hgrn_cums((zf_ref, zb_ref)[d][0, pl.ds(r, c), :].astype(F32), lb_ref[d, 0], w01_ref[d])
                 for d, r in insts]
        out = []
        for idx, ((d, r), (fm, k, allc)) in enumerate(zip(insts, gates)):
            raws, qi, ks, dec = _hgrn_products(q_ref[0, pl.ds(r, c), :].astype(F32), k, fm, allc, rows, d)
            if prev is not None:
                state_step(prev_insts[idx], prev[idx], finalize)
            scores = sm_ref[d, nl] * raws[0]
            for li in range(nl):
                scores = scores + sm_ref[d, li] * raws[li + 1]
            out.append((scores.astype(BF16), qi, ks, dec))
        return tuple(out)

    fused = ntrips % 2 == 0
    half = ntrips // 2 if fused else ntrips
    local = lax.fori_loop(1, half + 1 if fused else ntrips, functools.partial(trip, finalize=False),
                          trip(0, None, False))
    if fused:
        local = lax.fori_loop(half + 1, ntrips, functools.partial(trip, finalize=True), local)
    for inst, loc in zip(instances(ntrips - 1), local):
        state_step(inst, loc, fused)
    if want_state:
        for d in range(2):
            sf_ref[0, d, 0] = st_scr[d]

    if not fused:
        fb = min(l, 256)

        def finish(i, carry):
            r = pl.multiple_of(i * fb, fb)
            emit(o_scr[0, pl.ds(r, fb), :] + o_scr[1, pl.ds(r, fb), :], r, fb)
            return carry

        lax.fori_loop(0, l // fb, finish, 0)


def _hgrn_branch(proj, lower_bound, norm_g, tables, s0_t, want_state):
    b, l, _ = proj.shape
    nc = l // HGRN_CHUNK
    w = HGRN_DK
    has_s0 = s0_t is not None
    w01, sm = tables

    def col(off):
        return lambda bi, h: (bi, 0, off // w + h)

    in_specs = [pl.BlockSpec((1, l, w), col(COL_QD)),
                pl.BlockSpec((1, l, w), col(COL_ID)),
                pl.BlockSpec((1, l, w), col(COL_ZF)),
                pl.BlockSpec((1, l, w), col(COL_ZB)),
                pl.BlockSpec((1, l, w), col(COL_GD)),
                pl.BlockSpec((2, 1, 1, w), lambda bi, h: (0, h, 0, 0)),
                pl.BlockSpec((1, w), lambda bi, h: (0, 0)),
                pl.BlockSpec(w01.shape, lambda bi, h: (0, 0, 0)),
                pl.BlockSpec(sm.shape, lambda bi, h: (0, 0, 0, 0))]
    args = [proj, proj, proj, proj, proj, lower_bound.reshape(2, HGRN_HEADS, 1, w), norm_g.reshape(1, w),
            w01, sm]
    st_spec = pl.BlockSpec((1, 2, 1, HGRN_DV, HGRN_DK), lambda bi, h: (bi, 0, h, 0, 0))
    if has_s0:
        in_specs.append(st_spec)
        args.append(s0_t)
    out_specs = [pl.BlockSpec((1, l, w), lambda bi, h: (bi, 0, h))]
    out_shape = [jax.ShapeDtypeStruct((b, l, HGRN_HEADS * HGRN_DV), BF16)]
    if want_state:
        out_specs.append(st_spec)
        out_shape.append(jax.ShapeDtypeStruct((b, 2, HGRN_HEADS, HGRN_DV, HGRN_DK), F32))
    res = pl.pallas_call(
        functools.partial(_hgrn_kernel, nc=nc, has_s0=has_s0, want_state=want_state),
        grid=(b, HGRN_HEADS),
        in_specs=in_specs, out_specs=out_specs, out_shape=out_shape,
        scratch_shapes=[pltpu.VMEM((2, l, HGRN_DV), F32), pltpu.VMEM((2, HGRN_DV, HGRN_DK), F32)],
        compiler_params=_cparams(2),
        name="hgrn",
    )(*args)
    return res if want_state else (res[0], None)


def _final_kernel(x_ref, gt_ref, ma_ref, mb_ref, mc_ref, md_ref, ya_ref, yb_ref, yc_ref, yd_ref,
                  wa_ref, wb_ref, wc_ref, wd_ref, wo_ref, gp_ref, o_ref, *, nct):
    ci = pl.program_id(2)
    tm = o_ref.shape[1]
    subs = [slice(r0, r0 + FINAL_ROWS) for r0 in range(0, tm, FINAL_ROWS)]
    m_refs = (ma_ref, mb_ref, mc_ref, md_ref)
    y_refs = (ya_ref, yb_ref, yc_ref, yd_ref)
    w_refs = (wa_ref, wb_ref, wc_ref, wd_ref)

    def branch_dots(rs):
        return [_dot(y_ref[0, rs, :], w_ref[0]) for y_ref, w_ref in zip(y_refs, w_refs)]

    def step(first, last):
        nxt = branch_dots(subs[0])
        for si, rs in enumerate(subs):
            branch = nxt
            if si + 1 < len(subs):
                nxt = branch_dots(subs[si + 1])
            mixed = None
            for m_ref, dot in zip(m_refs, branch):
                term = _gate_sigmoid(m_ref[0, rs, :].astype(F32)) * dot
                mixed = term if mixed is None else mixed + term
            contrib = _dot(mixed.astype(BF16), wo_ref[0])
            out = contrib if first else o_ref[0, rs, :] + contrib
            if last:
                ms = jnp.mean(out * out, axis=-1, keepdims=True)
                y = out * lax.rsqrt(ms + NORM_EPS) * gp_ref[...]
                o_ref[0, rs, :] = x_ref[0, rs, :] + gt_ref[0] * y
            else:
                o_ref[0, rs, :] = out

    if nct == 1:
        step(True, True)
    else:
        pl.when(ci == 0)(lambda: step(True, False))
        pl.when(jnp.logical_and(ci > 0, ci < nct - 1))(lambda: step(False, False))
        pl.when(ci == nct - 1)(lambda: step(False, True))


def _final(x, gate, proj, ys, w_projs, w_out, norm_post, layer, tm, tc):
    b, l, d = x.shape
    nct = d // tc
    per_batch = gate.shape[0] > 1
    gate_map = (lambda bi, i, c: (bi, 0, 0)) if per_batch else (lambda bi, i, c: (0, 0, 0))

    def mspec(j):
        return pl.BlockSpec((1, tm, tc), lambda bi, i, c: (bi, i, (COL_M + j * d) // tc + c))

    in_specs = [pl.BlockSpec((1, tm, d), lambda bi, i, c: (bi, i, 0)),
                pl.BlockSpec((1, 1, d), gate_map)]
    in_specs += [mspec(j) for j in range(N_BRANCHES)]
    in_specs += [pl.BlockSpec((1, tm, y.shape[-1]), lambda bi, i, c: (bi, i, 0)) for y in ys]
    in_specs += [pl.BlockSpec((1, w.shape[1], tc), lambda bi, i, c: (layer, 0, c)) for w in w_projs]
    in_specs += [pl.BlockSpec((1, tc, d), lambda bi, i, c: (layer, c, 0)),
                 pl.BlockSpec((1, d), lambda bi, i, c: (0, 0))]
    return pl.pallas_call(
        functools.partial(_final_kernel, nct=nct),
        grid=(b, l // tm, nct),
        in_specs=in_specs,
        out_specs=pl.BlockSpec((1, tm, d), lambda bi, i, c: (bi, i, 0)),
        out_shape=jax.ShapeDtypeStruct((b, l, d), F32),
        compiler_params=pltpu.CompilerParams(dimension_semantics=("arbitrary",) * 3,
                                             vmem_limit_bytes=FINAL_VMEM_LIMIT),
        name="merge_out",
    )(x, gate, proj, proj, proj, proj, *ys, *w_projs, w_out, norm_post.reshape(1, d))


def _trunk_layer(x, shift, scale, gate, p, ctx, layer, consts):
    b, l, d = x.shape
    shared_mod = shift.shape[0] == 1
    if shared_mod:
        proj, ub = _inproj(x.reshape(1, b * l, d), shift, scale, p['norm_pre'], p['w_in'], layer,
                           INPROJ_TM, INPROJ_TN)
        proj = proj.reshape(b, l, IN_COLS)
        ub = ub.reshape(b, l, S5_WIDTH)
    else:
        proj, ub = _inproj(x, shift, scale, p['norm_pre'], p['w_in'], layer, INPROJ_TM, INPROJ_TN)

    dl, cs = consts['dft'][l]
    y_a = _fourier_branch(proj, p['fourier_w'], dl, cs)

    if ctx is None:
        y_b, s5_re, s5_im = _s5_branch(proj, ub, p['s5_mats'], layer, p['s5_glu_w'], p['s5_glu_b'], None, None)
        qs, kr, vt, kf = _qk_prep(proj, p['q_norm'], p['k_norm'], None, True, min(l, QK_PREP_TM))
        y_c = _attention(qs, kr, vt, proj, None, None, layer, min(l, ATTN_TQ))
        y_d, hg = _hgrn_branch(proj, p['lower_bound'], p['hgrn_norm'], consts['hgrn'], None, True)
    else:
        cache_k, cache_v, st_re, st_im, s0_t = ctx
        y_b, _, _ = _s5_branch(proj, ub, p['s5_mats'], layer, p['s5_glu_w'], p['s5_glu_b'],
                               st_re[:, layer], st_im[:, layer])
        qs, kr, vt = _qk_prep(proj, p['q_norm'], p['k_norm'], consts['rope'], False, min(l, QK_PREP_TM))
        y_c = _attention(qs, kr, vt, proj, cache_k, cache_v, layer, min(l, ATTN_TQ))
        y_d, _ = _hgrn_branch(proj, p['lower_bound'], p['hgrn_norm'], consts['hgrn'], s0_t[:, layer], False)

    ys = (y_a, y_b, y_c, y_d)
    wps = (p['w_proj_a'], p['w_proj_b'], p['w_proj_c'], p['w_proj_d'])
    if shared_mod:
        n = b * l
        x_new = _final(x.reshape(1, n, d), gate, proj.reshape(1, n, IN_COLS),
                       tuple(y.reshape(1, n, y.shape[-1]) for y in ys), wps, p['w_out'], p['norm_post'], layer,
                       FINAL_TM, FINAL_TC)
        x_new = x_new.reshape(b, l, d)
    else:
        x_new = _final(x, gate, proj, ys, wps, p['w_out'], p['norm_post'], layer, FINAL_TM, FINAL_TC)

    if ctx is None:
        v_f = proj[:, :, COL_VC:COL_VC + KV_WIDTH].astype(F32)
        return x_new, (kf, v_f, s5_re, s5_im, hg)
    return x_new, None


def kernel(x_prompt, x_sample, c, cache_k, cache_v, state_s5_re, state_s5_im, state_hgrn, c_ctx,
           norm_pre, norm_post, w_mod, b_mod, w_in, fourier_w,
           s5_lambda_re, s5_lambda_im, s5_log_step, s5_b_re, s5_b_im, s5_c_re, s5_c_im,
           s5_d, s5_glu_w, s5_glu_b, q_norm, k_norm, hgrn_lb_logits, hgrn_norm,
           w_proj_a, w_proj_b, w_proj_c, w_proj_d, w_out):
    depth = w_in.shape[0]
    bp, lp, d = x_prompt.shape
    bs, ls, _ = x_sample.shape

    lb_w = jax.nn.softmax(hgrn_lb_logits.astype(F32), axis=0)
    lower_bounds = jnp.cumsum(lb_w, axis=0) - lb_w[0]

    rows = ((bs + 1 + 7) // 8) * 8
    c_all = jnp.zeros((rows, d), F32).at[:bs].set(c).at[bs].set(c_ctx)
    mod = _modulation(c_all, w_mod, b_mod)

    w_in_bf = w_in.astype(BF16)
    wpa, wpb, wpc, wpd = (w.astype(BF16) for w in (w_proj_a, w_proj_b, w_proj_c, w_proj_d))
    w_out_bf = w_out.astype(BF16)

    consts = {'dft': {lp: _dft_tables(lp), ls: _dft_tables(ls)},
              'rope': _rope_tables(ls),
              'hgrn': _hgrn_tables()}
    past = cache_k.shape[2]
    ctx_k = cache_k.reshape(bs, depth, past, KV_WIDTH)
    ctx_v = cache_v.transpose(0, 1, 3, 4, 2)
    s0_t = jnp.swapaxes(state_hgrn.astype(F32), -1, -2)

    s5_mats = jax.vmap(_s5_matrices)(s5_lambda_re, s5_lambda_im, s5_log_step, s5_b_re, s5_b_im, s5_c_re, s5_c_im,
                                     s5_d.reshape(depth, S5_GROUPS, S5_GROUP_DIM))

    y_p, y_s = x_prompt, x_sample
    ks, vs, s5r, s5i, hg = [], [], [], [], []
    for l in range(depth):
        p = {'norm_pre': norm_pre[l].reshape(1, d), 'norm_post': norm_post[l], 'w_in': w_in_bf,
             'fourier_w': fourier_w[l],
             's5_mats': s5_mats,
             's5_glu_w': s5_glu_w[l], 's5_glu_b': s5_glu_b[l],
             'q_norm': q_norm[l], 'k_norm': k_norm[l],
             'lower_bound': lower_bounds[l], 'hgrn_norm': hgrn_norm[l],
             'w_proj_a': wpa, 'w_proj_b': wpb, 'w_proj_c': wpc, 'w_proj_d': wpd,
             'w_out': w_out_bf}
        m_ctx = mod[l, bs].reshape(1, 1, 3 * d)
        sh, sc, gt = (m_ctx[..., i * d:(i + 1) * d] for i in range(3))
        y_p, (k_l, v_l, sr_l, si_l, hg_l) = _trunk_layer(y_p, sh, sc, gt, p, None, l, consts)
        ks.append(k_l.reshape(bp, lp, N_KV_HEADS, HEAD_DIM))
        vs.append(v_l.reshape(bp, lp, N_KV_HEADS, HEAD_DIM))
        s5r.append(sr_l)
        s5i.append(si_l)
        hg.append(jnp.swapaxes(hg_l, -1, -2))
        m_s = mod[l, :bs].reshape(bs, 1, 3 * d)
        sh, sc, gt = (m_s[..., i * d:(i + 1) * d] for i in range(3))
        ctx = (ctx_k, ctx_v, state_s5_re, state_s5_im, s0_t)
        y_s, _ = _trunk_layer(y_s, sh, sc, gt, p, ctx, l, consts)

    return (y_p, y_s, jnp.stack(ks, axis=1), jnp.stack(vs, axis=1),
            jnp.stack(s5r, axis=1), jnp.stack(s5i, axis=1), jnp.stack(hg, axis=1))
```

```python
import functools
import math

import jax
import jax.numpy as jnp
import numpy as np
from jax import lax
from jax.experimental import pallas as pl
from jax.experimental.pallas import tpu as pltpu

F32 = jnp.float32
BF16 = jnp.bfloat16

NORM_EPS = 1e-6
F_FLOOR = 1e-30
LOG2E = 1.4426950408889634

N_BRANCHES = 4
FOURIER_WIDTH = 512
FOURIER_GROUPS = 4
FOURIER_GROUP_DIM = 128
S5_WIDTH = 512
S5_GROUP_DIM = 16
S5_GROUPS = 32
S5_STATE = 64
S5_CHUNK = 8
S5_GB = 8
S5_NBLK = S5_GROUPS // S5_GB
S5_K = S5_CHUNK * S5_GB * S5_GROUP_DIM
S5_PART = S5_GB * S5_STATE
S5_MAX_ROWS = 512
S5_SCAN_SLABS = 4
N_HEADS = 8
N_KV_HEADS = 2
HEAD_DIM = 128
Q_PER_KV = 4
ATTN_WIDTH = 1024
KV_WIDTH = 256
ROPE_THETA = 10000.0
GRID_W = 64
HGRN_HEADS = 4
HGRN_DK = 128
HGRN_DV = 128
HGRN_CHUNK = 64

COL_UA, COL_GA, COL_UB, COL_GB = 0, 512, 1024, 1536
COL_QC, COL_KC, COL_VC, COL_GC = 2048, 3072, 3328, 3584
COL_QD, COL_ID, COL_ZF, COL_ZB, COL_GD = 4608, 5120, 5632, 6144, 6656
COL_M = 7168
IN_COLS = 15360

FFT_RADIX = 8
FINAL_ROWS = 256
ATTN_KEY_CHUNK = 1024
ATTN_TQ = 512
INPROJ_TM = 1024
INPROJ_TN = 1536
INPROJ_ROWS = 256
FINAL_TM = 1024
FINAL_TC = 256
MOD_TN = 1024
QK_PREP_TM = 512
S5_GLU_TM = 1024
VMEM_LIMIT = 56 * 1024 * 1024
FINAL_VMEM_LIMIT = 58 * 1024 * 1024


def _cparams(n_grid):
    return pltpu.CompilerParams(dimension_semantics=("arbitrary",) * n_grid,
                                vmem_limit_bytes=VMEM_LIMIT)


def _sigmoid(x):
    return 1.0 / (1.0 + jnp.exp(-x))


def _gate_sigmoid(x):
    return 0.5 * jnp.tanh(0.5 * x) + 0.5


def _silu(x):
    return x * _gate_sigmoid(x)


def _dot(a, b):
    return jnp.dot(a, b, preferred_element_type=F32)


def _dot_nt(a, b):
    return lax.dot_general(a, b, (((1,), (1,)), ((), ())), preferred_element_type=F32)


def _dot_tn(a, b):
    return lax.dot_general(a, b, (((0,), (0,)), ((), ())), preferred_element_type=F32)


def _split2(x):
    hi = x.astype(BF16)
    lo = (x - hi.astype(F32)).astype(BF16)
    return hi, lo


def _dot3(a, b):
    ah, al = _split2(a)
    bh, bl = _split2(b)
    return _dot(ah, bh) + (_dot(ah, bl) + _dot(al, bh))


def _mod_kernel(c_ref, w_ref, b_ref, o_ref):
    a = _silu(c_ref[...])
    o_ref[0] = _dot3(a, w_ref[0]) + b_ref[0]


def _modulation(c_all, w_mod, b_mod):
    depth, d, n = w_mod.shape
    rows = c_all.shape[0]
    tn = MOD_TN
    return pl.pallas_call(
        _mod_kernel,
        grid=(depth, n // tn),
        in_specs=[pl.BlockSpec((rows, d), lambda l, j: (0, 0)),
                  pl.BlockSpec((1, d, tn), lambda l, j: (l, 0, j)),
                  pl.BlockSpec((1, 1, tn), lambda l, j: (l, 0, j))],
        out_specs=pl.BlockSpec((1, rows, tn), lambda l, j: (l, 0, j)),
        out_shape=jax.ShapeDtypeStruct((depth, rows, n), F32),
        compiler_params=_cparams(2),
        name="modulation",
    )(c_all, w_mod, b_mod.reshape(depth, 1, n))


def _inproj_kernel(x_ref, sh_ref, sc_ref, g_ref, w_ref, o_ref, ub_ref, h_scr):
    first = pl.program_id(2) == 0
    tm = h_scr.shape[0]

    @pl.when(first)
    def _():
        for r0 in range(0, tm, INPROJ_ROWS):
            rs = slice(r0, r0 + INPROJ_ROWS)
            x = x_ref[0, rs, :]
            ms = jnp.mean(x * x, axis=-1, keepdims=True)
            y = x * lax.rsqrt(ms + NORM_EPS) * g_ref[...]
            h = (y * (1.0 + sc_ref[0]) + sh_ref[0]).astype(BF16)
            h_scr[rs, :] = h
            res = _dot(h, w_ref[0])
            o_ref[0, rs, :] = res.astype(BF16)
            ub_ref[0, rs, :] = res[:, COL_UB:COL_UB + S5_WIDTH]

    @pl.when(jnp.logical_not(first))
    def _():
        o_ref[0] = _dot(h_scr[...], w_ref[0]).astype(BF16)


def _inproj(x, shift, scale, g, w_bf, layer, tm, tn):
    b, l, d = x.shape
    n = w_bf.shape[2]
    assert COL_UB + S5_WIDTH <= tn
    per_batch = shift.shape[0] > 1
    mod_map = (lambda bi, i, j: (bi, 0, 0)) if per_batch else (lambda bi, i, j: (0, 0, 0))
    return pl.pallas_call(
        _inproj_kernel,
        grid=(b, l // tm, n // tn),
        in_specs=[pl.BlockSpec((1, tm, d), lambda bi, i, j: (bi, i, 0)),
                  pl.BlockSpec((1, 1, d), mod_map),
                  pl.BlockSpec((1, 1, d), mod_map),
                  pl.BlockSpec((1, d), lambda bi, i, j: (0, 0)),
                  pl.BlockSpec((1, d, tn), lambda bi, i, j: (layer, 0, j))],
        out_specs=[pl.BlockSpec((1, tm, tn), lambda bi, i, j: (bi, i, j)),
                   pl.BlockSpec((1, tm, S5_WIDTH), lambda bi, i, j: (bi, i, 0))],
        out_shape=[jax.ShapeDtypeStruct((b, l, n), BF16),
                   jax.ShapeDtypeStruct((b, l, S5_WIDTH), F32)],
        scratch_shapes=[pltpu.VMEM((tm, d), BF16)],
        compiler_params=_cparams(3),
        name="inproj",
    )(x, shift, scale, g, w_bf)


def _dft_tables(l):
    r = FFT_RADIX
    l2 = l // r
    scale = 1.0 / math.sqrt(l * FOURIER_GROUP_DIM)
    k1 = np.arange(r, dtype=np.int64)[:, None]
    t2 = np.arange(l2, dtype=np.int64)[None, :]
    ang = 2.0 * np.pi * ((k1 * t2) % l) / l
    lanes = (l, FOURIER_GROUP_DIM)
    tw_re = jnp.broadcast_to(jnp.asarray(np.cos(ang).reshape(l, 1), F32), lanes)
    tw_im = jnp.broadcast_to(jnp.asarray(-np.sin(ang).reshape(l, 1), F32), lanes)
    k2 = np.arange(l2, dtype=np.int64)
    a3 = 2.0 * np.pi * ((k2[:, None] * k2[None, :]) % l2) / l2
    d3 = jnp.asarray(np.concatenate([np.cos(a3), np.sin(a3)], axis=1) * scale, F32).astype(BF16)
    c = np.arange(FOURIER_GROUP_DIM, dtype=np.int64)
    ac = 2.0 * np.pi * ((c[:, None] * c[None, :]) % FOURIER_GROUP_DIM) / FOURIER_GROUP_DIM
    cs = jnp.asarray(np.concatenate([np.cos(ac), np.sin(ac)], axis=1), F32).astype(BF16)
    return (tw_re, tw_im, d3), cs


def _cadd(a, b):
    return a[0] + b[0], a[1] + b[1]


def _csub(a, b):
    return a[0] - b[0], a[1] - b[1]


def _cmul_neg_i(a):
    return a[1], -a[0]


def _dft4(y0, y1, y2, y3):
    c0, c1 = _cadd(y0, y2), _cadd(y1, y3)
    d0, d1 = _csub(y0, y2), _cmul_neg_i(_csub(y1, y3))
    return _cadd(c0, c1), _cadd(d0, d1), _csub(c0, c1), _csub(d0, d1)


def _dft8(x):
    h = math.sqrt(0.5)
    a = [_cadd(x[n], x[n + 4]) for n in range(4)]
    d = [_csub(x[n], x[n + 4]) for n in range(4)]
    b = [d[0],
         ((d[1][0] + d[1][1]) * h, (d[1][1] - d[1][0]) * h),
         _cmul_neg_i(d[2]),
         ((d[3][1] - d[3][0]) * h, -(d[3][0] + d[3][1]) * h)]
    even, odd = _dft4(*a), _dft4(*b)
    return [even[0], odd[0], even[1], odd[1], even[2], odd[2], even[3], odd[3]]


def _fourier_kernel(u_ref, g_ref, cs_ref, twr_ref, twi_ref, d3_ref, w_ref, o_ref, b_scr, f_scr, *, l):
    r = FFT_RADIX
    l2 = l // r
    gd = FOURIER_GROUP_DIM
    t = _dot(u_ref[0].astype(BF16), cs_ref[...])
    blocks = [(t[j * l2:(j + 1) * l2, :gd], -t[j * l2:(j + 1) * l2, gd:]) for j in range(r)]
    for k1, (ar, ai) in enumerate(_dft8(blocks)):
        twr = twr_ref[k1 * l2:(k1 + 1) * l2, :]
        twi = twi_ref[k1 * l2:(k1 + 1) * l2, :]
        b_scr[k1, 0:l2, :] = (ar * twr - ai * twi).astype(BF16)
        b_scr[k1, l2:2 * l2, :] = (ar * twi + ai * twr).astype(BF16)
    for k1 in range(r):
        f_scr[pl.ds(k1, l2, stride=r), :] = _dot(d3_ref[...], b_scr[k1])
    y = _dot(f_scr[...].astype(BF16), w_ref[0].astype(BF16))
    o_ref[0] = (y * _silu(g_ref[0].astype(F32))).astype(BF16)


def _fourier_branch(proj, fourier_w, dft, cs):
    b, l, _ = proj.shape
    gd = FOURIER_GROUP_DIM
    l2 = l // FFT_RADIX
    tw_re, tw_im, d3 = dft
    return pl.pallas_call(
        functools.partial(_fourier_kernel, l=l),
        grid=(b, FOURIER_GROUPS),
        in_specs=[pl.BlockSpec((1, l, gd), lambda bi, g: (bi, 0, COL_UA // gd + g)),
                  pl.BlockSpec((1, l, gd), lambda bi, g: (bi, 0, COL_GA // gd + g)),
                  pl.BlockSpec((gd, 2 * gd), lambda bi, g: (0, 0)),
                  pl.BlockSpec((l, gd), lambda bi, g: (0, 0)),
                  pl.BlockSpec((l, gd), lambda bi, g: (0, 0)),
                  pl.BlockSpec((l2, 2 * l2), lambda bi, g: (0, 0)),
                  pl.BlockSpec((1, gd, gd), lambda bi, g: (g, 0, 0))],
        out_specs=pl.BlockSpec((1, l, gd), lambda bi, g: (bi, 0, g)),
        out_shape=jax.ShapeDtypeStruct((b, l, FOURIER_WIDTH), BF16),
        scratch_shapes=[pltpu.VMEM((FFT_RADIX, 2 * l2, gd), BF16), pltpu.VMEM((l, gd), F32)],
        compiler_params=_cparams(2),
        name="fourier",
    )(proj, proj, cs, tw_re, tw_im, d3, fourier_w)


def _s5_matrices(lam_re, lam_im, log_step, b_re, b_im, c_re, c_im, dskip):
    t = S5_CHUNK
    g, n, p = S5_GROUPS, S5_STATE, S5_GROUP_DIM
    hp = lax.Precision.HIGHEST
    lam_re = lam_re.astype(F32)
    lam_im = lam_im.astype(F32)
    step = jnp.exp(log_step.astype(F32))[..., None]
    rho = lam_re * step
    th = lam_im * step
    mag = jnp.exp(rho)
    lb_re = mag * jnp.cos(th)
    lb_im = mag * jnp.sin(th)
    nr = lb_re - 1.0
    den = lam_re * lam_re + lam_im * lam_im
    fr = (nr * lam_re + lb_im * lam_im) / den
    fi = (lb_im * lam_re - nr * lam_im) / den
    b_re = b_re.astype(F32)
    b_im = b_im.astype(F32)
    bb_re = fr[..., None] * b_re - fi[..., None] * b_im
    bb_im = fr[..., None] * b_im + fi[..., None] * b_re
    tau = jnp.arange(t + 1, dtype=F32)
    pw_mag = jnp.exp(rho[..., None] * tau)
    pw_re = pw_mag * jnp.cos(th[..., None] * tau)
    pw_im = pw_mag * jnp.sin(th[..., None] * tau)
    c_re = c_re.astype(F32)
    c_im = c_im.astype(F32)
    ca_re = c_re[..., None] * pw_re[:, :, None] - c_im[..., None] * pw_im[:, :, None]
    ca_im = c_re[..., None] * pw_im[:, :, None] + c_im[..., None] * pw_re[:, :, None]
    kern = jnp.einsum('dgpnt,dgnq->dgtpq', jnp.concatenate([ca_re[..., :t], -ca_im[..., :t]], axis=3),
                      jnp.concatenate([bb_re, bb_im], axis=2), precision=hp)
    sig = np.arange(t)[:, None]
    ta = np.arange(t)[None, :]
    idx_f = np.clip(ta - sig, 0, t - 1)
    idx_b = np.clip(sig - ta, 0, t - 1)
    m_f = jnp.asarray((ta >= sig).astype(np.float32))[None, :, :, None, None]
    m_b = jnp.asarray((sig >= ta).astype(np.float32))[None, :, :, None, None]
    tf = kern[0][:, idx_f] * m_f
    tb = kern[1][:, idx_b] * m_b
    eye_t = jnp.asarray(np.eye(t, dtype=np.float32))[None, :, :, None, None]
    dmat = dskip.astype(F32)[:, None, None, :, None] * jnp.asarray(np.eye(p, dtype=np.float32))[None, None, None]
    tt = tf + tb + eye_t * dmat
    nb, gb = S5_NBLK, S5_GB
    one = np.ones
    eye = lambda k: np.eye(k, dtype=np.float32)
    exp_t = jnp.asarray(np.einsum('tu,pq,b->tpubq', eye(t), eye(p), one(gb)).reshape(t * p, S5_K))
    exp_n = jnp.asarray(np.einsum('nm,b->nbm', eye(n), one(gb)).reshape(n, S5_PART))
    mask_t = jnp.asarray(np.einsum('ab,s,q,u,p->saqubp', eye(gb), one(t), one(p), one(t), one(p))
                         .reshape(S5_K, S5_K))
    mask_ba = jnp.asarray(np.einsum('ab,s,q,n->saqbn', eye(gb), one(t), one(p), one(n)).reshape(S5_K, S5_PART))
    t_small = tt.reshape(nb, gb, t, t, p, p).transpose(0, 2, 1, 5, 3, 4).reshape(nb, S5_K, t * p)
    tmat = jnp.matmul(t_small, exp_t, precision=hp) * mask_t

    def ba(d, powers):
        pr = pw_re[d][:, :, powers]
        pi = pw_im[d][:, :, powers]
        re = pr[..., None] * bb_re[d][:, :, None, :] - pi[..., None] * bb_im[d][:, :, None, :]
        im = pr[..., None] * bb_im[d][:, :, None, :] + pi[..., None] * bb_re[d][:, :, None, :]
        blk = lambda m: jnp.matmul(m.reshape(nb, gb, n, t, p).transpose(0, 3, 1, 4, 2).reshape(nb, S5_K, n),
                                   exp_n, precision=hp) * mask_ba
        return blk(re), blk(im)

    baf_re, baf_im = ba(0, np.arange(t - 1, -1, -1))
    bab_re, bab_im = ba(1, np.arange(t))
    bamat = jnp.concatenate([baf_re, bab_re, baf_im, bab_im], axis=-1)

    def ca(d, powers):
        blk = lambda m: jnp.matmul(exp_n.T, m[..., powers].reshape(nb, gb, p, n, t).transpose(0, 3, 4, 1, 2)
                                   .reshape(nb, n, S5_K), precision=hp) * mask_ba.T
        return blk(ca_re[d]), blk(ca_im[d])

    caf_re, caf_im = ca(0, np.arange(1, t + 1))
    cab_re, cab_im = ca(1, np.arange(t, 0, -1))
    camat = jnp.concatenate([caf_re, cab_re, -caf_im, -cab_im], axis=1)
    part = lambda m: m[..., t].reshape(nb, S5_PART)
    a_pow = jnp.stack([jnp.concatenate([part(pw_re[0]), part(pw_re[1])], axis=-1),
                       jnp.concatenate([part(pw_im[0]), part(pw_im[1])], axis=-1)], axis=1)
    return tmat.astype(BF16), bamat.astype(BF16), camat.astype(BF16), a_pow


def _s5_kernel(u_ref, t_ref, ba_ref, ca_ref, a_ref, h0_ref, y_ref, hf_ref, ub_scr, s_scr, hin_scr, y_scr,
               *, nc, bt):
    t, w, hp = S5_CHUNK, S5_GB * S5_GROUP_DIM, S5_PART
    for bi in range(bt):
        for tau in range(t):
            ub_scr[tau, pl.ds(bi, nc, stride=bt), :] = u_ref[bi, pl.ds(tau, nc, stride=t), :]
    ub = jnp.concatenate([ub_scr[tau] for tau in range(t)], axis=1).astype(BF16)
    s_scr[...] = _dot(ub, ba_ref[0, 0])
    ar_f, ar_b = a_ref[0, 0, 0:1, 0:hp], a_ref[0, 0, 0:1, hp:2 * hp]
    ai_f, ai_b = a_ref[0, 0, 1:2, 0:hp], a_ref[0, 0, 1:2, hp:2 * hp]

    def step(i, carry):
        re_f, im_f, re_b, im_b = carry
        rf = pl.ds(i * bt, bt)
        rb = pl.ds((nc - 1 - i) * bt, bt)
        hin_scr[rf, 0:hp] = re_f
        hin_scr[rf, 2 * hp:3 * hp] = im_f
        hin_scr[rb, hp:2 * hp] = re_b
        hin_scr[rb, 3 * hp:4 * hp] = im_b
        n_re_f = ar_f * re_f - ai_f * im_f + s_scr[rf, 0:hp]
        n_im_f = ar_f * im_f + ai_f * re_f + s_scr[rf, 2 * hp:3 * hp]
        n_re_b = ar_b * re_b - ai_b * im_b + s_scr[rb, hp:2 * hp]
        n_im_b = ar_b * im_b + ai_b * re_b + s_scr[rb, 3 * hp:4 * hp]
        return n_re_f, n_im_f, n_re_b, n_im_b

    h0 = h0_ref[...].reshape(bt, 4 * hp)
    carry = (h0[:, 0:hp], h0[:, 2 * hp:3 * hp], h0[:, hp:2 * hp], h0[:, 3 * hp:4 * hp])
    slab = S5_K // S5_SCAN_SLABS
    y_parts = []
    for sl in range(S5_SCAN_SLABS):
        y_parts.append(_dot(ub, t_ref[0, 0, :, sl * slab:(sl + 1) * slab]))
        for i in range(sl * nc // S5_SCAN_SLABS, (sl + 1) * nc // S5_SCAN_SLABS):
            carry = step(i, carry)
    hf_ref[...] = jnp.concatenate([carry[0], carry[2], carry[1], carry[3]], axis=1).reshape(hf_ref.shape)
    y = jnp.concatenate(y_parts, axis=1) + _dot(hin_scr[...].astype(BF16), ca_ref[0, 0])
    for tau in range(t):
        y_scr[tau] = y[:, tau * w:(tau + 1) * w]
    for bi in range(bt):
        for tau in range(t):
            y_ref[bi, pl.ds(tau, nc, stride=t), :] = y_scr[tau, pl.ds(bi, nc, stride=bt), :]


def _s5_core(ub, tmat, bamat, camat, a_pow, h0, layer):
    b, l, _ = ub.shape
    nc = l // S5_CHUNK
    w = S5_GB * S5_GROUP_DIM
    bt = b if nc * b <= S5_MAX_ROWS else 1
    if bt == 1:
        h0 = h0.reshape(S5_NBLK, b, 1, 4 * S5_PART)
        st_spec = pl.BlockSpec((1, 1, 1, 4 * S5_PART), lambda j, bi: (j, bi, 0, 0))
    else:
        st_spec = pl.BlockSpec((1, bt, 4 * S5_PART), lambda j, bi: (j, bi, 0))
    rows = nc * bt
    y, hf = pl.pallas_call(
        functools.partial(_s5_kernel, nc=nc, bt=bt),
        grid=(S5_NBLK, b // bt),
        in_specs=[pl.BlockSpec((bt, l, w), lambda j, bi: (bi, 0, j)),
                  pl.BlockSpec((1, 1, S5_K, S5_K), lambda j, bi: (layer, j, 0, 0)),
                  pl.BlockSpec((1, 1, S5_K, 4 * S5_PART), lambda j, bi: (layer, j, 0, 0)),
                  pl.BlockSpec((1, 1, 4 * S5_PART, S5_K), lambda j, bi: (layer, j, 0, 0)),
                  pl.BlockSpec((1, 1, 2, 2 * S5_PART), lambda j, bi: (layer, j, 0, 0)),
                  st_spec],
        out_specs=[pl.BlockSpec((bt, l, w), lambda j, bi: (bi, 0, j)), st_spec],
        out_shape=[jax.ShapeDtypeStruct((b, l, S5_WIDTH), F32),
                   jax.ShapeDtypeStruct(h0.shape, F32)],
        scratch_shapes=[pltpu.VMEM((S5_CHUNK, rows, w), F32), pltpu.VMEM((rows, 4 * S5_PART), F32),
                        pltpu.VMEM((rows, 4 * S5_PART), F32), pltpu.VMEM((S5_CHUNK, rows, w), F32)],
        compiler_params=_cparams(2),
        name="s5_core",
    )(ub, tmat, bamat, camat, a_pow, h0)
    return y, hf.reshape(S5_NBLK, b, 4 * S5_PART)


def _s5_glu_kernel(y_ref, g_ref, w_ref, b_ref, o_ref):
    y = y_ref[0]
    y = 0.5 * y * (1.0 + jnp.tanh(math.sqrt(2.0 / math.pi) * (y + 0.044715 * (y * y * y))))
    z = _dot(y.astype(BF16), w_ref[...].astype(BF16)) + b_ref[...]
    o_ref[0] = (y * _sigmoid(z) * _silu(g_ref[0].astype(F32))).astype(BF16)


def _s5_glu(y, proj, glu_w, glu_b, tm):
    b, l, w = y.shape
    return pl.pallas_call(
        _s5_glu_kernel,
        grid=(b, l // tm),
        in_specs=[pl.BlockSpec((1, tm, w), lambda bi, i: (bi, i, 0)),
                  pl.BlockSpec((1, tm, w), lambda bi, i: (bi, i, COL_GB // w)),
                  pl.BlockSpec((w, w), lambda bi, i: (0, 0)),
                  pl.BlockSpec((1, w), lambda bi, i: (0, 0))],
        out_specs=pl.BlockSpec((1, tm, w), lambda bi, i: (bi, i, 0)),
        out_shape=jax.ShapeDtypeStruct((b, l, w), BF16),
        compiler_params=_cparams(2),
        name="s5_glu",
    )(y, proj, glu_w, glu_b.reshape(1, w))


def _s5_branch(proj, ub, mats, layer, glu_w, glu_b, h0_re, h0_im):
    b, l, _ = proj.shape
    g, n = S5_GROUPS, S5_STATE
    if h0_re is None:
        h0 = jnp.zeros((S5_NBLK, b, 4 * S5_PART), F32)
    else:
        blk = lambda h: h.astype(F32).reshape(b, S5_NBLK, S5_PART).transpose(1, 0, 2)
        h0 = jnp.concatenate([blk(h0_re[:, 0]), blk(h0_re[:, 1]), blk(h0_im[:, 0]), blk(h0_im[:, 1])], axis=-1)
    y, hf = _s5_core(ub, *mats, h0, layer)
    out = _s5_glu(y, proj, glu_w, glu_b, min(l, S5_GLU_TM))
    part = lambda i: hf[..., i * S5_PART:(i + 1) * S5_PART].transpose(1, 0, 2).reshape(b, g, n)
    fin_re = jnp.stack([part(0), part(1)], axis=1)
    fin_im = jnp.stack([part(2), part(3)], axis=1)
    return out, fin_re, fin_im


def _rope_tables(l):
    rows = l // GRID_W
    row = jnp.broadcast_to(jnp.arange(rows, dtype=F32)[:, None], (rows, GRID_W)).reshape(-1)
    col = jnp.broadcast_to(jnp.arange(GRID_W, dtype=F32)[None, :], (rows, GRID_W)).reshape(-1)
    half = HEAD_DIM // 2
    inv = ROPE_THETA ** (-jnp.arange(0, half, 2, dtype=F32) / half)
    ar = row[:, None] * inv
    ac = col[:, None] * inv
    cos = jnp.concatenate([jnp.cos(ar), jnp.cos(ar), jnp.cos(ac), jnp.cos(ac)], axis=1)
    sin = jnp.concatenate([-jnp.sin(ar), jnp.sin(ar), -jnp.sin(ac), jnp.sin(ac)], axis=1)
    return cos, sin


def _qk_kernel(*refs, rope, want_kf):
    q_ref, k_ref, v_ref, qn_ref, kn_ref = refs[:5]
    pos = 5
    if rope:
        cos_ref, sin_ref = refs[pos:pos + 2]
        pos += 2
    qo_ref, ko_ref, vo_ref = refs[pos:pos + 3]
    pos += 3
    if want_kf:
        kf_ref = refs[pos]
    hd = HEAD_DIM
    tm = q_ref.shape[1]
    quarter = hd // 4
    if rope:
        cos = cos_ref[...]
        sin = sin_ref[...]
        first = (lax.broadcasted_iota(jnp.int32, (tm, hd), 1) % (2 * quarter)) < quarter

    def norm(x, gain):
        ms = jnp.mean(x * x, axis=-1, keepdims=True)
        return x * lax.rsqrt(ms + NORM_EPS) * gain

    def rot(x):
        if not rope:
            return x
        nxt = pltpu.roll(x, hd - quarter, 1)
        prv = pltpu.roll(x, quarter, 1)
        return x * cos + jnp.where(first, nxt, prv) * sin

    qscale = (hd ** -0.5) * LOG2E
    for h in range(N_HEADS):
        x = norm(q_ref[0, :, h * hd:(h + 1) * hd].astype(F32), qn_ref[...])
        qo_ref[0, :, h * hd:(h + 1) * hd] = (rot(x) * qscale).astype(BF16)
    for h in range(N_KV_HEADS):
        x = norm(k_ref[0, :, h * hd:(h + 1) * hd].astype(F32), kn_ref[...])
        if want_kf:
            kf_ref[0, :, h * hd:(h + 1) * hd] = x
        ko_ref[0, :, h * hd:(h + 1) * hd] = rot(x).astype(BF16)
        vo_ref[0, h] = v_ref[0, :, h * hd:(h + 1) * hd].astype(F32).T.astype(BF16)


def _qk_prep(proj, q_norm, k_norm, rope_tabs, want_kf, tm):
    b, l, _ = proj.shape
    rope = rope_tabs is not None
    in_specs = [pl.BlockSpec((1, tm, ATTN_WIDTH), lambda bi, i: (bi, i, COL_QC // ATTN_WIDTH)),
                pl.BlockSpec((1, tm, KV_WIDTH), lambda bi, i: (bi, i, COL_KC // KV_WIDTH)),
                pl.BlockSpec((1, tm, KV_WIDTH), lambda bi, i: (bi, i, COL_VC // KV_WIDTH)),
                pl.BlockSpec((1, HEAD_DIM), lambda bi, i: (0, 0)),
                pl.BlockSpec((1, HEAD_DIM), lambda bi, i: (0, 0))]
    args = [proj, proj, proj, q_norm.reshape(1, HEAD_DIM), k_norm.reshape(1, HEAD_DIM)]
    if rope:
        in_specs += [pl.BlockSpec((tm, HEAD_DIM), lambda bi, i: (i, 0))] * 2
        args += list(rope_tabs)
    out_specs = [pl.BlockSpec((1, tm, ATTN_WIDTH), lambda bi, i: (bi, i, 0)),
                 pl.BlockSpec((1, tm, KV_WIDTH), lambda bi, i: (bi, i, 0)),
                 pl.BlockSpec((1, N_KV_HEADS, HEAD_DIM, tm), lambda bi, i: (bi, 0, 0, i))]
    out_shape = [jax.ShapeDtypeStruct((b, l, ATTN_WIDTH), BF16),
                 jax.ShapeDtypeStruct((b, l, KV_WIDTH), BF16),
                 jax.ShapeDtypeStruct((b, N_KV_HEADS, HEAD_DIM, l), BF16)]
    if want_kf:
        out_specs.append(pl.BlockSpec((1, tm, KV_WIDTH), lambda bi, i: (bi, i, 0)))
        out_shape.append(jax.ShapeDtypeStruct((b, l, KV_WIDTH), F32))
    return pl.pallas_call(
        functools.partial(_qk_kernel, rope=rope, want_kf=want_kf),
        grid=(b, l // tm),
        in_specs=in_specs, out_specs=out_specs, out_shape=out_shape,
        compiler_params=_cparams(2),
        name="qk_prep",
    )(*args)


def _attn_kernel(*refs, has_ctx):
    s_scr = refs[-1]
    q_ref, k_ref, v_ref = refs[:3]
    if has_ctx:
        ck_ref, cv_ref, g_ref, o_ref = refs[3:7]
        ck = ck_ref[0, 0].astype(BF16)
        cvt = cv_ref[0, 0, 0].astype(BF16)
        past = ck.shape[0]
    else:
        g_ref, o_ref = refs[3:5]
    hd = HEAD_DIM
    tq = q_ref.shape[1]
    lk = k_ref.shape[1]
    cr = min(lk, ATTN_KEY_CHUNK)
    chunks = [(c * cr, cr, (lambda c=c: k_ref[0, c * cr:(c + 1) * cr, :]),
               (lambda c=c: v_ref[0, 0, :, c * cr:(c + 1) * cr])) for c in range(lk // cr)]
    if has_ctx:
        chunks.append((lk, past, lambda: ck, lambda: cvt))
    qs = [q_ref[0, :, g * hd:(g + 1) * hd] for g in range(Q_PER_KV)]
    gate = _silu(g_ref[0].astype(F32))

    def score(g, chunk, m):
        off, rows, keys, _ = chunk
        s = _dot_nt(keys(), qs[g])
        s_scr[g % 2, off:off + rows, :] = s
        return jnp.maximum(m, jnp.max(s, axis=0, keepdims=True))

    def accumulate(g, chunk, m, den, ot):
        off, rows, _, values_t = chunk
        p = jnp.exp2(s_scr[g % 2, off:off + rows, :] - m)
        return den + jnp.sum(p, axis=0, keepdims=True), ot + _dot(values_t(), p.astype(BF16))

    neg = jnp.full((1, tq), -jnp.inf, F32)
    m_cur = neg
    for chunk in chunks:
        m_cur = score(0, chunk, m_cur)
    for g in range(Q_PER_KV):
        m_next = neg
        den = jnp.zeros((1, tq), F32)
        ot = jnp.zeros((hd, tq), F32)
        for chunk in chunks:
            den, ot = accumulate(g, chunk, m_cur, den, ot)
            if g + 1 < Q_PER_KV:
                m_next = score(g + 1, chunk, m_next)
        m_cur = m_next
        o = (ot / den).T
        o_ref[0, :, g * hd:(g + 1) * hd] = (o * gate[:, g * hd:(g + 1) * hd]).astype(BF16)


def _attention(qs, kr, vt, proj, ctx_k, ctx_v, layer, tq):
    b, l, _ = qs.shape
    gw = Q_PER_KV * HEAD_DIM
    has_ctx = ctx_k is not None
    in_specs = [pl.BlockSpec((1, tq, gw), lambda bi, h, i: (bi, i, h)),
                pl.BlockSpec((1, l, HEAD_DIM), lambda bi, h, i: (bi, 0, h)),
                pl.BlockSpec((1, 1, HEAD_DIM, l), lambda bi, h, i: (bi, h, 0, 0))]
    args = [qs, kr, vt]
    if has_ctx:
        past = ctx_k.shape[2]
        in_specs += [pl.BlockSpec((1, 1, past, HEAD_DIM), lambda bi, h, i: (bi, layer, 0, h)),
                     pl.BlockSpec((1, 1, 1, HEAD_DIM, past), lambda bi, h, i: (bi, layer, h, 0, 0))]
        args += [ctx_k, ctx_v]
    in_specs.append(pl.BlockSpec((1, tq, gw), lambda bi, h, i: (bi, i, COL_GC // gw + h)))
    args.append(proj)
    return pl.pallas_call(
        functools.partial(_attn_kernel, has_ctx=has_ctx),
        grid=(b, N_KV_HEADS, l // tq),
        in_specs=in_specs,
        out_specs=pl.BlockSpec((1, tq, gw), lambda bi, h, i: (bi, i, h)),
        out_shape=jax.ShapeDtypeStruct((b, l, ATTN_WIDTH), BF16),
        scratch_shapes=[pltpu.VMEM((2, l + (ctx_k.shape[2] if has_ctx else 0), tq), F32)],
        compiler_params=_cparams(3),
        name="attention",
    )(*args)


HGRN_COARSE = (32, 16, 8, 4)
HGRN_FINE = ()
HGRN_LEVELS = HGRN_COARSE + HGRN_FINE + (2, 1)
HGRN_CHUNKS_PER_TRIP = 4


def _hgrn_tables():
    c = HGRN_CHUNK
    t = np.arange(c)
    w01, sm = [], []
    for d in range(2):
        cum = (t[None, :] <= t[:, None]) if d == 0 else (t[None, :] >= t[:, None])
        cum = cum.astype(np.float32)
        w01.append(np.concatenate([cum] + [cum[_hgrn_ref_row(t, hs, d)] for hs in HGRN_FINE], axis=0))
        sd = []
        for hs in HGRN_LEVELS:
            is_q = _hgrn_is_query(t, hs, d)
            same = (t[:, None] // (2 * hs)) == (t[None, :] // (2 * hs))
            sd.append((same & is_q[:, None] & ~is_q[None, :]).astype(np.float32))
        sd.append(np.eye(c, dtype=np.float32))
        sm.append(np.stack(sd))
    return jnp.asarray(np.stack(w01), BF16), jnp.asarray(np.stack(sm), F32)


def _hgrn_ref_row(t, hs, d):
    base = t & ~(2 * hs - 1)
    return base + hs - 1 if d == 0 else base + hs


def _hgrn_is_query(t, hs, d):
    return ((t & hs) != 0) if d == 0 else ((t & hs) == 0)


def _hgrn_cums(z, lbv, w01):
    f = lbv + (1.0 - lbv) * _sigmoid(z)
    fm = jnp.maximum(f, F_FLOOR)
    hi, lo = _split2(jnp.log2(fm))
    return fm, 1.0 - f, _dot(w01, hi) + _dot(w01, lo)


def _hgrn_products(q, k, fm, allc, rows, d):
    c = HGRN_CHUNK
    sub = 8
    cum = allc[0:c]
    refs = {}
    for hs in HGRN_COARSE:
        pieces = []
        for j in range(c // sub):
            rr = int(_hgrn_ref_row(np.int64(j * sub), hs, d))
            pieces.append(jnp.broadcast_to(cum[rr:rr + 1, :], (sub, HGRN_DK)))
        refs[hs] = jnp.concatenate(pieces, axis=0)
    for i, hs in enumerate(HGRN_FINE):
        refs[hs] = allc[(i + 1) * c:(i + 2) * c]
    lrow = c - 1 if d == 0 else 0
    last = cum[lrow:lrow + 1, :]
    raws = [_dot_nt(q.astype(BF16), k.astype(BF16))]
    for hs in HGRN_LEVELS:
        is_q = ((rows & hs) != 0) if d == 0 else ((rows & hs) == 0)
        if hs == 1:
            x = jnp.where(is_q, q * fm, k).astype(BF16)
        elif hs == 2:
            pos = rows & 3
            prv = pltpu.roll(fm, 1, 0)
            nxt = pltpu.roll(fm, c - 1, 0)
            if d == 0:
                e = jnp.where(pos == 0, nxt, jnp.where(pos == 1, 1.0, jnp.where(pos == 2, fm, prv * fm)))
            else:
                e = jnp.where(pos == 0, fm * nxt, jnp.where(pos == 1, fm, jnp.where(pos == 2, 1.0, prv)))
            x = (jnp.where(is_q, q, k) * e).astype(BF16)
        else:
            x = (jnp.where(is_q, q, k) * jnp.exp2((cum - refs[hs]) * jnp.where(is_q, 1.0, -1.0))).astype(BF16)
        raws.append(_dot_nt(x, x))
    qi = (q * jnp.exp2(cum)).astype(BF16)
    ks = (k * jnp.exp2(last - cum)).astype(BF16)
    return raws, qi, ks, jnp.exp2(last)


def _hgrn_kernel(*refs, nc, has_s0, want_state):
    q_ref, v_ref, zf_ref, zb_ref, g_ref, lb_ref, ng_ref, w01_ref, sm_ref = refs[:9]
    pos = 9
    if has_s0:
        s0_ref = refs[pos]
        pos += 1
    o_ref = refs[pos]
    pos += 1
    if want_state:
        sf_ref = refs[pos]
        pos += 1
    o_scr, st_scr = refs[pos:pos + 2]
    c = HGRN_CHUNK
    l = nc * c
    for d in range(2):
        if has_s0:
            st_scr[d] = s0_ref[0, d, 0]
        else:
            st_scr[d] = jnp.zeros((HGRN_DV, HGRN_DK), F32)
    rows = lax.broadcasted_iota(jnp.int32, (c, HGRN_DK), 0)

    per_trip = math.gcd(nc, HGRN_CHUNKS_PER_TRIP)
    ntrips = nc // per_trip
    nl = len(HGRN_LEVELS)

    def instances(i):
        insts = []
        for u in range(per_trip):
            for d in range(2):
                ci = per_trip * i + u if d == 0 else nc - 1 - (per_trip * i + u)
                r = ci * c
                insts.append((d, r if isinstance(r, int) else pl.multiple_of(r, c)))
        return insts

    def emit(tot, r, n):
        ms = jnp.mean(tot * tot, axis=-1, keepdims=True)
        y = tot * lax.rsqrt(ms + NORM_EPS) * ng_ref[...]
        o_ref[0, pl.ds(r, n), :] = (y * _silu(g_ref[0, pl.ds(r, n), :].astype(F32))).astype(BF16)

    def state_step(inst, local, finalize):
        (d, r), (scores, qi, ks, dec) = inst, local
        vb = v_ref[0, pl.ds(r, c), :].astype(BF16)
        st = st_scr[d]
        out = _dot(scores, vb) + _dot_nt(qi, st.astype(BF16))
        st_scr[d] = dec * st + _dot_tn(vb, ks)
        if finalize:
            emit(out + o_scr[1 - d, pl.ds(r, c), :], r, c)
        else:
            o_scr[d, pl.ds(r, c), :] = out

    def trip(i, prev, finalize):
        insts = instances(i)
        prev_insts = instances(i - 1) if prev is not None else None
        gates = [_hgrn_cums((zf_ref, zb_ref)[d][0, pl.ds(r, c), :].astype(F32), lb_ref[d, 0], w01_ref[d])
                 for d, r in insts]
        out = []
        for idx, ((d, r), (fm, k, allc)) in enumerate(zip(insts, gates)):
            raws, qi, ks, dec = _hgrn_products(q_ref[0, pl.ds(r, c), :].astype(F32), k, fm, allc, rows, d)
            if prev is not None:
                state_step(prev_insts[idx], prev[idx], finalize)
            scores = sm_ref[d, nl] * raws[0]
            for li in range(nl):
                scores = scores + sm_ref[d, li] * raws[li + 1]
            out.append((scores.astype(BF16), qi, ks, dec))
        return tuple(out)

    fused = ntrips % 2 == 0
    half = ntrips // 2 if fused else ntrips
    local = lax.fori_loop(1, half + 1 if fused else ntrips, functools.partial(trip, finalize=False),
                          trip(0, None, False))
    if fused:
        local = lax.fori_loop(half + 1, ntrips, functools.partial(trip, finalize=True), local)
    for inst, loc in zip(instances(ntrips - 1), local):
        state_step(inst, loc, fused)
    if want_state:
        for d in range(2):
            sf_ref[0, d, 0] = st_scr[d]

    if not fused:
        fb = min(l, 256)

        def finish(i, carry):
            r = pl.multiple_of(i * fb, fb)
            emit(o_scr[0, pl.ds(r, fb), :] + o_scr[1, pl.ds(r, fb), :], r, fb)
            return carry

        lax.fori_loop(0, l // fb, finish, 0)


def _hgrn_branch(proj, lower_bound, norm_g, tables, s0_t, want_state):
    b, l, _ = proj.shape
    nc = l // HGRN_CHUNK
    w = HGRN_DK
    has_s0 = s0_t is not None
    w01, sm = tables

    def col(off):
        return lambda bi, h: (bi, 0, off // w + h)

    in_specs = [pl.BlockSpec((1, l, w), col(COL_QD)),
                pl.BlockSpec((1, l, w), col(COL_ID)),
                pl.BlockSpec((1, l, w), col(COL_ZF)),
                pl.BlockSpec((1, l, w), col(COL_ZB)),
                pl.BlockSpec((1, l, w), col(COL_GD)),
                pl.BlockSpec((2, 1, 1, w), lambda bi, h: (0, h, 0, 0)),
                pl.BlockSpec((1, w), lambda bi, h: (0, 0)),
                pl.BlockSpec(w01.shape, lambda bi, h: (0, 0, 0)),
                pl.BlockSpec(sm.shape, lambda bi, h: (0, 0, 0, 0))]
    args = [proj, proj, proj, proj, proj, lower_bound.reshape(2, HGRN_HEADS, 1, w), norm_g.reshape(1, w),
            w01, sm]
    st_spec = pl.BlockSpec((1, 2, 1, HGRN_DV, HGRN_DK), lambda bi, h: (bi, 0, h, 0, 0))
    if has_s0:
        in_specs.append(st_spec)
        args.append(s0_t)
    out_specs = [pl.BlockSpec((1, l, w), lambda bi, h: (bi, 0, h))]
    out_shape = [jax.ShapeDtypeStruct((b, l, HGRN_HEADS * HGRN_DV), BF16)]
    if want_state:
        out_specs.append(st_spec)
        out_shape.append(jax.ShapeDtypeStruct((b, 2, HGRN_HEADS, HGRN_DV, HGRN_DK), F32))
    res = pl.pallas_call(
        functools.partial(_hgrn_kernel, nc=nc, has_s0=has_s0, want_state=want_state),
        grid=(b, HGRN_HEADS),
        in_specs=in_specs, out_specs=out_specs, out_shape=out_shape,
        scratch_shapes=[pltpu.VMEM((2, l, HGRN_DV), F32), pltpu.VMEM((2, HGRN_DV, HGRN_DK), F32)],
        compiler_params=_cparams(2),
        name="hgrn",
    )(*args)
    return res if want_state else (res[0], None)


def _final_kernel(x_ref, gt_ref, ma_ref, mb_ref, mc_ref, md_ref, ya_ref, yb_ref, yc_ref, yd_ref,
                  wa_ref, wb_ref, wc_ref, wd_ref, wo_ref, gp_ref, o_ref, *, nct):
    ci = pl.program_id(2)
    tm = o_ref.shape[1]
    subs = [slice(r0, r0 + FINAL_ROWS) for r0 in range(0, tm, FINAL_ROWS)]
    m_refs = (ma_ref, mb_ref, mc_ref, md_ref)
    y_refs = (ya_ref, yb_ref, yc_ref, yd_ref)
    w_refs = (wa_ref, wb_ref, wc_ref, wd_ref)

    def branch_dots(rs):
        return [_dot(y_ref[0, rs, :], w_ref[0]) for y_ref, w_ref in zip(y_refs, w_refs)]

    def step(first, last):
        nxt = branch_dots(subs[0])
        for si, rs in enumerate(subs):
            branch = nxt
            if si + 1 < len(subs):
                nxt = branch_dots(subs[si + 1])
            mixed = None
            for m_ref, dot in zip(m_refs, branch):
                term = _gate_sigmoid(m_ref[0, rs, :].astype(F32)) * dot
                mixed = term if mixed is None else mixed + term
            contrib = _dot(mixed.astype(BF16), wo_ref[0])
            out = contrib if first else o_ref[0, rs, :] + contrib
            if last:
                ms = jnp.mean(out * out, axis=-1, keepdims=True)
                y = out * lax.rsqrt(ms + NORM_EPS) * gp_ref[...]
                o_ref[0, rs, :] = x_ref[0, rs, :] + gt_ref[0] * y
            else:
                o_ref[0, rs, :] = out

    if nct == 1:
        step(True, True)
    else:
        pl.when(ci == 0)(lambda: step(True, False))
        pl.when(jnp.logical_and(ci > 0, ci < nct - 1))(lambda: step(False, False))
        pl.when(ci == nct - 1)(lambda: step(False, True))


def _final(x, gate, proj, ys, w_projs, w_out, norm_post, layer, tm, tc):
    b, l, d = x.shape
    nct = d // tc
    per_batch = gate.shape[0] > 1
    gate_map = (lambda bi, i, c: (bi, 0, 0)) if per_batch else (lambda bi, i, c: (0, 0, 0))

    def mspec(j):
        return pl.BlockSpec((1, tm, tc), lambda bi, i, c: (bi, i, (COL_M + j * d) // tc + c))

    in_specs = [pl.BlockSpec((1, tm, d), lambda bi, i, c: (bi, i, 0)),
                pl.BlockSpec((1, 1, d), gate_map)]
    in_specs += [mspec(j) for j in range(N_BRANCHES)]
    in_specs += [pl.BlockSpec((1, tm, y.shape[-1]), lambda bi, i, c: (bi, i, 0)) for y in ys]
    in_specs += [pl.BlockSpec((1, w.shape[1], tc), lambda bi, i, c: (layer, 0, c)) for w in w_projs]
    in_specs += [pl.BlockSpec((1, tc, d), lambda bi, i, c: (layer, c, 0)),
                 pl.BlockSpec((1, d), lambda bi, i, c: (0, 0))]
    return pl.pallas_call(
        functools.partial(_final_kernel, nct=nct),
        grid=(b, l // tm, nct),
        in_specs=in_specs,
        out_specs=pl.BlockSpec((1, tm, d), lambda bi, i, c: (bi, i, 0)),
        out_shape=jax.ShapeDtypeStruct((b, l, d), F32),
        compiler_params=pltpu.CompilerParams(dimension_semantics=("arbitrary",) * 3,
                                             vmem_limit_bytes=FINAL_VMEM_LIMIT),
        name="merge_out",
    )(x, gate, proj, proj, proj, proj, *ys, *w_projs, w_out, norm_post.reshape(1, d))


def _trunk_layer(x, shift, scale, gate, p, ctx, layer, consts):
    b, l, d = x.shape
    shared_mod = shift.shape[0] == 1
    if shared_mod:
        proj, ub = _inproj(x.reshape(1, b * l, d), shift, scale, p['norm_pre'], p['w_in'], layer,
                           INPROJ_TM, INPROJ_TN)
        proj = proj.reshape(b, l, IN_COLS)
        ub = ub.reshape(b, l, S5_WIDTH)
    else:
        proj, ub = _inproj(x, shift, scale, p['norm_pre'], p['w_in'], layer, INPROJ_TM, INPROJ_TN)

    dl, cs = consts['dft'][l]
    y_a = _fourier_branch(proj, p['fourier_w'], dl, cs)

    if ctx is None:
        y_b, s5_re, s5_im = _s5_branch(proj, ub, p['s5_mats'], layer, p['s5_glu_w'], p['s5_glu_b'], None, None)
        qs, kr, vt, kf = _qk_prep(proj, p['q_norm'], p['k_norm'], None, True, min(l, QK_PREP_TM))
        y_c = _attention(qs, kr, vt, proj, None, None, layer, min(l, ATTN_TQ))
        y_d, hg = _hgrn_branch(proj, p['lower_bound'], p['hgrn_norm'], consts['hgrn'], None, True)
    else:
        cache_k, cache_v, st_re, st_im, s0_t = ctx
        y_b, _, _ = _s5_branch(proj, ub, p['s5_mats'], layer, p['s5_glu_w'], p['s5_glu_b'],
                               st_re[:, layer], st_im[:, layer])
        qs, kr, vt = _qk_prep(proj, p['q_norm'], p['k_norm'], consts['rope'], False, min(l, QK_PREP_TM))
        y_c = _attention(qs, kr, vt, proj, cache_k, cache_v, layer, min(l, ATTN_TQ))
        y_d, _ = _hgrn_branch(proj, p['lower_bound'], p['hgrn_norm'], consts['hgrn'], s0_t[:, layer], False)

    ys = (y_a, y_b, y_c, y_d)
    wps = (p['w_proj_a'], p['w_proj_b'], p['w_proj_c'], p['w_proj_d'])
    if shared_mod:
        n = b * l
        x_new = _final(x.reshape(1, n, d), gate, proj.reshape(1, n, IN_COLS),
                       tuple(y.reshape(1, n, y.shape[-1]) for y in ys), wps, p['w_out'], p['norm_post'], layer,
                       FINAL_TM, FINAL_TC)
        x_new = x_new.reshape(b, l, d)
    else:
        x_new = _final(x, gate, proj, ys, wps, p['w_out'], p['norm_post'], layer, FINAL_TM, FINAL_TC)

    if ctx is None:
        v_f = proj[:, :, COL_VC:COL_VC + KV_WIDTH].astype(F32)
        return x_new, (kf, v_f, s5_re, s5_im, hg)
    return x_new, None


def kernel(x_prompt, x_sample, c, cache_k, cache_v, state_s5_re, state_s5_im, state_hgrn, c_ctx,
           norm_pre, norm_post, w_mod, b_mod, w_in, fourier_w,
           s5_lambda_re, s5_lambda_im, s5_log_step, s5_b_re, s5_b_im, s5_c_re, s5_c_im,
           s5_d, s5_glu_w, s5_glu_b, q_norm, k_norm, hgrn_lb_logits, hgrn_norm,
           w_proj_a, w_proj_b, w_proj_c, w_proj_d, w_out):
    depth = w_in.shape[0]
    bp, lp, d = x_prompt.shape
    bs, ls, _ = x_sample.shape

    lb_w = jax.nn.softmax(hgrn_lb_logits.astype(F32), axis=0)
    lower_bounds = jnp.cumsum(lb_w, axis=0) - lb_w[0]

    rows = ((bs + 1 + 7) // 8) * 8
    c_all = jnp.zeros((rows, d), F32).at[:bs].set(c).at[bs].set(c_ctx)
    mod = _modulation(c_all, w_mod, b_mod)

    w_in_bf = w_in.astype(BF16)
    wpa, wpb, wpc, wpd = (w.astype(BF16) for w in (w_proj_a, w_proj_b, w_proj_c, w_proj_d))
    w_out_bf = w_out.astype(BF16)

    consts = {'dft': {lp: _dft_tables(lp), ls: _dft_tables(ls)},
              'rope': _rope_tables(ls),
              'hgrn': _hgrn_tables()}
    past = cache_k.shape[2]
    ctx_k = cache_k.reshape(bs, depth, past, KV_WIDTH)
    ctx_v = cache_v.transpose(0, 1, 3, 4, 2)
    s0_t = jnp.swapaxes(state_hgrn.astype(F32), -1, -2)

    s5_mats = jax.vmap(_s5_matrices)(s5_lambda_re, s5_lambda_im, s5_log_step, s5_b_re, s5_b_im, s5_c_re, s5_c_im,
                                     s5_d.reshape(depth, S5_GROUPS, S5_GROUP_DIM))

    y_p, y_s = x_prompt, x_sample
    ks, vs, s5r, s5i, hg = [], [], [], [], []
    for l in range(depth):
        p = {'norm_pre': norm_pre[l].reshape(1, d), 'norm_post': norm_post[l], 'w_in': w_in_bf,
             'fourier_w': fourier_w[l],
             's5_mats': s5_mats,
             's5_glu_w': s5_glu_w[l], 's5_glu_b': s5_glu_b[l],
             'q_norm': q_norm[l], 'k_norm': k_norm[l],
             'lower_bound': lower_bounds[l], 'hgrn_norm': hgrn_norm[l],
             'w_proj_a': wpa, 'w_proj_b': wpb, 'w_proj_c': wpc, 'w_proj_d': wpd,
             'w_out': w_out_bf}
        m_ctx = mod[l, bs].reshape(1, 1, 3 * d)
        sh, sc, gt = (m_ctx[..., i * d:(i + 1) * d] for i in range(3))
        y_p, (k_l, v_l, sr_l, si_l, hg_l) = _trunk_layer(y_p, sh, sc, gt, p, None, l, consts)
        ks.append(k_l.reshape(bp, lp, N_KV_HEADS, HEAD_DIM))
        vs.append(v_l.reshape(bp, lp, N_KV_HEADS, HEAD_DIM))
        s5r.append(sr_l)
        s5i.append(si_l)
        hg.append(jnp.swapaxes(hg_l, -1, -2))
        m_s = mod[l, :bs].reshape(bs, 1, 3 * d)
        sh, sc, gt = (m_s[..., i * d:(i + 1) * d] for i in range(3))
        ctx = (ctx_k, ctx_v, state_s5_re, state_s5_im, s0_t)
        y_s, _ = _trunk_layer(y_s, sh, sc, gt, p, ctx, l, consts)

    return (y_p, y_s, jnp.stack(ks, axis=1), jnp.stack(vs, axis=1),
            jnp.stack(s5r, axis=1), jnp.stack(s5i, axis=1), jnp.stack(hg, axis=1))
```
